```python
import jax, jax.numpy as jnp
from jax import lax
import numpy as np

D_MODEL = 2048
BATCH = 2
SEQ = 16384
DEPTH = 1

MEM_LEN = 256
EPS = 1e-6

SSD_INNER = 3 * D_MODEL // 4
SSD_HEAD_DIM = 64
SSD_HEADS = SSD_INNER // SSD_HEAD_DIM
SSD_GROUPS = 4
SSD_HEADS_PER_GROUP = SSD_HEADS // SSD_GROUPS
SSD_STATE = 128
SSD_CONV = 4
SSD_CHUNK = 128
SSD_BC = SSD_GROUPS * SSD_STATE
SSD_CONV_DIM = SSD_INNER + 2 * SSD_BC

SB_HEADS = 4
SB_HEAD_DIM = 128
SB_WIDTH = SB_HEADS * SB_HEAD_DIM
SB_BLOCK = 128

N_BRANCH = 2
COL_Z = SSD_INNER
COL_XBC = COL_Z + SSD_CONV_DIM
COL_DT = COL_XBC + SSD_HEADS
COL_QKV = COL_DT + 3 * SB_WIDTH
IN_COLS = COL_QKV + N_BRANCH * D_MODEL

XA_HEADS = 4
XA_HEAD_DIM = 128
XA_WIDTH = XA_HEADS * XA_HEAD_DIM

N_GROUPS = 4
EXPERTS_PER_GROUP = 8
N_EXPERTS = N_GROUPS * EXPERTS_PER_GROUP
TOP_K = 2
EXPERT_FF = D_MODEL // 4
MOE_BLOCK = 256

kernel_name = "hybrid_ssd_stickbreak_hiermoe"


def rms_norm(x, w):
    xf = x.astype(jnp.float32)
    y = xf * lax.rsqrt(jnp.mean(xf * xf, axis=-1, keepdims=True) + EPS)
    return (y * w.astype(jnp.float32)).astype(x.dtype)


def causal_dwconv(u, w, b):
    c = u.shape[-1]
    out = lax.conv_general_dilated(
        u, w[:, None, :].astype(u.dtype), window_strides=(1,),
        padding=[(SSD_CONV - 1, 0)], dimension_numbers=('NWC', 'WIO', 'NWC'),
        feature_group_count=c)
    return out + b.astype(u.dtype)


def ssd_mixer(z, xbc, dt_raw, conv_w, conv_b, dt_bias, a_log, d_skip, norm_w):
    bsz, s, _ = z.shape
    g, r, p, n, L = SSD_GROUPS, SSD_HEADS_PER_GROUP, SSD_HEAD_DIM, SSD_STATE, SSD_CHUNK
    xbc = jax.nn.silu(causal_dwconv(xbc, conv_w, conv_b))
    xs = xbc[..., :SSD_INNER].reshape(bsz, s, g, r, p)
    bm = xbc[..., SSD_INNER:SSD_INNER + SSD_BC].reshape(bsz, s, g, n).astype(jnp.float32)
    cm = xbc[..., SSD_INNER + SSD_BC:].reshape(bsz, s, g, n).astype(jnp.float32)
    dt = jax.nn.softplus(dt_raw.astype(jnp.float32) + dt_bias.astype(jnp.float32))
    a = -jnp.exp(a_log.astype(jnp.float32))
    da = (dt * a).reshape(bsz, s, g, r)
    xdt = xs.astype(jnp.float32) * dt.reshape(bsz, s, g, r)[..., None]
    nc = s // L

    def to_chunks(t):
        return jnp.moveaxis(t.reshape(bsz, nc, L, *t.shape[2:]), 1, 0)

    tri = jnp.arange(L)[:, None] >= jnp.arange(L)[None, :]

    def step(state, inp):
        x_c, a_c, b_c, c_c = inp
        a_cum = jnp.cumsum(a_c, axis=1)
        seg = a_cum[:, :, None] - a_cum[:, None, :]
        decay = jnp.exp(jnp.where(tri[None, :, :, None, None], seg, -jnp.inf))
        cb = jnp.einsum('btgn,bsgn->btsg', c_c, b_c)
        y_diag = jnp.einsum('btsgr,bsgrp->btgrp', cb[..., None] * decay, x_c)
        y_off = jnp.einsum('btgn,bgrpn->btgrp', c_c, state) * jnp.exp(a_cum)[..., None]
        a_last = a_cum[:, -1]
        w_s = jnp.exp(a_last[:, None] - a_cum)
        new_state = state * jnp.exp(a_last)[..., None, None] + jnp.einsum(
            'bsgn,bsgrp->bgrpn', b_c, x_c * w_s[..., None])
        return new_state, y_diag + y_off

    state0 = jnp.zeros((bsz, g, r, p, n), jnp.float32)
    _, ys = lax.scan(step, state0, (to_chunks(xdt), to_chunks(da), to_chunks(bm), to_chunks(cm)))
    y = jnp.moveaxis(ys, 0, 1).reshape(bsz, s, g, r, p)
    y = y + xs.astype(jnp.float32) * d_skip.astype(jnp.float32).reshape(g, r)[:, :, None]
    y = y.reshape(bsz, s, SSD_INNER) * jax.nn.silu(z.astype(jnp.float32))
    return rms_norm(y, norm_w).astype(z.dtype)


def stick_breaking_attention(q, k, v):
    bsz, s, h, dh = q.shape
    scale = dh ** -0.5
    qh = q.transpose(0, 2, 1, 3).astype(jnp.float32)
    kh = k.transpose(0, 2, 1, 3).astype(jnp.float32)
    vh = v.transpose(0, 2, 1, 3).astype(jnp.float32)
    nb = s // SB_BLOCK
    r = jnp.arange(SB_BLOCK)
    later = (r[:, None] > r[None, :]).astype(jnp.float32)
    outs = []
    for i in range(nb):
        n_keys = (i + 1) * SB_BLOCK
        qb = qh[:, :, i * SB_BLOCK:(i + 1) * SB_BLOCK]
        kb = kh[:, :, :n_keys]
        vb = vh[:, :, :n_keys]
        t_idx = i * SB_BLOCK + r
        causal = jnp.arange(n_keys)[None, :] < t_idx[:, None]
        zlog = jnp.einsum('bhtd,bhsd->bhts', qb, kb) * scale
        log_beta = jax.nn.log_sigmoid(zlog)
        log_keep = jnp.where(causal, log_beta - zlog, 0.0)
        lk = log_keep.reshape(bsz, h, SB_BLOCK, i + 1, SB_BLOCK)
        after_in = jnp.einsum('bhtcj,js->bhtcs', lk, later)
        blk_sum = jnp.sum(lk, axis=-1)
        carry = lax.cumsum(blk_sum, axis=3, reverse=True) - blk_sum
        after = (after_in + carry[..., None]).reshape(bsz, h, SB_BLOCK, n_keys)
        att = jnp.where(causal, jnp.exp(log_beta + after), 0.0)
        outs.append(jnp.einsum('bhts,bhsd->bhtd', att, vb))
    out = jnp.concatenate(outs, axis=2).transpose(0, 2, 1, 3)
    return out.reshape(bsz, s, h * dh).astype(q.dtype)


def memory_cross_attention(u, mem_n, wq, wk, wv, wo):
    bsz, s, _ = u.shape
    m = mem_n.shape[1]
    q = (u @ wq).reshape(bsz, s, XA_HEADS, XA_HEAD_DIM)
    k = (mem_n @ wk).reshape(bsz, m, XA_HEADS, XA_HEAD_DIM)
    v = (mem_n @ wv).reshape(bsz, m, XA_HEADS, XA_HEAD_DIM)
    sc = jnp.einsum('bshd,bmhd->bhsm', q, k).astype(jnp.float32) * (XA_HEAD_DIM ** -0.5)
    pr = jax.nn.softmax(sc, axis=-1).astype(v.dtype)
    o = jnp.einsum('bhsm,bmhd->bshd', pr, v).reshape(bsz, s, XA_WIDTH)
    return o @ wo


def hierarchical_moe(u, w_rg, w_re, w_gate, w_up, w_down):
    bsz, s, dm = u.shape
    t = bsz * s
    tok = u.reshape(t, dm)
    g_logits = (tok @ w_rg).astype(jnp.float32)
    g_prob = jax.nn.softmax(g_logits, axis=-1)
    g_sel = jnp.argmax(g_logits, axis=-1)
    g_gate = jnp.take_along_axis(g_prob, g_sel[:, None], axis=-1)[:, 0]
    e_logits = (tok @ w_re).astype(jnp.float32).reshape(t, N_GROUPS, EXPERTS_PER_GROUP)
    e_in_group = jnp.take_along_axis(e_logits, g_sel[:, None, None], axis=1)[:, 0]
    top_val, top_idx = lax.top_k(e_in_group, TOP_K)
    gate = jax.nn.softmax(top_val, axis=-1) * g_gate[:, None]
    expert_id = (g_sel[:, None] * EXPERTS_PER_GROUP + top_idx).astype(jnp.int32)
    flat_e = expert_id.reshape(-1)
    flat_t = jnp.repeat(jnp.arange(t, dtype=jnp.int32), TOP_K)
    flat_w = gate.reshape(-1)
    order = jnp.argsort(flat_e)
    se = flat_e[order]
    counts = jnp.bincount(flat_e, length=N_EXPERTS)
    offsets = jnp.cumsum(counts) - counts
    padded = (counts + MOE_BLOCK - 1) // MOE_BLOCK * MOE_BLOCK
    pad_end = jnp.cumsum(padded)
    pad_start = pad_end - padded
    dest = pad_start[se] + (jnp.arange(t * TOP_K) - offsets[se])
    n_slots = t * TOP_K + N_EXPERTS * MOE_BLOCK
    n_blocks = n_slots // MOE_BLOCK
    slot_tok = jnp.full((n_slots,), t, jnp.int32).at[dest].set(flat_t[order])
    slot_w = jnp.zeros((n_slots,), jnp.float32).at[dest].set(flat_w[order])
    block_e = jnp.minimum(
        jnp.searchsorted(pad_end, jnp.arange(n_blocks) * MOE_BLOCK, side='right'),
        N_EXPERTS - 1)
    tok_pad = jnp.concatenate([tok, jnp.zeros((1, dm), tok.dtype)], axis=0)

    def run_block(args):
        idx, e = args
        xb = tok_pad[idx]
        hid = jax.nn.silu(xb @ w_gate[e]) * (xb @ w_up[e])
        return hid @ w_down[e]

    yb = lax.map(run_block, (slot_tok.reshape(n_blocks, MOE_BLOCK), block_e))
    yb = yb.reshape(n_slots, dm).astype(jnp.float32) * slot_w[:, None]
    out = jnp.zeros((t + 1, dm), jnp.float32).at[slot_tok].add(yb)[:t]
    return out.reshape(bsz, s, dm).astype(u.dtype)


def setup_inputs(seed: int = 0) -> dict:
    key = jax.random.key(seed)
    ks = iter(jax.random.split(key, 40))
    f32 = jnp.float32

    def w(shape, fan_in):
        return jax.random.normal(next(ks), shape, f32) * (fan_in ** -0.5)

    def gain(shape):
        return 1.0 + 0.01 * jax.random.normal(next(ks), shape, f32)

    x = jax.random.normal(next(ks), (BATCH, SEQ, D_MODEL), f32)
    mem = jax.random.normal(next(ks), (BATCH, MEM_LEN, D_MODEL), f32)
    dt0 = jnp.exp(jax.random.uniform(next(ks), (DEPTH, SSD_HEADS), f32,
                                     np.log(1e-3), np.log(1e-1)))
    dt_bias = dt0 + jnp.log(-jnp.expm1(-dt0))
    a_log = jnp.log(jax.random.uniform(next(ks), (DEPTH, SSD_HEADS), f32, 1.0, 16.0))
    return {
        "x": x,
        "mem": mem,
        "norm_mix_w": gain((DEPTH, D_MODEL)),
        "w_in": w((DEPTH, D_MODEL, IN_COLS), D_MODEL),
        "conv_w": w((DEPTH, SSD_CONV, SSD_CONV_DIM), SSD_CONV),
        "conv_b": 0.01 * jax.random.normal(next(ks), (DEPTH, SSD_CONV_DIM), f32),
        "dt_bias": dt_bias,
        "a_log": a_log,
        "d_skip": 1.0 + 0.1 * jax.random.normal(next(ks), (DEPTH, SSD_HEADS), f32),
        "ssd_norm_w": gain((DEPTH, SSD_INNER)),
        "w_ssd_branch": w((DEPTH, SSD_INNER, D_MODEL), SSD_INNER),
        "w_sb_branch": w((DEPTH, SB_WIDTH, D_MODEL), SB_WIDTH),
        "w_mix_out": w((DEPTH, D_MODEL, D_MODEL), D_MODEL),
        "norm_xa_w": gain((DEPTH, D_MODEL)),
        "norm_mem_w": gain((DEPTH, D_MODEL)),
        "w_xq": w((DEPTH, D_MODEL, XA_WIDTH), D_MODEL),
        "w_xk": w((DEPTH, D_MODEL, XA_WIDTH), D_MODEL),
        "w_xv": w((DEPTH, D_MODEL, XA_WIDTH), D_MODEL),
        "w_xo": w((DEPTH, XA_WIDTH, D_MODEL), XA_WIDTH),
        "norm_moe_w": gain((DEPTH, D_MODEL)),
        "w_router_group": w((DEPTH, D_MODEL, N_GROUPS), D_MODEL),
        "w_router_expert": w((DEPTH, D_MODEL, N_EXPERTS), D_MODEL),
        "w_expert_gate": w((DEPTH, N_EXPERTS, D_MODEL, EXPERT_FF), D_MODEL),
        "w_expert_up": w((DEPTH, N_EXPERTS, D_MODEL, EXPERT_FF), D_MODEL),
        "w_expert_down": w((DEPTH, N_EXPERTS, EXPERT_FF, D_MODEL), EXPERT_FF),
        "norm_final_w": gain((D_MODEL,)),
    }


def reference(x, mem, norm_mix_w, w_in, conv_w, conv_b, dt_bias, a_log, d_skip,
              ssd_norm_w, w_ssd_branch, w_sb_branch, w_mix_out, norm_xa_w, norm_mem_w,
              w_xq, w_xk, w_xv, w_xo, norm_moe_w, w_router_group, w_router_expert,
              w_expert_gate, w_expert_up, w_expert_down, norm_final_w):
    bsz, s, _ = x.shape
    h = x
    for l in range(DEPTH):
        u = rms_norm(h, norm_mix_w[l])
        w_l = w_in[l]
        z = u @ w_l[:, :COL_Z]
        xbc = u @ w_l[:, COL_Z:COL_XBC]
        dt_raw = u @ w_l[:, COL_XBC:COL_DT]
        qkv = u @ w_l[:, COL_DT:COL_QKV]
        gates = u @ w_l[:, COL_QKV:]
        y_ssd = ssd_mixer(z, xbc, dt_raw, conv_w[l], conv_b[l], dt_bias[l], a_log[l],
                          d_skip[l], ssd_norm_w[l])
        q = qkv[..., :SB_WIDTH].reshape(bsz, s, SB_HEADS, SB_HEAD_DIM)
        k = qkv[..., SB_WIDTH:2 * SB_WIDTH].reshape(bsz, s, SB_HEADS, SB_HEAD_DIM)
        v = qkv[..., 2 * SB_WIDTH:].reshape(bsz, s, SB_HEADS, SB_HEAD_DIM)
        y_sb = stick_breaking_attention(q, k, v)
        g = jax.nn.sigmoid(gates.astype(jnp.float32)).astype(h.dtype)
        merged = (g[..., :D_MODEL] * (y_ssd @ w_ssd_branch[l])
                  + g[..., D_MODEL:] * (y_sb @ w_sb_branch[l]))
        h = h + merged @ w_mix_out[l]
        mem_n = rms_norm(mem, norm_mem_w[l])
        h = h + memory_cross_attention(rms_norm(h, norm_xa_w[l]), mem_n,
                                       w_xq[l], w_xk[l], w_xv[l], w_xo[l])
        h = h + hierarchical_moe(rms_norm(h, norm_moe_w[l]), w_router_group[l],
                                 w_router_expert[l], w_expert_gate[l], w_expert_up[l],
                                 w_expert_down[l])
    return rms_norm(h, norm_final_w)
```

```python
import functools

import jax
import jax.numpy as jnp
from jax import lax
from jax.experimental import pallas as pl
from jax.experimental.pallas import tpu as pltpu

F32 = jnp.float32
BF16 = jnp.bfloat16
EPS = 1e-6

SSD_HEAD_DIM = 64
SSD_GROUPS = 4
SSD_STATE = 128
SSD_CONV = 4
SSD_CHUNK = 128
SB_HEADS = 4
SB_HEAD_DIM = 128
SB_BLOCK = 128
XA_HEADS = 4
XA_HEAD_DIM = 128
N_GROUPS = 4
EXPERTS_PER_GROUP = 8
N_EXPERTS = N_GROUPS * EXPERTS_PER_GROUP
TOP_K = 2
MOE_BLOCK = 256

LANES = 128
SUBLANES = 8
VMEM_LIMIT = 56 * 1024 * 1024

SB_EXIT_LOG = -104.0
NEG_BIG = -1e30


def _cparams(sem):
    return pltpu.CompilerParams(dimension_semantics=sem, vmem_limit_bytes=VMEM_LIMIT)


def _rms(x, w):
    var = jnp.mean(x * x, axis=-1, keepdims=True)
    return x * lax.rsqrt(var + EPS) * w


def _split2(v):
    hi = v.astype(BF16)
    lo = (v - hi.astype(F32)).astype(BF16)
    return hi, lo


def _dot(a, b):
    return jnp.dot(a, b, preferred_element_type=F32)


def _silu(x):
    return x / (1.0 + jnp.exp(-x))


def _softplus(x):
    return jnp.maximum(x, 0.0) + jnp.log(1.0 + jnp.exp(-jnp.abs(x)))


def _norm_matmul_kernel(x_ref, nw_ref, w_ref, ws_ref, o_ref, os_ref, u_ref):
    @pl.when(pl.program_id(1) == 0)
    def _():
        u = _rms(x_ref[...], nw_ref[...]).astype(BF16)
        u_ref[...] = u
        os_ref[...] = _dot(u, ws_ref[...])

    o_ref[...] = _dot(u_ref[...], w_ref[...]).astype(o_ref.dtype)


def _norm_matmul(x, nw, w, w_side, tm, tn):
    m, k = x.shape
    n = w.shape[1]
    ns = w_side.shape[1]
    tm = min(tm, m)
    return pl.pallas_call(
        _norm_matmul_kernel,
        grid=(m // tm, n // tn),
        in_specs=[
            pl.BlockSpec((tm, k), lambda i, j: (i, 0)),
            pl.BlockSpec((1, k), lambda i, j: (0, 0)),
            pl.BlockSpec((k, tn), lambda i, j: (0, j)),
            pl.BlockSpec((k, ns), lambda i, j: (0, 0)),
        ],
        out_specs=[
            pl.BlockSpec((tm, tn), lambda i, j: (i, j)),
            pl.BlockSpec((tm, ns), lambda i, j: (i, 0)),
        ],
        out_shape=[jax.ShapeDtypeStruct((m, n), BF16), jax.ShapeDtypeStruct((m, ns), F32)],
        scratch_shapes=[pltpu.VMEM((tm, k), BF16)],
        compiler_params=_cparams(("parallel", "arbitrary")),
        name="norm_matmul",
    )(x, nw, w, w_side)


def _ssd_kernel(z_ref, xs_ref, b_ref, c_ref, dtr_ref, cw_ref, cb_ref, dtb_ref, alog_ref, dsk_ref,
                nw_ref, e_ref, o_ref,
                tail_ref, state_ref, xact_ref, bact_ref, cact_ref, dtx_ref, eax_ref, wsx_ref, y_ref):
    L = SSD_CHUNK
    inner = xs_ref.shape[1]
    bc = b_ref.shape[1]
    n_pairs = inner // LANES

    @pl.when(pl.program_id(1) == 0)
    def _():
        tail_ref[...] = jnp.zeros_like(tail_ref)
        state_ref[...] = jnp.zeros_like(state_ref)

    row8 = lax.broadcasted_iota(jnp.int32, (SUBLANES, 2 * LANES), 0)

    def conv_seg(src_ref, col0, width, dst_ref):
        for j in range(0, width, 2 * LANES):
            cols = slice(col0 + j, col0 + j + 2 * LANES)
            xin = src_ref[:, j:j + 2 * LANES].astype(F32)
            t8 = tail_ref[:, cols]
            acc = xin * cw_ref[SSD_CONV - 1:SSD_CONV, cols] + cb_ref[:, cols]
            for s in range(1, SSD_CONV):
                r = pltpu.roll(xin, s, 0)
                top = jnp.where(row8 < s, pltpu.roll(t8, s, 0), r[:SUBLANES])
                r = jnp.concatenate([top, r[SUBLANES:]], axis=0)
                acc = acc + r * cw_ref[SSD_CONV - 1 - s:SSD_CONV - s, cols]
            tail_ref[:, cols] = xin[L - SUBLANES:]
            dst_ref[:, j:j + 2 * LANES] = _silu(acc).astype(dst_ref.dtype)

    conv_seg(xs_ref, 0, inner, xact_ref)
    conv_seg(b_ref, inner, bc, bact_ref)
    conv_seg(c_ref, inner + bc, bc, cact_ref)

    dt = _softplus(dtr_ref[...] + dtb_ref[...])
    da = dt * (-jnp.exp(alog_ref[...]))
    rowl = lax.broadcasted_iota(jnp.int32, (L, L), 0)
    coll = lax.broadcasted_iota(jnp.int32, (L, L), 1)
    lower = rowl >= coll
    tri = jnp.where(lower, 1.0, 0.0).astype(BF16)
    d1 = da.astype(BF16)
    r1 = da - d1.astype(F32)
    d2 = r1.astype(BF16)
    d3 = (r1 - d2.astype(F32)).astype(BF16)
    a_cum = _dot(tri, d1) + _dot(tri, d2) + _dot(tri, d3)
    a_cum_t = a_cum.T
    a_last = a_cum[L - 1:L, :]
    e_mat = e_ref[...]

    def expand(v):
        hi, lo = _split2(v)
        return _dot(hi, e_mat) + _dot(lo, e_mat)

    dtx_ref[...] = expand(dt)
    eax_ref[...] = expand(jnp.exp(a_cum))
    wsx_ref[...] = expand(jnp.exp(a_last - a_cum))
    elx = expand(jnp.broadcast_to(jnp.exp(a_last), (SUBLANES, LANES)))[0:1]

    lane = lax.broadcasted_iota(jnp.int32, (L, LANES), 1)
    pairs_per_group = n_pairs // SSD_GROUPS
    for g in range(SSD_GROUPS):
        gcols = slice(g * SSD_STATE, (g + 1) * SSD_STATE)
        bg_t = bact_ref[:, gcols].T.astype(BF16)
        cg = cact_ref[:, gcols]
        cb = _dot(cg, bg_t)
        for pp in range(pairs_per_group):
            p = g * pairs_per_group + pp
            pc = slice(p * LANES, (p + 1) * LANES)
            ms = []
            for hh in (2 * p, 2 * p + 1):
                seg = a_cum[:, hh:hh + 1] - a_cum_t[hh:hh + 1, :]
                dec = jnp.exp(jnp.where(lower, seg, NEG_BIG))
                ms.append((cb * dec).astype(BF16))
            lhs = jnp.concatenate(ms, axis=1)
            xs_p = xact_ref[:, pc]
            xdt = xs_p * dtx_ref[:, pc]
            rhs = jnp.concatenate([jnp.where(lane < SSD_HEAD_DIM, xdt, 0.0).astype(BF16),
                                   jnp.where(lane >= SSD_HEAD_DIM, xdt, 0.0).astype(BF16)], axis=0)
            st = state_ref[p]
            y = _dot(lhs, rhs)
            y = y + _dot(cg, st.astype(BF16)) * eax_ref[:, pc]
            y = y + xs_p * dsk_ref[:, pc]
            y_ref[:, pc] = y
            xw = (xdt * wsx_ref[:, pc]).astype(BF16)
            state_ref[p] = st * elx[:, pc] + _dot(bg_t, xw)

    yg = y_ref[...] * _silu(z_ref[...].astype(F32))
    o_ref[...] = _rms(yg, nw_ref[...]).astype(o_ref.dtype)


def _ssd(proj, dt_raw, conv_w, conv_b, dt_bias_p, a_log_p, dskip_x, norm_w, e_mat, bsz, seq, inner, bc):
    L = SSD_CHUNK
    nc = seq // L
    conv_dim = inner + 2 * bc
    n_pairs = inner // LANES
    ib = inner // bc

    def rows(b, c):
        return b * nc + c

    return pl.pallas_call(
        _ssd_kernel,
        grid=(bsz, nc),
        in_specs=[
            pl.BlockSpec((L, inner), lambda b, c: (rows(b, c), 0)),
            pl.BlockSpec((L, inner), lambda b, c: (rows(b, c), 1)),
            pl.BlockSpec((L, bc), lambda b, c: (rows(b, c), 2 * ib)),
            pl.BlockSpec((L, bc), lambda b, c: (rows(b, c), 2 * ib + 1)),
            pl.BlockSpec((L, LANES), lambda b, c: (rows(b, c), 0)),
            pl.BlockSpec((SSD_CONV, conv_dim), lambda b, c: (0, 0)),
            pl.BlockSpec((1, conv_dim), lambda b, c: (0, 0)),
            pl.BlockSpec((1, LANES), lambda b, c: (0, 0)),
            pl.BlockSpec((1, LANES), lambda b, c: (0, 0)),
            pl.BlockSpec((1, inner), lambda b, c: (0, 0)),
            pl.BlockSpec((1, inner), lambda b, c: (0, 0)),
            pl.BlockSpec((LANES, inner), lambda b, c: (0, 0)),
        ],
        out_specs=pl.BlockSpec((L, inner), lambda b, c: (rows(b, c), 0)),
        out_shape=jax.ShapeDtypeStruct((bsz * seq, inner), BF16),
        scratch_shapes=[
            pltpu.VMEM((SUBLANES, conv_dim), F32),
            pltpu.VMEM((n_pairs, SSD_STATE, LANES), F32),
            pltpu.VMEM((L, inner), F32),
            pltpu.VMEM((L, bc), F32),
            pltpu.VMEM((L, bc), BF16),
            pltpu.VMEM((L, inner), F32),
            pltpu.VMEM((L, inner), F32),
            pltpu.VMEM((L, inner), F32),
            pltpu.VMEM((L, inner), F32),
        ],
        compiler_params=_cparams(("parallel", "arbitrary")),
        name="ssd",
    )(proj, proj, proj, proj, dt_raw, conv_w, conv_b, dt_bias_p, a_log_p, dskip_x, norm_w, e_mat)


def _sb_kernel(q_ref, k_ref, v_ref, o_ref):
    BL = SB_BLOCK
    i = pl.program_id(2)
    q = q_ref[...]
    scale = SB_HEAD_DIM ** -0.5
    row = lax.broadcasted_iota(jnp.int32, (BL, BL), 0)
    col = lax.broadcasted_iota(jnp.int32, (BL, BL), 1)
    causal = col < row
    r2 = lax.broadcasted_iota(jnp.int32, (BL, 2 * BL), 0)
    c2 = lax.broadcasted_iota(jnp.int32, (BL, 2 * BL), 1)
    tri = jnp.where((r2 > c2) | (c2 >= BL), 1.0, 0.0).astype(BF16)

    def block(kb, carry, acc, masked):
        start = pl.multiple_of(kb * BL, BL)
        k = k_ref[pl.ds(start, BL), :]
        v = v_ref[pl.ds(start, BL), :]
        zl = lax.dot_general(q, k, (((1,), (1,)), ((), ())), preferred_element_type=F32) * scale
        sp = _softplus(zl)
        log_keep = -sp
        log_beta = zl - sp
        if masked:
            log_keep = jnp.where(causal, log_keep, 0.0)
        hi, lo = _split2(log_keep)
        t2 = _dot(hi, tri) + _dot(lo, tri)
        att = jnp.exp(log_beta + t2[:, :BL] + carry)
        if masked:
            att = jnp.where(causal, att, 0.0)
        acc = acc + _dot(att.astype(BF16), v)
        return carry + t2[:, BL:], acc

    zero = jnp.zeros((BL, BL), F32)
    carry, acc = block(i, zero, zero, True)

    def cond(st):
        kb, carry, _ = st
        return jnp.logical_and(kb >= 0, jnp.max(carry) > SB_EXIT_LOG)

    def body(st):
        kb, carry, acc = st
        carry, acc = block(kb, carry, acc, False)
        return kb - 1, carry, acc

    _, _, acc = lax.while_loop(cond, body, (i - 1, carry, acc))
    o_ref[...] = acc.astype(o_ref.dtype)


def _stickbreak(proj3, bsz, seq, q_blk0):
    BL = SB_BLOCK
    nq = seq // BL
    return pl.pallas_call(
        _sb_kernel,
        grid=(bsz, SB_HEADS, nq),
        in_specs=[
            pl.BlockSpec((None, BL, SB_HEAD_DIM), lambda b, h, i: (b, i, q_blk0 + h)),
            pl.BlockSpec((None, seq, SB_HEAD_DIM), lambda b, h, i: (b, 0, q_blk0 + SB_HEADS + h)),
            pl.BlockSpec((None, seq, SB_HEAD_DIM), lambda b, h, i: (b, 0, q_blk0 + 2 * SB_HEADS + h)),
        ],
        out_specs=pl.BlockSpec((None, BL, SB_HEAD_DIM), lambda b, h, i: (b, i, h)),
        out_shape=jax.ShapeDtypeStruct((bsz, seq, SB_HEADS * SB_HEAD_DIM), BF16),
        compiler_params=_cparams(("parallel", "parallel", "arbitrary")),
        name="stickbreak",
    )(proj3, proj3, proj3)


def _merge_kernel(x_ref, ys_ref, yb_ref, g1_ref, g2_ref, w1_ref, w2_ref, wm_ref, o_ref, acc_ref):
    j = pl.program_id(1)

    @pl.when(j == 0)
    def _():
        acc_ref[...] = jnp.zeros_like(acc_ref)

    a = _dot(ys_ref[...], w1_ref[...])
    b = _dot(yb_ref[...], w2_ref[...])
    s1 = jax.nn.sigmoid(g1_ref[...].astype(F32))
    s2 = jax.nn.sigmoid(g2_ref[...].astype(F32))
    m = (s1 * a + s2 * b).astype(BF16)
    acc_ref[...] += _dot(m, wm_ref[...])

    @pl.when(j == pl.num_programs(1) - 1)
    def _():
        o_ref[...] = x_ref[...] + acc_ref[...]


def _merge(x2, y_ssd, y_sb, proj, w1, w2, wm, g_blk0, tm, tn):
    t, d = x2.shape
    tm = min(tm, t)
    nj = d // tn
    return pl.pallas_call(
        _merge_kernel,
        grid=(t // tm, nj),
        in_specs=[
            pl.BlockSpec((tm, d), lambda i, j: (i, 0)),
            pl.BlockSpec((tm, y_ssd.shape[1]), lambda i, j: (i, 0)),
            pl.BlockSpec((tm, y_sb.shape[1]), lambda i, j: (i, 0)),
            pl.BlockSpec((tm, tn), lambda i, j: (i, g_blk0 + j)),
            pl.BlockSpec((tm, tn), lambda i, j: (i, g_blk0 + nj + j)),
            pl.BlockSpec((w1.shape[0], tn), lambda i, j: (0, j)),
            pl.BlockSpec((w2.shape[0], tn), lambda i, j: (0, j)),
            pl.BlockSpec((tn, d), lambda i, j: (j, 0)),
        ],
        out_specs=pl.BlockSpec((tm, d), lambda i, j: (i, 0)),
        out_shape=jax.ShapeDtypeStruct((t, d), F32),
        scratch_shapes=[pltpu.VMEM((tm, d), F32)],
        compiler_params=_cparams(("parallel", "arbitrary")),
        name="merge",
    )(x2, y_ssd, y_sb, proj, proj, w1, w2, wm)


def _xattn_kernel(h_ref, kv_ref, nxa_ref, wq_ref, wo_ref, nmoe_ref, wrh_ref, wrl_ref,
                  h2_ref, um_ref, lg_ref):
    h1 = h_ref[...]
    un = _rms(h1, nxa_ref[...]).astype(BF16)
    q = _dot(un, wq_ref[...]).astype(BF16)
    width = XA_HEADS * XA_HEAD_DIM
    outs = []
    for hd in range(XA_HEADS):
        cs = slice(hd * XA_HEAD_DIM, (hd + 1) * XA_HEAD_DIM)
        k = kv_ref[:, cs]
        v = kv_ref[:, width + hd * XA_HEAD_DIM: width + (hd + 1) * XA_HEAD_DIM]
        sc = lax.dot_general(q[:, cs], k, (((1,), (1,)), ((), ())),
                             preferred_element_type=F32) * (XA_HEAD_DIM ** -0.5)
        sc = sc - jnp.max(sc, axis=-1, keepdims=True)
        p = jnp.exp(sc)
        p = p / jnp.sum(p, axis=-1, keepdims=True)
        outs.append(_dot(p.astype(BF16), v).astype(BF16))
    o = jnp.concatenate(outs, axis=1)
    h2 = h1 + _dot(o, wo_ref[...])
    h2_ref[...] = h2
    um = _rms(h2, nmoe_ref[...])
    um_ref[...] = um
    hi, lo = _split2(um)
    lg_ref[...] = _dot(hi, wrh_ref[...]) + _dot(lo, wrh_ref[...]) + _dot(hi, wrl_ref[...])


def _xattn(h1, kv, nxa, wq, wo, nmoe, wr_hi, wr_lo, seq, tm):
    t, d = h1.shape
    tm = min(tm, seq)
    per_b = seq // tm
    m_len = kv.shape[0] // (t // seq)
    nr = wr_hi.shape[1]
    return pl.pallas_call(
        _xattn_kernel,
        grid=(t // tm,),
        in_specs=[
            pl.BlockSpec((tm, d), lambda i: (i, 0)),
            pl.BlockSpec((m_len, kv.shape[1]), lambda i: (i // per_b, 0)),
            pl.BlockSpec((1, d), lambda i: (0, 0)),
            pl.BlockSpec(wq.shape, lambda i: (0, 0)),
            pl.BlockSpec(wo.shape, lambda i: (0, 0)),
            pl.BlockSpec((1, d), lambda i: (0, 0)),
            pl.BlockSpec((d, nr), lambda i: (0, 0)),
            pl.BlockSpec((d, nr), lambda i: (0, 0)),
        ],
        out_specs=[
            pl.BlockSpec((tm, d), lambda i: (i, 0)),
            pl.BlockSpec((tm, d), lambda i: (i, 0)),
            pl.BlockSpec((tm, nr), lambda i: (i, 0)),
        ],
        out_shape=[jax.ShapeDtypeStruct((t, d), F32), jax.ShapeDtypeStruct((t, d), F32),
                   jax.ShapeDtypeStruct((t, nr), F32)],
        compiler_params=_cparams(("parallel",)),
        name="xattn",
    )(h1, kv, nxa, wq, wo, nmoe, wr_hi, wr_lo)


def _row_copy(src_hbm, row, dst_ref, r, sem):
    return pltpu.make_async_copy(src_hbm.at[pl.ds(row, 1)], dst_ref.at[pl.ds(r, 1)], sem)


def _expert_kernel(be_ref, nv_ref, tok_ref, um_hbm, wg_ref, wu_ref, wd_ref, o_ref, xb_ref, sem):
    del be_ref
    i = pl.program_id(0)

    @pl.when(i < nv_ref[0])
    def _():
        base = i * MOE_BLOCK

        def issue(r, c):
            _row_copy(um_hbm, tok_ref[base + r], xb_ref, r, sem).start()
            return c

        lax.fori_loop(0, MOE_BLOCK, issue, 0)

        def drain(r, c):
            _row_copy(um_hbm, 0, xb_ref, r, sem).wait()
            return c

        lax.fori_loop(0, MOE_BLOCK, drain, 0)
        xb = xb_ref[...].astype(BF16)
        hid = _silu(_dot(xb, wg_ref[...])) * _dot(xb, wu_ref[...])
        o_ref[...] = _dot(hid.astype(BF16), wd_ref[...])

    @pl.when(i >= nv_ref[0])
    def _():
        o_ref[...] = jnp.zeros_like(o_ref)


def _experts(block_e, n_valid, slot_tok, um, wg, wu, wd):
    t, d = um.shape
    n_slots = slot_tok.shape[0]
    ff = wg.shape[2]
    grid_spec = pltpu.PrefetchScalarGridSpec(
        num_scalar_prefetch=3,
        grid=(n_slots // MOE_BLOCK,),
        in_specs=[
            pl.BlockSpec(memory_space=pl.ANY),
            pl.BlockSpec((None, d, ff), lambda i, be, nv, tok: (be[i], 0, 0)),
            pl.BlockSpec((None, d, ff), lambda i, be, nv, tok: (be[i], 0, 0)),
            pl.BlockSpec((None, ff, d), lambda i, be, nv, tok: (be[i], 0, 0)),
        ],
        out_specs=pl.BlockSpec((MOE_BLOCK, d), lambda i, be, nv, tok: (i, 0)),
        scratch_shapes=[pltpu.VMEM((MOE_BLOCK, d), F32), pltpu.SemaphoreType.DMA(())],
    )
    return pl.pallas_call(
        _expert_kernel,
        grid_spec=grid_spec,
        out_shape=jax.ShapeDtypeStruct((n_slots, d), F32),
        compiler_params=_cparams(("arbitrary",)),
        name="experts",
    )(block_e, n_valid, slot_tok, um, wg, wu, wd)


def _combine_kernel(slot_ref, h_ref, w_ref, nw_ref, yb_hbm, o_ref, y0_ref, y1_ref, sem):
    i = pl.program_id(0)
    tm = h_ref.shape[0]
    base = i * tm * TOP_K

    def issue(r, c):
        _row_copy(yb_hbm, slot_ref[base + TOP_K * r], y0_ref, r, sem).start()
        _row_copy(yb_hbm, slot_ref[base + TOP_K * r + 1], y1_ref, r, sem).start()
        return c

    lax.fori_loop(0, tm, issue, 0)

    def drain(r, c):
        _row_copy(yb_hbm, 0, y0_ref, r, sem).wait()
        _row_copy(yb_hbm, 0, y1_ref, r, sem).wait()
        return c

    lax.fori_loop(0, tm, drain, 0)
    w = w_ref[...]
    h3 = h_ref[...] + w[:, 0:1] * y0_ref[...] + w[:, 1:2] * y1_ref[...]
    o_ref[...] = _rms(h3, nw_ref[...])


def _combine(slot, h2, gate, nw, yb, tm):
    t, d = h2.shape
    tm = min(tm, t)
    grid_spec = pltpu.PrefetchScalarGridSpec(
        num_scalar_prefetch=1,
        grid=(t // tm,),
        in_specs=[
            pl.BlockSpec((tm, d), lambda i, s: (i, 0)),
            pl.BlockSpec((tm, TOP_K), lambda i, s: (i, 0)),
            pl.BlockSpec((1, d), lambda i, s: (0, 0)),
            pl.BlockSpec(memory_space=pl.ANY),
        ],
        out_specs=pl.BlockSpec((tm, d), lambda i, s: (i, 0)),
        scratch_shapes=[pltpu.VMEM((tm, d), F32), pltpu.VMEM((tm, d), F32), pltpu.SemaphoreType.DMA(())],
    )
    return pl.pallas_call(
        _combine_kernel,
        grid_spec=grid_spec,
        out_shape=jax.ShapeDtypeStruct((t, d), F32),
        compiler_params=_cparams(("arbitrary",)),
        name="combine",
    )(slot, h2, gate, nw, yb)


def _route(logits, t):
    g_logits = logits[:, :N_GROUPS]
    e_logits = logits[:, N_GROUPS:N_GROUPS + N_EXPERTS].reshape(t, N_GROUPS, EXPERTS_PER_GROUP)
    g_prob = jax.nn.softmax(g_logits, axis=-1)
    g_sel = jnp.argmax(g_logits, axis=-1)
    g_gate = jnp.take_along_axis(g_prob, g_sel[:, None], axis=-1)[:, 0]
    e_in_group = jnp.take_along_axis(e_logits, g_sel[:, None, None], axis=1)[:, 0]
    top_val, top_idx = lax.top_k(e_in_group, TOP_K)
    gate = jax.nn.softmax(top_val, axis=-1) * g_gate[:, None]
    flat_e = (g_sel[:, None] * EXPERTS_PER_GROUP + top_idx).astype(jnp.int32).reshape(-1)
    onehot = (flat_e[:, None] == jnp.arange(N_EXPERTS, dtype=jnp.int32)[None, :]).astype(jnp.int32)
    csum = jnp.cumsum(onehot, axis=0)
    rank = jnp.take_along_axis(csum, flat_e[:, None], axis=1)[:, 0] - 1
    counts = csum[-1]
    padded = (counts + MOE_BLOCK - 1) // MOE_BLOCK * MOE_BLOCK
    pad_end = jnp.cumsum(padded)
    pad_start = pad_end - padded
    slot = (pad_start[flat_e] + rank).astype(jnp.int32)
    n_slots = t * TOP_K + N_EXPERTS * MOE_BLOCK
    n_blocks = n_slots // MOE_BLOCK
    flat_t = jnp.arange(t * TOP_K, dtype=jnp.int32) // TOP_K
    slot_tok = jnp.zeros((n_slots,), jnp.int32).at[slot].set(flat_t)
    block_e = jnp.minimum(
        jnp.searchsorted(pad_end, jnp.arange(n_blocks, dtype=jnp.int32) * MOE_BLOCK, side='right'),
        N_EXPERTS - 1).astype(jnp.int32)
    n_valid = (pad_end[-1:] // MOE_BLOCK).astype(jnp.int32)
    return gate, slot, slot_tok, block_e, n_valid


def _pad_lanes(v, n=LANES):
    return jnp.pad(v, ((0, 0), (0, n - v.shape[1])))


def kernel(x, mem, norm_mix_w, w_in, conv_w, conv_b, dt_bias, a_log, d_skip, ssd_norm_w, w_ssd_branch, w_sb_branch, w_mix_out, norm_xa_w, norm_mem_w, w_xq, w_xk, w_xv, w_xo, norm_moe_w, w_router_group, w_router_expert, w_expert_gate, w_expert_up, w_expert_down, norm_final_w):
    bsz, seq, d = x.shape
    depth = w_in.shape[0]
    t = bsz * seq
    inner = ssd_norm_w.shape[1]
    heads = dt_bias.shape[1]
    bc = SSD_GROUPS * SSD_STATE
    conv_dim = inner + 2 * bc
    sb_width = SB_HEADS * SB_HEAD_DIM
    col_dt = inner + conv_dim
    n_main = col_dt + 3 * sb_width + 2 * d
    q_col0 = col_dt
    g_col0 = col_dt + 3 * sb_width
    tn_merge = 512

    assert depth == 1, "single-layer configuration: the final RMSNorm is fused into the last MoE combine"
    h = x.reshape(t, d)
    for l in range(depth):
        w_l = w_in[l]
        w_main = jnp.concatenate([w_l[:, :col_dt], w_l[:, col_dt + heads:]], axis=1).astype(BF16)
        w_dt = _pad_lanes(w_l[:, col_dt:col_dt + heads]).astype(BF16)
        e_mat = (jnp.arange(inner, dtype=jnp.int32)[None, :] // SSD_HEAD_DIM
                 == jnp.arange(LANES, dtype=jnp.int32)[:, None]).astype(BF16)
        dskip_x = jnp.repeat(d_skip[l], SSD_HEAD_DIM)[None, :]

        proj, dt_raw = _norm_matmul(h, norm_mix_w[l][None, :], w_main, w_dt, tm=1024, tn=512)
        y_ssd = _ssd(proj, dt_raw, conv_w[l], conv_b[l][None, :], _pad_lanes(dt_bias[l][None, :]),
                     _pad_lanes(a_log[l][None, :]), dskip_x, ssd_norm_w[l][None, :], e_mat,
                     bsz, seq, inner, bc)
        y_sb = _stickbreak(proj.reshape(bsz, seq, n_main), bsz, seq, q_col0 // SB_HEAD_DIM)
        h1 = _merge(h, y_ssd, y_sb.reshape(t, sb_width), proj,
                    w_ssd_branch[l].astype(BF16), w_sb_branch[l].astype(BF16), w_mix_out[l].astype(BF16),
                    g_col0 // tn_merge, tm=512, tn=tn_merge)

        m_len = mem.shape[1]
        w_kv = jnp.concatenate([w_xk[l], w_xv[l]], axis=1).astype(BF16)
        kv, _ = _norm_matmul(mem.reshape(bsz * m_len, d), norm_mem_w[l][None, :], w_kv,
                             jnp.zeros((d, LANES), BF16), tm=bsz * m_len, tn=512)
        w_r = _pad_lanes(jnp.concatenate([w_router_group[l], w_router_expert[l]], axis=1))
        wr_hi = w_r.astype(BF16)
        wr_lo = (w_r - wr_hi.astype(F32)).astype(BF16)
        h2, um, logits = _xattn(h1, kv, norm_xa_w[l][None, :], w_xq[l].astype(BF16), w_xo[l].astype(BF16),
                                norm_moe_w[l][None, :], wr_hi, wr_lo, seq, tm=512)

        gate, slot, slot_tok, block_e, n_valid = _route(logits, t)
        yb = _experts(block_e, n_valid, slot_tok, um, w_expert_gate[l].astype(BF16),
                      w_expert_up[l].astype(BF16), w_expert_down[l].astype(BF16))
        h = _combine(slot, h2, gate, norm_final_w[None, :], yb, tm=256)
    return h.reshape(bsz, seq, d)
```

```python
import functools

import jax
import jax.numpy as jnp
from jax import lax
from jax.experimental import pallas as pl
from jax.experimental.pallas import tpu as pltpu

F32 = jnp.float32
BF16 = jnp.bfloat16
EPS = 1e-6

SSD_HEAD_DIM = 64
SSD_GROUPS = 4
SSD_STATE = 128
SSD_CONV = 4
SSD_CHUNK = 128
SB_HEADS = 4
SB_HEAD_DIM = 128
SB_BLOCK = 128
SB_SUB = 4
GATHER_UNROLL = 8
XA_HEADS = 4
XA_HEAD_DIM = 128
N_GROUPS = 4
EXPERTS_PER_GROUP = 8
N_EXPERTS = N_GROUPS * EXPERTS_PER_GROUP
TOP_K = 2
MOE_BLOCK = 256

LANES = 128
SUBLANES = 8
VMEM_LIMIT = 56 * 1024 * 1024

SB_EXIT_LOG = -104.0
NEG_BIG = -1e30


def _cparams(sem):
    return pltpu.CompilerParams(dimension_semantics=sem, vmem_limit_bytes=VMEM_LIMIT)


def _rms(x, w):
    var = jnp.mean(x * x, axis=-1, keepdims=True)
    return x * lax.rsqrt(var + EPS) * w


def _split2(v):
    hi = v.astype(BF16)
    lo = (v - hi.astype(F32)).astype(BF16)
    return hi, lo


def _dot(a, b):
    return jnp.dot(a, b, preferred_element_type=F32)


def _silu(x):
    return x / (1.0 + jnp.exp(-x))


def _softplus(x):
    return jnp.maximum(x, 0.0) + jnp.log(1.0 + jnp.exp(-jnp.abs(x)))


def _norm_matmul_kernel(x_ref, nw_ref, w_ref, ws_ref, o_ref, os_ref, u_ref):
    @pl.when(pl.program_id(1) == 0)
    def _():
        u = _rms(x_ref[...], nw_ref[...]).astype(BF16)
        u_ref[...] = u
        os_ref[...] = _dot(u, ws_ref[...])

    o_ref[...] = _dot(u_ref[...], w_ref[...]).astype(o_ref.dtype)


def _norm_matmul(x, nw, w, w_side, tm, tn):
    m, k = x.shape
    n = w.shape[1]
    ns = w_side.shape[1]
    tm = min(tm, m)
    return pl.pallas_call(
        _norm_matmul_kernel,
        grid=(m // tm, n // tn),
        in_specs=[
            pl.BlockSpec((tm, k), lambda i, j: (i, 0)),
            pl.BlockSpec((1, k), lambda i, j: (0, 0)),
            pl.BlockSpec((k, tn), lambda i, j: (0, j)),
            pl.BlockSpec((k, ns), lambda i, j: (0, 0)),
        ],
        out_specs=[
            pl.BlockSpec((tm, tn), lambda i, j: (i, j)),
            pl.BlockSpec((tm, ns), lambda i, j: (i, 0)),
        ],
        out_shape=[jax.ShapeDtypeStruct((m, n), BF16), jax.ShapeDtypeStruct((m, ns), F32)],
        scratch_shapes=[pltpu.VMEM((tm, k), BF16)],
        compiler_params=_cparams(("parallel", "arbitrary")),
        name="norm_matmul",
    )(x, nw, w, w_side)


def _ssd_kernel(z_ref, xs_ref, b_ref, c_ref, dtr_ref, cw_ref, cb_ref, dtb_ref, alog_ref, dsk_ref,
                nw_ref, e_ref, o_ref,
                tail_ref, state_ref, xact_ref, bact_ref, cact_ref, dtx_ref, eax_ref, wsx_ref, y_ref):
    L = SSD_CHUNK
    inner = xs_ref.shape[1]
    bc = b_ref.shape[1]
    n_pairs = inner // LANES

    @pl.when(pl.program_id(1) == 0)
    def _():
        tail_ref[...] = jnp.zeros_like(tail_ref)
        state_ref[...] = jnp.zeros_like(state_ref)

    row8 = lax.broadcasted_iota(jnp.int32, (SUBLANES, 2 * LANES), 0)

    def conv_seg(src_ref, col0, width, dst_ref):
        for j in range(0, width, 2 * LANES):
            cols = slice(col0 + j, col0 + j + 2 * LANES)
            xin = src_ref[:, j:j + 2 * LANES].astype(F32)
            t8 = tail_ref[:, cols]
            acc = xin * cw_ref[SSD_CONV - 1:SSD_CONV, cols] + cb_ref[:, cols]
            for s in range(1, SSD_CONV):
                r = pltpu.roll(xin, s, 0)
                top = jnp.where(row8 < s, pltpu.roll(t8, s, 0), r[:SUBLANES])
                r = jnp.concatenate([top, r[SUBLANES:]], axis=0)
                acc = acc + r * cw_ref[SSD_CONV - 1 - s:SSD_CONV - s, cols]
            tail_ref[:, cols] = xin[L - SUBLANES:]
            dst_ref[:, j:j + 2 * LANES] = _silu(acc).astype(dst_ref.dtype)

    conv_seg(xs_ref, 0, inner, xact_ref)
    conv_seg(b_ref, inner, bc, bact_ref)
    conv_seg(c_ref, inner + bc, bc, cact_ref)

    dt = _softplus(dtr_ref[...] + dtb_ref[...])
    da = dt * (-jnp.exp(alog_ref[...]))
    rowl = lax.broadcasted_iota(jnp.int32, (L, L), 0)
    coll = lax.broadcasted_iota(jnp.int32, (L, L), 1)
    lower = rowl >= coll
    tri = jnp.where(lower, 1.0, 0.0).astype(BF16)
    d1 = da.astype(BF16)
    r1 = da - d1.astype(F32)
    d2 = r1.astype(BF16)
    d3 = (r1 - d2.astype(F32)).astype(BF16)
    a_cum = _dot(tri, d1) + _dot(tri, d2) + _dot(tri, d3)
    a_cum_t = a_cum.T
    a_last = a_cum[L - 1:L, :]
    e_mat = e_ref[...]

    def expand(v):
        hi, lo = _split2(v)
        return _dot(hi, e_mat) + _dot(lo, e_mat)

    dtx_ref[...] = expand(dt)
    eax_ref[...] = expand(jnp.exp(a_cum))
    wsx_ref[...] = expand(jnp.exp(a_last - a_cum))
    elx = expand(jnp.broadcast_to(jnp.exp(a_last), (SUBLANES, LANES)))[0:1]

    lane = lax.broadcasted_iota(jnp.int32, (L, LANES), 1)
    pairs_per_group = n_pairs // SSD_GROUPS
    for g in range(SSD_GROUPS):
        gcols = slice(g * SSD_STATE, (g + 1) * SSD_STATE)
        bg_t = bact_ref[:, gcols].T.astype(BF16)
        cg = cact_ref[:, gcols]
        cb = _dot(cg, bg_t)
        for pp in range(pairs_per_group):
            p = g * pairs_per_group + pp
            pc = slice(p * LANES, (p + 1) * LANES)
            ms = []
            for hh in (2 * p, 2 * p + 1):
                seg = a_cum[:, hh:hh + 1] - a_cum_t[hh:hh + 1, :]
                dec = jnp.exp(jnp.where(lower, seg, NEG_BIG))
                ms.append((cb * dec).astype(BF16))
            lhs = jnp.concatenate(ms, axis=1)
            xs_p = xact_ref[:, pc]
            xdt = xs_p * dtx_ref[:, pc]
            rhs = jnp.concatenate([jnp.where(lane < SSD_HEAD_DIM, xdt, 0.0).astype(BF16),
                                   jnp.where(lane >= SSD_HEAD_DIM, xdt, 0.0).astype(BF16)], axis=0)
            st = state_ref[p]
            y = _dot(lhs, rhs)
            y = y + _dot(cg, st.astype(BF16)) * eax_ref[:, pc]
            y = y + xs_p * dsk_ref[:, pc]
            y_ref[:, pc] = y
            xw = (xdt * wsx_ref[:, pc]).astype(BF16)
            state_ref[p] = st * elx[:, pc] + _dot(bg_t, xw)

    yg = y_ref[...] * _silu(z_ref[...].astype(F32))
    o_ref[...] = _rms(yg, nw_ref[...]).astype(o_ref.dtype)


def _ssd(proj, dt_raw, conv_w, conv_b, dt_bias_p, a_log_p, dskip_x, norm_w, e_mat, bsz, seq, inner, bc):
    L = SSD_CHUNK
    nc = seq // L
    conv_dim = inner + 2 * bc
    n_pairs = inner // LANES
    ib = inner // bc

    def rows(b, c):
        return b * nc + c

    return pl.pallas_call(
        _ssd_kernel,
        grid=(bsz, nc),
        in_specs=[
            pl.BlockSpec((L, inner), lambda b, c: (rows(b, c), 0)),
            pl.BlockSpec((L, inner), lambda b, c: (rows(b, c), 1)),
            pl.BlockSpec((L, bc), lambda b, c: (rows(b, c), 2 * ib)),
            pl.BlockSpec((L, bc), lambda b, c: (rows(b, c), 2 * ib + 1)),
            pl.BlockSpec((L, LANES), lambda b, c: (rows(b, c), 0)),
            pl.BlockSpec((SSD_CONV, conv_dim), lambda b, c: (0, 0)),
            pl.BlockSpec((1, conv_dim), lambda b, c: (0, 0)),
            pl.BlockSpec((1, LANES), lambda b, c: (0, 0)),
            pl.BlockSpec((1, LANES), lambda b, c: (0, 0)),
            pl.BlockSpec((1, inner), lambda b, c: (0, 0)),
            pl.BlockSpec((1, inner), lambda b, c: (0, 0)),
            pl.BlockSpec((LANES, inner), lambda b, c: (0, 0)),
        ],
        out_specs=pl.BlockSpec((L, inner), lambda b, c: (rows(b, c), 0)),
        out_shape=jax.ShapeDtypeStruct((bsz * seq, inner), BF16),
        scratch_shapes=[
            pltpu.VMEM((SUBLANES, conv_dim), F32),
            pltpu.VMEM((n_pairs, SSD_STATE, LANES), F32),
            pltpu.VMEM((L, inner), F32),
            pltpu.VMEM((L, bc), F32),
            pltpu.VMEM((L, bc), BF16),
            pltpu.VMEM((L, inner), F32),
            pltpu.VMEM((L, inner), F32),
            pltpu.VMEM((L, inner), F32),
            pltpu.VMEM((L, inner), F32),
        ],
        compiler_params=_cparams(("parallel", "arbitrary")),
        name="ssd",
    )(proj, proj, proj, proj, dt_raw, conv_w, conv_b, dt_bias_p, a_log_p, dskip_x, norm_w, e_mat)


def _sb_kernel(q_ref, k_ref, v_ref, o_ref, acc_ref, carry_ref):
    BL = SB_BLOCK
    i = pl.program_id(2)
    scale = SB_HEAD_DIM ** -0.5
    row = lax.broadcasted_iota(jnp.int32, (BL, BL), 0)
    col = lax.broadcasted_iota(jnp.int32, (BL, BL), 1)
    causal = col < row
    r2 = lax.broadcasted_iota(jnp.int32, (BL, 2 * BL), 0)
    c2 = lax.broadcasted_iota(jnp.int32, (BL, 2 * BL), 1)
    tri = jnp.where((r2 > c2) | (c2 >= BL), 1.0, 0.0).astype(BF16)

    def sweep(s, first):
        subs = range(SB_SUB)
        kbs = [i * SB_SUB + j - s for j in subs]
        starts = [pl.multiple_of(jnp.maximum(kb, 0) * BL, BL) for kb in kbs]
        zls = [lax.dot_general(q_ref[j * BL:(j + 1) * BL, :], k_ref[pl.ds(starts[j], BL), :],
                               (((1,), (1,)), ((), ())), preferred_element_type=F32) * scale for j in subs]
        sps = [_softplus(zl) for zl in zls]
        log_betas = [zl - sp for zl, sp in zip(zls, sps)]
        log_keeps = [jnp.where(causal, -sp, 0.0) if first else -sp for sp in sps]
        splits = [_split2(lk) for lk in log_keeps]
        t2s = [_dot(hi, tri) + _dot(lo, tri) for hi, lo in splits]
        if first:
            atts = [jnp.where(causal, jnp.exp(lb + t2[:, :BL]), 0.0) for lb, t2 in zip(log_betas, t2s)]
            carries = [t2[:, BL:] for t2 in t2s]
        else:
            olds = [carry_ref[j] for j in subs]
            atts = [jnp.where(kb >= 0, jnp.exp(lb + t2[:, :BL] + old), 0.0)
                    for kb, lb, t2, old in zip(kbs, log_betas, t2s, olds)]
            carries = [old + jnp.where(kb >= 0, t2[:, BL:], 0.0) for kb, t2, old in zip(kbs, t2s, olds)]
        pvs = [_dot(atts[j].astype(BF16), v_ref[pl.ds(starts[j], BL), :]) for j in subs]
        worst = None
        for j in subs:
            if first:
                acc_ref[j] = pvs[j]
            else:
                acc_ref[j] += pvs[j]
            carry_ref[j] = carries[j]
            live = jnp.where(kbs[j] >= 1, carries[j], NEG_BIG)
            worst = live if worst is None else jnp.maximum(worst, live)
        return jnp.max(worst) > SB_EXIT_LOG

    sweep(0, True)
    go = sweep(1, False)

    def body(st):
        s, _ = st
        return s + 1, sweep(s, False)

    lax.while_loop(lambda st: st[1], body, (jnp.int32(2), go))
    for j in range(SB_SUB):
        o_ref[j * BL:(j + 1) * BL, :] = acc_ref[j].astype(o_ref.dtype)


def _stickbreak(proj3, bsz, seq, q_blk0):
    tq = SB_SUB * SB_BLOCK
    return pl.pallas_call(
        _sb_kernel,
        grid=(bsz, SB_HEADS, seq // tq),
        in_specs=[
            pl.BlockSpec((None, tq, SB_HEAD_DIM), lambda b, h, i: (b, i, q_blk0 + h)),
            pl.BlockSpec((None, seq, SB_HEAD_DIM), lambda b, h, i: (b, 0, q_blk0 + SB_HEADS + h)),
            pl.BlockSpec((None, seq, SB_HEAD_DIM), lambda b, h, i: (b, 0, q_blk0 + 2 * SB_HEADS + h)),
        ],
        out_specs=pl.BlockSpec((None, tq, SB_HEAD_DIM), lambda b, h, i: (b, i, h)),
        out_shape=jax.ShapeDtypeStruct((bsz, seq, SB_HEADS * SB_HEAD_DIM), BF16),
        scratch_shapes=[pltpu.VMEM((SB_SUB, SB_BLOCK, SB_HEAD_DIM), F32),
                        pltpu.VMEM((SB_SUB, SB_BLOCK, SB_BLOCK), F32)],
        compiler_params=_cparams(("parallel", "parallel", "arbitrary")),
        name="stickbreak",
    )(proj3, proj3, proj3)


def _merge_kernel(x_ref, ys_ref, yb_ref, g1_ref, g2_ref, w1_ref, w2_ref, wm_ref, o_ref, acc_ref):
    j = pl.program_id(1)

    @pl.when(j == 0)
    def _():
        acc_ref[...] = jnp.zeros_like(acc_ref)

    a = _dot(ys_ref[...], w1_ref[...])
    b = _dot(yb_ref[...], w2_ref[...])
    s1 = jax.nn.sigmoid(g1_ref[...].astype(F32))
    s2 = jax.nn.sigmoid(g2_ref[...].astype(F32))
    m = (s1 * a + s2 * b).astype(BF16)
    acc_ref[...] += _dot(m, wm_ref[...])

    @pl.when(j == pl.num_programs(1) - 1)
    def _():
        o_ref[...] = x_ref[...] + acc_ref[...]


def _merge(x2, y_ssd, y_sb, proj, w1, w2, wm, g_blk0, tm, tn):
    t, d = x2.shape
    tm = min(tm, t)
    nj = d // tn
    return pl.pallas_call(
        _merge_kernel,
        grid=(t // tm, nj),
        in_specs=[
            pl.BlockSpec((tm, d), lambda i, j: (i, 0)),
            pl.BlockSpec((tm, y_ssd.shape[1]), lambda i, j: (i, 0)),
            pl.BlockSpec((tm, y_sb.shape[1]), lambda i, j: (i, 0)),
            pl.BlockSpec((tm, tn), lambda i, j: (i, g_blk0 + j)),
            pl.BlockSpec((tm, tn), lambda i, j: (i, g_blk0 + nj + j)),
            pl.BlockSpec((w1.shape[0], tn), lambda i, j: (0, j)),
            pl.BlockSpec((w2.shape[0], tn), lambda i, j: (0, j)),
            pl.BlockSpec((tn, d), lambda i, j: (j, 0)),
        ],
        out_specs=pl.BlockSpec((tm, d), lambda i, j: (i, 0)),
        out_shape=jax.ShapeDtypeStruct((t, d), F32),
        scratch_shapes=[pltpu.VMEM((tm, d), F32)],
        compiler_params=_cparams(("parallel", "arbitrary")),
        name="merge",
    )(x2, y_ssd, y_sb, proj, proj, w1, w2, wm)


def _xattn_kernel(h_ref, kv_ref, nxa_ref, wq_ref, wo_ref, nmoe_ref, wrh_ref, wrl_ref,
                  h2_ref, um_ref, lg_ref):
    h1 = h_ref[...]
    un = _rms(h1, nxa_ref[...]).astype(BF16)
    q = _dot(un, wq_ref[...]).astype(BF16)
    width = XA_HEADS * XA_HEAD_DIM
    outs = []
    for hd in range(XA_HEADS):
        cs = slice(hd * XA_HEAD_DIM, (hd + 1) * XA_HEAD_DIM)
        k = kv_ref[:, cs]
        v = kv_ref[:, width + hd * XA_HEAD_DIM: width + (hd + 1) * XA_HEAD_DIM]
        sc = lax.dot_general(q[:, cs], k, (((1,), (1,)), ((), ())),
                             preferred_element_type=F32) * (XA_HEAD_DIM ** -0.5)
        sc = sc - jnp.max(sc, axis=-1, keepdims=True)
        p = jnp.exp(sc)
        p = p / jnp.sum(p, axis=-1, keepdims=True)
        outs.append(_dot(p.astype(BF16), v).astype(BF16))
    o = jnp.concatenate(outs, axis=1)
    h2 = h1 + _dot(o, wo_ref[...])
    h2_ref[...] = h2
    um = _rms(h2, nmoe_ref[...])
    um_ref[...] = um
    hi, lo = _split2(um)
    lg_ref[...] = _dot(hi, wrh_ref[...]) + _dot(lo, wrh_ref[...]) + _dot(hi, wrl_ref[...])


def _xattn(h1, kv, nxa, wq, wo, nmoe, wr_hi, wr_lo, seq, tm):
    t, d = h1.shape
    tm = min(tm, seq)
    per_b = seq // tm
    m_len = kv.shape[0] // (t // seq)
    nr = wr_hi.shape[1]
    return pl.pallas_call(
        _xattn_kernel,
        grid=(t // tm,),
        in_specs=[
            pl.BlockSpec((tm, d), lambda i: (i, 0)),
            pl.BlockSpec((m_len, kv.shape[1]), lambda i: (i // per_b, 0)),
            pl.BlockSpec((1, d), lambda i: (0, 0)),
            pl.BlockSpec(wq.shape, lambda i: (0, 0)),
            pl.BlockSpec(wo.shape, lambda i: (0, 0)),
            pl.BlockSpec((1, d), lambda i: (0, 0)),
            pl.BlockSpec((d, nr), lambda i: (0, 0)),
            pl.BlockSpec((d, nr), lambda i: (0, 0)),
        ],
        out_specs=[
            pl.BlockSpec((tm, d), lambda i: (i, 0)),
            pl.BlockSpec((tm, d), lambda i: (i, 0)),
            pl.BlockSpec((tm, nr), lambda i: (i, 0)),
        ],
        out_shape=[jax.ShapeDtypeStruct((t, d), F32), jax.ShapeDtypeStruct((t, d), F32),
                   jax.ShapeDtypeStruct((t, nr), F32)],
        compiler_params=_cparams(("parallel",)),
        name="xattn",
    )(h1, kv, nxa, wq, wo, nmoe, wr_hi, wr_lo)


def _row_copy(src_hbm, row, dst_ref, r, sem):
    return pltpu.make_async_copy(src_hbm.at[pl.ds(row, 1)], dst_ref.at[pl.ds(r, 1)], sem)


def _expert_kernel(be_ref, nv_ref, tok_ref, um_hbm, wg_ref, wu_ref, wd_ref, o_ref, xb_ref, sem):
    del be_ref
    i = pl.program_id(0)
    nv = nv_ref[0]

    def gather(blk):
        buf = blk % 2
        base = blk * MOE_BLOCK

        def issue(r, c):
            _row_copy(um_hbm, tok_ref[base + r], xb_ref.at[buf], r, sem.at[buf]).start()
            return c

        lax.fori_loop(0, MOE_BLOCK, issue, 0, unroll=GATHER_UNROLL)

    @pl.when(jnp.logical_and(i == 0, nv > 0))
    def _():
        gather(i)

    @pl.when(i + 1 < nv)
    def _():
        gather(i + 1)

    @pl.when(i < nv)
    def _():
        buf = i % 2
        pltpu.make_async_copy(um_hbm.at[pl.ds(0, MOE_BLOCK)], xb_ref.at[buf], sem.at[buf]).wait()
        xb = xb_ref[buf].astype(BF16)
        hid = _silu(_dot(xb, wg_ref[...])) * _dot(xb, wu_ref[...])
        o_ref[...] = _dot(hid.astype(BF16), wd_ref[...])

    @pl.when(i >= nv)
    def _():
        o_ref[...] = jnp.zeros_like(o_ref)


def _experts(block_e, n_valid, slot_tok, um, wg, wu, wd):
    t, d = um.shape
    n_slots = slot_tok.shape[0]
    ff = wg.shape[2]
    grid_spec = pltpu.PrefetchScalarGridSpec(
        num_scalar_prefetch=3,
        grid=(n_slots // MOE_BLOCK,),
        in_specs=[
            pl.BlockSpec(memory_space=pl.ANY),
            pl.BlockSpec((None, d, ff), lambda i, be, nv, tok: (be[i], 0, 0)),
            pl.BlockSpec((None, d, ff), lambda i, be, nv, tok: (be[i], 0, 0)),
            pl.BlockSpec((None, ff, d), lambda i, be, nv, tok: (be[i], 0, 0)),
        ],
        out_specs=pl.BlockSpec((MOE_BLOCK, d), lambda i, be, nv, tok: (i, 0)),
        scratch_shapes=[pltpu.VMEM((2, MOE_BLOCK, d), F32), pltpu.SemaphoreType.DMA((2,))],
    )
    return pl.pallas_call(
        _expert_kernel,
        grid_spec=grid_spec,
        out_shape=jax.ShapeDtypeStruct((n_slots, d), F32),
        compiler_params=_cparams(("arbitrary",)),
        name="experts",
    )(block_e, n_valid, slot_tok, um, wg, wu, wd)


def _combine_kernel(slot_ref, h_ref, w_ref, nw_ref, yb_hbm, o_ref, y0_ref, y1_ref, sem):
    i = pl.program_id(0)
    tm = h_ref.shape[0]

    def gather(blk):
        buf = blk % 2
        base = blk * tm * TOP_K

        def issue(r, c):
            _row_copy(yb_hbm, slot_ref[base + TOP_K * r], y0_ref.at[buf], r, sem.at[buf]).start()
            _row_copy(yb_hbm, slot_ref[base + TOP_K * r + 1], y1_ref.at[buf], r, sem.at[buf]).start()
            return c

        lax.fori_loop(0, tm, issue, 0, unroll=GATHER_UNROLL)

    @pl.when(i == 0)
    def _():
        gather(i)

    @pl.when(i + 1 < pl.num_programs(0))
    def _():
        gather(i + 1)

    buf = i % 2
    pltpu.make_async_copy(yb_hbm.at[pl.ds(0, tm)], y0_ref.at[buf], sem.at[buf]).wait()
    pltpu.make_async_copy(yb_hbm.at[pl.ds(0, tm)], y1_ref.at[buf], sem.at[buf]).wait()
    w = w_ref[...]
    h3 = h_ref[...] + w[:, 0:1] * y0_ref[buf] + w[:, 1:2] * y1_ref[buf]
    o_ref[...] = _rms(h3, nw_ref[...])


def _combine(slot, h2, gate, nw, yb, tm):
    t, d = h2.shape
    tm = min(tm, t)
    grid_spec = pltpu.PrefetchScalarGridSpec(
        num_scalar_prefetch=1,
        grid=(t // tm,),
        in_specs=[
            pl.BlockSpec((tm, d), lambda i, s: (i, 0)),
            pl.BlockSpec((tm, TOP_K), lambda i, s: (i, 0)),
            pl.BlockSpec((1, d), lambda i, s: (0, 0)),
            pl.BlockSpec(memory_space=pl.ANY),
        ],
        out_specs=pl.BlockSpec((tm, d), lambda i, s: (i, 0)),
        scratch_shapes=[pltpu.VMEM((2, tm, d), F32), pltpu.VMEM((2, tm, d), F32),
                        pltpu.SemaphoreType.DMA((2,))],
    )
    return pl.pallas_call(
        _combine_kernel,
        grid_spec=grid_spec,
        out_shape=jax.ShapeDtypeStruct((t, d), F32),
        compiler_params=_cparams(("arbitrary",)),
        name="combine",
    )(slot, h2, gate, nw, yb)


def _route(logits, t):
    g_logits = logits[:, :N_GROUPS]
    e_logits = logits[:, N_GROUPS:N_GROUPS + N_EXPERTS].reshape(t, N_GROUPS, EXPERTS_PER_GROUP)
    g_prob = jax.nn.softmax(g_logits, axis=-1)
    g_sel = jnp.argmax(g_logits, axis=-1)
    g_gate = jnp.take_along_axis(g_prob, g_sel[:, None], axis=-1)[:, 0]
    e_in_group = jnp.take_along_axis(e_logits, g_sel[:, None, None], axis=1)[:, 0]
    top_val, top_idx = lax.top_k(e_in_group, TOP_K)
    gate = jax.nn.softmax(top_val, axis=-1) * g_gate[:, None]
    flat_e = (g_sel[:, None] * EXPERTS_PER_GROUP + top_idx).astype(jnp.int32).reshape(-1)
    onehot = (flat_e[:, None] == jnp.arange(N_EXPERTS, dtype=jnp.int32)[None, :]).astype(jnp.int32)
    csum = jnp.cumsum(onehot, axis=0)
    rank = jnp.take_along_axis(csum, flat_e[:, None], axis=1)[:, 0] - 1
    counts = csum[-1]
    padded = (counts + MOE_BLOCK - 1) // MOE_BLOCK * MOE_BLOCK
    pad_end = jnp.cumsum(padded)
    pad_start = pad_end - padded
    slot = (pad_start[flat_e] + rank).astype(jnp.int32)
    n_slots = t * TOP_K + N_EXPERTS * MOE_BLOCK
    n_blocks = n_slots // MOE_BLOCK
    flat_t = jnp.arange(t * TOP_K, dtype=jnp.int32) // TOP_K
    slot_tok = jnp.zeros((n_slots,), jnp.int32).at[slot].set(flat_t)
    block_e = jnp.minimum(
        jnp.searchsorted(pad_end, jnp.arange(n_blocks, dtype=jnp.int32) * MOE_BLOCK, side='right'),
        N_EXPERTS - 1).astype(jnp.int32)
    n_valid = (pad_end[-1:] // MOE_BLOCK).astype(jnp.int32)
    return gate, slot, slot_tok, block_e, n_valid


def _pad_lanes(v, n=LANES):
    return jnp.pad(v, ((0, 0), (0, n - v.shape[1])))


def kernel(x, mem, norm_mix_w, w_in, conv_w, conv_b, dt_bias, a_log, d_skip, ssd_norm_w, w_ssd_branch, w_sb_branch, w_mix_out, norm_xa_w, norm_mem_w, w_xq, w_xk, w_xv, w_xo, norm_moe_w, w_router_group, w_router_expert, w_expert_gate, w_expert_up, w_expert_down, norm_final_w):
    bsz, seq, d = x.shape
    depth = w_in.shape[0]
    t = bsz * seq
    inner = ssd_norm_w.shape[1]
    heads = dt_bias.shape[1]
    bc = SSD_GROUPS * SSD_STATE
    conv_dim = inner + 2 * bc
    sb_width = SB_HEADS * SB_HEAD_DIM
    col_dt = inner + conv_dim
    n_main = col_dt + 3 * sb_width + 2 * d
    q_col0 = col_dt
    g_col0 = col_dt + 3 * sb_width
    tn_merge = 512

    assert depth == 1, "single-layer configuration: the final RMSNorm is fused into the last MoE combine"
    h = x.reshape(t, d)
    for l in range(depth):
        w_l = w_in[l]
        w_main = jnp.concatenate([w_l[:, :col_dt], w_l[:, col_dt + heads:]], axis=1).astype(BF16)
        w_dt = _pad_lanes(w_l[:, col_dt:col_dt + heads]).astype(BF16)
        e_mat = (jnp.arange(inner, dtype=jnp.int32)[None, :] // SSD_HEAD_DIM
                 == jnp.arange(LANES, dtype=jnp.int32)[:, None]).astype(BF16)
        dskip_x = jnp.repeat(d_skip[l], SSD_HEAD_DIM)[None, :]

        proj, dt_raw = _norm_matmul(h, norm_mix_w[l][None, :], w_main, w_dt, tm=1024, tn=512)
        y_ssd = _ssd(proj, dt_raw, conv_w[l], conv_b[l][None, :], _pad_lanes(dt_bias[l][None, :]),
                     _pad_lanes(a_log[l][None, :]), dskip_x, ssd_norm_w[l][None, :], e_mat,
                     bsz, seq, inner, bc)
        y_sb = _stickbreak(proj.reshape(bsz, seq, n_main), bsz, seq, q_col0 // SB_HEAD_DIM)
        h1 = _merge(h, y_ssd, y_sb.reshape(t, sb_width), proj,
                    w_ssd_branch[l].astype(BF16), w_sb_branch[l].astype(BF16), w_mix_out[l].astype(BF16),
                    g_col0 // tn_merge, tm=512, tn=tn_merge)

        m_len = mem.shape[1]
        w_kv = jnp.concatenate([w_xk[l], w_xv[l]], axis=1).astype(BF16)
        kv, _ = _norm_matmul(mem.reshape(bsz * m_len, d), norm_mem_w[l][None, :], w_kv,
                             jnp.zeros((d, LANES), BF16), tm=bsz * m_len, tn=512)
        w_r = _pad_lanes(jnp.concatenate([w_router_group[l], w_router_expert[l]], axis=1))
        wr_hi = w_r.astype(BF16)
        wr_lo = (w_r - wr_hi.astype(F32)).astype(BF16)
        h2, um, logits = _xattn(h1, kv, norm_xa_w[l][None, :], w_xq[l].astype(BF16), w_xo[l].astype(BF16),
                                norm_moe_w[l][None, :], wr_hi, wr_lo, seq, tm=512)

        gate, slot, slot_tok, block_e, n_valid = _route(logits, t)
        yb = _experts(block_e, n_valid, slot_tok, um, w_expert_gate[l].astype(BF16),
                      w_expert_up[l].astype(BF16), w_expert_down[l].astype(BF16))
        h = _combine(slot, h2, gate, norm_final_w[None, :], yb, tm=256)
    return h.reshape(bsz, seq, d)
```

```python
import functools

import jax
import jax.numpy as jnp
from jax import lax
from jax.experimental import pallas as pl
from jax.experimental.pallas import tpu as pltpu

F32 = jnp.float32
BF16 = jnp.bfloat16
EPS = 1e-6

SSD_HEAD_DIM = 64
SSD_GROUPS = 4
SSD_STATE = 128
SSD_CONV = 4
SSD_CHUNK = 128
SB_HEADS = 4
SB_HEAD_DIM = 128
SB_BLOCK = 128
SB_SUB = 4
XA_HEADS = 4
XA_HEAD_DIM = 128
N_GROUPS = 4
EXPERTS_PER_GROUP = 8
N_EXPERTS = N_GROUPS * EXPERTS_PER_GROUP
TOP_K = 2
MOE_BLOCK = 256

LANES = 128
SUBLANES = 8
VMEM_LIMIT = 56 * 1024 * 1024

SB_EXIT_LOG = -104.0
NEG_BIG = -1e30


def _cparams(sem):
    return pltpu.CompilerParams(dimension_semantics=sem, vmem_limit_bytes=VMEM_LIMIT)


def _rms(x, w):
    var = jnp.mean(x * x, axis=-1, keepdims=True)
    return x * lax.rsqrt(var + EPS) * w


def _split2(v):
    hi = v.astype(BF16)
    lo = (v - hi.astype(F32)).astype(BF16)
    return hi, lo


def _dot(a, b):
    return jnp.dot(a, b, preferred_element_type=F32)


def _silu(x):
    return x / (1.0 + jnp.exp(-x))


def _softplus(x):
    return jnp.maximum(x, 0.0) + jnp.log(1.0 + jnp.exp(-jnp.abs(x)))


def _norm_matmul_kernel(x_ref, nw_ref, w_ref, ws_ref, o_ref, os_ref, u_ref):
    @pl.when(pl.program_id(1) == 0)
    def _():
        u = _rms(x_ref[...], nw_ref[...]).astype(BF16)
        u_ref[...] = u
        os_ref[...] = _dot(u, ws_ref[...])

    o_ref[...] = _dot(u_ref[...], w_ref[...]).astype(o_ref.dtype)


def _norm_matmul(x, nw, w, w_side, tm, tn):
    m, k = x.shape
    n = w.shape[1]
    ns = w_side.shape[1]
    tm = min(tm, m)
    return pl.pallas_call(
        _norm_matmul_kernel,
        grid=(m // tm, n // tn),
        in_specs=[
            pl.BlockSpec((tm, k), lambda i, j: (i, 0)),
            pl.BlockSpec((1, k), lambda i, j: (0, 0)),
            pl.BlockSpec((k, tn), lambda i, j: (0, j)),
            pl.BlockSpec((k, ns), lambda i, j: (0, 0)),
        ],
        out_specs=[
            pl.BlockSpec((tm, tn), lambda i, j: (i, j)),
            pl.BlockSpec((tm, ns), lambda i, j: (i, 0)),
        ],
        out_shape=[jax.ShapeDtypeStruct((m, n), BF16), jax.ShapeDtypeStruct((m, ns), F32)],
        scratch_shapes=[pltpu.VMEM((tm, k), BF16)],
        compiler_params=_cparams(("parallel", "arbitrary")),
        name="norm_matmul",
    )(x, nw, w, w_side)


def _ssd_kernel(z_ref, xs_ref, b_ref, c_ref, dtr_ref, cw_ref, cb_ref, dtb_ref, alog_ref, dsk_ref,
                nw_ref, e_ref, o_ref,
                tail_ref, state_ref, xact_ref, bact_ref, cact_ref, dtx_ref, eax_ref, wsx_ref, y_ref):
    L = SSD_CHUNK
    inner = xs_ref.shape[1]
    bc = b_ref.shape[1]
    n_pairs = inner // LANES

    @pl.when(pl.program_id(1) == 0)
    def _():
        tail_ref[...] = jnp.zeros_like(tail_ref)
        state_ref[...] = jnp.zeros_like(state_ref)

    row8 = lax.broadcasted_iota(jnp.int32, (SUBLANES, 2 * LANES), 0)

    def conv_seg(src_ref, col0, width, dst_ref):
        for j in range(0, width, 2 * LANES):
            cols = slice(col0 + j, col0 + j + 2 * LANES)
            xin = src_ref[:, j:j + 2 * LANES].astype(F32)
            t8 = tail_ref[:, cols]
            acc = xin * cw_ref[SSD_CONV - 1:SSD_CONV, cols] + cb_ref[:, cols]
            for s in range(1, SSD_CONV):
                r = pltpu.roll(xin, s, 0)
                top = jnp.where(row8 < s, pltpu.roll(t8, s, 0), r[:SUBLANES])
                r = jnp.concatenate([top, r[SUBLANES:]], axis=0)
                acc = acc + r * cw_ref[SSD_CONV - 1 - s:SSD_CONV - s, cols]
            tail_ref[:, cols] = xin[L - SUBLANES:]
            dst_ref[:, j:j + 2 * LANES] = _silu(acc).astype(dst_ref.dtype)

    conv_seg(xs_ref, 0, inner, xact_ref)
    conv_seg(b_ref, inner, bc, bact_ref)
    conv_seg(c_ref, inner + bc, bc, cact_ref)

    dt = _softplus(dtr_ref[...] + dtb_ref[...])
    da = dt * (-jnp.exp(alog_ref[...]))
    rowl = lax.broadcasted_iota(jnp.int32, (L, L), 0)
    coll = lax.broadcasted_iota(jnp.int32, (L, L), 1)
    lower = rowl >= coll
    tri = jnp.where(lower, 1.0, 0.0).astype(BF16)
    d1 = da.astype(BF16)
    r1 = da - d1.astype(F32)
    d2 = r1.astype(BF16)
    d3 = (r1 - d2.astype(F32)).astype(BF16)
    a_cum = _dot(tri, d1) + _dot(tri, d2) + _dot(tri, d3)
    a_cum_t = a_cum.T
    a_last = a_cum[L - 1:L, :]
    e_mat = e_ref[...]

    def expand(v):
        hi, lo = _split2(v)
        return _dot(hi, e_mat) + _dot(lo, e_mat)

    dtx_ref[...] = expand(dt)
    eax_ref[...] = expand(jnp.exp(a_cum))
    wsx_ref[...] = expand(jnp.exp(a_last - a_cum))
    elx = expand(jnp.broadcast_to(jnp.exp(a_last), (SUBLANES, LANES)))[0:1]

    lane = lax.broadcasted_iota(jnp.int32, (L, LANES), 1)
    pairs_per_group = n_pairs // SSD_GROUPS
    for g in range(SSD_GROUPS):
        gcols = slice(g * SSD_STATE, (g + 1) * SSD_STATE)
        bg_t = bact_ref[:, gcols].T.astype(BF16)
        cg = cact_ref[:, gcols]
        cb = _dot(cg, bg_t)
        for pp in range(pairs_per_group):
            p = g * pairs_per_group + pp
            pc = slice(p * LANES, (p + 1) * LANES)
            ms = []
            for hh in (2 * p, 2 * p + 1):
                seg = a_cum[:, hh:hh + 1] - a_cum_t[hh:hh + 1, :]
                dec = jnp.exp(jnp.where(lower, seg, NEG_BIG))
                ms.append((cb * dec).astype(BF16))
            lhs = jnp.concatenate(ms, axis=1)
            xs_p = xact_ref[:, pc]
            xdt = xs_p * dtx_ref[:, pc]
            rhs = jnp.concatenate([jnp.where(lane < SSD_HEAD_DIM, xdt, 0.0).astype(BF16),
                                   jnp.where(lane >= SSD_HEAD_DIM, xdt, 0.0).astype(BF16)], axis=0)
            st = state_ref[p]
            y = _dot(lhs, rhs)
            y = y + _dot(cg, st.astype(BF16)) * eax_ref[:, pc]
            y = y + xs_p * dsk_ref[:, pc]
            y_ref[:, pc] = y
            xw = (xdt * wsx_ref[:, pc]).astype(BF16)
            state_ref[p] = st * elx[:, pc] + _dot(bg_t, xw)

    yg = y_ref[...] * _silu(z_ref[...].astype(F32))
    o_ref[...] = _rms(yg, nw_ref[...]).astype(o_ref.dtype)


def _ssd(proj, dt_raw, conv_w, conv_b, dt_bias_p, a_log_p, dskip_x, norm_w, e_mat, bsz, seq, inner, bc):
    L = SSD_CHUNK
    nc = seq // L
    conv_dim = inner + 2 * bc
    n_pairs = inner // LANES
    ib = inner // bc

    def rows(b, c):
        return b * nc + c

    return pl.pallas_call(
        _ssd_kernel,
        grid=(bsz, nc),
        in_specs=[
            pl.BlockSpec((L, inner), lambda b, c: (rows(b, c), 0)),
            pl.BlockSpec((L, inner), lambda b, c: (rows(b, c), 1)),
            pl.BlockSpec((L, bc), lambda b, c: (rows(b, c), 2 * ib)),
            pl.BlockSpec((L, bc), lambda b, c: (rows(b, c), 2 * ib + 1)),
            pl.BlockSpec((L, LANES), lambda b, c: (rows(b, c), 0)),
            pl.BlockSpec((SSD_CONV, conv_dim), lambda b, c: (0, 0)),
            pl.BlockSpec((1, conv_dim), lambda b, c: (0, 0)),
            pl.BlockSpec((1, LANES), lambda b, c: (0, 0)),
            pl.BlockSpec((1, LANES), lambda b, c: (0, 0)),
            pl.BlockSpec((1, inner), lambda b, c: (0, 0)),
            pl.BlockSpec((1, inner), lambda b, c: (0, 0)),
            pl.BlockSpec((LANES, inner), lambda b, c: (0, 0)),
        ],
        out_specs=pl.BlockSpec((L, inner), lambda b, c: (rows(b, c), 0)),
        out_shape=jax.ShapeDtypeStruct((bsz * seq, inner), BF16),
        scratch_shapes=[
            pltpu.VMEM((SUBLANES, conv_dim), F32),
            pltpu.VMEM((n_pairs, SSD_STATE, LANES), F32),
            pltpu.VMEM((L, inner), F32),
            pltpu.VMEM((L, bc), F32),
            pltpu.VMEM((L, bc), BF16),
            pltpu.VMEM((L, inner), F32),
            pltpu.VMEM((L, inner), F32),
            pltpu.VMEM((L, inner), F32),
            pltpu.VMEM((L, inner), F32),
        ],
        compiler_params=_cparams(("parallel", "arbitrary")),
        name="ssd",
    )(proj, proj, proj, proj, dt_raw, conv_w, conv_b, dt_bias_p, a_log_p, dskip_x, norm_w, e_mat)


def _sb_kernel(q_ref, k_ref, v_ref, o_ref, acc_ref, carry_ref):
    BL = SB_BLOCK
    i = pl.program_id(2)
    scale = SB_HEAD_DIM ** -0.5
    row = lax.broadcasted_iota(jnp.int32, (BL, BL), 0)
    col = lax.broadcasted_iota(jnp.int32, (BL, BL), 1)
    causal = col < row
    r2 = lax.broadcasted_iota(jnp.int32, (BL, 2 * BL), 0)
    c2 = lax.broadcasted_iota(jnp.int32, (BL, 2 * BL), 1)
    tri = jnp.where((r2 > c2) | (c2 >= BL), 1.0, 0.0).astype(BF16)

    def sweep(s, first):
        subs = range(SB_SUB)
        kbs = [i * SB_SUB + j - s for j in subs]
        starts = [pl.multiple_of(jnp.maximum(kb, 0) * BL, BL) for kb in kbs]
        zls = [lax.dot_general(q_ref[j * BL:(j + 1) * BL, :], k_ref[pl.ds(starts[j], BL), :],
                               (((1,), (1,)), ((), ())), preferred_element_type=F32) * scale for j in subs]
        sps = [_softplus(zl) for zl in zls]
        log_betas = [zl - sp for zl, sp in zip(zls, sps)]
        log_keeps = [jnp.where(causal, -sp, 0.0) if first else -sp for sp in sps]
        splits = [_split2(lk) for lk in log_keeps]
        t2s = [_dot(hi, tri) + _dot(lo, tri) for hi, lo in splits]
        if first:
            atts = [jnp.where(causal, jnp.exp(lb + t2[:, :BL]), 0.0) for lb, t2 in zip(log_betas, t2s)]
            carries = [t2[:, BL:] for t2 in t2s]
        else:
            olds = [carry_ref[j] for j in subs]
            atts = [jnp.where(kb >= 0, jnp.exp(lb + t2[:, :BL] + old), 0.0)
                    for kb, lb, t2, old in zip(kbs, log_betas, t2s, olds)]
            carries = [old + jnp.where(kb >= 0, t2[:, BL:], 0.0) for kb, t2, old in zip(kbs, t2s, olds)]
        pvs = [_dot(atts[j].astype(BF16), v_ref[pl.ds(starts[j], BL), :]) for j in subs]
        worst = None
        for j in subs:
            if first:
                acc_ref[j] = pvs[j]
            else:
                acc_ref[j] += pvs[j]
            carry_ref[j] = carries[j]
            live = jnp.where(kbs[j] >= 1, carries[j], NEG_BIG)
            worst = live if worst is None else jnp.maximum(worst, live)
        return jnp.max(worst) > SB_EXIT_LOG

    sweep(0, True)
    go = sweep(1, False)

    def body(st):
        s, _ = st
        return s + 1, sweep(s, False)

    lax.while_loop(lambda st: st[1], body, (jnp.int32(2), go))
    for j in range(SB_SUB):
        o_ref[j * BL:(j + 1) * BL, :] = acc_ref[j].astype(o_ref.dtype)


def _stickbreak(proj3, bsz, seq, q_blk0):
    tq = SB_SUB * SB_BLOCK
    return pl.pallas_call(
        _sb_kernel,
        grid=(bsz, SB_HEADS, seq // tq),
        in_specs=[
            pl.BlockSpec((None, tq, SB_HEAD_DIM), lambda b, h, i: (b, i, q_blk0 + h)),
            pl.BlockSpec((None, seq, SB_HEAD_DIM), lambda b, h, i: (b, 0, q_blk0 + SB_HEADS + h)),
            pl.BlockSpec((None, seq, SB_HEAD_DIM), lambda b, h, i: (b, 0, q_blk0 + 2 * SB_HEADS + h)),
        ],
        out_specs=pl.BlockSpec((None, tq, SB_HEAD_DIM), lambda b, h, i: (b, i, h)),
        out_shape=jax.ShapeDtypeStruct((bsz, seq, SB_HEADS * SB_HEAD_DIM), BF16),
        scratch_shapes=[pltpu.VMEM((SB_SUB, SB_BLOCK, SB_HEAD_DIM), F32),
                        pltpu.VMEM((SB_SUB, SB_BLOCK, SB_BLOCK), F32)],
        compiler_params=_cparams(("parallel", "parallel", "arbitrary")),
        name="stickbreak",
    )(proj3, proj3, proj3)


def _merge_kernel(x_ref, ys_ref, yb_ref, g1_ref, g2_ref, w1_ref, w2_ref, wm_ref, o_ref, acc_ref):
    j = pl.program_id(1)

    @pl.when(j == 0)
    def _():
        acc_ref[...] = jnp.zeros_like(acc_ref)

    a = _dot(ys_ref[...], w1_ref[...])
    b = _dot(yb_ref[...], w2_ref[...])
    s1 = jax.nn.sigmoid(g1_ref[...].astype(F32))
    s2 = jax.nn.sigmoid(g2_ref[...].astype(F32))
    m = (s1 * a + s2 * b).astype(BF16)
    acc_ref[...] += _dot(m, wm_ref[...])

    @pl.when(j == pl.num_programs(1) - 1)
    def _():
        o_ref[...] = x_ref[...] + acc_ref[...]


def _merge(x2, y_ssd, y_sb, proj, w1, w2, wm, g_blk0, tm, tn):
    t, d = x2.shape
    tm = min(tm, t)
    nj = d // tn
    return pl.pallas_call(
        _merge_kernel,
        grid=(t // tm, nj),
        in_specs=[
            pl.BlockSpec((tm, d), lambda i, j: (i, 0)),
            pl.BlockSpec((tm, y_ssd.shape[1]), lambda i, j: (i, 0)),
            pl.BlockSpec((tm, y_sb.shape[1]), lambda i, j: (i, 0)),
            pl.BlockSpec((tm, tn), lambda i, j: (i, g_blk0 + j)),
            pl.BlockSpec((tm, tn), lambda i, j: (i, g_blk0 + nj + j)),
            pl.BlockSpec((w1.shape[0], tn), lambda i, j: (0, j)),
            pl.BlockSpec((w2.shape[0], tn), lambda i, j: (0, j)),
            pl.BlockSpec((tn, d), lambda i, j: (j, 0)),
        ],
        out_specs=pl.BlockSpec((tm, d), lambda i, j: (i, 0)),
        out_shape=jax.ShapeDtypeStruct((t, d), F32),
        scratch_shapes=[pltpu.VMEM((tm, d), F32)],
        compiler_params=_cparams(("parallel", "arbitrary")),
        name="merge",
    )(x2, y_ssd, y_sb, proj, proj, w1, w2, wm)


def _xattn_kernel(h_ref, kv_ref, nxa_ref, wq_ref, wo_ref, nmoe_ref, wrh_ref, wrl_ref,
                  h2_ref, um_ref, lg_ref):
    h1 = h_ref[...]
    un = _rms(h1, nxa_ref[...]).astype(BF16)
    q = _dot(un, wq_ref[...]).astype(BF16)
    width = XA_HEADS * XA_HEAD_DIM
    outs = []
    for hd in range(XA_HEADS):
        cs = slice(hd * XA_HEAD_DIM, (hd + 1) * XA_HEAD_DIM)
        k = kv_ref[:, cs]
        v = kv_ref[:, width + hd * XA_HEAD_DIM: width + (hd + 1) * XA_HEAD_DIM]
        sc = lax.dot_general(q[:, cs], k, (((1,), (1,)), ((), ())),
                             preferred_element_type=F32) * (XA_HEAD_DIM ** -0.5)
        sc = sc - jnp.max(sc, axis=-1, keepdims=True)
        p = jnp.exp(sc)
        p = p / jnp.sum(p, axis=-1, keepdims=True)
        outs.append(_dot(p.astype(BF16), v).astype(BF16))
    o = jnp.concatenate(outs, axis=1)
    h2 = h1 + _dot(o, wo_ref[...])
    h2_ref[...] = h2
    um = _rms(h2, nmoe_ref[...])
    um_ref[...] = um
    hi, lo = _split2(um)
    lg_ref[...] = _dot(hi, wrh_ref[...]) + _dot(lo, wrh_ref[...]) + _dot(hi, wrl_ref[...])


def _xattn(h1, kv, nxa, wq, wo, nmoe, wr_hi, wr_lo, seq, tm):
    t, d = h1.shape
    tm = min(tm, seq)
    per_b = seq // tm
    m_len = kv.shape[0] // (t // seq)
    nr = wr_hi.shape[1]
    return pl.pallas_call(
        _xattn_kernel,
        grid=(t // tm,),
        in_specs=[
            pl.BlockSpec((tm, d), lambda i: (i, 0)),
            pl.BlockSpec((m_len, kv.shape[1]), lambda i: (i // per_b, 0)),
            pl.BlockSpec((1, d), lambda i: (0, 0)),
            pl.BlockSpec(wq.shape, lambda i: (0, 0)),
            pl.BlockSpec(wo.shape, lambda i: (0, 0)),
            pl.BlockSpec((1, d), lambda i: (0, 0)),
            pl.BlockSpec((d, nr), lambda i: (0, 0)),
            pl.BlockSpec((d, nr), lambda i: (0, 0)),
        ],
        out_specs=[
            pl.BlockSpec((tm, d), lambda i: (i, 0)),
            pl.BlockSpec((tm, d), lambda i: (i, 0)),
            pl.BlockSpec((tm, nr), lambda i: (i, 0)),
        ],
        out_shape=[jax.ShapeDtypeStruct((t, d), F32), jax.ShapeDtypeStruct((t, d), F32),
                   jax.ShapeDtypeStruct((t, nr), F32)],
        compiler_params=_cparams(("parallel",)),
        name="xattn",
    )(h1, kv, nxa, wq, wo, nmoe, wr_hi, wr_lo)


def _row_copy(src_hbm, row, dst_ref, r, sem):
    return pltpu.make_async_copy(src_hbm.at[pl.ds(row, 1)], dst_ref.at[pl.ds(r, 1)], sem)


def _expert_kernel(be_ref, nv_ref, tok_ref, um_hbm, wg_ref, wu_ref, wd_ref, o_ref,
                   xb_ref, xbb_ref, wgb_ref, wub_ref, wdb_ref, sem):
    i = pl.program_id(0)
    last = pl.num_programs(0) - 1

    def gather(blk, buf):
        base = blk * MOE_BLOCK
        for r in range(MOE_BLOCK):
            _row_copy(um_hbm, tok_ref[base + r], xb_ref.at[buf], r, sem.at[buf]).start()

    def wait(buf):
        pltpu.make_async_copy(um_hbm.at[pl.ds(0, MOE_BLOCK)], xb_ref.at[buf], sem.at[buf]).wait()

    @pl.when(i == 0)
    def _():
        gather(i, 0)

    @pl.when(jnp.logical_or(i == 0, be_ref[i] != be_ref[jnp.maximum(i - 1, 0)]))
    def _():
        wgb_ref[...] = wg_ref[...].astype(BF16)
        wub_ref[...] = wu_ref[...].astype(BF16)
        wdb_ref[...] = wd_ref[...].astype(BF16)

    buf = i % 2
    nxt = jnp.minimum(i + 1, last)

    @pl.when(i < nv_ref[0])
    def _():
        wait(buf)
        xbb_ref[...] = xb_ref[buf].astype(BF16)
        gather(nxt, 1 - buf)
        xb = xbb_ref[...]
        hid = _silu(_dot(xb, wgb_ref[...])) * _dot(xb, wub_ref[...])
        o_ref[...] = _dot(hid.astype(BF16), wdb_ref[...])

    @pl.when(i >= nv_ref[0])
    def _():
        wait(buf)
        gather(nxt, 1 - buf)
        o_ref[...] = jnp.zeros_like(o_ref)

    @pl.when(i == last)
    def _():
        wait(1 - buf)


def _experts(block_e, n_valid, slot_tok, um, wg, wu, wd):
    t, d = um.shape
    n_slots = slot_tok.shape[0]
    ff = wg.shape[2]
    grid_spec = pltpu.PrefetchScalarGridSpec(
        num_scalar_prefetch=3,
        grid=(n_slots // MOE_BLOCK,),
        in_specs=[
            pl.BlockSpec(memory_space=pl.ANY),
            pl.BlockSpec((None, d, ff), lambda i, be, nv, tok: (be[i], 0, 0)),
            pl.BlockSpec((None, d, ff), lambda i, be, nv, tok: (be[i], 0, 0)),
            pl.BlockSpec((None, ff, d), lambda i, be, nv, tok: (be[i], 0, 0)),
        ],
        out_specs=pl.BlockSpec((MOE_BLOCK, d), lambda i, be, nv, tok: (i, 0)),
        scratch_shapes=[pltpu.VMEM((2, MOE_BLOCK, d), F32), pltpu.VMEM((MOE_BLOCK, d), BF16),
                        pltpu.VMEM((d, ff), BF16), pltpu.VMEM((d, ff), BF16), pltpu.VMEM((ff, d), BF16),
                        pltpu.SemaphoreType.DMA((2,))],
    )
    return pl.pallas_call(
        _expert_kernel,
        grid_spec=grid_spec,
        out_shape=jax.ShapeDtypeStruct((n_slots, d), F32),
        compiler_params=_cparams(("arbitrary",)),
        name="experts",
    )(block_e, n_valid, slot_tok, um, wg, wu, wd)


def _combine_kernel(slot_ref, h_ref, w_ref, nw_ref, yb_hbm, o_ref, ya0_ref, ya1_ref, yb0_ref, yb1_ref, sem):
    i = pl.program_id(0)
    tm = h_ref.shape[0]
    last = pl.num_programs(0) - 1
    bufs = ((ya0_ref, ya1_ref), (yb0_ref, yb1_ref))

    def gather(blk, par):
        base = blk * tm * TOP_K
        for r in range(tm):
            for k in range(TOP_K):
                _row_copy(yb_hbm, slot_ref[base + TOP_K * r + k], bufs[par][k], r, sem.at[par]).start()

    def wait(par):
        for k in range(TOP_K):
            pltpu.make_async_copy(yb_hbm.at[pl.ds(0, tm)], bufs[par][k], sem.at[par]).wait()

    @pl.when(i == 0)
    def _():
        gather(i, 0)

    def step(par):
        wait(par)
        gather(jnp.minimum(i + 1, last), 1 - par)
        w = w_ref[...]
        h3 = h_ref[...] + w[:, 0:1] * bufs[par][0][...] + w[:, 1:2] * bufs[par][1][...]
        o_ref[...] = _rms(h3, nw_ref[...])

        @pl.when(i == last)
        def _():
            wait(1 - par)

    for par in range(2):
        pl.when(i % 2 == par)(functools.partial(step, par))


def _combine(slot, h2, gate, nw, yb, tm):
    t, d = h2.shape
    tm = min(tm, t)
    grid_spec = pltpu.PrefetchScalarGridSpec(
        num_scalar_prefetch=1,
        grid=(t // tm,),
        in_specs=[
            pl.BlockSpec((tm, d), lambda i, s: (i, 0)),
            pl.BlockSpec((tm, TOP_K), lambda i, s: (i, 0)),
            pl.BlockSpec((1, d), lambda i, s: (0, 0)),
            pl.BlockSpec(memory_space=pl.ANY),
        ],
        out_specs=pl.BlockSpec((tm, d), lambda i, s: (i, 0)),
        scratch_shapes=[pltpu.VMEM((tm, d), F32)] * (2 * TOP_K) + [pltpu.SemaphoreType.DMA((2,))],
    )
    return pl.pallas_call(
        _combine_kernel,
        grid_spec=grid_spec,
        out_shape=jax.ShapeDtypeStruct((t, d), F32),
        compiler_params=_cparams(("arbitrary",)),
        name="combine",
    )(slot, h2, gate, nw, yb)


def _route(logits, t):
    g_logits = logits[:, :N_GROUPS]
    e_logits = logits[:, N_GROUPS:N_GROUPS + N_EXPERTS].reshape(t, N_GROUPS, EXPERTS_PER_GROUP)
    g_prob = jax.nn.softmax(g_logits, axis=-1)
    g_sel = jnp.argmax(g_logits, axis=-1)
    g_gate = jnp.take_along_axis(g_prob, g_sel[:, None], axis=-1)[:, 0]
    e_in_group = jnp.take_along_axis(e_logits, g_sel[:, None, None], axis=1)[:, 0]
    top_val, top_idx = lax.top_k(e_in_group, TOP_K)
    gate = jax.nn.softmax(top_val, axis=-1) * g_gate[:, None]
    flat_e = (g_sel[:, None] * EXPERTS_PER_GROUP + top_idx).astype(jnp.int32).reshape(-1)
    n_pairs = t * TOP_K
    tile = min(256, n_pairs)
    onehot = flat_e[:, None] == jnp.arange(N_EXPERTS, dtype=jnp.int32)[None, :]
    oh3 = onehot.reshape(n_pairs // tile, tile, N_EXPERTS)
    tri = (jnp.arange(tile)[:, None] >= jnp.arange(tile)[None, :]).astype(BF16)
    within = jnp.einsum('ij,tjk->tik', tri, oh3.astype(BF16), preferred_element_type=F32)
    tot = within[:, -1, :]
    tile_base = jnp.cumsum(tot, axis=0) - tot
    csum = (within + tile_base[:, None, :]).reshape(n_pairs, N_EXPERTS)
    rank = jnp.sum(jnp.where(onehot, csum, 0.0), axis=1).astype(jnp.int32) - 1
    counts = (tile_base[-1] + tot[-1]).astype(jnp.int32)
    padded = (counts + MOE_BLOCK - 1) // MOE_BLOCK * MOE_BLOCK
    pad_end = jnp.cumsum(padded)
    pad_start = pad_end - padded
    slot = (pad_start[flat_e] + rank).astype(jnp.int32)
    n_slots = t * TOP_K + N_EXPERTS * MOE_BLOCK
    n_blocks = n_slots // MOE_BLOCK
    flat_t = jnp.arange(t * TOP_K, dtype=jnp.int32) // TOP_K
    slot_tok = jnp.zeros((n_slots,), jnp.int32).at[slot].set(flat_t)
    block_e = jnp.minimum(
        jnp.searchsorted(pad_end, jnp.arange(n_blocks, dtype=jnp.int32) * MOE_BLOCK, side='right'),
        N_EXPERTS - 1).astype(jnp.int32)
    n_valid = (pad_end[-1:] // MOE_BLOCK).astype(jnp.int32)
    return gate, slot, slot_tok, block_e, n_valid


def _pad_lanes(v, n=LANES):
    return jnp.pad(v, ((0, 0), (0, n - v.shape[1])))


def kernel(x, mem, norm_mix_w, w_in, conv_w, conv_b, dt_bias, a_log, d_skip, ssd_norm_w, w_ssd_branch, w_sb_branch, w_mix_out, norm_xa_w, norm_mem_w, w_xq, w_xk, w_xv, w_xo, norm_moe_w, w_router_group, w_router_expert, w_expert_gate, w_expert_up, w_expert_down, norm_final_w):
    bsz, seq, d = x.shape
    depth = w_in.shape[0]
    t = bsz * seq
    inner = ssd_norm_w.shape[1]
    heads = dt_bias.shape[1]
    bc = SSD_GROUPS * SSD_STATE
    conv_dim = inner + 2 * bc
    sb_width = SB_HEADS * SB_HEAD_DIM
    col_dt = inner + conv_dim
    n_main = col_dt + 3 * sb_width + 2 * d
    q_col0 = col_dt
    g_col0 = col_dt + 3 * sb_width
    tn_merge = 512

    assert depth == 1, "single-layer configuration: the final RMSNorm is fused into the last MoE combine"
    h = x.reshape(t, d)
    for l in range(depth):
        w_l = w_in[l]
        w_main = jnp.concatenate([w_l[:, :col_dt], w_l[:, col_dt + heads:]], axis=1).astype(BF16)
        w_dt = _pad_lanes(w_l[:, col_dt:col_dt + heads]).astype(BF16)
        e_mat = (jnp.arange(inner, dtype=jnp.int32)[None, :] // SSD_HEAD_DIM
                 == jnp.arange(LANES, dtype=jnp.int32)[:, None]).astype(BF16)
        dskip_x = jnp.repeat(d_skip[l], SSD_HEAD_DIM)[None, :]

        proj, dt_raw = _norm_matmul(h, norm_mix_w[l][None, :], w_main, w_dt, tm=1024, tn=512)
        y_ssd = _ssd(proj, dt_raw, conv_w[l], conv_b[l][None, :], _pad_lanes(dt_bias[l][None, :]),
                     _pad_lanes(a_log[l][None, :]), dskip_x, ssd_norm_w[l][None, :], e_mat,
                     bsz, seq, inner, bc)
        y_sb = _stickbreak(proj.reshape(bsz, seq, n_main), bsz, seq, q_col0 // SB_HEAD_DIM)
        h1 = _merge(h, y_ssd, y_sb.reshape(t, sb_width), proj,
                    w_ssd_branch[l].astype(BF16), w_sb_branch[l].astype(BF16), w_mix_out[l].astype(BF16),
                    g_col0 // tn_merge, tm=512, tn=tn_merge)

        m_len = mem.shape[1]
        w_kv = jnp.concatenate([w_xk[l], w_xv[l]], axis=1).astype(BF16)
        kv, _ = _norm_matmul(mem.reshape(bsz * m_len, d), norm_mem_w[l][None, :], w_kv,
                             jnp.zeros((d, LANES), BF16), tm=bsz * m_len, tn=512)
        w_r = _pad_lanes(jnp.concatenate([w_router_group[l], w_router_expert[l]], axis=1))
        wr_hi = w_r.astype(BF16)
        wr_lo = (w_r - wr_hi.astype(F32)).astype(BF16)
        h2, um, logits = _xattn(h1, kv, norm_xa_w[l][None, :], w_xq[l].astype(BF16), w_xo[l].astype(BF16),
                                norm_moe_w[l][None, :], wr_hi, wr_lo, seq, tm=512)

        gate, slot, slot_tok, block_e, n_valid = _route(logits, t)
        yb = _experts(block_e, n_valid, slot_tok, um, w_expert_gate[l], w_expert_up[l], w_expert_down[l])
        h = _combine(slot, h2, gate, norm_final_w[None, :], yb, tm=256)
    return h.reshape(bsz, seq, d)
```

```python
import functools

import jax
import jax.numpy as jnp
from jax import lax
from jax.experimental import pallas as pl
from jax.experimental.pallas import tpu as pltpu

F32 = jnp.float32
BF16 = jnp.bfloat16
EPS = 1e-6

SSD_HEAD_DIM = 64
SSD_GROUPS = 4
SSD_STATE = 128
SSD_CONV = 4
SSD_CHUNK = 128
SB_HEADS = 4
SB_HEAD_DIM = 128
SB_BLOCK = 128
SB_SUB = 4
XA_HEADS = 4
XA_HEAD_DIM = 128
N_GROUPS = 4
EXPERTS_PER_GROUP = 8
N_EXPERTS = N_GROUPS * EXPERTS_PER_GROUP
TOP_K = 2
MOE_BLOCK = 256

LANES = 128
SUBLANES = 8
SLAB_ROWS = 16
SLAB_COLS = LANES
MXU_DEPTH = 256
VMEM_LIMIT = 56 * 1024 * 1024

SB_EXIT_LOG = -104.0
NEG_BIG = -1e30


def _cparams(sem):
    return pltpu.CompilerParams(dimension_semantics=sem, vmem_limit_bytes=VMEM_LIMIT)


def _rms(x, w):
    var = jnp.mean(x * x, axis=-1, keepdims=True)
    return x * lax.rsqrt(var + EPS) * w


def _split2(v):
    hi = v.astype(BF16)
    lo = (v - hi.astype(F32)).astype(BF16)
    return hi, lo


def _dot(a, b):
    return jnp.dot(a, b, preferred_element_type=F32)


def _silu(x):
    return x / (1.0 + jnp.exp(-x))


def _softplus(x):
    return jnp.maximum(x, 0.0) + jnp.log(1.0 + jnp.exp(-jnp.abs(x)))


def _norm_matmul_kernel(x_ref, nw_ref, w_ref, ws_ref, o_ref, os_ref, u_ref):
    @pl.when(pl.program_id(1) == 0)
    def _():
        u = _rms(x_ref[...], nw_ref[...]).astype(BF16)
        u_ref[...] = u
        os_ref[...] = _dot(u, ws_ref[...])

    o_ref[...] = _dot(u_ref[...], w_ref[...]).astype(o_ref.dtype)


def _norm_matmul(x, nw, w, w_side, tm, tn):
    m, k = x.shape
    n = w.shape[1]
    ns = w_side.shape[1]
    tm = min(tm, m)
    return pl.pallas_call(
        _norm_matmul_kernel,
        grid=(m // tm, n // tn),
        in_specs=[
            pl.BlockSpec((tm, k), lambda i, j: (i, 0)),
            pl.BlockSpec((1, k), lambda i, j: (0, 0)),
            pl.BlockSpec((k, tn), lambda i, j: (0, j)),
            pl.BlockSpec((k, ns), lambda i, j: (0, 0)),
        ],
        out_specs=[
            pl.BlockSpec((tm, tn), lambda i, j: (i, j)),
            pl.BlockSpec((tm, ns), lambda i, j: (i, 0)),
        ],
        out_shape=[jax.ShapeDtypeStruct((m, n), BF16), jax.ShapeDtypeStruct((m, ns), F32)],
        scratch_shapes=[pltpu.VMEM((tm, k), BF16)],
        compiler_params=_cparams(("parallel", "arbitrary")),
        name="norm_matmul",
    )(x, nw, w, w_side)


def _ssd_kernel(z_ref, xs_ref, b_ref, c_ref, dtr_ref, cw_ref, cb_ref, dtb_ref, alog_ref, dsk_ref,
                nw_ref, e_ref, o_ref,
                tail_ref, state_ref, xact_ref, bact_ref, cact_ref, dtx_ref, eax_ref, wsx_ref, y_ref):
    L = SSD_CHUNK
    inner = xs_ref.shape[1]
    bc = b_ref.shape[1]
    n_pairs = inner // LANES

    @pl.when(pl.program_id(1) == 0)
    def _():
        tail_ref[...] = jnp.zeros_like(tail_ref)
        state_ref[...] = jnp.zeros_like(state_ref)

    row8 = lax.broadcasted_iota(jnp.int32, (SUBLANES, 2 * LANES), 0)

    def conv_seg(src_ref, col0, width, dst_ref):
        for j in range(0, width, 2 * LANES):
            cols = slice(col0 + j, col0 + j + 2 * LANES)
            xin = src_ref[:, j:j + 2 * LANES].astype(F32)
            t8 = tail_ref[:, cols]
            acc = xin * cw_ref[SSD_CONV - 1:SSD_CONV, cols] + cb_ref[:, cols]
            for s in range(1, SSD_CONV):
                r = pltpu.roll(xin, s, 0)
                top = jnp.where(row8 < s, pltpu.roll(t8, s, 0), r[:SUBLANES])
                r = jnp.concatenate([top, r[SUBLANES:]], axis=0)
                acc = acc + r * cw_ref[SSD_CONV - 1 - s:SSD_CONV - s, cols]
            tail_ref[:, cols] = xin[L - SUBLANES:]
            dst_ref[:, j:j + 2 * LANES] = _silu(acc).astype(dst_ref.dtype)

    conv_seg(xs_ref, 0, inner, xact_ref)
    conv_seg(b_ref, inner, bc, bact_ref)
    conv_seg(c_ref, inner + bc, bc, cact_ref)

    dt = _softplus(dtr_ref[...] + dtb_ref[...])
    da = dt * (-jnp.exp(alog_ref[...]))
    rowl = lax.broadcasted_iota(jnp.int32, (L, L), 0)
    coll = lax.broadcasted_iota(jnp.int32, (L, L), 1)
    lower = rowl >= coll
    tri = jnp.where(lower, 1.0, 0.0).astype(BF16)
    d1 = da.astype(BF16)
    r1 = da - d1.astype(F32)
    d2 = r1.astype(BF16)
    d3 = (r1 - d2.astype(F32)).astype(BF16)
    a_cum = _dot(tri, d1) + _dot(tri, d2) + _dot(tri, d3)
    a_cum_t = a_cum.T
    a_last = a_cum[L - 1:L, :]
    e_mat = e_ref[...]

    def expand(v):
        hi, lo = _split2(v)
        return _dot(hi, e_mat) + _dot(lo, e_mat)

    dtx_ref[...] = expand(dt)
    eax_ref[...] = expand(jnp.exp(a_cum))
    wsx_ref[...] = expand(jnp.exp(a_last - a_cum))
    elx = expand(jnp.broadcast_to(jnp.exp(a_last), (SUBLANES, LANES)))[0:1]

    lane = lax.broadcasted_iota(jnp.int32, (L, LANES), 1)
    pairs_per_group = n_pairs // SSD_GROUPS
    for g in range(SSD_GROUPS):
        gcols = slice(g * SSD_STATE, (g + 1) * SSD_STATE)
        bg_t = bact_ref[:, gcols].T.astype(BF16)
        cg = cact_ref[:, gcols]
        cb = _dot(cg, bg_t)
        for pp in range(pairs_per_group):
            p = g * pairs_per_group + pp
            pc = slice(p * LANES, (p + 1) * LANES)
            ms = []
            for hh in (2 * p, 2 * p + 1):
                seg = a_cum[:, hh:hh + 1] - a_cum_t[hh:hh + 1, :]
                dec = jnp.exp(jnp.where(lower, seg, NEG_BIG))
                ms.append((cb * dec).astype(BF16))
            lhs = jnp.concatenate(ms, axis=1)
            xs_p = xact_ref[:, pc]
            xdt = xs_p * dtx_ref[:, pc]
            rhs = jnp.concatenate([jnp.where(lane < SSD_HEAD_DIM, xdt, 0.0).astype(BF16),
                                   jnp.where(lane >= SSD_HEAD_DIM, xdt, 0.0).astype(BF16)], axis=0)
            st = state_ref[p]
            y = _dot(lhs, rhs)
            y = y + _dot(cg, st.astype(BF16)) * eax_ref[:, pc]
            y = y + xs_p * dsk_ref[:, pc]
            y_ref[:, pc] = y
            xw = (xdt * wsx_ref[:, pc]).astype(BF16)
            state_ref[p] = st * elx[:, pc] + _dot(bg_t, xw)

    yg = y_ref[...] * _silu(z_ref[...].astype(F32))
    o_ref[...] = _rms(yg, nw_ref[...]).astype(o_ref.dtype)


def _ssd(proj, dt_raw, conv_w, conv_b, dt_bias_p, a_log_p, dskip_x, norm_w, e_mat, bsz, seq, inner, bc):
    L = SSD_CHUNK
    nc = seq // L
    conv_dim = inner + 2 * bc
    n_pairs = inner // LANES
    ib = inner // bc

    def rows(b, c):
        return b * nc + c

    return pl.pallas_call(
        _ssd_kernel,
        grid=(bsz, nc),
        in_specs=[
            pl.BlockSpec((L, inner), lambda b, c: (rows(b, c), 0)),
            pl.BlockSpec((L, inner), lambda b, c: (rows(b, c), 1)),
            pl.BlockSpec((L, bc), lambda b, c: (rows(b, c), 2 * ib)),
            pl.BlockSpec((L, bc), lambda b, c: (rows(b, c), 2 * ib + 1)),
            pl.BlockSpec((L, LANES), lambda b, c: (rows(b, c), 0)),
            pl.BlockSpec((SSD_CONV, conv_dim), lambda b, c: (0, 0)),
            pl.BlockSpec((1, conv_dim), lambda b, c: (0, 0)),
            pl.BlockSpec((1, LANES), lambda b, c: (0, 0)),
            pl.BlockSpec((1, LANES), lambda b, c: (0, 0)),
            pl.BlockSpec((1, inner), lambda b, c: (0, 0)),
            pl.BlockSpec((1, inner), lambda b, c: (0, 0)),
            pl.BlockSpec((LANES, inner), lambda b, c: (0, 0)),
        ],
        out_specs=pl.BlockSpec((L, inner), lambda b, c: (rows(b, c), 0)),
        out_shape=jax.ShapeDtypeStruct((bsz * seq, inner), BF16),
        scratch_shapes=[
            pltpu.VMEM((SUBLANES, conv_dim), F32),
            pltpu.VMEM((n_pairs, SSD_STATE, LANES), F32),
            pltpu.VMEM((L, inner), F32),
            pltpu.VMEM((L, bc), F32),
            pltpu.VMEM((L, bc), BF16),
            pltpu.VMEM((L, inner), F32),
            pltpu.VMEM((L, inner), F32),
            pltpu.VMEM((L, inner), F32),
            pltpu.VMEM((L, inner), F32),
        ],
        compiler_params=_cparams(("parallel", "arbitrary")),
        name="ssd",
    )(proj, proj, proj, proj, dt_raw, conv_w, conv_b, dt_bias_p, a_log_p, dskip_x, norm_w, e_mat)


def _sb_kernel(q_ref, k_ref, v_ref, o_ref, acc_ref, carry_ref):
    BL = SB_BLOCK
    i = pl.program_id(2)
    scale = SB_HEAD_DIM ** -0.5
    row = lax.broadcasted_iota(jnp.int32, (BL, BL), 0)
    col = lax.broadcasted_iota(jnp.int32, (BL, BL), 1)
    causal = col < row
    r2 = lax.broadcasted_iota(jnp.int32, (BL, 2 * BL), 0)
    c2 = lax.broadcasted_iota(jnp.int32, (BL, 2 * BL), 1)
    tri = jnp.where((r2 > c2) | (c2 >= BL), 1.0, 0.0).astype(BF16)

    def sweep(s, first):
        subs = range(SB_SUB)
        kbs = [i * SB_SUB + j - s for j in subs]
        starts = [pl.multiple_of(jnp.maximum(kb, 0) * BL, BL) for kb in kbs]
        zls = [lax.dot_general(q_ref[j * BL:(j + 1) * BL, :], k_ref[pl.ds(starts[j], BL), :],
                               (((1,), (1,)), ((), ())), preferred_element_type=F32) * scale for j in subs]
        sps = [_softplus(zl) for zl in zls]
        log_betas = [zl - sp for zl, sp in zip(zls, sps)]
        log_keeps = [jnp.where(causal, -sp, 0.0) if first else -sp for sp in sps]
        splits = [_split2(lk) for lk in log_keeps]
        t2s = [_dot(hi, tri) + _dot(lo, tri) for hi, lo in splits]
        if first:
            atts = [jnp.where(causal, jnp.exp(lb + t2[:, :BL]), 0.0) for lb, t2 in zip(log_betas, t2s)]
            carries = [t2[:, BL:] for t2 in t2s]
        else:
            olds = [carry_ref[j] for j in subs]
            atts = [jnp.where(kb >= 0, jnp.exp(lb + t2[:, :BL] + old), 0.0)
                    for kb, lb, t2, old in zip(kbs, log_betas, t2s, olds)]
            carries = [old + jnp.where(kb >= 0, t2[:, BL:], 0.0) for kb, t2, old in zip(kbs, t2s, olds)]
        pvs = [_dot(atts[j].astype(BF16), v_ref[pl.ds(starts[j], BL), :]) for j in subs]
        worst = None
        for j in subs:
            if first:
                acc_ref[j] = pvs[j]
            else:
                acc_ref[j] += pvs[j]
            carry_ref[j] = carries[j]
            live = jnp.where(kbs[j] >= 1, carries[j], NEG_BIG)
            worst = live if worst is None else jnp.maximum(worst, live)
        return jnp.max(worst) > SB_EXIT_LOG

    sweep(0, True)
    go = sweep(1, False)

    def body(st):
        s, _ = st
        return s + 1, sweep(s, False)

    lax.while_loop(lambda st: st[1], body, (jnp.int32(2), go))
    for j in range(SB_SUB):
        o_ref[j * BL:(j + 1) * BL, :] = acc_ref[j].astype(o_ref.dtype)


def _stickbreak(proj3, bsz, seq, q_blk0):
    tq = SB_SUB * SB_BLOCK
    return pl.pallas_call(
        _sb_kernel,
        grid=(bsz, SB_HEADS, seq // tq),
        in_specs=[
            pl.BlockSpec((None, tq, SB_HEAD_DIM), lambda b, h, i: (b, i, q_blk0 + h)),
            pl.BlockSpec((None, seq, SB_HEAD_DIM), lambda b, h, i: (b, 0, q_blk0 + SB_HEADS + h)),
            pl.BlockSpec((None, seq, SB_HEAD_DIM), lambda b, h, i: (b, 0, q_blk0 + 2 * SB_HEADS + h)),
        ],
        out_specs=pl.BlockSpec((None, tq, SB_HEAD_DIM), lambda b, h, i: (b, i, h)),
        out_shape=jax.ShapeDtypeStruct((bsz, seq, SB_HEADS * SB_HEAD_DIM), BF16),
        scratch_shapes=[pltpu.VMEM((SB_SUB, SB_BLOCK, SB_HEAD_DIM), F32),
                        pltpu.VMEM((SB_SUB, SB_BLOCK, SB_BLOCK), F32)],
        compiler_params=_cparams(("parallel", "parallel", "arbitrary")),
        name="stickbreak",
    )(proj3, proj3, proj3)


def _merge_kernel(x_ref, ys_ref, yb_ref, g1_ref, g2_ref, w1_ref, w2_ref, wm_ref, o_ref, acc_ref):
    j = pl.program_id(1)

    @pl.when(j == 0)
    def _():
        acc_ref[...] = jnp.zeros_like(acc_ref)

    a = _dot(ys_ref[...], w1_ref[...])
    b = _dot(yb_ref[...], w2_ref[...])
    s1 = jax.nn.sigmoid(g1_ref[...].astype(F32))
    s2 = jax.nn.sigmoid(g2_ref[...].astype(F32))
    m = (s1 * a + s2 * b).astype(BF16)
    acc_ref[...] += _dot(m, wm_ref[...])

    @pl.when(j == pl.num_programs(1) - 1)
    def _():
        o_ref[...] = x_ref[...] + acc_ref[...]


def _merge(x2, y_ssd, y_sb, proj, w1, w2, wm, g_blk0, tm, tn):
    t, d = x2.shape
    tm = min(tm, t)
    nj = d // tn
    return pl.pallas_call(
        _merge_kernel,
        grid=(t // tm, nj),
        in_specs=[
            pl.BlockSpec((tm, d), lambda i, j: (i, 0)),
            pl.BlockSpec((tm, y_ssd.shape[1]), lambda i, j: (i, 0)),
            pl.BlockSpec((tm, y_sb.shape[1]), lambda i, j: (i, 0)),
            pl.BlockSpec((tm, tn), lambda i, j: (i, g_blk0 + j)),
            pl.BlockSpec((tm, tn), lambda i, j: (i, g_blk0 + nj + j)),
            pl.BlockSpec((w1.shape[0], tn), lambda i, j: (0, j)),
            pl.BlockSpec((w2.shape[0], tn), lambda i, j: (0, j)),
            pl.BlockSpec((tn, d), lambda i, j: (j, 0)),
        ],
        out_specs=pl.BlockSpec((tm, d), lambda i, j: (i, 0)),
        out_shape=jax.ShapeDtypeStruct((t, d), F32),
        scratch_shapes=[pltpu.VMEM((tm, d), F32)],
        compiler_params=_cparams(("parallel", "arbitrary")),
        name="merge",
    )(x2, y_ssd, y_sb, proj, proj, w1, w2, wm)


def _xattn_kernel(h_ref, kv_ref, nxa_ref, wq_ref, wo_ref, nmoe_ref, wrh_ref, wrl_ref,
                  h2_ref, um_ref, rt_ref):
    h1 = h_ref[...]
    un = _rms(h1, nxa_ref[...]).astype(BF16)
    q = _dot(un, wq_ref[...]).astype(BF16)
    width = XA_HEADS * XA_HEAD_DIM
    outs = []
    for hd in range(XA_HEADS):
        cs = slice(hd * XA_HEAD_DIM, (hd + 1) * XA_HEAD_DIM)
        k = kv_ref[:, cs]
        v = kv_ref[:, width + hd * XA_HEAD_DIM: width + (hd + 1) * XA_HEAD_DIM]
        sc = lax.dot_general(q[:, cs], k, (((1,), (1,)), ((), ())),
                             preferred_element_type=F32) * (XA_HEAD_DIM ** -0.5)
        sc = sc - jnp.max(sc, axis=-1, keepdims=True)
        p = jnp.exp(sc)
        p = p / jnp.sum(p, axis=-1, keepdims=True)
        outs.append(_dot(p.astype(BF16), v).astype(BF16))
    o = jnp.concatenate(outs, axis=1)
    h2 = h1 + _dot(o, wo_ref[...])
    h2_ref[...] = h2
    um = _rms(h2, nmoe_ref[...])
    tm = um.shape[0]
    for c in range(SLAB_ROWS):
        um_ref[pl.ds(c, tm, stride=SLAB_ROWS), :] = um[:, c * SLAB_COLS:(c + 1) * SLAB_COLS]
    hi, lo = _split2(um)
    lg = _dot(hi, wrh_ref[...]) + _dot(lo, wrh_ref[...]) + _dot(hi, wrl_ref[...])

    lane = lax.broadcasted_iota(jnp.int32, lg.shape, 1)
    lane_f = lane.astype(F32)
    is_g = lane < N_GROUPS
    gl = jnp.where(is_g, lg, NEG_BIG)
    gmax = jnp.max(gl, axis=-1, keepdims=True)
    g_sel = jnp.min(jnp.where(gl == gmax, lane_f, float(LANES)), axis=-1, keepdims=True)
    g_gate = 1.0 / jnp.sum(jnp.where(is_g, jnp.exp(gl - gmax), 0.0), axis=-1, keepdims=True)
    lo_lane = N_GROUPS + EXPERTS_PER_GROUP * g_sel
    el = jnp.where((lane_f >= lo_lane) & (lane_f < lo_lane + EXPERTS_PER_GROUP), lg, NEG_BIG)
    m1 = jnp.max(el, axis=-1, keepdims=True)
    i1 = jnp.min(jnp.where(el == m1, lane_f, float(LANES)), axis=-1, keepdims=True)
    el2 = jnp.where(lane_f == i1, NEG_BIG, el)
    m2 = jnp.max(el2, axis=-1, keepdims=True)
    i2 = jnp.min(jnp.where(el2 == m2, lane_f, float(LANES)), axis=-1, keepdims=True)
    ex = jnp.exp(m2 - m1)
    p1 = 1.0 / (1.0 + ex)
    rt_ref[...] = jnp.where(lane == 0, i1 - N_GROUPS,
                            jnp.where(lane == 1, i2 - N_GROUPS,
                                      jnp.where(lane == 2, p1 * g_gate,
                                                jnp.where(lane == 3, ex * p1 * g_gate, 0.0))))


def _xattn(h1, kv, nxa, wq, wo, nmoe, wr_hi, wr_lo, seq, tm):
    t, d = h1.shape
    tm = min(tm, seq)
    per_b = seq // tm
    m_len = kv.shape[0] // (t // seq)
    nr = wr_hi.shape[1]
    return pl.pallas_call(
        _xattn_kernel,
        grid=(t // tm,),
        in_specs=[
            pl.BlockSpec((tm, d), lambda i: (i, 0)),
            pl.BlockSpec((m_len, kv.shape[1]), lambda i: (i // per_b, 0)),
            pl.BlockSpec((1, d), lambda i: (0, 0)),
            pl.BlockSpec(wq.shape, lambda i: (0, 0)),
            pl.BlockSpec(wo.shape, lambda i: (0, 0)),
            pl.BlockSpec((1, d), lambda i: (0, 0)),
            pl.BlockSpec((d, nr), lambda i: (0, 0)),
            pl.BlockSpec((d, nr), lambda i: (0, 0)),
        ],
        out_specs=[
            pl.BlockSpec((tm, d), lambda i: (i, 0)),
            pl.BlockSpec((tm * SLAB_ROWS, SLAB_COLS), lambda i: (i, 0)),
            pl.BlockSpec((tm, nr), lambda i: (i, 0)),
        ],
        out_shape=[jax.ShapeDtypeStruct((t, d), F32), jax.ShapeDtypeStruct((t * SLAB_ROWS, SLAB_COLS), F32),
                   jax.ShapeDtypeStruct((t, nr), F32)],
        compiler_params=_cparams(("parallel",)),
        name="xattn",
    )(h1, kv, nxa, wq, wo, nmoe, wr_hi, wr_lo)


def _slab_copy(src_hbm, row, dst_ref, r, sem):
    return pltpu.make_async_copy(src_hbm.at[pl.ds(pl.multiple_of(row * SLAB_ROWS, SLAB_ROWS), SLAB_ROWS)],
                                 dst_ref.at[pl.ds(r * SLAB_ROWS, SLAB_ROWS)], sem)


def _slab_chunk(ref, c, rows):
    return ref[pl.ds(c, rows, stride=SLAB_ROWS), :]


def _expert_kernel(be_ref, nv_ref, tok_ref, um_hbm, wg_ref, wu_ref, wd_ref, o_ref,
                   xb_ref, xbb_ref, wgb_ref, wub_ref, wdb_ref, sem):
    i = pl.program_id(0)
    last = pl.num_programs(0) - 1

    def gather(blk, buf):
        base = blk * MOE_BLOCK
        for r in range(MOE_BLOCK):
            _slab_copy(um_hbm, tok_ref[base + r], xb_ref.at[buf], r, sem.at[buf]).start()

    def wait(buf):
        pltpu.make_async_copy(um_hbm.at[pl.ds(0, MOE_BLOCK * SLAB_ROWS)], xb_ref.at[buf], sem.at[buf]).wait()

    @pl.when(i == 0)
    def _():
        gather(i, 0)

    @pl.when(jnp.logical_or(i == 0, be_ref[i] != be_ref[jnp.maximum(i - 1, 0)]))
    def _():
        wgb_ref[...] = wg_ref[...].astype(BF16)
        wub_ref[...] = wu_ref[...].astype(BF16)
        wdb_ref[...] = wd_ref[...].astype(BF16)

    buf = i % 2
    nxt = jnp.minimum(i + 1, last)

    @pl.when(i < nv_ref[0])
    def _():
        wait(buf)
        for c in range(SLAB_ROWS):
            xbb_ref[:, c * SLAB_COLS:(c + 1) * SLAB_COLS] = _slab_chunk(
                xb_ref.at[buf], c, MOE_BLOCK).astype(BF16)
        gather(nxt, 1 - buf)
        xb = xbb_ref[...]
        hid = (_silu(_dot(xb, wgb_ref[...])) * _dot(xb, wub_ref[...])).astype(BF16)
        y = _dot(hid, wdb_ref[...])
        for c in range(SLAB_ROWS):
            o_ref[pl.ds(c, MOE_BLOCK, stride=SLAB_ROWS), :] = y[:, c * SLAB_COLS:(c + 1) * SLAB_COLS]

    @pl.when(i >= nv_ref[0])
    def _():
        wait(buf)
        gather(nxt, 1 - buf)
        o_ref[...] = jnp.zeros_like(o_ref)

    @pl.when(i == last)
    def _():
        wait(1 - buf)


def _experts(block_e, n_valid, slot_tok, um, wg, wu, wd):
    n_slots = slot_tok.shape[0]
    d, ff = wg.shape[1:]
    rows = MOE_BLOCK * SLAB_ROWS
    grid_spec = pltpu.PrefetchScalarGridSpec(
        num_scalar_prefetch=3,
        grid=(n_slots // MOE_BLOCK,),
        in_specs=[
            pl.BlockSpec(memory_space=pl.ANY),
            pl.BlockSpec((None, d, ff), lambda i, be, nv, tok: (be[i], 0, 0)),
            pl.BlockSpec((None, d, ff), lambda i, be, nv, tok: (be[i], 0, 0)),
            pl.BlockSpec((None, ff, d), lambda i, be, nv, tok: (be[i], 0, 0)),
        ],
        out_specs=pl.BlockSpec((rows, SLAB_COLS), lambda i, be, nv, tok: (i, 0)),
        scratch_shapes=[pltpu.VMEM((2, rows, SLAB_COLS), F32),
                        pltpu.VMEM((MOE_BLOCK, d), BF16),
                        pltpu.VMEM((d, ff), BF16), pltpu.VMEM((d, ff), BF16), pltpu.VMEM((ff, d), BF16),
                        pltpu.SemaphoreType.DMA((2,))],
    )
    return pl.pallas_call(
        _expert_kernel,
        grid_spec=grid_spec,
        out_shape=jax.ShapeDtypeStruct((n_slots * SLAB_ROWS, SLAB_COLS), F32),
        compiler_params=_cparams(("arbitrary",)),
        name="experts",
    )(block_e, n_valid, slot_tok, um, wg, wu, wd)


def _combine_kernel(slot_ref, h_ref, w_ref, nw_ref, yb_hbm, o_ref, ya0_ref, ya1_ref, yb0_ref, yb1_ref, sem):
    i = pl.program_id(0)
    tm = h_ref.shape[0]
    last = pl.num_programs(0) - 1
    bufs = ((ya0_ref, ya1_ref), (yb0_ref, yb1_ref))

    def gather(blk, par):
        base = blk * tm * TOP_K
        for r in range(tm):
            for k in range(TOP_K):
                _slab_copy(yb_hbm, slot_ref[base + TOP_K * r + k], bufs[par][k], r, sem.at[par]).start()

    def wait(par):
        for k in range(TOP_K):
            pltpu.make_async_copy(yb_hbm.at[pl.ds(0, tm * SLAB_ROWS)], bufs[par][k], sem.at[par]).wait()

    @pl.when(i == 0)
    def _():
        gather(i, 0)

    def step(par):
        wait(par)
        gather(jnp.minimum(i + 1, last), 1 - par)
        w0 = w_ref[:, TOP_K:TOP_K + 1]
        w1 = w_ref[:, TOP_K + 1:TOP_K + 2]
        sq = jnp.zeros((tm, SLAB_COLS), F32)
        for c in range(SLAB_ROWS):
            cs = slice(c * SLAB_COLS, (c + 1) * SLAB_COLS)
            h3 = (h_ref[:, cs] + w0 * _slab_chunk(bufs[par][0], c, tm)
                  + w1 * _slab_chunk(bufs[par][1], c, tm))
            sq = sq + h3 * h3
            o_ref[:, cs] = h3
        inv = lax.rsqrt(jnp.sum(sq, axis=-1, keepdims=True) / h_ref.shape[1] + EPS)
        o_ref[...] = o_ref[...] * inv * nw_ref[...]

        @pl.when(i == last)
        def _():
            wait(1 - par)

    for par in range(2):
        pl.when(i % 2 == par)(functools.partial(step, par))


def _combine(slot, h2, route, nw, yb, tm):
    t, d = h2.shape
    tm = min(tm, t)
    grid_spec = pltpu.PrefetchScalarGridSpec(
        num_scalar_prefetch=1,
        grid=(t // tm,),
        in_specs=[
            pl.BlockSpec((tm, d), lambda i, s: (i, 0)),
            pl.BlockSpec((tm, route.shape[1]), lambda i, s: (i, 0)),
            pl.BlockSpec((1, d), lambda i, s: (0, 0)),
            pl.BlockSpec(memory_space=pl.ANY),
        ],
        out_specs=pl.BlockSpec((tm, d), lambda i, s: (i, 0)),
        scratch_shapes=([pltpu.VMEM((tm * SLAB_ROWS, SLAB_COLS), F32)] * (2 * TOP_K)
                        + [pltpu.SemaphoreType.DMA((2,))]),
    )
    return pl.pallas_call(
        _combine_kernel,
        grid_spec=grid_spec,
        out_shape=jax.ShapeDtypeStruct((t, d), F32),
        compiler_params=_cparams(("arbitrary",)),
        name="combine",
    )(slot, h2, route, nw, yb)


def _slots(route, t):
    flat_e = route[:, :TOP_K].astype(jnp.int32).reshape(-1)
    n_pairs = t * TOP_K
    tile = min(256, n_pairs)
    onehot = flat_e[:, None] == jnp.arange(N_EXPERTS, dtype=jnp.int32)[None, :]
    oh3 = onehot.reshape(n_pairs // tile, tile, N_EXPERTS)
    tri = (jnp.arange(tile)[:, None] >= jnp.arange(tile)[None, :]).astype(BF16)
    within = jnp.einsum('ij,tjk->tik', tri, oh3.astype(BF16), preferred_element_type=F32)
    tot = within[:, -1, :]
    tile_base = jnp.cumsum(tot, axis=0) - tot
    csum = (within + tile_base[:, None, :]).reshape(n_pairs, N_EXPERTS)
    rank = jnp.sum(jnp.where(onehot, csum, 0.0), axis=1).astype(jnp.int32) - 1
    counts = (tile_base[-1] + tot[-1]).astype(jnp.int32)
    padded = (counts + MOE_BLOCK - 1) // MOE_BLOCK * MOE_BLOCK
    pad_end = jnp.cumsum(padded)
    pad_start = pad_end - padded
    slot = (pad_start[flat_e] + rank).astype(jnp.int32)
    n_slots = t * TOP_K + N_EXPERTS * MOE_BLOCK
    n_blocks = n_slots // MOE_BLOCK
    flat_t = jnp.arange(t * TOP_K, dtype=jnp.int32) // TOP_K
    slot_tok = jnp.zeros((n_slots,), jnp.int32).at[slot].set(flat_t)
    block_start = jnp.arange(n_blocks, dtype=jnp.int32) * MOE_BLOCK
    block_e = jnp.minimum(jnp.sum((pad_end[None, :] <= block_start[:, None]).astype(jnp.int32), axis=1),
                          N_EXPERTS - 1)
    n_valid = (pad_end[-1:] // MOE_BLOCK).astype(jnp.int32)
    return slot, slot_tok, block_e, n_valid


def _pad_lanes(v, n=LANES):
    return jnp.pad(v, ((0, 0), (0, n - v.shape[1])))


def kernel(x, mem, norm_mix_w, w_in, conv_w, conv_b, dt_bias, a_log, d_skip, ssd_norm_w, w_ssd_branch, w_sb_branch, w_mix_out, norm_xa_w, norm_mem_w, w_xq, w_xk, w_xv, w_xo, norm_moe_w, w_router_group, w_router_expert, w_expert_gate, w_expert_up, w_expert_down, norm_final_w):
    bsz, seq, d = x.shape
    depth = w_in.shape[0]
    t = bsz * seq
    inner = ssd_norm_w.shape[1]
    heads = dt_bias.shape[1]
    bc = SSD_GROUPS * SSD_STATE
    conv_dim = inner + 2 * bc
    sb_width = SB_HEADS * SB_HEAD_DIM
    col_dt = inner + conv_dim
    n_main = col_dt + 3 * sb_width + 2 * d
    q_col0 = col_dt
    g_col0 = col_dt + 3 * sb_width
    tn_merge = 512

    assert depth == 1, "single-layer configuration: the final RMSNorm is fused into the last MoE combine"
    assert d == SLAB_ROWS * SLAB_COLS
    h = x.reshape(t, d)
    for l in range(depth):
        w_l = w_in[l]
        w_main = jnp.concatenate([w_l[:, :col_dt], w_l[:, col_dt + heads:]], axis=1).astype(BF16)
        w_dt = _pad_lanes(w_l[:, col_dt:col_dt + heads]).astype(BF16)
        e_mat = (jnp.arange(inner, dtype=jnp.int32)[None, :] // SSD_HEAD_DIM
                 == jnp.arange(LANES, dtype=jnp.int32)[:, None]).astype(BF16)
        dskip_x = jnp.repeat(d_skip[l], SSD_HEAD_DIM)[None, :]

        proj, dt_raw = _norm_matmul(h, norm_mix_w[l][None, :], w_main, w_dt, tm=1024, tn=512)
        y_ssd = _ssd(proj, dt_raw, conv_w[l], conv_b[l][None, :], _pad_lanes(dt_bias[l][None, :]),
                     _pad_lanes(a_log[l][None, :]), dskip_x, ssd_norm_w[l][None, :], e_mat,
                     bsz, seq, inner, bc)
        y_sb = _stickbreak(proj.reshape(bsz, seq, n_main), bsz, seq, q_col0 // SB_HEAD_DIM)
        h1 = _merge(h, y_ssd, y_sb.reshape(t, sb_width), proj,
                    w_ssd_branch[l].astype(BF16), w_sb_branch[l].astype(BF16), w_mix_out[l].astype(BF16),
                    g_col0 // tn_merge, tm=512, tn=tn_merge)

        m_len = mem.shape[1]
        w_kv = jnp.concatenate([w_xk[l], w_xv[l]], axis=1).astype(BF16)
        kv, _ = _norm_matmul(mem.reshape(bsz * m_len, d), norm_mem_w[l][None, :], w_kv,
                             jnp.zeros((d, LANES), BF16), tm=bsz * m_len, tn=512)
        w_r = _pad_lanes(jnp.concatenate([w_router_group[l], w_router_expert[l]], axis=1))
        wr_hi = w_r.astype(BF16)
        wr_lo = (w_r - wr_hi.astype(F32)).astype(BF16)
        h2, um, route = _xattn(h1, kv, norm_xa_w[l][None, :], w_xq[l].astype(BF16), w_xo[l].astype(BF16),
                               norm_moe_w[l][None, :], wr_hi, wr_lo, seq, tm=512)

        slot, slot_tok, block_e, n_valid = _slots(route, t)
        yb = _experts(block_e, n_valid, slot_tok, um, w_expert_gate[l], w_expert_up[l], w_expert_down[l])
        h = _combine(slot, h2, route, norm_final_w[None, :], yb, tm=256)
    return h.reshape(bsz, seq, d)
```

```python
import functools

import jax
import jax.numpy as jnp
from jax import lax
from jax.experimental import pallas as pl
from jax.experimental.pallas import tpu as pltpu

F32 = jnp.float32
BF16 = jnp.bfloat16
EPS = 1e-6

SSD_HEAD_DIM = 64
SSD_GROUPS = 4
SSD_STATE = 128
SSD_CONV = 4
SSD_CHUNK = 128
SB_HEADS = 4
SB_HEAD_DIM = 128
SB_BLOCK = 128
SB_SUB = 4
XA_HEADS = 4
XA_HEAD_DIM = 128
N_GROUPS = 4
EXPERTS_PER_GROUP = 8
N_EXPERTS = N_GROUPS * EXPERTS_PER_GROUP
TOP_K = 2
MOE_BLOCK = 256

LANES = 128
SUBLANES = 8
SLAB_ROWS = 16
SLAB_COLS = LANES
MXU_DEPTH = 256
VMEM_LIMIT = 56 * 1024 * 1024

SB_EXIT_LOG = -104.0
NEG_BIG = -1e30


def _cparams(sem):
    return pltpu.CompilerParams(dimension_semantics=sem, vmem_limit_bytes=VMEM_LIMIT)


def _rms(x, w):
    var = jnp.mean(x * x, axis=-1, keepdims=True)
    return x * lax.rsqrt(var + EPS) * w


def _split2(v):
    hi = v.astype(BF16)
    lo = (v - hi.astype(F32)).astype(BF16)
    return hi, lo


def _dot(a, b):
    return jnp.dot(a, b, preferred_element_type=F32)


def _silu(x):
    return x / (1.0 + jnp.exp(-x))


def _softplus(x):
    return jnp.maximum(x, 0.0) + jnp.log(1.0 + jnp.exp(-jnp.abs(x)))


def _norm_matmul_kernel(x_ref, nw_ref, w_ref, ws_ref, o_ref, os_ref, u_ref):
    @pl.when(pl.program_id(1) == 0)
    def _():
        u = _rms(x_ref[...], nw_ref[...]).astype(BF16)
        u_ref[...] = u
        os_ref[...] = _dot(u, ws_ref[...])

    o_ref[...] = _dot(u_ref[...], w_ref[...]).astype(o_ref.dtype)


def _norm_matmul(x, nw, w, w_side, tm, tn):
    m, k = x.shape
    n = w.shape[1]
    ns = w_side.shape[1]
    tm = min(tm, m)
    return pl.pallas_call(
        _norm_matmul_kernel,
        grid=(m // tm, n // tn),
        in_specs=[
            pl.BlockSpec((tm, k), lambda i, j: (i, 0)),
            pl.BlockSpec((1, k), lambda i, j: (0, 0)),
            pl.BlockSpec((k, tn), lambda i, j: (0, j)),
            pl.BlockSpec((k, ns), lambda i, j: (0, 0)),
        ],
        out_specs=[
            pl.BlockSpec((tm, tn), lambda i, j: (i, j)),
            pl.BlockSpec((tm, ns), lambda i, j: (i, 0)),
        ],
        out_shape=[jax.ShapeDtypeStruct((m, n), BF16), jax.ShapeDtypeStruct((m, ns), F32)],
        scratch_shapes=[pltpu.VMEM((tm, k), BF16)],
        compiler_params=_cparams(("parallel", "arbitrary")),
        name="norm_matmul",
    )(x, nw, w, w_side)


def _ssd_kernel(z_ref, xs_ref, b_ref, c_ref, dtr_ref, cw_ref, cb_ref, dtb_ref, alog_ref, dsk_ref,
                nw_ref, e_ref, o_ref,
                tail_ref, state_ref, xact_ref, bact_ref, cact_ref, dtx_ref, eax_ref, wsx_ref, y_ref):
    L = SSD_CHUNK
    inner = xs_ref.shape[1]
    bc = b_ref.shape[1]
    n_pairs = inner // LANES

    @pl.when(pl.program_id(1) == 0)
    def _():
        tail_ref[...] = jnp.zeros_like(tail_ref)
        state_ref[...] = jnp.zeros_like(state_ref)

    row8 = lax.broadcasted_iota(jnp.int32, (SUBLANES, 2 * LANES), 0)

    def conv_seg(src_ref, col0, width, dst_ref):
        for j in range(0, width, 2 * LANES):
            cols = slice(col0 + j, col0 + j + 2 * LANES)
            xin = src_ref[:, j:j + 2 * LANES].astype(F32)
            t8 = tail_ref[:, cols]
            acc = xin * cw_ref[SSD_CONV - 1:SSD_CONV, cols] + cb_ref[:, cols]
            for s in range(1, SSD_CONV):
                r = pltpu.roll(xin, s, 0)
                top = jnp.where(row8 < s, pltpu.roll(t8, s, 0), r[:SUBLANES])
                r = jnp.concatenate([top, r[SUBLANES:]], axis=0)
                acc = acc + r * cw_ref[SSD_CONV - 1 - s:SSD_CONV - s, cols]
            tail_ref[:, cols] = xin[L - SUBLANES:]
            dst_ref[:, j:j + 2 * LANES] = _silu(acc).astype(dst_ref.dtype)

    conv_seg(xs_ref, 0, inner, xact_ref)
    conv_seg(b_ref, inner, bc, bact_ref)
    conv_seg(c_ref, inner + bc, bc, cact_ref)

    dt = _softplus(dtr_ref[...] + dtb_ref[...])
    da = dt * (-jnp.exp(alog_ref[...]))
    rowl = lax.broadcasted_iota(jnp.int32, (L, L), 0)
    coll = lax.broadcasted_iota(jnp.int32, (L, L), 1)
    lower = rowl >= coll
    tri = jnp.where(lower, 1.0, 0.0).astype(BF16)
    d1 = da.astype(BF16)
    r1 = da - d1.astype(F32)
    d2 = r1.astype(BF16)
    d3 = (r1 - d2.astype(F32)).astype(BF16)
    a_cum = _dot(tri, d1) + _dot(tri, d2) + _dot(tri, d3)
    a_cum_t = a_cum.T
    a_last = a_cum[L - 1:L, :]
    e_mat = e_ref[...]

    def expand(v):
        hi, lo = _split2(v)
        return _dot(hi, e_mat) + _dot(lo, e_mat)

    dtx_ref[...] = expand(dt)
    eax_ref[...] = expand(jnp.exp(a_cum))
    wsx_ref[...] = expand(jnp.exp(a_last - a_cum))
    elx = expand(jnp.broadcast_to(jnp.exp(a_last), (SUBLANES, LANES)))[0:1]

    lane = lax.broadcasted_iota(jnp.int32, (L, LANES), 1)
    pairs_per_group = n_pairs // SSD_GROUPS
    for g in range(SSD_GROUPS):
        gcols = slice(g * SSD_STATE, (g + 1) * SSD_STATE)
        bg_t = bact_ref[:, gcols].T.astype(BF16)
        cg = cact_ref[:, gcols]
        cb = _dot(cg, bg_t)
        for pp in range(pairs_per_group):
            p = g * pairs_per_group + pp
            pc = slice(p * LANES, (p + 1) * LANES)
            ms = []
            for hh in (2 * p, 2 * p + 1):
                seg = a_cum[:, hh:hh + 1] - a_cum_t[hh:hh + 1, :]
                dec = jnp.exp(jnp.where(lower, seg, NEG_BIG))
                ms.append((cb * dec).astype(BF16))
            lhs = jnp.concatenate(ms, axis=1)
            xs_p = xact_ref[:, pc]
            xdt = xs_p * dtx_ref[:, pc]
            rhs = jnp.concatenate([jnp.where(lane < SSD_HEAD_DIM, xdt, 0.0).astype(BF16),
                                   jnp.where(lane >= SSD_HEAD_DIM, xdt, 0.0).astype(BF16)], axis=0)
            st = state_ref[p]
            y = _dot(lhs, rhs)
            y = y + _dot(cg, st.astype(BF16)) * eax_ref[:, pc]
            y = y + xs_p * dsk_ref[:, pc]
            y_ref[:, pc] = y
            xw = (xdt * wsx_ref[:, pc]).astype(BF16)
            state_ref[p] = st * elx[:, pc] + _dot(bg_t, xw)

    yg = y_ref[...] * _silu(z_ref[...].astype(F32))
    o_ref[...] = _rms(yg, nw_ref[...]).astype(o_ref.dtype)


def _ssd(proj, dt_raw, conv_w, conv_b, dt_bias_p, a_log_p, dskip_x, norm_w, e_mat, bsz, seq, inner, bc):
    L = SSD_CHUNK
    nc = seq // L
    conv_dim = inner + 2 * bc
    n_pairs = inner // LANES
    ib = inner // bc

    def rows(b, c):
        return b * nc + c

    return pl.pallas_call(
        _ssd_kernel,
        grid=(bsz, nc),
        in_specs=[
            pl.BlockSpec((L, inner), lambda b, c: (rows(b, c), 0)),
            pl.BlockSpec((L, inner), lambda b, c: (rows(b, c), 1)),
            pl.BlockSpec((L, bc), lambda b, c: (rows(b, c), 2 * ib)),
            pl.BlockSpec((L, bc), lambda b, c: (rows(b, c), 2 * ib + 1)),
            pl.BlockSpec((L, LANES), lambda b, c: (rows(b, c), 0)),
            pl.BlockSpec((SSD_CONV, conv_dim), lambda b, c: (0, 0)),
            pl.BlockSpec((1, conv_dim), lambda b, c: (0, 0)),
            pl.BlockSpec((1, LANES), lambda b, c: (0, 0)),
            pl.BlockSpec((1, LANES), lambda b, c: (0, 0)),
            pl.BlockSpec((1, inner), lambda b, c: (0, 0)),
            pl.BlockSpec((1, inner), lambda b, c: (0, 0)),
            pl.BlockSpec((LANES, inner), lambda b, c: (0, 0)),
        ],
        out_specs=pl.BlockSpec((L, inner), lambda b, c: (rows(b, c), 0)),
        out_shape=jax.ShapeDtypeStruct((bsz * seq, inner), BF16),
        scratch_shapes=[
            pltpu.VMEM((SUBLANES, conv_dim), F32),
            pltpu.VMEM((n_pairs, SSD_STATE, LANES), F32),
            pltpu.VMEM((L, inner), F32),
            pltpu.VMEM((L, bc), F32),
            pltpu.VMEM((L, bc), BF16),
            pltpu.VMEM((L, inner), F32),
            pltpu.VMEM((L, inner), F32),
            pltpu.VMEM((L, inner), F32),
            pltpu.VMEM((L, inner), F32),
        ],
        compiler_params=_cparams(("parallel", "arbitrary")),
        name="ssd",
    )(proj, proj, proj, proj, dt_raw, conv_w, conv_b, dt_bias_p, a_log_p, dskip_x, norm_w, e_mat)


def _sb_kernel(q_ref, k_ref, v_ref, o_ref, acc_ref, carry_ref):
    BL = SB_BLOCK
    i = pl.program_id(2)
    scale = SB_HEAD_DIM ** -0.5
    row = lax.broadcasted_iota(jnp.int32, (BL, BL), 0)
    col = lax.broadcasted_iota(jnp.int32, (BL, BL), 1)
    causal = col < row
    r2 = lax.broadcasted_iota(jnp.int32, (BL, 2 * BL), 0)
    c2 = lax.broadcasted_iota(jnp.int32, (BL, 2 * BL), 1)
    tri = jnp.where((r2 > c2) | (c2 >= BL), 1.0, 0.0).astype(BF16)

    def sweep(s, first):
        subs = range(SB_SUB)
        kbs = [i * SB_SUB + j - s for j in subs]
        starts = [pl.multiple_of(jnp.maximum(kb, 0) * BL, BL) for kb in kbs]
        zls = [lax.dot_general(q_ref[j * BL:(j + 1) * BL, :], k_ref[pl.ds(starts[j], BL), :],
                               (((1,), (1,)), ((), ())), preferred_element_type=F32) * scale for j in subs]
        sps = [_softplus(zl) for zl in zls]
        log_betas = [zl - sp for zl, sp in zip(zls, sps)]
        log_keeps = [jnp.where(causal, -sp, 0.0) if first else -sp for sp in sps]
        splits = [_split2(lk) for lk in log_keeps]
        t2s = [_dot(hi, tri) + _dot(lo, tri) for hi, lo in splits]
        if first:
            atts = [jnp.where(causal, jnp.exp(lb + t2[:, :BL]), 0.0) for lb, t2 in zip(log_betas, t2s)]
            carries = [t2[:, BL:] for t2 in t2s]
        else:
            olds = [carry_ref[j] for j in subs]
            atts = [jnp.where(kb >= 0, jnp.exp(lb + t2[:, :BL] + old), 0.0)
                    for kb, lb, t2, old in zip(kbs, log_betas, t2s, olds)]
            carries = [old + jnp.where(kb >= 0, t2[:, BL:], 0.0) for kb, t2, old in zip(kbs, t2s, olds)]
        pvs = [_dot(atts[j].astype(BF16), v_ref[pl.ds(starts[j], BL), :]) for j in subs]
        worst = None
        for j in subs:
            if first:
                acc_ref[j] = pvs[j]
            else:
                acc_ref[j] += pvs[j]
            carry_ref[j] = carries[j]
            live = jnp.where(kbs[j] >= 1, carries[j], NEG_BIG)
            worst = live if worst is None else jnp.maximum(worst, live)
        return jnp.max(worst) > SB_EXIT_LOG

    sweep(0, True)
    go = sweep(1, False)

    def body(st):
        s, _ = st
        return s + 1, sweep(s, False)

    lax.while_loop(lambda st: st[1], body, (jnp.int32(2), go))
    for j in range(SB_SUB):
        o_ref[j * BL:(j + 1) * BL, :] = acc_ref[j].astype(o_ref.dtype)


def _stickbreak(proj3, bsz, seq, q_blk0):
    tq = SB_SUB * SB_BLOCK
    return pl.pallas_call(
        _sb_kernel,
        grid=(bsz, SB_HEADS, seq // tq),
        in_specs=[
            pl.BlockSpec((None, tq, SB_HEAD_DIM), lambda b, h, i: (b, i, q_blk0 + h)),
            pl.BlockSpec((None, seq, SB_HEAD_DIM), lambda b, h, i: (b, 0, q_blk0 + SB_HEADS + h)),
            pl.BlockSpec((None, seq, SB_HEAD_DIM), lambda b, h, i: (b, 0, q_blk0 + 2 * SB_HEADS + h)),
        ],
        out_specs=pl.BlockSpec((None, tq, SB_HEAD_DIM), lambda b, h, i: (b, i, h)),
        out_shape=jax.ShapeDtypeStruct((bsz, seq, SB_HEADS * SB_HEAD_DIM), BF16),
        scratch_shapes=[pltpu.VMEM((SB_SUB, SB_BLOCK, SB_HEAD_DIM), F32),
                        pltpu.VMEM((SB_SUB, SB_BLOCK, SB_BLOCK), F32)],
        compiler_params=_cparams(("parallel", "parallel", "arbitrary")),
        name="stickbreak",
    )(proj3, proj3, proj3)


def _merge_kernel(x_ref, ys_ref, yb_ref, g1_ref, g2_ref, w1_ref, w2_ref, wm_ref, o_ref, acc_ref):
    j = pl.program_id(1)

    @pl.when(j == 0)
    def _():
        acc_ref[...] = jnp.zeros_like(acc_ref)

    a = _dot(ys_ref[...], w1_ref[...])
    b = _dot(yb_ref[...], w2_ref[...])
    s1 = jax.nn.sigmoid(g1_ref[...].astype(F32))
    s2 = jax.nn.sigmoid(g2_ref[...].astype(F32))
    m = (s1 * a + s2 * b).astype(BF16)
    acc_ref[...] += _dot(m, wm_ref[...])

    @pl.when(j == pl.num_programs(1) - 1)
    def _():
        o_ref[...] = x_ref[...] + acc_ref[...]


def _merge(x2, y_ssd, y_sb, proj, w1, w2, wm, g_blk0, tm, tn):
    t, d = x2.shape
    tm = min(tm, t)
    nj = d // tn
    return pl.pallas_call(
        _merge_kernel,
        grid=(t // tm, nj),
        in_specs=[
            pl.BlockSpec((tm, d), lambda i, j: (i, 0)),
            pl.BlockSpec((tm, y_ssd.shape[1]), lambda i, j: (i, 0)),
            pl.BlockSpec((tm, y_sb.shape[1]), lambda i, j: (i, 0)),
            pl.BlockSpec((tm, tn), lambda i, j: (i, g_blk0 + j)),
            pl.BlockSpec((tm, tn), lambda i, j: (i, g_blk0 + nj + j)),
            pl.BlockSpec((w1.shape[0], tn), lambda i, j: (0, j)),
            pl.BlockSpec((w2.shape[0], tn), lambda i, j: (0, j)),
            pl.BlockSpec((tn, d), lambda i, j: (j, 0)),
        ],
        out_specs=pl.BlockSpec((tm, d), lambda i, j: (i, 0)),
        out_shape=jax.ShapeDtypeStruct((t, d), F32),
        scratch_shapes=[pltpu.VMEM((tm, d), F32)],
        compiler_params=_cparams(("parallel", "arbitrary")),
        name="merge",
    )(x2, y_ssd, y_sb, proj, proj, w1, w2, wm)


def _xattn_kernel(h_ref, kv_ref, nxa_ref, wq_ref, wo_ref, nmoe_ref, wrh_ref, wrl_ref,
                  h2_ref, um_ref, rt_ref):
    h1 = h_ref[...]
    un = _rms(h1, nxa_ref[...]).astype(BF16)
    q = _dot(un, wq_ref[...]).astype(BF16)
    width = XA_HEADS * XA_HEAD_DIM
    outs = []
    for hd in range(XA_HEADS):
        cs = slice(hd * XA_HEAD_DIM, (hd + 1) * XA_HEAD_DIM)
        k = kv_ref[:, cs]
        v = kv_ref[:, width + hd * XA_HEAD_DIM: width + (hd + 1) * XA_HEAD_DIM]
        sc = lax.dot_general(q[:, cs], k, (((1,), (1,)), ((), ())),
                             preferred_element_type=F32) * (XA_HEAD_DIM ** -0.5)
        sc = sc - jnp.max(sc, axis=-1, keepdims=True)
        p = jnp.exp(sc)
        p = p / jnp.sum(p, axis=-1, keepdims=True)
        outs.append(_dot(p.astype(BF16), v).astype(BF16))
    o = jnp.concatenate(outs, axis=1)
    h2 = h1 + _dot(o, wo_ref[...])
    h2_ref[...] = h2
    um = _rms(h2, nmoe_ref[...])
    tm = um.shape[0]
    for c in range(SLAB_ROWS):
        um_ref[pl.ds(c, tm, stride=SLAB_ROWS), :] = um[:, c * SLAB_COLS:(c + 1) * SLAB_COLS]
    hi, lo = _split2(um)
    lg = _dot(hi, wrh_ref[...]) + _dot(lo, wrh_ref[...]) + _dot(hi, wrl_ref[...])

    lane = lax.broadcasted_iota(jnp.int32, lg.shape, 1)
    lane_f = lane.astype(F32)
    is_g = lane < N_GROUPS
    gl = jnp.where(is_g, lg, NEG_BIG)
    gmax = jnp.max(gl, axis=-1, keepdims=True)
    g_sel = jnp.min(jnp.where(gl == gmax, lane_f, float(LANES)), axis=-1, keepdims=True)
    g_gate = 1.0 / jnp.sum(jnp.where(is_g, jnp.exp(gl - gmax), 0.0), axis=-1, keepdims=True)
    lo_lane = N_GROUPS + EXPERTS_PER_GROUP * g_sel
    el = jnp.where((lane_f >= lo_lane) & (lane_f < lo_lane + EXPERTS_PER_GROUP), lg, NEG_BIG)
    m1 = jnp.max(el, axis=-1, keepdims=True)
    i1 = jnp.min(jnp.where(el == m1, lane_f, float(LANES)), axis=-1, keepdims=True)
    el2 = jnp.where(lane_f == i1, NEG_BIG, el)
    m2 = jnp.max(el2, axis=-1, keepdims=True)
    i2 = jnp.min(jnp.where(el2 == m2, lane_f, float(LANES)), axis=-1, keepdims=True)
    ex = jnp.exp(m2 - m1)
    p1 = 1.0 / (1.0 + ex)
    rt_ref[...] = jnp.where(lane == 0, i1 - N_GROUPS,
                            jnp.where(lane == 1, i2 - N_GROUPS,
                                      jnp.where(lane == 2, p1 * g_gate,
                                                jnp.where(lane == 3, ex * p1 * g_gate, 0.0))))


def _xattn(h1, kv, nxa, wq, wo, nmoe, wr_hi, wr_lo, seq, tm):
    t, d = h1.shape
    tm = min(tm, seq)
    per_b = seq // tm
    m_len = kv.shape[0] // (t // seq)
    nr = wr_hi.shape[1]
    return pl.pallas_call(
        _xattn_kernel,
        grid=(t // tm,),
        in_specs=[
            pl.BlockSpec((tm, d), lambda i: (i, 0)),
            pl.BlockSpec((m_len, kv.shape[1]), lambda i: (i // per_b, 0)),
            pl.BlockSpec((1, d), lambda i: (0, 0)),
            pl.BlockSpec(wq.shape, lambda i: (0, 0)),
            pl.BlockSpec(wo.shape, lambda i: (0, 0)),
            pl.BlockSpec((1, d), lambda i: (0, 0)),
            pl.BlockSpec((d, nr), lambda i: (0, 0)),
            pl.BlockSpec((d, nr), lambda i: (0, 0)),
        ],
        out_specs=[
            pl.BlockSpec((tm, d), lambda i: (i, 0)),
            pl.BlockSpec((tm * SLAB_ROWS, SLAB_COLS), lambda i: (i, 0)),
            pl.BlockSpec((tm, nr), lambda i: (i, 0)),
        ],
        out_shape=[jax.ShapeDtypeStruct((t, d), F32), jax.ShapeDtypeStruct((t * SLAB_ROWS, SLAB_COLS), F32),
                   jax.ShapeDtypeStruct((t, nr), F32)],
        compiler_params=_cparams(("parallel",)),
        name="xattn",
    )(h1, kv, nxa, wq, wo, nmoe, wr_hi, wr_lo)


def _slab_copy(src_hbm, row, dst_ref, r, sem):
    return pltpu.make_async_copy(src_hbm.at[pl.ds(pl.multiple_of(row * SLAB_ROWS, SLAB_ROWS), SLAB_ROWS)],
                                 dst_ref.at[pl.ds(r * SLAB_ROWS, SLAB_ROWS)], sem)


def _slab_chunk(ref, c, rows):
    return ref[pl.ds(c, rows, stride=SLAB_ROWS), :]


def _expert_kernel(be_ref, nv_ref, tok_ref, um_hbm, wg_ref, wu_ref, wd_ref, o_ref,
                   xb_ref, xbb_ref, wgb_ref, wub_ref, wdb_ref, sem):
    i = pl.program_id(0)
    last = pl.num_programs(0) - 1

    def gather(blk, buf):
        base = blk * MOE_BLOCK
        for r in range(MOE_BLOCK):
            _slab_copy(um_hbm, tok_ref[base + r], xb_ref.at[buf], r, sem.at[buf]).start(priority=r % 2)

    def wait(buf):
        pltpu.make_async_copy(um_hbm.at[pl.ds(0, MOE_BLOCK * SLAB_ROWS)], xb_ref.at[buf], sem.at[buf]).wait()

    @pl.when(i == 0)
    def _():
        gather(i, 0)

    @pl.when(jnp.logical_or(i == 0, be_ref[i] != be_ref[jnp.maximum(i - 1, 0)]))
    def _():
        wgb_ref[...] = wg_ref[...].astype(BF16)
        wub_ref[...] = wu_ref[...].astype(BF16)
        wdb_ref[...] = wd_ref[...].astype(BF16)

    buf = i % 2
    nxt = jnp.minimum(i + 1, last)

    @pl.when(i < nv_ref[0])
    def _():
        wait(buf)
        for c in range(SLAB_ROWS):
            xbb_ref[:, c * SLAB_COLS:(c + 1) * SLAB_COLS] = _slab_chunk(
                xb_ref.at[buf], c, MOE_BLOCK).astype(BF16)
        gather(nxt, 1 - buf)
        xb = xbb_ref[...]
        hid = (_silu(_dot(xb, wgb_ref[...])) * _dot(xb, wub_ref[...])).astype(BF16)
        y = _dot(hid, wdb_ref[...])
        for c in range(SLAB_ROWS):
            o_ref[pl.ds(c, MOE_BLOCK, stride=SLAB_ROWS), :] = y[:, c * SLAB_COLS:(c + 1) * SLAB_COLS]

    @pl.when(i >= nv_ref[0])
    def _():
        wait(buf)
        gather(nxt, 1 - buf)
        o_ref[...] = jnp.zeros_like(o_ref)

    @pl.when(i == last)
    def _():
        wait(1 - buf)


def _experts(block_e, n_valid, slot_tok, um, wg, wu, wd):
    n_slots = slot_tok.shape[0]
    d, ff = wg.shape[1:]
    rows = MOE_BLOCK * SLAB_ROWS
    grid_spec = pltpu.PrefetchScalarGridSpec(
        num_scalar_prefetch=3,
        grid=(n_slots // MOE_BLOCK,),
        in_specs=[
            pl.BlockSpec(memory_space=pl.ANY),
            pl.BlockSpec((None, d, ff), lambda i, be, nv, tok: (be[i], 0, 0)),
            pl.BlockSpec((None, d, ff), lambda i, be, nv, tok: (be[i], 0, 0)),
            pl.BlockSpec((None, ff, d), lambda i, be, nv, tok: (be[i], 0, 0)),
        ],
        out_specs=pl.BlockSpec((rows, SLAB_COLS), lambda i, be, nv, tok: (i, 0)),
        scratch_shapes=[pltpu.VMEM((2, rows, SLAB_COLS), F32),
                        pltpu.VMEM((MOE_BLOCK, d), BF16),
                        pltpu.VMEM((d, ff), BF16), pltpu.VMEM((d, ff), BF16), pltpu.VMEM((ff, d), BF16),
                        pltpu.SemaphoreType.DMA((2,))],
    )
    return pl.pallas_call(
        _expert_kernel,
        grid_spec=grid_spec,
        out_shape=jax.ShapeDtypeStruct((n_slots * SLAB_ROWS, SLAB_COLS), F32),
        compiler_params=_cparams(("arbitrary",)),
        name="experts",
    )(block_e, n_valid, slot_tok, um, wg, wu, wd)


def _combine_kernel(slot_ref, h_ref, w_ref, nw_ref, yb_hbm, o_ref, ya0_ref, ya1_ref, yb0_ref, yb1_ref, sem):
    i = pl.program_id(0)
    tm = h_ref.shape[0]
    last = pl.num_programs(0) - 1
    bufs = ((ya0_ref, ya1_ref), (yb0_ref, yb1_ref))

    def gather(blk, par):
        base = blk * tm * TOP_K
        for r in range(tm):
            for k in range(TOP_K):
                _slab_copy(yb_hbm, slot_ref[base + TOP_K * r + k], bufs[par][k], r, sem.at[par]).start(priority=k)

    def wait(par):
        for k in range(TOP_K):
            pltpu.make_async_copy(yb_hbm.at[pl.ds(0, tm * SLAB_ROWS)], bufs[par][k], sem.at[par]).wait()

    @pl.when(i == 0)
    def _():
        gather(i, 0)

    def step(par):
        wait(par)
        gather(jnp.minimum(i + 1, last), 1 - par)
        w0 = w_ref[:, TOP_K:TOP_K + 1]
        w1 = w_ref[:, TOP_K + 1:TOP_K + 2]
        sq = jnp.zeros((tm, SLAB_COLS), F32)
        for c in range(SLAB_ROWS):
            cs = slice(c * SLAB_COLS, (c + 1) * SLAB_COLS)
            h3 = (h_ref[:, cs] + w0 * _slab_chunk(bufs[par][0], c, tm)
                  + w1 * _slab_chunk(bufs[par][1], c, tm))
            sq = sq + h3 * h3
            o_ref[:, cs] = h3
        inv = lax.rsqrt(jnp.sum(sq, axis=-1, keepdims=True) / h_ref.shape[1] + EPS)
        o_ref[...] = o_ref[...] * inv * nw_ref[...]

        @pl.when(i == last)
        def _():
            wait(1 - par)

    for par in range(2):
        pl.when(i % 2 == par)(functools.partial(step, par))


def _combine(slot, h2, route, nw, yb, tm):
    t, d = h2.shape
    tm = min(tm, t)
    grid_spec = pltpu.PrefetchScalarGridSpec(
        num_scalar_prefetch=1,
        grid=(t // tm,),
        in_specs=[
            pl.BlockSpec((tm, d), lambda i, s: (i, 0)),
            pl.BlockSpec((tm, route.shape[1]), lambda i, s: (i, 0)),
            pl.BlockSpec((1, d), lambda i, s: (0, 0)),
            pl.BlockSpec(memory_space=pl.ANY),
        ],
        out_specs=pl.BlockSpec((tm, d), lambda i, s: (i, 0)),
        scratch_shapes=([pltpu.VMEM((tm * SLAB_ROWS, SLAB_COLS), F32)] * (2 * TOP_K)
                        + [pltpu.SemaphoreType.DMA((2,))]),
    )
    return pl.pallas_call(
        _combine_kernel,
        grid_spec=grid_spec,
        out_shape=jax.ShapeDtypeStruct((t, d), F32),
        compiler_params=_cparams(("arbitrary",)),
        name="combine",
    )(slot, h2, route, nw, yb)


def _slots(route, t):
    flat_e = route[:, :TOP_K].astype(jnp.int32).reshape(-1)
    n_pairs = t * TOP_K
    tile = min(256, n_pairs)
    onehot = flat_e[:, None] == jnp.arange(N_EXPERTS, dtype=jnp.int32)[None, :]
    oh3 = onehot.reshape(n_pairs // tile, tile, N_EXPERTS)
    tri = (jnp.arange(tile)[:, None] >= jnp.arange(tile)[None, :]).astype(BF16)
    within = jnp.einsum('ij,tjk->tik', tri, oh3.astype(BF16), preferred_element_type=F32)
    tot = within[:, -1, :]
    tile_base = jnp.cumsum(tot, axis=0) - tot
    csum = (within + tile_base[:, None, :]).reshape(n_pairs, N_EXPERTS)
    rank = jnp.sum(jnp.where(onehot, csum, 0.0), axis=1).astype(jnp.int32) - 1
    counts = (tile_base[-1] + tot[-1]).astype(jnp.int32)
    padded = (counts + MOE_BLOCK - 1) // MOE_BLOCK * MOE_BLOCK
    pad_end = jnp.cumsum(padded)
    pad_start = pad_end - padded
    slot = (pad_start[flat_e] + rank).astype(jnp.int32)
    n_slots = t * TOP_K + N_EXPERTS * MOE_BLOCK
    n_blocks = n_slots // MOE_BLOCK
    flat_t = jnp.arange(t * TOP_K, dtype=jnp.int32) // TOP_K
    slot_tok = jnp.zeros((n_slots,), jnp.int32).at[slot].set(flat_t)
    block_start = jnp.arange(n_blocks, dtype=jnp.int32) * MOE_BLOCK
    block_e = jnp.minimum(jnp.sum((pad_end[None, :] <= block_start[:, None]).astype(jnp.int32), axis=1),
                          N_EXPERTS - 1)
    n_valid = (pad_end[-1:] // MOE_BLOCK).astype(jnp.int32)
    return slot, slot_tok, block_e, n_valid


def _pad_lanes(v, n=LANES):
    return jnp.pad(v, ((0, 0), (0, n - v.shape[1])))


def kernel(x, mem, norm_mix_w, w_in, conv_w, conv_b, dt_bias, a_log, d_skip, ssd_norm_w, w_ssd_branch, w_sb_branch, w_mix_out, norm_xa_w, norm_mem_w, w_xq, w_xk, w_xv, w_xo, norm_moe_w, w_router_group, w_router_expert, w_expert_gate, w_expert_up, w_expert_down, norm_final_w):
    bsz, seq, d = x.shape
    depth = w_in.shape[0]
    t = bsz * seq
    inner = ssd_norm_w.shape[1]
    heads = dt_bias.shape[1]
    bc = SSD_GROUPS * SSD_STATE
    conv_dim = inner + 2 * bc
    sb_width = SB_HEADS * SB_HEAD_DIM
    col_dt = inner + conv_dim
    n_main = col_dt + 3 * sb_width + 2 * d
    q_col0 = col_dt
    g_col0 = col_dt + 3 * sb_width
    tn_merge = 512

    assert depth == 1, "single-layer configuration: the final RMSNorm is fused into the last MoE combine"
    assert d == SLAB_ROWS * SLAB_COLS
    h = x.reshape(t, d)
    for l in range(depth):
        w_l = w_in[l]
        w_main = jnp.concatenate([w_l[:, :col_dt], w_l[:, col_dt + heads:]], axis=1).astype(BF16)
        w_dt = _pad_lanes(w_l[:, col_dt:col_dt + heads]).astype(BF16)
        e_mat = (jnp.arange(inner, dtype=jnp.int32)[None, :] // SSD_HEAD_DIM
                 == jnp.arange(LANES, dtype=jnp.int32)[:, None]).astype(BF16)
        dskip_x = jnp.repeat(d_skip[l], SSD_HEAD_DIM)[None, :]

        proj, dt_raw = _norm_matmul(h, norm_mix_w[l][None, :], w_main, w_dt, tm=1024, tn=512)
        y_ssd = _ssd(proj, dt_raw, conv_w[l], conv_b[l][None, :], _pad_lanes(dt_bias[l][None, :]),
                     _pad_lanes(a_log[l][None, :]), dskip_x, ssd_norm_w[l][None, :], e_mat,
                     bsz, seq, inner, bc)
        y_sb = _stickbreak(proj.reshape(bsz, seq, n_main), bsz, seq, q_col0 // SB_HEAD_DIM)
        h1 = _merge(h, y_ssd, y_sb.reshape(t, sb_width), proj,
                    w_ssd_branch[l].astype(BF16), w_sb_branch[l].astype(BF16), w_mix_out[l].astype(BF16),
                    g_col0 // tn_merge, tm=512, tn=tn_merge)

        m_len = mem.shape[1]
        w_kv = jnp.concatenate([w_xk[l], w_xv[l]], axis=1).astype(BF16)
        kv, _ = _norm_matmul(mem.reshape(bsz * m_len, d), norm_mem_w[l][None, :], w_kv,
                             jnp.zeros((d, LANES), BF16), tm=bsz * m_len, tn=512)
        w_r = _pad_lanes(jnp.concatenate([w_router_group[l], w_router_expert[l]], axis=1))
        wr_hi = w_r.astype(BF16)
        wr_lo = (w_r - wr_hi.astype(F32)).astype(BF16)
        h2, um, route = _xattn(h1, kv, norm_xa_w[l][None, :], w_xq[l].astype(BF16), w_xo[l].astype(BF16),
                               norm_moe_w[l][None, :], wr_hi, wr_lo, seq, tm=512)

        slot, slot_tok, block_e, n_valid = _slots(route, t)
        yb = _experts(block_e, n_valid, slot_tok, um, w_expert_gate[l], w_expert_up[l], w_expert_down[l])
        h = _combine(slot, h2, route, norm_final_w[None, :], yb, tm=256)
    return h.reshape(bsz, seq, d)
```

```python
import functools

import jax
import jax.numpy as jnp
from jax import lax
from jax.experimental import pallas as pl
from jax.experimental.pallas import tpu as pltpu

F32 = jnp.float32
BF16 = jnp.bfloat16
EPS = 1e-6

SSD_HEAD_DIM = 64
SSD_GROUPS = 4
SSD_STATE = 128
SSD_CONV = 4
SSD_CHUNK = 128
SB_HEADS = 4
SB_HEAD_DIM = 128
SB_BLOCK = 128
SB_SUB = 4
XA_HEADS = 4
XA_HEAD_DIM = 128
N_GROUPS = 4
EXPERTS_PER_GROUP = 8
N_EXPERTS = N_GROUPS * EXPERTS_PER_GROUP
TOP_K = 2
MOE_BLOCK = 256

LANES = 128
SUBLANES = 8
VMEM_LIMIT = 56 * 1024 * 1024

SB_EXIT_LOG = -104.0
NEG_BIG = -1e30


def _cparams(sem):
    return pltpu.CompilerParams(dimension_semantics=sem, vmem_limit_bytes=VMEM_LIMIT)


def _rms(x, w):
    var = jnp.mean(x * x, axis=-1, keepdims=True)
    return x * lax.rsqrt(var + EPS) * w


def _split2(v):
    hi = v.astype(BF16)
    lo = (v - hi.astype(F32)).astype(BF16)
    return hi, lo


def _dot(a, b):
    return jnp.dot(a, b, preferred_element_type=F32)


def _silu(x):
    return x / (1.0 + jnp.exp(-x))


def _softplus(x):
    return jnp.maximum(x, 0.0) + jnp.log(1.0 + jnp.exp(-jnp.abs(x)))


def _norm_matmul_kernel(x_ref, nw_ref, w_ref, ws_ref, o_ref, os_ref, u_ref):
    @pl.when(pl.program_id(1) == 0)
    def _():
        u = _rms(x_ref[...], nw_ref[...]).astype(BF16)
        u_ref[...] = u
        os_ref[...] = _dot(u, ws_ref[...])

    o_ref[...] = _dot(u_ref[...], w_ref[...]).astype(o_ref.dtype)


def _norm_matmul(x, nw, w, w_side, tm, tn):
    m, k = x.shape
    n = w.shape[1]
    ns = w_side.shape[1]
    tm = min(tm, m)
    return pl.pallas_call(
        _norm_matmul_kernel,
        grid=(m // tm, n // tn),
        in_specs=[
            pl.BlockSpec((tm, k), lambda i, j: (i, 0)),
            pl.BlockSpec((1, k), lambda i, j: (0, 0)),
            pl.BlockSpec((k, tn), lambda i, j: (0, j)),
            pl.BlockSpec((k, ns), lambda i, j: (0, 0)),
        ],
        out_specs=[
            pl.BlockSpec((tm, tn), lambda i, j: (i, j)),
            pl.BlockSpec((tm, ns), lambda i, j: (i, 0)),
        ],
        out_shape=[jax.ShapeDtypeStruct((m, n), BF16), jax.ShapeDtypeStruct((m, ns), F32)],
        scratch_shapes=[pltpu.VMEM((tm, k), BF16)],
        compiler_params=_cparams(("parallel", "arbitrary")),
        name="norm_matmul",
    )(x, nw, w, w_side)


def _ssd_kernel(z_ref, xs_ref, b_ref, c_ref, dtr_ref, cw_ref, cb_ref, dtb_ref, alog_ref, dsk_ref,
                nw_ref, e_ref, o_ref,
                tail_ref, state_ref, xact_ref, bact_ref, cact_ref, dtx_ref, eax_ref, wsx_ref, y_ref):
    L = SSD_CHUNK
    inner = xs_ref.shape[1]
    bc = b_ref.shape[1]
    n_pairs = inner // LANES

    @pl.when(pl.program_id(1) == 0)
    def _():
        tail_ref[...] = jnp.zeros_like(tail_ref)
        state_ref[...] = jnp.zeros_like(state_ref)

    row8 = lax.broadcasted_iota(jnp.int32, (SUBLANES, 2 * LANES), 0)

    def conv_seg(src_ref, col0, width, dst_ref):
        for j in range(0, width, 2 * LANES):
            cols = slice(col0 + j, col0 + j + 2 * LANES)
            xin = src_ref[:, j:j + 2 * LANES].astype(F32)
            t8 = tail_ref[:, cols]
            acc = xin * cw_ref[SSD_CONV - 1:SSD_CONV, cols] + cb_ref[:, cols]
            for s in range(1, SSD_CONV):
                r = pltpu.roll(xin, s, 0)
                top = jnp.where(row8 < s, pltpu.roll(t8, s, 0), r[:SUBLANES])
                r = jnp.concatenate([top, r[SUBLANES:]], axis=0)
                acc = acc + r * cw_ref[SSD_CONV - 1 - s:SSD_CONV - s, cols]
            tail_ref[:, cols] = xin[L - SUBLANES:]
            dst_ref[:, j:j + 2 * LANES] = _silu(acc).astype(dst_ref.dtype)

    conv_seg(xs_ref, 0, inner, xact_ref)
    conv_seg(b_ref, inner, bc, bact_ref)
    conv_seg(c_ref, inner + bc, bc, cact_ref)

    dt = _softplus(dtr_ref[...] + dtb_ref[...])
    da = dt * (-jnp.exp(alog_ref[...]))
    rowl = lax.broadcasted_iota(jnp.int32, (L, L), 0)
    coll = lax.broadcasted_iota(jnp.int32, (L, L), 1)
    lower = rowl >= coll
    tri = jnp.where(lower, 1.0, 0.0).astype(BF16)
    d1 = da.astype(BF16)
    r1 = da - d1.astype(F32)
    d2 = r1.astype(BF16)
    d3 = (r1 - d2.astype(F32)).astype(BF16)
    a_cum = _dot(tri, d1) + _dot(tri, d2) + _dot(tri, d3)
    a_cum_t = a_cum.T
    a_last = a_cum[L - 1:L, :]
    e_mat = e_ref[...]

    def expand(v):
        hi, lo = _split2(v)
        return _dot(hi, e_mat) + _dot(lo, e_mat)

    dtx_ref[...] = expand(dt)
    eax_ref[...] = expand(jnp.exp(a_cum))
    wsx_ref[...] = expand(jnp.exp(a_last - a_cum))
    elx = expand(jnp.broadcast_to(jnp.exp(a_last), (SUBLANES, LANES)))[0:1]

    lane = lax.broadcasted_iota(jnp.int32, (L, LANES), 1)
    pairs_per_group = n_pairs // SSD_GROUPS
    for g in range(SSD_GROUPS):
        gcols = slice(g * SSD_STATE, (g + 1) * SSD_STATE)
        bg_t = bact_ref[:, gcols].T.astype(BF16)
        cg = cact_ref[:, gcols]
        cb = _dot(cg, bg_t)
        for pp in range(pairs_per_group):
            p = g * pairs_per_group + pp
            pc = slice(p * LANES, (p + 1) * LANES)
            ms = []
            for hh in (2 * p, 2 * p + 1):
                seg = a_cum[:, hh:hh + 1] - a_cum_t[hh:hh + 1, :]
                dec = jnp.exp(jnp.where(lower, seg, NEG_BIG))
                ms.append((cb * dec).astype(BF16))
            lhs = jnp.concatenate(ms, axis=1)
            xs_p = xact_ref[:, pc]
            xdt = xs_p * dtx_ref[:, pc]
            rhs = jnp.concatenate([jnp.where(lane < SSD_HEAD_DIM, xdt, 0.0).astype(BF16),
                                   jnp.where(lane >= SSD_HEAD_DIM, xdt, 0.0).astype(BF16)], axis=0)
            st = state_ref[p]
            y = _dot(lhs, rhs)
            y = y + _dot(cg, st.astype(BF16)) * eax_ref[:, pc]
            y = y + xs_p * dsk_ref[:, pc]
            y_ref[:, pc] = y
            xw = (xdt * wsx_ref[:, pc]).astype(BF16)
            state_ref[p] = st * elx[:, pc] + _dot(bg_t, xw)

    yg = y_ref[...] * _silu(z_ref[...].astype(F32))
    o_ref[...] = _rms(yg, nw_ref[...]).astype(o_ref.dtype)


def _ssd(proj, dt_raw, conv_w, conv_b, dt_bias_p, a_log_p, dskip_x, norm_w, e_mat, bsz, seq, inner, bc):
    L = SSD_CHUNK
    nc = seq // L
    conv_dim = inner + 2 * bc
    n_pairs = inner // LANES
    ib = inner // bc

    def rows(b, c):
        return b * nc + c

    return pl.pallas_call(
        _ssd_kernel,
        grid=(bsz, nc),
        in_specs=[
            pl.BlockSpec((L, inner), lambda b, c: (rows(b, c), 0)),
            pl.BlockSpec((L, inner), lambda b, c: (rows(b, c), 1)),
            pl.BlockSpec((L, bc), lambda b, c: (rows(b, c), 2 * ib)),
            pl.BlockSpec((L, bc), lambda b, c: (rows(b, c), 2 * ib + 1)),
            pl.BlockSpec((L, LANES), lambda b, c: (rows(b, c), 0)),
            pl.BlockSpec((SSD_CONV, conv_dim), lambda b, c: (0, 0)),
            pl.BlockSpec((1, conv_dim), lambda b, c: (0, 0)),
            pl.BlockSpec((1, LANES), lambda b, c: (0, 0)),
            pl.BlockSpec((1, LANES), lambda b, c: (0, 0)),
            pl.BlockSpec((1, inner), lambda b, c: (0, 0)),
            pl.BlockSpec((1, inner), lambda b, c: (0, 0)),
            pl.BlockSpec((LANES, inner), lambda b, c: (0, 0)),
        ],
        out_specs=pl.BlockSpec((L, inner), lambda b, c: (rows(b, c), 0)),
        out_shape=jax.ShapeDtypeStruct((bsz * seq, inner), BF16),
        scratch_shapes=[
            pltpu.VMEM((SUBLANES, conv_dim), F32),
            pltpu.VMEM((n_pairs, SSD_STATE, LANES), F32),
            pltpu.VMEM((L, inner), F32),
            pltpu.VMEM((L, bc), F32),
            pltpu.VMEM((L, bc), BF16),
            pltpu.VMEM((L, inner), F32),
            pltpu.VMEM((L, inner), F32),
            pltpu.VMEM((L, inner), F32),
            pltpu.VMEM((L, inner), F32),
        ],
        compiler_params=_cparams(("parallel", "arbitrary")),
        name="ssd",
    )(proj, proj, proj, proj, dt_raw, conv_w, conv_b, dt_bias_p, a_log_p, dskip_x, norm_w, e_mat)


def _sb_kernel(q_ref, k_ref, v_ref, o_ref, acc_ref, carry_ref):
    BL = SB_BLOCK
    i = pl.program_id(2)
    scale = SB_HEAD_DIM ** -0.5
    row = lax.broadcasted_iota(jnp.int32, (BL, BL), 0)
    col = lax.broadcasted_iota(jnp.int32, (BL, BL), 1)
    causal = col < row
    r2 = lax.broadcasted_iota(jnp.int32, (BL, 2 * BL), 0)
    c2 = lax.broadcasted_iota(jnp.int32, (BL, 2 * BL), 1)
    tri = jnp.where((r2 > c2) | (c2 >= BL), 1.0, 0.0).astype(BF16)

    def sweep(s, first):
        subs = range(SB_SUB)
        kbs = [i * SB_SUB + j - s for j in subs]
        starts = [pl.multiple_of(jnp.maximum(kb, 0) * BL, BL) for kb in kbs]
        zls = [lax.dot_general(q_ref[j * BL:(j + 1) * BL, :], k_ref[pl.ds(starts[j], BL), :],
                               (((1,), (1,)), ((), ())), preferred_element_type=F32) * scale for j in subs]
        sps = [_softplus(zl) for zl in zls]
        log_betas = [zl - sp for zl, sp in zip(zls, sps)]
        log_keeps = [jnp.where(causal, -sp, 0.0) if first else -sp for sp in sps]
        splits = [_split2(lk) for lk in log_keeps]
        t2s = [_dot(hi, tri) + _dot(lo, tri) for hi, lo in splits]
        if first:
            atts = [jnp.where(causal, jnp.exp(lb + t2[:, :BL]), 0.0) for lb, t2 in zip(log_betas, t2s)]
            carries = [t2[:, BL:] for t2 in t2s]
        else:
            olds = [carry_ref[j] for j in subs]
            atts = [jnp.where(kb >= 0, jnp.exp(lb + t2[:, :BL] + old), 0.0)
                    for kb, lb, t2, old in zip(kbs, log_betas, t2s, olds)]
            carries = [old + jnp.where(kb >= 0, t2[:, BL:], 0.0) for kb, t2, old in zip(kbs, t2s, olds)]
        pvs = [_dot(atts[j].astype(BF16), v_ref[pl.ds(starts[j], BL), :]) for j in subs]
        worst = None
        for j in subs:
            if first:
                acc_ref[j] = pvs[j]
            else:
                acc_ref[j] += pvs[j]
            carry_ref[j] = carries[j]
            live = jnp.where(kbs[j] >= 1, carries[j], NEG_BIG)
            worst = live if worst is None else jnp.maximum(worst, live)
        return jnp.max(worst) > SB_EXIT_LOG

    sweep(0, True)
    go = sweep(1, False)

    def body(st):
        s, _ = st
        return s + 1, sweep(s, False)

    lax.while_loop(lambda st: st[1], body, (jnp.int32(2), go))
    for j in range(SB_SUB):
        o_ref[j * BL:(j + 1) * BL, :] = acc_ref[j].astype(o_ref.dtype)


def _stickbreak(proj3, bsz, seq, q_blk0):
    tq = SB_SUB * SB_BLOCK
    return pl.pallas_call(
        _sb_kernel,
        grid=(bsz, SB_HEADS, seq // tq),
        in_specs=[
            pl.BlockSpec((None, tq, SB_HEAD_DIM), lambda b, h, i: (b, i, q_blk0 + h)),
            pl.BlockSpec((None, seq, SB_HEAD_DIM), lambda b, h, i: (b, 0, q_blk0 + SB_HEADS + h)),
            pl.BlockSpec((None, seq, SB_HEAD_DIM), lambda b, h, i: (b, 0, q_blk0 + 2 * SB_HEADS + h)),
        ],
        out_specs=pl.BlockSpec((None, tq, SB_HEAD_DIM), lambda b, h, i: (b, i, h)),
        out_shape=jax.ShapeDtypeStruct((bsz, seq, SB_HEADS * SB_HEAD_DIM), BF16),
        scratch_shapes=[pltpu.VMEM((SB_SUB, SB_BLOCK, SB_HEAD_DIM), F32),
                        pltpu.VMEM((SB_SUB, SB_BLOCK, SB_BLOCK), F32)],
        compiler_params=_cparams(("parallel", "parallel", "arbitrary")),
        name="stickbreak",
    )(proj3, proj3, proj3)


def _merge_kernel(x_ref, ys_ref, yb_ref, g1_ref, g2_ref, w1_ref, w2_ref, wm_ref, o_ref, acc_ref):
    j = pl.program_id(1)

    @pl.when(j == 0)
    def _():
        acc_ref[...] = jnp.zeros_like(acc_ref)

    a = _dot(ys_ref[...], w1_ref[...])
    b = _dot(yb_ref[...], w2_ref[...])
    s1 = jax.nn.sigmoid(g1_ref[...].astype(F32))
    s2 = jax.nn.sigmoid(g2_ref[...].astype(F32))
    m = (s1 * a + s2 * b).astype(BF16)
    acc_ref[...] += _dot(m, wm_ref[...])

    @pl.when(j == pl.num_programs(1) - 1)
    def _():
        o_ref[...] = x_ref[...] + acc_ref[...]


def _merge(x2, y_ssd, y_sb, proj, w1, w2, wm, g_blk0, tm, tn):
    t, d = x2.shape
    tm = min(tm, t)
    nj = d // tn
    return pl.pallas_call(
        _merge_kernel,
        grid=(t // tm, nj),
        in_specs=[
            pl.BlockSpec((tm, d), lambda i, j: (i, 0)),
            pl.BlockSpec((tm, y_ssd.shape[1]), lambda i, j: (i, 0)),
            pl.BlockSpec((tm, y_sb.shape[1]), lambda i, j: (i, 0)),
            pl.BlockSpec((tm, tn), lambda i, j: (i, g_blk0 + j)),
            pl.BlockSpec((tm, tn), lambda i, j: (i, g_blk0 + nj + j)),
            pl.BlockSpec((w1.shape[0], tn), lambda i, j: (0, j)),
            pl.BlockSpec((w2.shape[0], tn), lambda i, j: (0, j)),
            pl.BlockSpec((tn, d), lambda i, j: (j, 0)),
        ],
        out_specs=pl.BlockSpec((tm, d), lambda i, j: (i, 0)),
        out_shape=jax.ShapeDtypeStruct((t, d), F32),
        scratch_shapes=[pltpu.VMEM((tm, d), F32)],
        compiler_params=_cparams(("parallel", "arbitrary")),
        name="merge",
    )(x2, y_ssd, y_sb, proj, proj, w1, w2, wm)


def _xattn_kernel(h_ref, kv_ref, nxa_ref, wq_ref, wo_ref, nmoe_ref, wr_ref, h2_ref, um_ref, rt_ref):
    h1 = h_ref[...]
    un = _rms(h1, nxa_ref[...]).astype(BF16)
    q = _dot(un, wq_ref[...]).astype(BF16)
    width = XA_HEADS * XA_HEAD_DIM
    outs = []
    for hd in range(XA_HEADS):
        cs = slice(hd * XA_HEAD_DIM, (hd + 1) * XA_HEAD_DIM)
        k = kv_ref[:, cs]
        v = kv_ref[:, width + hd * XA_HEAD_DIM: width + (hd + 1) * XA_HEAD_DIM]
        sc = lax.dot_general(q[:, cs], k, (((1,), (1,)), ((), ())),
                             preferred_element_type=F32) * (XA_HEAD_DIM ** -0.5)
        sc = sc - jnp.max(sc, axis=-1, keepdims=True)
        p = jnp.exp(sc)
        p = p / jnp.sum(p, axis=-1, keepdims=True)
        outs.append(_dot(p.astype(BF16), v).astype(BF16))
    o = jnp.concatenate(outs, axis=1)
    h2 = h1 + _dot(o, wo_ref[...])
    h2_ref[...] = h2
    um = _rms(h2, nmoe_ref[...])
    um_ref[...] = um
    hi, lo = _split2(um)
    nr = rt_ref.shape[1]
    both = _dot(hi, wr_ref[...])
    lg = both[:, :nr] + both[:, nr:] + _dot(lo, wr_ref[:, :nr])

    lane = lax.broadcasted_iota(jnp.int32, lg.shape, 1)
    lane_f = lane.astype(F32)
    is_g = lane < N_GROUPS
    gl = jnp.where(is_g, lg, NEG_BIG)
    gmax = jnp.max(gl, axis=-1, keepdims=True)
    g_sel = jnp.min(jnp.where(gl == gmax, lane_f, float(LANES)), axis=-1, keepdims=True)
    g_gate = 1.0 / jnp.sum(jnp.where(is_g, jnp.exp(gl - gmax), 0.0), axis=-1, keepdims=True)
    lo_lane = N_GROUPS + EXPERTS_PER_GROUP * g_sel
    el = jnp.where((lane_f >= lo_lane) & (lane_f < lo_lane + EXPERTS_PER_GROUP), lg, NEG_BIG)
    m1 = jnp.max(el, axis=-1, keepdims=True)
    i1 = jnp.min(jnp.where(el == m1, lane_f, float(LANES)), axis=-1, keepdims=True)
    el2 = jnp.where(lane_f == i1, NEG_BIG, el)
    m2 = jnp.max(el2, axis=-1, keepdims=True)
    i2 = jnp.min(jnp.where(el2 == m2, lane_f, float(LANES)), axis=-1, keepdims=True)
    ex = jnp.exp(m2 - m1)
    p1 = 1.0 / (1.0 + ex)
    rt_ref[...] = jnp.where(lane == 0, i1 - N_GROUPS,
                            jnp.where(lane == 1, i2 - N_GROUPS,
                                      jnp.where(lane == 2, p1 * g_gate,
                                                jnp.where(lane == 3, ex * p1 * g_gate, 0.0))))


def _xattn(h1, kv, nxa, wq, wo, nmoe, wr, seq, tm):
    t, d = h1.shape
    tm = min(tm, seq)
    per_b = seq // tm
    m_len = kv.shape[0] // (t // seq)
    nr = wr.shape[1] // 2
    return pl.pallas_call(
        _xattn_kernel,
        grid=(t // tm,),
        in_specs=[
            pl.BlockSpec((tm, d), lambda i: (i, 0)),
            pl.BlockSpec((m_len, kv.shape[1]), lambda i: (i // per_b, 0)),
            pl.BlockSpec((1, d), lambda i: (0, 0)),
            pl.BlockSpec(wq.shape, lambda i: (0, 0)),
            pl.BlockSpec(wo.shape, lambda i: (0, 0)),
            pl.BlockSpec((1, d), lambda i: (0, 0)),
            pl.BlockSpec((d, 2 * nr), lambda i: (0, 0)),
        ],
        out_specs=[
            pl.BlockSpec((tm, d), lambda i: (i, 0)),
            pl.BlockSpec((tm, d), lambda i: (i, 0)),
            pl.BlockSpec((tm, nr), lambda i: (i, 0)),
        ],
        out_shape=[jax.ShapeDtypeStruct((t, d), F32), jax.ShapeDtypeStruct((t, d), F32),
                   jax.ShapeDtypeStruct((t, nr), F32)],
        compiler_params=_cparams(("parallel",)),
        name="xattn",
    )(h1, kv, nxa, wq, wo, nmoe, wr)


def _dispatch_kernel(slot_ref, padrow_ref, padn_ref, nv_ref, um_ref, xs_hbm, sa_ref, sb_ref, z_ref, sem, zsem):
    i = pl.program_id(0)
    tm = um_ref.shape[0]
    last = pl.num_programs(0) - 1
    stage = (sa_ref, sb_ref)

    def zero_rows(start):
        for e in range(N_EXPERTS):
            def body(j, c, e=e):
                cp = pltpu.make_async_copy(z_ref.at[pl.ds(0, 1)], xs_hbm.at[pl.ds(padrow_ref[e] + j, 1)], zsem)
                if start:
                    cp.start()
                else:
                    cp.wait()
                return c

            lax.fori_loop(0, padn_ref[e], body, 0)

        def tail(b, c):
            cp = pltpu.make_async_copy(
                z_ref, xs_hbm.at[pl.ds(pl.multiple_of(b * MOE_BLOCK, MOE_BLOCK), MOE_BLOCK)], zsem)
            if start:
                cp.start()
            else:
                cp.wait()
            return c

        lax.fori_loop(nv_ref[0], xs_hbm.shape[0] // MOE_BLOCK, tail, 0)

    @pl.when(i == 0)
    def _():
        z_ref[...] = jnp.zeros_like(z_ref)
        zero_rows(True)

    def wait(par):
        for _ in range(TOP_K):
            pltpu.make_async_copy(stage[par], xs_hbm.at[pl.ds(0, tm)], sem.at[par]).wait()

    def step(par):
        pl.when(i >= 2)(functools.partial(wait, par))
        stage[par][...] = um_ref[...]
        base = i * tm * TOP_K
        for r in range(tm):
            for k in range(TOP_K):
                pltpu.make_async_copy(stage[par].at[pl.ds(r, 1)],
                                      xs_hbm.at[pl.ds(slot_ref[base + TOP_K * r + k], 1)],
                                      sem.at[par]).start(priority=k)

        @pl.when(i == last)
        def _():
            wait(par)
            pl.when(i >= 1)(functools.partial(wait, 1 - par))

    for par in range(2):
        pl.when(i % 2 == par)(functools.partial(step, par))

    @pl.when(i == last)
    def _():
        zero_rows(False)


def _dispatch(slot, pad_row, pad_n, n_valid, um, n_slots, tm):
    t, d = um.shape
    tm = min(tm, t)
    grid_spec = pltpu.PrefetchScalarGridSpec(
        num_scalar_prefetch=4,
        grid=(t // tm,),
        in_specs=[pl.BlockSpec((tm, d), lambda i, s, pr, pn, nv: (i, 0))],
        out_specs=pl.BlockSpec(memory_space=pl.ANY),
        scratch_shapes=[pltpu.VMEM((tm, d), F32), pltpu.VMEM((tm, d), F32),
                        pltpu.VMEM((MOE_BLOCK, d), F32),
                        pltpu.SemaphoreType.DMA((2,)), pltpu.SemaphoreType.DMA(())],
    )
    return pl.pallas_call(
        _dispatch_kernel,
        grid_spec=grid_spec,
        out_shape=jax.ShapeDtypeStruct((n_slots, d), F32),
        compiler_params=_cparams(("arbitrary",)),
        name="dispatch",
    )(slot, pad_row, pad_n, n_valid, um)


def _expert_kernel(be_ref, nv_ref, x_ref, wg_ref, wu_ref, wd_ref, o_ref, wgb_ref, wub_ref, wdb_ref):
    i = pl.program_id(0)

    @pl.when(jnp.logical_or(i == 0, be_ref[i] != be_ref[jnp.maximum(i - 1, 0)]))
    def _():
        wgb_ref[...] = wg_ref[...].astype(BF16)
        wub_ref[...] = wu_ref[...].astype(BF16)
        wdb_ref[...] = wd_ref[...].astype(BF16)

    @pl.when(i < nv_ref[0])
    def _():
        xb = x_ref[...].astype(BF16)
        hid = (_silu(_dot(xb, wgb_ref[...])) * _dot(xb, wub_ref[...])).astype(BF16)
        o_ref[...] = _dot(hid, wdb_ref[...])

    @pl.when(i >= nv_ref[0])
    def _():
        o_ref[...] = jnp.zeros_like(o_ref)


def _experts(block_e, n_valid, xs, wg, wu, wd):
    n_slots = xs.shape[0]
    d, ff = wg.shape[1:]

    def x_map(i, be, nv):
        return (jnp.minimum(i, nv[0] - 1), 0)

    grid_spec = pltpu.PrefetchScalarGridSpec(
        num_scalar_prefetch=2,
        grid=(n_slots // MOE_BLOCK,),
        in_specs=[
            pl.BlockSpec((MOE_BLOCK, d), x_map),
            pl.BlockSpec((None, d, ff), lambda i, be, nv: (be[i], 0, 0)),
            pl.BlockSpec((None, d, ff), lambda i, be, nv: (be[i], 0, 0)),
            pl.BlockSpec((None, ff, d), lambda i, be, nv: (be[i], 0, 0)),
        ],
        out_specs=pl.BlockSpec((MOE_BLOCK, d), lambda i, be, nv: (i, 0)),
        scratch_shapes=[pltpu.VMEM((d, ff), BF16), pltpu.VMEM((d, ff), BF16), pltpu.VMEM((ff, d), BF16)],
    )
    return pl.pallas_call(
        _expert_kernel,
        grid_spec=grid_spec,
        out_shape=jax.ShapeDtypeStruct((n_slots, d), F32),
        compiler_params=_cparams(("arbitrary",)),
        name="experts",
    )(block_e, n_valid, xs, wg, wu, wd)


def _combine_kernel(slot_ref, h_ref, w_ref, nw_ref, yb_hbm, o_ref, ya0_ref, ya1_ref, yb0_ref, yb1_ref, sem):
    i = pl.program_id(0)
    tm = h_ref.shape[0]
    last = pl.num_programs(0) - 1
    bufs = ((ya0_ref, ya1_ref), (yb0_ref, yb1_ref))

    def gather(blk, par):
        base = blk * tm * TOP_K
        for r in range(tm):
            for k in range(TOP_K):
                pltpu.make_async_copy(yb_hbm.at[pl.ds(slot_ref[base + TOP_K * r + k], 1)],
                                      bufs[par][k].at[pl.ds(r, 1)], sem.at[par]).start(priority=k)

    def wait(par):
        for k in range(TOP_K):
            pltpu.make_async_copy(yb_hbm.at[pl.ds(0, tm)], bufs[par][k], sem.at[par]).wait()

    @pl.when(i == 0)
    def _():
        gather(i, 0)

    def step(par):
        wait(par)
        gather(jnp.minimum(i + 1, last), 1 - par)
        w0 = w_ref[:, TOP_K:TOP_K + 1]
        w1 = w_ref[:, TOP_K + 1:TOP_K + 2]
        h3 = h_ref[...] + w0 * bufs[par][0][...] + w1 * bufs[par][1][...]
        o_ref[...] = _rms(h3, nw_ref[...])

        @pl.when(i == last)
        def _():
            wait(1 - par)

    for par in range(2):
        pl.when(i % 2 == par)(functools.partial(step, par))


def _combine(slot, h2, route, nw, yb, tm):
    t, d = h2.shape
    tm = min(tm, t)
    grid_spec = pltpu.PrefetchScalarGridSpec(
        num_scalar_prefetch=1,
        grid=(t // tm,),
        in_specs=[
            pl.BlockSpec((tm, d), lambda i, s: (i, 0)),
            pl.BlockSpec((tm, route.shape[1]), lambda i, s: (i, 0)),
            pl.BlockSpec((1, d), lambda i, s: (0, 0)),
            pl.BlockSpec(memory_space=pl.ANY),
        ],
        out_specs=pl.BlockSpec((tm, d), lambda i, s: (i, 0)),
        scratch_shapes=[pltpu.VMEM((tm, d), F32)] * (2 * TOP_K) + [pltpu.SemaphoreType.DMA((2,))],
    )
    return pl.pallas_call(
        _combine_kernel,
        grid_spec=grid_spec,
        out_shape=jax.ShapeDtypeStruct((t, d), F32),
        compiler_params=_cparams(("arbitrary",)),
        name="combine",
    )(slot, h2, route, nw, yb)


def _slots(route, t):
    flat_e = route[:, :TOP_K].astype(jnp.int32).reshape(-1)
    n_pairs = t * TOP_K
    tile = min(256, n_pairs)
    onehot = flat_e[:, None] == jnp.arange(N_EXPERTS, dtype=jnp.int32)[None, :]
    oh3 = onehot.reshape(n_pairs // tile, tile, N_EXPERTS)
    tri = (jnp.arange(tile)[:, None] >= jnp.arange(tile)[None, :]).astype(BF16)
    within = jnp.einsum('ij,tjk->tik', tri, oh3.astype(BF16), preferred_element_type=F32)
    tot = within[:, -1, :]
    tile_base = jnp.cumsum(tot, axis=0) - tot
    csum = (within + tile_base[:, None, :]).reshape(n_pairs, N_EXPERTS)
    rank = jnp.sum(jnp.where(onehot, csum, 0.0), axis=1).astype(jnp.int32) - 1
    counts = (tile_base[-1] + tot[-1]).astype(jnp.int32)
    padded = (counts + MOE_BLOCK - 1) // MOE_BLOCK * MOE_BLOCK
    pad_end = jnp.cumsum(padded)
    pad_start = pad_end - padded
    slot = jnp.sum(jnp.where(onehot, pad_start[None, :], 0), axis=1).astype(jnp.int32) + rank
    n_slots = t * TOP_K + N_EXPERTS * MOE_BLOCK
    n_blocks = n_slots // MOE_BLOCK
    block_start = jnp.arange(n_blocks, dtype=jnp.int32) * MOE_BLOCK
    block_e = jnp.minimum(jnp.sum((pad_end[None, :] <= block_start[:, None]).astype(jnp.int32), axis=1),
                          N_EXPERTS - 1)
    n_valid = (pad_end[-1:] // MOE_BLOCK).astype(jnp.int32)
    return slot, pad_start + counts, padded - counts, block_e, n_valid, n_slots


def _pad_lanes(v, n=LANES):
    return jnp.pad(v, ((0, 0), (0, n - v.shape[1])))


def kernel(x, mem, norm_mix_w, w_in, conv_w, conv_b, dt_bias, a_log, d_skip, ssd_norm_w, w_ssd_branch, w_sb_branch, w_mix_out, norm_xa_w, norm_mem_w, w_xq, w_xk, w_xv, w_xo, norm_moe_w, w_router_group, w_router_expert, w_expert_gate, w_expert_up, w_expert_down, norm_final_w):
    bsz, seq, d = x.shape
    depth = w_in.shape[0]
    t = bsz * seq
    inner = ssd_norm_w.shape[1]
    heads = dt_bias.shape[1]
    bc = SSD_GROUPS * SSD_STATE
    conv_dim = inner + 2 * bc
    sb_width = SB_HEADS * SB_HEAD_DIM
    col_dt = inner + conv_dim
    n_main = col_dt + 3 * sb_width + 2 * d
    q_col0 = col_dt
    g_col0 = col_dt + 3 * sb_width
    tn_merge = 512

    assert depth == 1, "single-layer configuration: the final RMSNorm is fused into the last MoE combine"
    h = x.reshape(t, d)
    for l in range(depth):
        w_l = w_in[l]
        w_main = jnp.concatenate([w_l[:, :col_dt], w_l[:, col_dt + heads:]], axis=1).astype(BF16)
        w_dt = _pad_lanes(w_l[:, col_dt:col_dt + heads]).astype(BF16)
        e_mat = (jnp.arange(inner, dtype=jnp.int32)[None, :] // SSD_HEAD_DIM
                 == jnp.arange(LANES, dtype=jnp.int32)[:, None]).astype(BF16)
        dskip_x = jnp.repeat(d_skip[l], SSD_HEAD_DIM)[None, :]

        proj, dt_raw = _norm_matmul(h, norm_mix_w[l][None, :], w_main, w_dt, tm=1024, tn=512)
        y_ssd = _ssd(proj, dt_raw, conv_w[l], conv_b[l][None, :], _pad_lanes(dt_bias[l][None, :]),
                     _pad_lanes(a_log[l][None, :]), dskip_x, ssd_norm_w[l][None, :], e_mat,
                     bsz, seq, inner, bc)
        y_sb = _stickbreak(proj.reshape(bsz, seq, n_main), bsz, seq, q_col0 // SB_HEAD_DIM)
        h1 = _merge(h, y_ssd, y_sb.reshape(t, sb_width), proj,
                    w_ssd_branch[l].astype(BF16), w_sb_branch[l].astype(BF16), w_mix_out[l].astype(BF16),
                    g_col0 // tn_merge, tm=512, tn=tn_merge)

        m_len = mem.shape[1]
        w_kv = jnp.concatenate([w_xk[l], w_xv[l]], axis=1).astype(BF16)
        kv, _ = _norm_matmul(mem.reshape(bsz * m_len, d), norm_mem_w[l][None, :], w_kv,
                             jnp.zeros((d, LANES), BF16), tm=bsz * m_len, tn=512)
        w_r = _pad_lanes(jnp.concatenate([w_router_group[l], w_router_expert[l]], axis=1))
        wr_hi = w_r.astype(BF16)
        wr_lo = (w_r - wr_hi.astype(F32)).astype(BF16)
        h2, um, route = _xattn(h1, kv, norm_xa_w[l][None, :], w_xq[l].astype(BF16), w_xo[l].astype(BF16),
                               norm_moe_w[l][None, :], jnp.concatenate([wr_hi, wr_lo], axis=1), seq, tm=512)

        slot, pad_row, pad_n, block_e, n_valid, n_slots = _slots(route, t)
        xs = _dispatch(slot, pad_row, pad_n, n_valid, um, n_slots, tm=256)
        yb = _experts(block_e, n_valid, xs, w_expert_gate[l], w_expert_up[l], w_expert_down[l])
        h = _combine(slot, h2, route, norm_final_w[None, :], yb, tm=256)
    return h.reshape(bsz, seq, d)
```

```python
import functools

import jax
import jax.numpy as jnp
from jax import lax
from jax.experimental import pallas as pl
from jax.experimental.pallas import tpu as pltpu

F32 = jnp.float32
BF16 = jnp.bfloat16
EPS = 1e-6

SSD_HEAD_DIM = 64
SSD_GROUPS = 4
SSD_STATE = 128
SSD_CONV = 4
SSD_CHUNK = 128
SB_HEADS = 4
SB_HEAD_DIM = 128
SB_BLOCK = 128
SB_SUB = 4
XA_HEADS = 4
XA_HEAD_DIM = 128
N_GROUPS = 4
EXPERTS_PER_GROUP = 8
N_EXPERTS = N_GROUPS * EXPERTS_PER_GROUP
TOP_K = 2
MOE_BLOCK = 256

LANES = 128
SUBLANES = 8
VMEM_LIMIT = 56 * 1024 * 1024

SB_EXIT_LOG = -104.0
NEG_BIG = -1e30


def _cparams(sem):
    return pltpu.CompilerParams(dimension_semantics=sem, vmem_limit_bytes=VMEM_LIMIT)


def _rms(x, w):
    var = jnp.mean(x * x, axis=-1, keepdims=True)
    return x * lax.rsqrt(var + EPS) * w


def _split2(v):
    hi = v.astype(BF16)
    lo = (v - hi.astype(F32)).astype(BF16)
    return hi, lo


def _dot(a, b):
    return jnp.dot(a, b, preferred_element_type=F32)


def _silu(x):
    return x / (1.0 + jnp.exp(-x))


def _softplus(x):
    return jnp.maximum(x, 0.0) + jnp.log(1.0 + jnp.exp(-jnp.abs(x)))


def _norm_matmul_kernel(x_ref, nw_ref, w_ref, ws_ref, o_ref, os_ref, u_ref):
    @pl.when(pl.program_id(1) == 0)
    def _():
        u = _rms(x_ref[...], nw_ref[...]).astype(BF16)
        u_ref[...] = u
        os_ref[...] = _dot(u, ws_ref[...])

    o_ref[...] = _dot(u_ref[...], w_ref[...]).astype(o_ref.dtype)


def _norm_matmul(x, nw, w, w_side, tm, tn):
    m, k = x.shape
    n = w.shape[1]
    ns = w_side.shape[1]
    tm = min(tm, m)
    return pl.pallas_call(
        _norm_matmul_kernel,
        grid=(m // tm, n // tn),
        in_specs=[
            pl.BlockSpec((tm, k), lambda i, j: (i, 0)),
            pl.BlockSpec((1, k), lambda i, j: (0, 0)),
            pl.BlockSpec((k, tn), lambda i, j: (0, j)),
            pl.BlockSpec((k, ns), lambda i, j: (0, 0)),
        ],
        out_specs=[
            pl.BlockSpec((tm, tn), lambda i, j: (i, j)),
            pl.BlockSpec((tm, ns), lambda i, j: (i, 0)),
        ],
        out_shape=[jax.ShapeDtypeStruct((m, n), BF16), jax.ShapeDtypeStruct((m, ns), F32)],
        scratch_shapes=[pltpu.VMEM((tm, k), BF16)],
        compiler_params=_cparams(("parallel", "arbitrary")),
        name="norm_matmul",
    )(x, nw, w, w_side)


def _ssd_kernel(z_ref, xs_ref, b_ref, c_ref, dtr_ref, cw_ref, cb_ref, dtb_ref, alog_ref, dsk_ref,
                nw_ref, e_ref, o_ref,
                tail_ref, state_ref, xact_ref, bact_ref, cact_ref, dtx_ref, eax_ref, wsx_ref, y_ref):
    L = SSD_CHUNK
    inner = xs_ref.shape[1]
    bc = b_ref.shape[1]
    n_pairs = inner // LANES

    @pl.when(pl.program_id(1) == 0)
    def _():
        tail_ref[...] = jnp.zeros_like(tail_ref)
        state_ref[...] = jnp.zeros_like(state_ref)

    row8 = lax.broadcasted_iota(jnp.int32, (SUBLANES, 2 * LANES), 0)

    def conv_seg(src_ref, col0, width, dst_ref):
        for j in range(0, width, 2 * LANES):
            cols = slice(col0 + j, col0 + j + 2 * LANES)
            xin = src_ref[:, j:j + 2 * LANES].astype(F32)
            t8 = tail_ref[:, cols]
            acc = xin * cw_ref[SSD_CONV - 1:SSD_CONV, cols] + cb_ref[:, cols]
            for s in range(1, SSD_CONV):
                r = pltpu.roll(xin, s, 0)
                top = jnp.where(row8 < s, pltpu.roll(t8, s, 0), r[:SUBLANES])
                r = jnp.concatenate([top, r[SUBLANES:]], axis=0)
                acc = acc + r * cw_ref[SSD_CONV - 1 - s:SSD_CONV - s, cols]
            tail_ref[:, cols] = xin[L - SUBLANES:]
            dst_ref[:, j:j + 2 * LANES] = _silu(acc).astype(dst_ref.dtype)

    conv_seg(xs_ref, 0, inner, xact_ref)
    conv_seg(b_ref, inner, bc, bact_ref)
    conv_seg(c_ref, inner + bc, bc, cact_ref)

    dt = _softplus(dtr_ref[...] + dtb_ref[...])
    da = dt * (-jnp.exp(alog_ref[...]))
    rowl = lax.broadcasted_iota(jnp.int32, (L, L), 0)
    coll = lax.broadcasted_iota(jnp.int32, (L, L), 1)
    lower = rowl >= coll
    tri = jnp.where(lower, 1.0, 0.0).astype(BF16)
    d1 = da.astype(BF16)
    r1 = da - d1.astype(F32)
    d2 = r1.astype(BF16)
    d3 = (r1 - d2.astype(F32)).astype(BF16)
    a_cum = _dot(tri, d1) + _dot(tri, d2) + _dot(tri, d3)
    a_cum_t = a_cum.T
    a_last = a_cum[L - 1:L, :]
    e_mat = e_ref[...]

    def expand(v):
        hi, lo = _split2(v)
        return _dot(hi, e_mat) + _dot(lo, e_mat)

    dtx_ref[...] = expand(dt)
    eax_ref[...] = expand(jnp.exp(a_cum))
    wsx_ref[...] = expand(jnp.exp(a_last - a_cum))
    elx = expand(jnp.broadcast_to(jnp.exp(a_last), (SUBLANES, LANES)))[0:1]

    lane = lax.broadcasted_iota(jnp.int32, (L, LANES), 1)
    pairs_per_group = n_pairs // SSD_GROUPS
    for g in range(SSD_GROUPS):
        gcols = slice(g * SSD_STATE, (g + 1) * SSD_STATE)
        bg_t = bact_ref[:, gcols].T.astype(BF16)
        cg = cact_ref[:, gcols]
        cb = _dot(cg, bg_t)
        for pp in range(pairs_per_group):
            p = g * pairs_per_group + pp
            pc = slice(p * LANES, (p + 1) * LANES)
            ms = []
            for hh in (2 * p, 2 * p + 1):
                seg = a_cum[:, hh:hh + 1] - a_cum_t[hh:hh + 1, :]
                dec = jnp.exp(jnp.where(lower, seg, NEG_BIG))
                ms.append((cb * dec).astype(BF16))
            lhs = jnp.concatenate(ms, axis=1)
            xs_p = xact_ref[:, pc]
            xdt = xs_p * dtx_ref[:, pc]
            rhs = jnp.concatenate([jnp.where(lane < SSD_HEAD_DIM, xdt, 0.0).astype(BF16),
                                   jnp.where(lane >= SSD_HEAD_DIM, xdt, 0.0).astype(BF16)], axis=0)
            st = state_ref[p]
            y = _dot(lhs, rhs)
            y = y + _dot(cg, st.astype(BF16)) * eax_ref[:, pc]
            y = y + xs_p * dsk_ref[:, pc]
            y_ref[:, pc] = y
            xw = (xdt * wsx_ref[:, pc]).astype(BF16)
            state_ref[p] = st * elx[:, pc] + _dot(bg_t, xw)

    yg = y_ref[...] * _silu(z_ref[...].astype(F32))
    o_ref[...] = _rms(yg, nw_ref[...]).astype(o_ref.dtype)


def _ssd(proj, dt_raw, conv_w, conv_b, dt_bias_p, a_log_p, dskip_x, norm_w, e_mat, bsz, seq, inner, bc, b_blk):
    L = SSD_CHUNK
    nc = seq // L
    conv_dim = inner + 2 * bc
    n_pairs = inner // LANES

    def rows(b, c):
        return b * nc + c

    return pl.pallas_call(
        _ssd_kernel,
        grid=(bsz, nc),
        in_specs=[
            pl.BlockSpec((L, inner), lambda b, c: (rows(b, c), 0)),
            pl.BlockSpec((L, inner), lambda b, c: (rows(b, c), 1)),
            pl.BlockSpec((L, bc), lambda b, c: (rows(b, c), b_blk)),
            pl.BlockSpec((L, bc), lambda b, c: (rows(b, c), b_blk + 1)),
            pl.BlockSpec((L, LANES), lambda b, c: (rows(b, c), 0)),
            pl.BlockSpec((SSD_CONV, conv_dim), lambda b, c: (0, 0)),
            pl.BlockSpec((1, conv_dim), lambda b, c: (0, 0)),
            pl.BlockSpec((1, LANES), lambda b, c: (0, 0)),
            pl.BlockSpec((1, LANES), lambda b, c: (0, 0)),
            pl.BlockSpec((1, inner), lambda b, c: (0, 0)),
            pl.BlockSpec((1, inner), lambda b, c: (0, 0)),
            pl.BlockSpec((LANES, inner), lambda b, c: (0, 0)),
        ],
        out_specs=pl.BlockSpec((L, inner), lambda b, c: (rows(b, c), 0)),
        out_shape=jax.ShapeDtypeStruct((bsz * seq, inner), BF16),
        scratch_shapes=[
            pltpu.VMEM((SUBLANES, conv_dim), F32),
            pltpu.VMEM((n_pairs, SSD_STATE, LANES), F32),
            pltpu.VMEM((L, inner), F32),
            pltpu.VMEM((L, bc), F32),
            pltpu.VMEM((L, bc), BF16),
            pltpu.VMEM((L, inner), F32),
            pltpu.VMEM((L, inner), F32),
            pltpu.VMEM((L, inner), F32),
            pltpu.VMEM((L, inner), F32),
        ],
        compiler_params=_cparams(("parallel", "arbitrary")),
        name="ssd",
    )(proj, proj, proj, proj, dt_raw, conv_w, conv_b, dt_bias_p, a_log_p, dskip_x, norm_w, e_mat)


def _sb_kernel(q_ref, k_ref, v_ref, o_ref, acc_ref, carry_ref):
    BL = SB_BLOCK
    i = pl.program_id(2)
    scale = SB_HEAD_DIM ** -0.5
    row = lax.broadcasted_iota(jnp.int32, (BL, BL), 0)
    col = lax.broadcasted_iota(jnp.int32, (BL, BL), 1)
    causal = col < row
    r2 = lax.broadcasted_iota(jnp.int32, (BL, 2 * BL), 0)
    c2 = lax.broadcasted_iota(jnp.int32, (BL, 2 * BL), 1)
    tri = jnp.where((r2 > c2) | (c2 >= BL), 1.0, 0.0).astype(BF16)

    def sweep(s, first):
        subs = range(SB_SUB)
        kbs = [i * SB_SUB + j - s for j in subs]
        starts = [pl.multiple_of(jnp.maximum(kb, 0) * BL, BL) for kb in kbs]
        zls = [lax.dot_general(q_ref[j * BL:(j + 1) * BL, :], k_ref[pl.ds(starts[j], BL), :],
                               (((1,), (1,)), ((), ())), preferred_element_type=F32) * scale for j in subs]
        sps = [_softplus(zl) for zl in zls]
        log_betas = [zl - sp for zl, sp in zip(zls, sps)]
        log_keeps = [jnp.where(causal, -sp, 0.0) if first else -sp for sp in sps]
        splits = [_split2(lk) for lk in log_keeps]
        t2s = [_dot(hi, tri) + _dot(lo, tri) for hi, lo in splits]
        if first:
            atts = [jnp.where(causal, jnp.exp(lb + t2[:, :BL]), 0.0) for lb, t2 in zip(log_betas, t2s)]
            carries = [t2[:, BL:] for t2 in t2s]
        else:
            olds = [carry_ref[j] for j in subs]
            atts = [jnp.where(kb >= 0, jnp.exp(lb + t2[:, :BL] + old), 0.0)
                    for kb, lb, t2, old in zip(kbs, log_betas, t2s, olds)]
            carries = [old + jnp.where(kb >= 0, t2[:, BL:], 0.0) for kb, t2, old in zip(kbs, t2s, olds)]
        pvs = [_dot(atts[j].astype(BF16), v_ref[pl.ds(starts[j], BL), :]) for j in subs]
        worst = None
        for j in subs:
            if first:
                acc_ref[j] = pvs[j]
            else:
                acc_ref[j] += pvs[j]
            carry_ref[j] = carries[j]
            live = jnp.where(kbs[j] >= 1, carries[j], NEG_BIG)
            worst = live if worst is None else jnp.maximum(worst, live)
        return jnp.max(worst) > SB_EXIT_LOG

    sweep(0, True)
    go = sweep(1, False)

    def body(st):
        s, _ = st
        return s + 1, sweep(s, False)

    lax.while_loop(lambda st: st[1], body, (jnp.int32(2), go))
    for j in range(SB_SUB):
        o_ref[j * BL:(j + 1) * BL, :] = acc_ref[j].astype(o_ref.dtype)


def _stickbreak(proj3, bsz, seq, q_blk0):
    tq = SB_SUB * SB_BLOCK
    return pl.pallas_call(
        _sb_kernel,
        grid=(bsz, SB_HEADS, seq // tq),
        in_specs=[
            pl.BlockSpec((None, tq, SB_HEAD_DIM), lambda b, h, i: (b, i, q_blk0 + h)),
            pl.BlockSpec((None, seq, SB_HEAD_DIM), lambda b, h, i: (b, 0, q_blk0 + SB_HEADS + h)),
            pl.BlockSpec((None, seq, SB_HEAD_DIM), lambda b, h, i: (b, 0, q_blk0 + 2 * SB_HEADS + h)),
        ],
        out_specs=pl.BlockSpec((None, tq, SB_HEAD_DIM), lambda b, h, i: (b, i, h)),
        out_shape=jax.ShapeDtypeStruct((bsz, seq, SB_HEADS * SB_HEAD_DIM), BF16),
        scratch_shapes=[pltpu.VMEM((SB_SUB, SB_BLOCK, SB_HEAD_DIM), F32),
                        pltpu.VMEM((SB_SUB, SB_BLOCK, SB_BLOCK), F32)],
        compiler_params=_cparams(("parallel", "parallel", "arbitrary")),
        name="stickbreak",
    )(proj3, proj3, proj3)


def _merge_kernel(x_ref, ys_ref, yb_ref, g1_ref, g2_ref, w1_ref, w2_ref, wm_ref, o_ref):
    @pl.when(pl.program_id(1) == 0)
    def _():
        o_ref[...] = x_ref[...]

    a = _dot(ys_ref[...], w1_ref[...])
    b = _dot(yb_ref[...], w2_ref[...])
    s1 = jax.nn.sigmoid(g1_ref[...].astype(F32))
    s2 = jax.nn.sigmoid(g2_ref[...].astype(F32))
    m = (s1 * a + s2 * b).astype(BF16)
    o_ref[...] += _dot(m, wm_ref[...])


def _merge(x2, y_ssd, y_sb, proj, w1, w2, wm, g_blk0, tm, tn):
    t, d = x2.shape
    tm = min(tm, t)
    nj = d // tn
    return pl.pallas_call(
        _merge_kernel,
        grid=(t // tm, nj),
        in_specs=[
            pl.BlockSpec((tm, d), lambda i, j: (i, 0)),
            pl.BlockSpec((tm, y_ssd.shape[1]), lambda i, j: (i, 0)),
            pl.BlockSpec((tm, y_sb.shape[1]), lambda i, j: (i, 0)),
            pl.BlockSpec((tm, tn), lambda i, j: (i, g_blk0 + j)),
            pl.BlockSpec((tm, tn), lambda i, j: (i, g_blk0 + nj + j)),
            pl.BlockSpec((w1.shape[0], tn), lambda i, j: (0, j)),
            pl.BlockSpec((w2.shape[0], tn), lambda i, j: (0, j)),
            pl.BlockSpec((tn, d), lambda i, j: (j, 0)),
        ],
        out_specs=pl.BlockSpec((tm, d), lambda i, j: (i, 0)),
        out_shape=jax.ShapeDtypeStruct((t, d), F32),
        compiler_params=_cparams(("parallel", "arbitrary")),
        name="merge",
    )(x2, y_ssd, y_sb, proj, proj, w1, w2, wm)


def _xattn_kernel(h_ref, kv_ref, nxa_ref, wq_ref, wo_ref, nmoe_ref, wr_ref, h2_ref, um_ref, rt_ref):
    h1 = h_ref[...]
    un = _rms(h1, nxa_ref[...]).astype(BF16)
    q = _dot(un, wq_ref[...]).astype(BF16)
    width = XA_HEADS * XA_HEAD_DIM
    outs = []
    for hd in range(XA_HEADS):
        cs = slice(hd * XA_HEAD_DIM, (hd + 1) * XA_HEAD_DIM)
        k = kv_ref[:, cs]
        v = kv_ref[:, width + hd * XA_HEAD_DIM: width + (hd + 1) * XA_HEAD_DIM]
        sc = lax.dot_general(q[:, cs], k, (((1,), (1,)), ((), ())),
                             preferred_element_type=F32) * (XA_HEAD_DIM ** -0.5)
        sc = sc - jnp.max(sc, axis=-1, keepdims=True)
        p = jnp.exp(sc)
        p = p / jnp.sum(p, axis=-1, keepdims=True)
        outs.append(_dot(p.astype(BF16), v).astype(BF16))
    o = jnp.concatenate(outs, axis=1)
    h2 = h1 + _dot(o, wo_ref[...])
    h2_ref[...] = h2
    um = _rms(h2, nmoe_ref[...])
    um_ref[...] = um
    hi, lo = _split2(um)
    nr = rt_ref.shape[1]
    both = _dot(hi, wr_ref[...])
    lg = both[:, :nr] + both[:, nr:] + _dot(lo, wr_ref[:, :nr])

    lane = lax.broadcasted_iota(jnp.int32, lg.shape, 1)
    lane_f = lane.astype(F32)
    is_g = lane < N_GROUPS
    gl = jnp.where(is_g, lg, NEG_BIG)
    gmax = jnp.max(gl, axis=-1, keepdims=True)
    g_sel = jnp.min(jnp.where(gl == gmax, lane_f, float(LANES)), axis=-1, keepdims=True)
    g_gate = 1.0 / jnp.sum(jnp.where(is_g, jnp.exp(gl - gmax), 0.0), axis=-1, keepdims=True)
    lo_lane = N_GROUPS + EXPERTS_PER_GROUP * g_sel
    el = jnp.where((lane_f >= lo_lane) & (lane_f < lo_lane + EXPERTS_PER_GROUP), lg, NEG_BIG)
    m1 = jnp.max(el, axis=-1, keepdims=True)
    i1 = jnp.min(jnp.where(el == m1, lane_f, float(LANES)), axis=-1, keepdims=True)
    el2 = jnp.where(lane_f == i1, NEG_BIG, el)
    m2 = jnp.max(el2, axis=-1, keepdims=True)
    i2 = jnp.min(jnp.where(el2 == m2, lane_f, float(LANES)), axis=-1, keepdims=True)
    ex = jnp.exp(m2 - m1)
    p1 = 1.0 / (1.0 + ex)
    rt_ref[...] = jnp.where(lane == 0, i1 - N_GROUPS,
                            jnp.where(lane == 1, i2 - N_GROUPS,
                                      jnp.where(lane == 2, p1 * g_gate,
                                                jnp.where(lane == 3, ex * p1 * g_gate, 0.0))))


def _xattn(h1, kv, nxa, wq, wo, nmoe, wr, seq, tm):
    t, d = h1.shape
    tm = min(tm, seq)
    per_b = seq // tm
    m_len = kv.shape[0] // (t // seq)
    nr = wr.shape[1] // 2
    return pl.pallas_call(
        _xattn_kernel,
        grid=(t // tm,),
        in_specs=[
            pl.BlockSpec((tm, d), lambda i: (i, 0)),
            pl.BlockSpec((m_len, kv.shape[1]), lambda i: (i // per_b, 0)),
            pl.BlockSpec((1, d), lambda i: (0, 0)),
            pl.BlockSpec(wq.shape, lambda i: (0, 0)),
            pl.BlockSpec(wo.shape, lambda i: (0, 0)),
            pl.BlockSpec((1, d), lambda i: (0, 0)),
            pl.BlockSpec((d, 2 * nr), lambda i: (0, 0)),
        ],
        out_specs=[
            pl.BlockSpec((tm, d), lambda i: (i, 0)),
            pl.BlockSpec((tm, d), lambda i: (i, 0)),
            pl.BlockSpec((tm, nr), lambda i: (i, 0)),
        ],
        out_shape=[jax.ShapeDtypeStruct((t, d), F32), jax.ShapeDtypeStruct((t, d), F32),
                   jax.ShapeDtypeStruct((t, nr), F32)],
        compiler_params=_cparams(("parallel",)),
        name="xattn",
    )(h1, kv, nxa, wq, wo, nmoe, wr)


def _dispatch_kernel(slot_ref, padrow_ref, padn_ref, nv_ref, um_ref, xs_hbm, sa_ref, sb_ref, z_ref, sem, zsem):
    i = pl.program_id(0)
    tm = um_ref.shape[0]
    last = pl.num_programs(0) - 1
    stage = (sa_ref, sb_ref)

    def zero_rows(start):
        for e in range(N_EXPERTS):
            def body(j, c, e=e):
                cp = pltpu.make_async_copy(z_ref.at[pl.ds(0, 1)], xs_hbm.at[pl.ds(padrow_ref[e] + j, 1)], zsem)
                if start:
                    cp.start()
                else:
                    cp.wait()
                return c

            lax.fori_loop(0, padn_ref[e], body, 0)

        def tail(b, c):
            cp = pltpu.make_async_copy(
                z_ref, xs_hbm.at[pl.ds(pl.multiple_of(b * MOE_BLOCK, MOE_BLOCK), MOE_BLOCK)], zsem)
            if start:
                cp.start()
            else:
                cp.wait()
            return c

        lax.fori_loop(nv_ref[0], xs_hbm.shape[0] // MOE_BLOCK, tail, 0)

    @pl.when(i == 0)
    def _():
        z_ref[...] = jnp.zeros_like(z_ref)
        zero_rows(True)

    def wait(par):
        for _ in range(TOP_K):
            pltpu.make_async_copy(stage[par], xs_hbm.at[pl.ds(0, tm)], sem.at[par]).wait()

    def step(par):
        pl.when(i >= 2)(functools.partial(wait, par))
        stage[par][...] = um_ref[...]
        base = i * tm * TOP_K
        for r in range(tm):
            for k in range(TOP_K):
                pltpu.make_async_copy(stage[par].at[pl.ds(r, 1)],
                                      xs_hbm.at[pl.ds(slot_ref[base + TOP_K * r + k], 1)],
                                      sem.at[par]).start(priority=k)

        @pl.when(i == last)
        def _():
            wait(par)
            pl.when(i >= 1)(functools.partial(wait, 1 - par))

    for par in range(2):
        pl.when(i % 2 == par)(functools.partial(step, par))

    @pl.when(i == last)
    def _():
        zero_rows(False)


def _dispatch(slot, pad_row, pad_n, n_valid, um, n_slots, tm):
    t, d = um.shape
    tm = min(tm, t)
    grid_spec = pltpu.PrefetchScalarGridSpec(
        num_scalar_prefetch=4,
        grid=(t // tm,),
        in_specs=[pl.BlockSpec((tm, d), lambda i, s, pr, pn, nv: (i, 0))],
        out_specs=pl.BlockSpec(memory_space=pl.ANY),
        scratch_shapes=[pltpu.VMEM((tm, d), F32), pltpu.VMEM((tm, d), F32),
                        pltpu.VMEM((MOE_BLOCK, d), F32),
                        pltpu.SemaphoreType.DMA((2,)), pltpu.SemaphoreType.DMA(())],
    )
    return pl.pallas_call(
        _dispatch_kernel,
        grid_spec=grid_spec,
        out_shape=jax.ShapeDtypeStruct((n_slots, d), F32),
        compiler_params=_cparams(("arbitrary",)),
        name="dispatch",
    )(slot, pad_row, pad_n, n_valid, um)


def _expert_kernel(be_ref, nv_ref, x_ref, wg_ref, wu_ref, wd_ref, o_ref, wgb_ref, wub_ref, wdb_ref):
    i = pl.program_id(0)

    @pl.when(jnp.logical_or(i == 0, be_ref[i] != be_ref[jnp.maximum(i - 1, 0)]))
    def _():
        wgb_ref[...] = wg_ref[...].astype(BF16)
        wub_ref[...] = wu_ref[...].astype(BF16)
        wdb_ref[...] = wd_ref[...].astype(BF16)

    @pl.when(i < nv_ref[0])
    def _():
        xb = x_ref[...].astype(BF16)
        hid = (_silu(_dot(xb, wgb_ref[...])) * _dot(xb, wub_ref[...])).astype(BF16)
        o_ref[...] = _dot(hid, wdb_ref[...])

    @pl.when(i >= nv_ref[0])
    def _():
        o_ref[...] = jnp.zeros_like(o_ref)


def _experts(block_e, n_valid, xs, wg, wu, wd):
    n_slots = xs.shape[0]
    d, ff = wg.shape[1:]

    def x_map(i, be, nv):
        return (jnp.minimum(i, nv[0] - 1), 0)

    grid_spec = pltpu.PrefetchScalarGridSpec(
        num_scalar_prefetch=2,
        grid=(n_slots // MOE_BLOCK,),
        in_specs=[
            pl.BlockSpec((MOE_BLOCK, d), x_map),
            pl.BlockSpec((None, d, ff), lambda i, be, nv: (be[i], 0, 0)),
            pl.BlockSpec((None, d, ff), lambda i, be, nv: (be[i], 0, 0)),
            pl.BlockSpec((None, ff, d), lambda i, be, nv: (be[i], 0, 0)),
        ],
        out_specs=pl.BlockSpec((MOE_BLOCK, d), lambda i, be, nv: (i, 0)),
        scratch_shapes=[pltpu.VMEM((d, ff), BF16), pltpu.VMEM((d, ff), BF16), pltpu.VMEM((ff, d), BF16)],
    )
    return pl.pallas_call(
        _expert_kernel,
        grid_spec=grid_spec,
        out_shape=jax.ShapeDtypeStruct((n_slots, d), F32),
        compiler_params=_cparams(("arbitrary",)),
        name="experts",
    )(block_e, n_valid, xs, wg, wu, wd)


def _combine_kernel(slot_ref, h_ref, w_ref, nw_ref, yb_hbm, o_ref, ya0_ref, ya1_ref, yb0_ref, yb1_ref, sem):
    i = pl.program_id(0)
    tm = h_ref.shape[0]
    last = pl.num_programs(0) - 1
    bufs = ((ya0_ref, ya1_ref), (yb0_ref, yb1_ref))

    def gather(blk, par):
        base = blk * tm * TOP_K
        for r in range(tm):
            for k in range(TOP_K):
                pltpu.make_async_copy(yb_hbm.at[pl.ds(slot_ref[base + TOP_K * r + k], 1)],
                                      bufs[par][k].at[pl.ds(r, 1)], sem.at[par]).start(priority=k)

    def wait(par):
        for k in range(TOP_K):
            pltpu.make_async_copy(yb_hbm.at[pl.ds(0, tm)], bufs[par][k], sem.at[par]).wait()

    @pl.when(i == 0)
    def _():
        gather(i, 0)

    def step(par):
        wait(par)
        gather(jnp.minimum(i + 1, last), 1 - par)
        w0 = w_ref[:, TOP_K:TOP_K + 1]
        w1 = w_ref[:, TOP_K + 1:TOP_K + 2]
        h3 = h_ref[...] + w0 * bufs[par][0][...] + w1 * bufs[par][1][...]
        o_ref[...] = _rms(h3, nw_ref[...])

        @pl.when(i == last)
        def _():
            wait(1 - par)

    for par in range(2):
        pl.when(i % 2 == par)(functools.partial(step, par))


def _combine(slot, h2, route, nw, yb, tm):
    t, d = h2.shape
    tm = min(tm, t)
    grid_spec = pltpu.PrefetchScalarGridSpec(
        num_scalar_prefetch=1,
        grid=(t // tm,),
        in_specs=[
            pl.BlockSpec((tm, d), lambda i, s: (i, 0)),
            pl.BlockSpec((tm, route.shape[1]), lambda i, s: (i, 0)),
            pl.BlockSpec((1, d), lambda i, s: (0, 0)),
            pl.BlockSpec(memory_space=pl.ANY),
        ],
        out_specs=pl.BlockSpec((tm, d), lambda i, s: (i, 0)),
        scratch_shapes=[pltpu.VMEM((tm, d), F32)] * (2 * TOP_K) + [pltpu.SemaphoreType.DMA((2,))],
    )
    return pl.pallas_call(
        _combine_kernel,
        grid_spec=grid_spec,
        out_shape=jax.ShapeDtypeStruct((t, d), F32),
        compiler_params=_cparams(("arbitrary",)),
        name="combine",
    )(slot, h2, route, nw, yb)


def _slots(route, t):
    flat_e = route[:, :TOP_K].astype(jnp.int32).reshape(-1)
    n_pairs = t * TOP_K
    tile = min(256, n_pairs)
    onehot = flat_e[:, None] == jnp.arange(N_EXPERTS, dtype=jnp.int32)[None, :]
    oh3 = onehot.reshape(n_pairs // tile, tile, N_EXPERTS)
    tri = (jnp.arange(tile)[:, None] >= jnp.arange(tile)[None, :]).astype(BF16)
    within = jnp.einsum('ij,tjk->tik', tri, oh3.astype(BF16), preferred_element_type=F32)
    tot = within[:, -1, :]
    tile_base = jnp.cumsum(tot, axis=0) - tot
    csum = (within + tile_base[:, None, :]).reshape(n_pairs, N_EXPERTS)
    rank = jnp.sum(jnp.where(onehot, csum, 0.0), axis=1).astype(jnp.int32) - 1
    counts = (tile_base[-1] + tot[-1]).astype(jnp.int32)
    padded = (counts + MOE_BLOCK - 1) // MOE_BLOCK * MOE_BLOCK
    pad_end = jnp.cumsum(padded)
    pad_start = pad_end - padded
    slot = jnp.sum(jnp.where(onehot, pad_start[None, :], 0), axis=1).astype(jnp.int32) + rank
    n_slots = t * TOP_K + N_EXPERTS * MOE_BLOCK
    n_blocks = n_slots // MOE_BLOCK
    block_start = jnp.arange(n_blocks, dtype=jnp.int32) * MOE_BLOCK
    block_e = jnp.minimum(jnp.sum((pad_end[None, :] <= block_start[:, None]).astype(jnp.int32), axis=1),
                          N_EXPERTS - 1)
    n_valid = (pad_end[-1:] // MOE_BLOCK).astype(jnp.int32)
    return slot, pad_start + counts, padded - counts, block_e, n_valid, n_slots


def _pad_lanes(v, n=LANES):
    return jnp.pad(v, ((0, 0), (0, n - v.shape[1])))


def kernel(x, mem, norm_mix_w, w_in, conv_w, conv_b, dt_bias, a_log, d_skip, ssd_norm_w, w_ssd_branch, w_sb_branch, w_mix_out, norm_xa_w, norm_mem_w, w_xq, w_xk, w_xv, w_xo, norm_moe_w, w_router_group, w_router_expert, w_expert_gate, w_expert_up, w_expert_down, norm_final_w):
    bsz, seq, d = x.shape
    depth = w_in.shape[0]
    t = bsz * seq
    inner = ssd_norm_w.shape[1]
    heads = dt_bias.shape[1]
    bc = SSD_GROUPS * SSD_STATE
    conv_dim = inner + 2 * bc
    sb_width = SB_HEADS * SB_HEAD_DIM
    col_dt = inner + conv_dim
    col_qkv = col_dt + heads
    col_g = col_qkv + 3 * sb_width
    g_col0 = 2 * inner
    bc_col0 = g_col0 + 2 * d
    q_col0 = bc_col0 + 2 * bc
    n_main = q_col0 + 3 * sb_width
    tn_proj = 1280
    n_proj = -(-n_main // tn_proj) * tn_proj
    tn_merge = 1024

    assert depth == 1, "single-layer configuration: the final RMSNorm is fused into the last MoE combine"
    h = x.reshape(t, d)
    for l in range(depth):
        w_l = w_in[l]
        w_main = jnp.concatenate([w_l[:, :2 * inner], w_l[:, col_g:], w_l[:, 2 * inner:col_dt],
                                  w_l[:, col_qkv:col_g], jnp.zeros((d, n_proj - n_main), F32)],
                                 axis=1).astype(BF16)
        w_dt = _pad_lanes(w_l[:, col_dt:col_dt + heads]).astype(BF16)
        e_mat = (jnp.arange(inner, dtype=jnp.int32)[None, :] // SSD_HEAD_DIM
                 == jnp.arange(LANES, dtype=jnp.int32)[:, None]).astype(BF16)
        dskip_x = jnp.repeat(d_skip[l], SSD_HEAD_DIM)[None, :]

        proj, dt_raw = _norm_matmul(h, norm_mix_w[l][None, :], w_main, w_dt, tm=1024, tn=tn_proj)
        y_ssd = _ssd(proj, dt_raw, conv_w[l], conv_b[l][None, :], _pad_lanes(dt_bias[l][None, :]),
                     _pad_lanes(a_log[l][None, :]), dskip_x, ssd_norm_w[l][None, :], e_mat,
                     bsz, seq, inner, bc, bc_col0 // bc)
        y_sb = _stickbreak(proj.reshape(bsz, seq, n_proj), bsz, seq, q_col0 // SB_HEAD_DIM)
        h1 = _merge(h, y_ssd, y_sb.reshape(t, sb_width), proj,
                    w_ssd_branch[l].astype(BF16), w_sb_branch[l].astype(BF16), w_mix_out[l].astype(BF16),
                    g_col0 // tn_merge, tm=512, tn=tn_merge)

        m_len = mem.shape[1]
        w_kv = jnp.concatenate([w_xk[l], w_xv[l]], axis=1).astype(BF16)
        kv, _ = _norm_matmul(mem.reshape(bsz * m_len, d), norm_mem_w[l][None, :], w_kv,
                             jnp.zeros((d, LANES), BF16), tm=bsz * m_len, tn=512)
        w_r = _pad_lanes(jnp.concatenate([w_router_group[l], w_router_expert[l]], axis=1))
        wr_hi = w_r.astype(BF16)
        wr_lo = (w_r - wr_hi.astype(F32)).astype(BF16)
        h2, um, route = _xattn(h1, kv, norm_xa_w[l][None, :], w_xq[l].astype(BF16), w_xo[l].astype(BF16),
                               norm_moe_w[l][None, :], jnp.concatenate([wr_hi, wr_lo], axis=1), seq, tm=512)

        slot, pad_row, pad_n, block_e, n_valid, n_slots = _slots(route, t)
        xs = _dispatch(slot, pad_row, pad_n, n_valid, um, n_slots, tm=256)
        yb = _experts(block_e, n_valid, xs, w_expert_gate[l], w_expert_up[l], w_expert_down[l])
        h = _combine(slot, h2, route, norm_final_w[None, :], yb, tm=256)
    return h.reshape(bsz, seq, d)
```

```python
import functools

import jax
import jax.numpy as jnp
from jax import lax
from jax.experimental import pallas as pl
from jax.experimental.pallas import tpu as pltpu

F32 = jnp.float32
BF16 = jnp.bfloat16
EPS = 1e-6

SSD_HEAD_DIM = 64
SSD_GROUPS = 4
SSD_STATE = 128
SSD_CONV = 4
SSD_CHUNK = 128
SB_HEADS = 4
SB_HEAD_DIM = 128
SB_BLOCK = 128
SB_SUB = 8
XA_HEADS = 4
XA_HEAD_DIM = 128
N_GROUPS = 4
EXPERTS_PER_GROUP = 8
N_EXPERTS = N_GROUPS * EXPERTS_PER_GROUP
TOP_K = 2
MOE_BLOCK = 256

LANES = 128
SUBLANES = 8
VMEM_LIMIT = 56 * 1024 * 1024

SB_EXIT_LOG = -88.0
NEG_BIG = -1e30


def _cparams(sem):
    return pltpu.CompilerParams(dimension_semantics=sem, vmem_limit_bytes=VMEM_LIMIT)


def _rms(x, w):
    var = jnp.mean(x * x, axis=-1, keepdims=True)
    return x * lax.rsqrt(var + EPS) * w


def _split2(v):
    hi = v.astype(BF16)
    lo = (v - hi.astype(F32)).astype(BF16)
    return hi, lo


def _dot(a, b):
    return jnp.dot(a, b, preferred_element_type=F32)


def _silu(x):
    return x / (1.0 + jnp.exp(-x))


def _softplus(x):
    return jnp.maximum(x, 0.0) + jnp.log(1.0 + jnp.exp(-jnp.abs(x)))


def _norm_matmul_kernel(x_ref, nw_ref, w_ref, ws_ref, o_ref, os_ref, u_ref):
    @pl.when(pl.program_id(1) == 0)
    def _():
        u = _rms(x_ref[...], nw_ref[...]).astype(BF16)
        u_ref[...] = u
        os_ref[...] = _dot(u, ws_ref[...])

    o_ref[...] = _dot(u_ref[...], w_ref[...]).astype(o_ref.dtype)


def _norm_matmul(x, nw, w, w_side, tm, tn):
    m, k = x.shape
    n = w.shape[1]
    ns = w_side.shape[1]
    tm = min(tm, m)
    return pl.pallas_call(
        _norm_matmul_kernel,
        grid=(m // tm, n // tn),
        in_specs=[
            pl.BlockSpec((tm, k), lambda i, j: (i, 0)),
            pl.BlockSpec((1, k), lambda i, j: (0, 0)),
            pl.BlockSpec((k, tn), lambda i, j: (0, j)),
            pl.BlockSpec((k, ns), lambda i, j: (0, 0)),
        ],
        out_specs=[
            pl.BlockSpec((tm, tn), lambda i, j: (i, j)),
            pl.BlockSpec((tm, ns), lambda i, j: (i, 0)),
        ],
        out_shape=[jax.ShapeDtypeStruct((m, n), BF16), jax.ShapeDtypeStruct((m, ns), F32)],
        scratch_shapes=[pltpu.VMEM((tm, k), BF16)],
        compiler_params=_cparams(("parallel", "arbitrary")),
        name="norm_matmul",
    )(x, nw, w, w_side)


def _ssd_kernel(z_ref, xs_ref, b_ref, c_ref, dtr_ref, cw_ref, cb_ref, dtb_ref, alog_ref, dsk_ref,
                nw_ref, e_ref, o_ref,
                tail_ref, state_ref, xact_ref, bact_ref, cact_ref, dtx_ref, eax_ref, wsx_ref, y_ref):
    L = SSD_CHUNK
    inner = xs_ref.shape[1]
    bc = b_ref.shape[1]
    n_pairs = inner // LANES

    @pl.when(pl.program_id(1) == 0)
    def _():
        tail_ref[...] = jnp.zeros_like(tail_ref)
        state_ref[...] = jnp.zeros_like(state_ref)

    row8 = lax.broadcasted_iota(jnp.int32, (SUBLANES, 2 * LANES), 0)

    def conv_seg(src_ref, col0, width, dst_ref):
        for j in range(0, width, 2 * LANES):
            cols = slice(col0 + j, col0 + j + 2 * LANES)
            xin = src_ref[:, j:j + 2 * LANES].astype(F32)
            t8 = tail_ref[:, cols]
            acc = xin * cw_ref[SSD_CONV - 1:SSD_CONV, cols] + cb_ref[:, cols]
            for s in range(1, SSD_CONV):
                r = pltpu.roll(xin, s, 0)
                top = jnp.where(row8 < s, pltpu.roll(t8, s, 0), r[:SUBLANES])
                r = jnp.concatenate([top, r[SUBLANES:]], axis=0)
                acc = acc + r * cw_ref[SSD_CONV - 1 - s:SSD_CONV - s, cols]
            tail_ref[:, cols] = xin[L - SUBLANES:]
            dst_ref[:, j:j + 2 * LANES] = _silu(acc).astype(dst_ref.dtype)

    conv_seg(xs_ref, 0, inner, xact_ref)
    conv_seg(b_ref, inner, bc, bact_ref)
    conv_seg(c_ref, inner + bc, bc, cact_ref)

    dt = _softplus(dtr_ref[...] + dtb_ref[...])
    da = dt * (-jnp.exp(alog_ref[...]))
    rowl = lax.broadcasted_iota(jnp.int32, (L, L), 0)
    coll = lax.broadcasted_iota(jnp.int32, (L, L), 1)
    lower = rowl >= coll
    tri = jnp.where(lower, 1.0, 0.0).astype(BF16)
    d1 = da.astype(BF16)
    r1 = da - d1.astype(F32)
    d2 = r1.astype(BF16)
    d3 = (r1 - d2.astype(F32)).astype(BF16)
    a_cum = _dot(tri, d1) + _dot(tri, d2) + _dot(tri, d3)
    a_cum_t = a_cum.T
    a_last = a_cum[L - 1:L, :]
    e_mat = e_ref[...]

    def expand(v):
        hi, lo = _split2(v)
        return _dot(hi, e_mat) + _dot(lo, e_mat)

    dtx_ref[...] = expand(dt)
    eax_ref[...] = expand(jnp.exp(a_cum))
    wsx_ref[...] = expand(jnp.exp(a_last - a_cum))
    elx = expand(jnp.broadcast_to(jnp.exp(a_last), (SUBLANES, LANES)))[0:1]

    lane = lax.broadcasted_iota(jnp.int32, (L, LANES), 1)
    pairs_per_group = n_pairs // SSD_GROUPS
    for g in range(SSD_GROUPS):
        gcols = slice(g * SSD_STATE, (g + 1) * SSD_STATE)
        bg_t = bact_ref[:, gcols].T.astype(BF16)
        cg = cact_ref[:, gcols]
        cb = _dot(cg, bg_t)
        for pp in range(pairs_per_group):
            p = g * pairs_per_group + pp
            pc = slice(p * LANES, (p + 1) * LANES)
            ms = []
            for hh in (2 * p, 2 * p + 1):
                seg = a_cum[:, hh:hh + 1] - a_cum_t[hh:hh + 1, :]
                dec = jnp.exp(jnp.where(lower, seg, NEG_BIG))
                ms.append((cb * dec).astype(BF16))
            lhs = jnp.concatenate(ms, axis=1)
            xs_p = xact_ref[:, pc]
            xdt = xs_p * dtx_ref[:, pc]
            rhs = jnp.concatenate([jnp.where(lane < SSD_HEAD_DIM, xdt, 0.0).astype(BF16),
                                   jnp.where(lane >= SSD_HEAD_DIM, xdt, 0.0).astype(BF16)], axis=0)
            st = state_ref[p]
            y = _dot(lhs, rhs)
            y = y + _dot(cg, st.astype(BF16)) * eax_ref[:, pc]
            y = y + xs_p * dsk_ref[:, pc]
            y_ref[:, pc] = y
            xw = (xdt * wsx_ref[:, pc]).astype(BF16)
            state_ref[p] = st * elx[:, pc] + _dot(bg_t, xw)

    yg = y_ref[...] * _silu(z_ref[...].astype(F32))
    o_ref[...] = _rms(yg, nw_ref[...]).astype(o_ref.dtype)


def _ssd(proj, dt_raw, conv_w, conv_b, dt_bias_p, a_log_p, dskip_x, norm_w, e_mat, bsz, seq, inner, bc, b_blk):
    L = SSD_CHUNK
    nc = seq // L
    conv_dim = inner + 2 * bc
    n_pairs = inner // LANES

    def rows(b, c):
        return b * nc + c

    return pl.pallas_call(
        _ssd_kernel,
        grid=(bsz, nc),
        in_specs=[
            pl.BlockSpec((L, inner), lambda b, c: (rows(b, c), 0)),
            pl.BlockSpec((L, inner), lambda b, c: (rows(b, c), 1)),
            pl.BlockSpec((L, bc), lambda b, c: (rows(b, c), b_blk)),
            pl.BlockSpec((L, bc), lambda b, c: (rows(b, c), b_blk + 1)),
            pl.BlockSpec((L, LANES), lambda b, c: (rows(b, c), 0)),
            pl.BlockSpec((SSD_CONV, conv_dim), lambda b, c: (0, 0)),
            pl.BlockSpec((1, conv_dim), lambda b, c: (0, 0)),
            pl.BlockSpec((1, LANES), lambda b, c: (0, 0)),
            pl.BlockSpec((1, LANES), lambda b, c: (0, 0)),
            pl.BlockSpec((1, inner), lambda b, c: (0, 0)),
            pl.BlockSpec((1, inner), lambda b, c: (0, 0)),
            pl.BlockSpec((LANES, inner), lambda b, c: (0, 0)),
        ],
        out_specs=pl.BlockSpec((L, inner), lambda b, c: (rows(b, c), 0)),
        out_shape=jax.ShapeDtypeStruct((bsz * seq, inner), BF16),
        scratch_shapes=[
            pltpu.VMEM((SUBLANES, conv_dim), F32),
            pltpu.VMEM((n_pairs, SSD_STATE, LANES), F32),
            pltpu.VMEM((L, inner), F32),
            pltpu.VMEM((L, bc), F32),
            pltpu.VMEM((L, bc), BF16),
            pltpu.VMEM((L, inner), F32),
            pltpu.VMEM((L, inner), F32),
            pltpu.VMEM((L, inner), F32),
            pltpu.VMEM((L, inner), F32),
        ],
        compiler_params=_cparams(("parallel", "arbitrary")),
        name="ssd",
    )(proj, proj, proj, proj, dt_raw, conv_w, conv_b, dt_bias_p, a_log_p, dskip_x, norm_w, e_mat)


def _sb_kernel(q_ref, k_ref, v_ref, o_ref, acc_ref, carry_ref):
    BL = SB_BLOCK
    i = pl.program_id(2)
    scale = SB_HEAD_DIM ** -0.5
    row = lax.broadcasted_iota(jnp.int32, (BL, BL), 0)
    col = lax.broadcasted_iota(jnp.int32, (BL, BL), 1)
    causal = col < row
    r2 = lax.broadcasted_iota(jnp.int32, (BL, 2 * BL), 0)
    c2 = lax.broadcasted_iota(jnp.int32, (BL, 2 * BL), 1)
    tri = jnp.where((r2 > c2) | (c2 >= BL), 1.0, 0.0).astype(BF16)

    def sweep(s, first):
        subs = range(SB_SUB)
        kbs = [i * SB_SUB + j - s for j in subs]
        starts = [pl.multiple_of(jnp.maximum(kb, 0) * BL, BL) for kb in kbs]
        zls = [lax.dot_general(q_ref[j * BL:(j + 1) * BL, :], k_ref[pl.ds(starts[j], BL), :],
                               (((1,), (1,)), ((), ())), preferred_element_type=F32) * scale for j in subs]
        sps = [_softplus(zl) for zl in zls]
        log_betas = [zl - sp for zl, sp in zip(zls, sps)]
        log_keeps = [jnp.where(causal, -sp, 0.0) if first else -sp for sp in sps]
        splits = [_split2(lk) for lk in log_keeps]
        t2s = [_dot(hi, tri) + _dot(lo, tri) for hi, lo in splits]
        if first:
            atts = [jnp.where(causal, jnp.exp(lb + t2[:, :BL]), 0.0) for lb, t2 in zip(log_betas, t2s)]
            carries = [t2[:, BL:] for t2 in t2s]
        else:
            olds = [carry_ref[j] for j in subs]
            atts = [jnp.where(kb >= 0, jnp.exp(lb + t2[:, :BL] + old), 0.0)
                    for kb, lb, t2, old in zip(kbs, log_betas, t2s, olds)]
            carries = [old + jnp.where(kb >= 0, t2[:, BL:], 0.0) for kb, t2, old in zip(kbs, t2s, olds)]
        pvs = [_dot(atts[j].astype(BF16), v_ref[pl.ds(starts[j], BL), :]) for j in subs]
        worst = None
        for j in subs:
            if first:
                acc_ref[j] = pvs[j]
            else:
                acc_ref[j] += pvs[j]
            carry_ref[j] = carries[j]
            live = jnp.where(kbs[j] >= 1, carries[j], NEG_BIG)
            worst = live if worst is None else jnp.maximum(worst, live)
        return jnp.max(worst) > SB_EXIT_LOG

    sweep(0, True)
    go = sweep(1, False)

    def body(st):
        s, _ = st
        return s + 1, sweep(s, False)

    lax.while_loop(lambda st: st[1], body, (jnp.int32(2), go))
    for j in range(SB_SUB):
        o_ref[j * BL:(j + 1) * BL, :] = acc_ref[j].astype(o_ref.dtype)


def _stickbreak(proj3, bsz, seq, q_blk0):
    tq = SB_SUB * SB_BLOCK
    return pl.pallas_call(
        _sb_kernel,
        grid=(bsz, SB_HEADS, seq // tq),
        in_specs=[
            pl.BlockSpec((None, tq, SB_HEAD_DIM), lambda b, h, i: (b, i, q_blk0 + h)),
            pl.BlockSpec((None, seq, SB_HEAD_DIM), lambda b, h, i: (b, 0, q_blk0 + SB_HEADS + h)),
            pl.BlockSpec((None, seq, SB_HEAD_DIM), lambda b, h, i: (b, 0, q_blk0 + 2 * SB_HEADS + h)),
        ],
        out_specs=pl.BlockSpec((None, tq, SB_HEAD_DIM), lambda b, h, i: (b, i, h)),
        out_shape=jax.ShapeDtypeStruct((bsz, seq, SB_HEADS * SB_HEAD_DIM), BF16),
        scratch_shapes=[pltpu.VMEM((SB_SUB, SB_BLOCK, SB_HEAD_DIM), F32),
                        pltpu.VMEM((SB_SUB, SB_BLOCK, SB_BLOCK), F32)],
        compiler_params=_cparams(("parallel", "parallel", "arbitrary")),
        name="stickbreak",
    )(proj3, proj3, proj3)


def _merge_kernel(x_ref, ys_ref, yb_ref, g1_ref, g2_ref, w1_ref, w2_ref, wm_ref, o_ref):
    @pl.when(pl.program_id(1) == 0)
    def _():
        o_ref[...] = x_ref[...]

    a = _dot(ys_ref[...], w1_ref[...])
    b = _dot(yb_ref[...], w2_ref[...])
    s1 = jax.nn.sigmoid(g1_ref[...].astype(F32))
    s2 = jax.nn.sigmoid(g2_ref[...].astype(F32))
    m = (s1 * a + s2 * b).astype(BF16)
    o_ref[...] += _dot(m, wm_ref[...])


def _merge(x2, y_ssd, y_sb, proj, w1, w2, wm, g_blk0, tm, tn):
    t, d = x2.shape
    tm = min(tm, t)
    nj = d // tn
    return pl.pallas_call(
        _merge_kernel,
        grid=(t // tm, nj),
        in_specs=[
            pl.BlockSpec((tm, d), lambda i, j: (i, 0)),
            pl.BlockSpec((tm, y_ssd.shape[1]), lambda i, j: (i, 0)),
            pl.BlockSpec((tm, y_sb.shape[1]), lambda i, j: (i, 0)),
            pl.BlockSpec((tm, tn), lambda i, j: (i, g_blk0 + j)),
            pl.BlockSpec((tm, tn), lambda i, j: (i, g_blk0 + nj + j)),
            pl.BlockSpec((w1.shape[0], tn), lambda i, j: (0, j)),
            pl.BlockSpec((w2.shape[0], tn), lambda i, j: (0, j)),
            pl.BlockSpec((tn, d), lambda i, j: (j, 0)),
        ],
        out_specs=pl.BlockSpec((tm, d), lambda i, j: (i, 0)),
        out_shape=jax.ShapeDtypeStruct((t, d), F32),
        compiler_params=_cparams(("parallel", "arbitrary")),
        name="merge",
    )(x2, y_ssd, y_sb, proj, proj, w1, w2, wm)


def _xattn_kernel(h_ref, kv_ref, nxa_ref, wq_ref, wo_ref, nmoe_ref, wr_ref, h2_ref, um_ref, rt_ref):
    h1 = h_ref[...]
    un = _rms(h1, nxa_ref[...]).astype(BF16)
    q = _dot(un, wq_ref[...]).astype(BF16)
    width = XA_HEADS * XA_HEAD_DIM
    outs = []
    for hd in range(XA_HEADS):
        cs = slice(hd * XA_HEAD_DIM, (hd + 1) * XA_HEAD_DIM)
        k = kv_ref[:, cs]
        v = kv_ref[:, width + hd * XA_HEAD_DIM: width + (hd + 1) * XA_HEAD_DIM]
        sc = lax.dot_general(q[:, cs], k, (((1,), (1,)), ((), ())),
                             preferred_element_type=F32) * (XA_HEAD_DIM ** -0.5)
        sc = sc - jnp.max(sc, axis=-1, keepdims=True)
        p = jnp.exp(sc)
        p = p / jnp.sum(p, axis=-1, keepdims=True)
        outs.append(_dot(p.astype(BF16), v).astype(BF16))
    o = jnp.concatenate(outs, axis=1)
    h2 = h1 + _dot(o, wo_ref[...])
    h2_ref[...] = h2
    um = _rms(h2, nmoe_ref[...])
    um_ref[...] = um
    hi, lo = _split2(um)
    nr = rt_ref.shape[1]
    both = _dot(hi, wr_ref[...])
    lg = both[:, :nr] + both[:, nr:] + _dot(lo, wr_ref[:, :nr])

    lane = lax.broadcasted_iota(jnp.int32, lg.shape, 1)
    lane_f = lane.astype(F32)
    is_g = lane < N_GROUPS
    gl = jnp.where(is_g, lg, NEG_BIG)
    gmax = jnp.max(gl, axis=-1, keepdims=True)
    g_sel = jnp.min(jnp.where(gl == gmax, lane_f, float(LANES)), axis=-1, keepdims=True)
    g_gate = 1.0 / jnp.sum(jnp.where(is_g, jnp.exp(gl - gmax), 0.0), axis=-1, keepdims=True)
    lo_lane = N_GROUPS + EXPERTS_PER_GROUP * g_sel
    el = jnp.where((lane_f >= lo_lane) & (lane_f < lo_lane + EXPERTS_PER_GROUP), lg, NEG_BIG)
    m1 = jnp.max(el, axis=-1, keepdims=True)
    i1 = jnp.min(jnp.where(el == m1, lane_f, float(LANES)), axis=-1, keepdims=True)
    el2 = jnp.where(lane_f == i1, NEG_BIG, el)
    m2 = jnp.max(el2, axis=-1, keepdims=True)
    i2 = jnp.min(jnp.where(el2 == m2, lane_f, float(LANES)), axis=-1, keepdims=True)
    ex = jnp.exp(m2 - m1)
    p1 = 1.0 / (1.0 + ex)
    rt_ref[...] = jnp.where(lane == 0, i1 - N_GROUPS,
                            jnp.where(lane == 1, i2 - N_GROUPS,
                                      jnp.where(lane == 2, p1 * g_gate,
                                                jnp.where(lane == 3, ex * p1 * g_gate, 0.0))))


def _xattn(h1, kv, nxa, wq, wo, nmoe, wr, seq, tm):
    t, d = h1.shape
    tm = min(tm, seq)
    per_b = seq // tm
    m_len = kv.shape[0] // (t // seq)
    nr = wr.shape[1] // 2
    return pl.pallas_call(
        _xattn_kernel,
        grid=(t // tm,),
        in_specs=[
            pl.BlockSpec((tm, d), lambda i: (i, 0)),
            pl.BlockSpec((m_len, kv.shape[1]), lambda i: (i // per_b, 0)),
            pl.BlockSpec((1, d), lambda i: (0, 0)),
            pl.BlockSpec(wq.shape, lambda i: (0, 0)),
            pl.BlockSpec(wo.shape, lambda i: (0, 0)),
            pl.BlockSpec((1, d), lambda i: (0, 0)),
            pl.BlockSpec((d, 2 * nr), lambda i: (0, 0)),
        ],
        out_specs=[
            pl.BlockSpec((tm, d), lambda i: (i, 0)),
            pl.BlockSpec((tm, d), lambda i: (i, 0)),
            pl.BlockSpec((tm, nr), lambda i: (i, 0)),
        ],
        out_shape=[jax.ShapeDtypeStruct((t, d), F32), jax.ShapeDtypeStruct((t, d), F32),
                   jax.ShapeDtypeStruct((t, nr), F32)],
        compiler_params=_cparams(("parallel",)),
        name="xattn",
    )(h1, kv, nxa, wq, wo, nmoe, wr)


def _dispatch_kernel(slot_ref, padrow_ref, padn_ref, nv_ref, um_ref, xs_hbm, sa_ref, sb_ref, z_ref, sem, zsem):
    i = pl.program_id(0)
    tm = um_ref.shape[0]
    last = pl.num_programs(0) - 1
    stage = (sa_ref, sb_ref)

    def zero_rows(start):
        for e in range(N_EXPERTS):
            def body(j, c, e=e):
                cp = pltpu.make_async_copy(z_ref.at[pl.ds(0, 1)], xs_hbm.at[pl.ds(padrow_ref[e] + j, 1)], zsem)
                if start:
                    cp.start()
                else:
                    cp.wait()
                return c

            lax.fori_loop(0, padn_ref[e], body, 0)

        def tail(b, c):
            cp = pltpu.make_async_copy(
                z_ref, xs_hbm.at[pl.ds(pl.multiple_of(b * MOE_BLOCK, MOE_BLOCK), MOE_BLOCK)], zsem)
            if start:
                cp.start()
            else:
                cp.wait()
            return c

        lax.fori_loop(nv_ref[0], xs_hbm.shape[0] // MOE_BLOCK, tail, 0)

    @pl.when(i == 0)
    def _():
        z_ref[...] = jnp.zeros_like(z_ref)
        zero_rows(True)

    def wait(par):
        for _ in range(TOP_K):
            pltpu.make_async_copy(stage[par], xs_hbm.at[pl.ds(0, tm)], sem.at[par]).wait()

    def step(par):
        pl.when(i >= 2)(functools.partial(wait, par))
        stage[par][...] = um_ref[...]
        base = i * tm * TOP_K
        for r in range(tm):
            for k in range(TOP_K):
                pltpu.make_async_copy(stage[par].at[pl.ds(r, 1)],
                                      xs_hbm.at[pl.ds(slot_ref[base + TOP_K * r + k], 1)],
                                      sem.at[par]).start(priority=k)

        @pl.when(i == last)
        def _():
            wait(par)
            pl.when(i >= 1)(functools.partial(wait, 1 - par))

    for par in range(2):
        pl.when(i % 2 == par)(functools.partial(step, par))

    @pl.when(i == last)
    def _():
        zero_rows(False)


def _dispatch(slot, pad_row, pad_n, n_valid, um, n_slots, tm):
    t, d = um.shape
    tm = min(tm, t)
    grid_spec = pltpu.PrefetchScalarGridSpec(
        num_scalar_prefetch=4,
        grid=(t // tm,),
        in_specs=[pl.BlockSpec((tm, d), lambda i, s, pr, pn, nv: (i, 0))],
        out_specs=pl.BlockSpec(memory_space=pl.ANY),
        scratch_shapes=[pltpu.VMEM((tm, d), F32), pltpu.VMEM((tm, d), F32),
                        pltpu.VMEM((MOE_BLOCK, d), F32),
                        pltpu.SemaphoreType.DMA((2,)), pltpu.SemaphoreType.DMA(())],
    )
    return pl.pallas_call(
        _dispatch_kernel,
        grid_spec=grid_spec,
        out_shape=jax.ShapeDtypeStruct((n_slots, d), F32),
        compiler_params=_cparams(("arbitrary",)),
        name="dispatch",
    )(slot, pad_row, pad_n, n_valid, um)


def _expert_kernel(be_ref, nv_ref, x_ref, wg_ref, wu_ref, wd_ref, o_ref, wgb_ref, wub_ref, wdb_ref):
    i = pl.program_id(0)

    @pl.when(jnp.logical_or(i == 0, be_ref[i] != be_ref[jnp.maximum(i - 1, 0)]))
    def _():
        wgb_ref[...] = wg_ref[...].astype(BF16)
        wub_ref[...] = wu_ref[...].astype(BF16)
        wdb_ref[...] = wd_ref[...].astype(BF16)

    @pl.when(i < nv_ref[0])
    def _():
        xb = x_ref[...].astype(BF16)
        hid = (_silu(_dot(xb, wgb_ref[...])) * _dot(xb, wub_ref[...])).astype(BF16)
        o_ref[...] = _dot(hid, wdb_ref[...])

    @pl.when(i >= nv_ref[0])
    def _():
        o_ref[...] = jnp.zeros_like(o_ref)


def _experts(block_e, n_valid, xs, wg, wu, wd):
    n_slots = xs.shape[0]
    d, ff = wg.shape[1:]

    def x_map(i, be, nv):
        return (jnp.minimum(i, nv[0] - 1), 0)

    grid_spec = pltpu.PrefetchScalarGridSpec(
        num_scalar_prefetch=2,
        grid=(n_slots // MOE_BLOCK,),
        in_specs=[
            pl.BlockSpec((MOE_BLOCK, d), x_map),
            pl.BlockSpec((None, d, ff), lambda i, be, nv: (be[i], 0, 0)),
            pl.BlockSpec((None, d, ff), lambda i, be, nv: (be[i], 0, 0)),
            pl.BlockSpec((None, ff, d), lambda i, be, nv: (be[i], 0, 0)),
        ],
        out_specs=pl.BlockSpec((MOE_BLOCK, d), lambda i, be, nv: (i, 0)),
        scratch_shapes=[pltpu.VMEM((d, ff), BF16), pltpu.VMEM((d, ff), BF16), pltpu.VMEM((ff, d), BF16)],
    )
    return pl.pallas_call(
        _expert_kernel,
        grid_spec=grid_spec,
        out_shape=jax.ShapeDtypeStruct((n_slots, d), F32),
        compiler_params=_cparams(("arbitrary",)),
        name="experts",
    )(block_e, n_valid, xs, wg, wu, wd)


def _combine_kernel(slot_ref, h_ref, w_ref, nw_ref, yb_hbm, o_ref, ya0_ref, ya1_ref, yb0_ref, yb1_ref, sem):
    i = pl.program_id(0)
    tm = h_ref.shape[0]
    last = pl.num_programs(0) - 1
    bufs = ((ya0_ref, ya1_ref), (yb0_ref, yb1_ref))

    def gather(blk, par):
        base = blk * tm * TOP_K
        for r in range(tm):
            for k in range(TOP_K):
                pltpu.make_async_copy(yb_hbm.at[pl.ds(slot_ref[base + TOP_K * r + k], 1)],
                                      bufs[par][k].at[pl.ds(r, 1)], sem.at[par]).start(priority=k)

    def wait(par):
        for k in range(TOP_K):
            pltpu.make_async_copy(yb_hbm.at[pl.ds(0, tm)], bufs[par][k], sem.at[par]).wait()

    @pl.when(i == 0)
    def _():
        gather(i, 0)

    def step(par):
        wait(par)
        gather(jnp.minimum(i + 1, last), 1 - par)
        w0 = w_ref[:, TOP_K:TOP_K + 1]
        w1 = w_ref[:, TOP_K + 1:TOP_K + 2]
        h3 = h_ref[...] + w0 * bufs[par][0][...] + w1 * bufs[par][1][...]
        o_ref[...] = _rms(h3, nw_ref[...])

        @pl.when(i == last)
        def _():
            wait(1 - par)

    for par in range(2):
        pl.when(i % 2 == par)(functools.partial(step, par))


def _combine(slot, h2, route, nw, yb, tm):
    t, d = h2.shape
    tm = min(tm, t)
    grid_spec = pltpu.PrefetchScalarGridSpec(
        num_scalar_prefetch=1,
        grid=(t // tm,),
        in_specs=[
            pl.BlockSpec((tm, d), lambda i, s: (i, 0)),
            pl.BlockSpec((tm, route.shape[1]), lambda i, s: (i, 0)),
            pl.BlockSpec((1, d), lambda i, s: (0, 0)),
            pl.BlockSpec(memory_space=pl.ANY),
        ],
        out_specs=pl.BlockSpec((tm, d), lambda i, s: (i, 0)),
        scratch_shapes=[pltpu.VMEM((tm, d), F32)] * (2 * TOP_K) + [pltpu.SemaphoreType.DMA((2,))],
    )
    return pl.pallas_call(
        _combine_kernel,
        grid_spec=grid_spec,
        out_shape=jax.ShapeDtypeStruct((t, d), F32),
        compiler_params=_cparams(("arbitrary",)),
        name="combine",
    )(slot, h2, route, nw, yb)


def _slots(route, t):
    flat_e = route[:, :TOP_K].astype(jnp.int32).reshape(-1)
    n_pairs = t * TOP_K
    tile = min(256, n_pairs)
    onehot = flat_e[:, None] == jnp.arange(N_EXPERTS, dtype=jnp.int32)[None, :]
    oh3 = onehot.reshape(n_pairs // tile, tile, N_EXPERTS)
    tri = (jnp.arange(tile)[:, None] >= jnp.arange(tile)[None, :]).astype(BF16)
    within = jnp.einsum('ij,tjk->tik', tri, oh3.astype(BF16), preferred_element_type=F32)
    tot = within[:, -1, :]
    tile_base = jnp.cumsum(tot, axis=0) - tot
    csum = (within + tile_base[:, None, :]).reshape(n_pairs, N_EXPERTS)
    rank = jnp.sum(jnp.where(onehot, csum, 0.0), axis=1).astype(jnp.int32) - 1
    counts = (tile_base[-1] + tot[-1]).astype(jnp.int32)
    padded = (counts + MOE_BLOCK - 1) // MOE_BLOCK * MOE_BLOCK
    pad_end = jnp.cumsum(padded)
    pad_start = pad_end - padded
    slot = jnp.sum(jnp.where(onehot, pad_start[None, :], 0), axis=1).astype(jnp.int32) + rank
    n_slots = t * TOP_K + N_EXPERTS * MOE_BLOCK
    n_blocks = n_slots // MOE_BLOCK
    block_start = jnp.arange(n_blocks, dtype=jnp.int32) * MOE_BLOCK
    block_e = jnp.minimum(jnp.sum((pad_end[None, :] <= block_start[:, None]).astype(jnp.int32), axis=1),
                          N_EXPERTS - 1)
    n_valid = (pad_end[-1:] // MOE_BLOCK).astype(jnp.int32)
    return slot, pad_start + counts, padded - counts, block_e, n_valid, n_slots


def _pad_lanes(v, n=LANES):
    return jnp.pad(v, ((0, 0), (0, n - v.shape[1])))


def kernel(x, mem, norm_mix_w, w_in, conv_w, conv_b, dt_bias, a_log, d_skip, ssd_norm_w, w_ssd_branch, w_sb_branch, w_mix_out, norm_xa_w, norm_mem_w, w_xq, w_xk, w_xv, w_xo, norm_moe_w, w_router_group, w_router_expert, w_expert_gate, w_expert_up, w_expert_down, norm_final_w):
    bsz, seq, d = x.shape
    depth = w_in.shape[0]
    t = bsz * seq
    inner = ssd_norm_w.shape[1]
    heads = dt_bias.shape[1]
    bc = SSD_GROUPS * SSD_STATE
    conv_dim = inner + 2 * bc
    sb_width = SB_HEADS * SB_HEAD_DIM
    col_dt = inner + conv_dim
    col_qkv = col_dt + heads
    col_g = col_qkv + 3 * sb_width
    g_col0 = 2 * inner
    bc_col0 = g_col0 + 2 * d
    q_col0 = bc_col0 + 2 * bc
    n_main = q_col0 + 3 * sb_width
    tn_proj = 1280
    n_proj = -(-n_main // tn_proj) * tn_proj
    tn_merge = 1024

    assert depth == 1, "single-layer configuration: the final RMSNorm is fused into the last MoE combine"
    h = x.reshape(t, d)
    for l in range(depth):
        w_l = w_in[l]
        w_main = jnp.concatenate([w_l[:, :2 * inner], w_l[:, col_g:], w_l[:, 2 * inner:col_dt],
                                  w_l[:, col_qkv:col_g], jnp.zeros((d, n_proj - n_main), F32)],
                                 axis=1).astype(BF16)
        w_dt = _pad_lanes(w_l[:, col_dt:col_dt + heads]).astype(BF16)
        e_mat = (jnp.arange(inner, dtype=jnp.int32)[None, :] // SSD_HEAD_DIM
                 == jnp.arange(LANES, dtype=jnp.int32)[:, None]).astype(BF16)
        dskip_x = jnp.repeat(d_skip[l], SSD_HEAD_DIM)[None, :]

        proj, dt_raw = _norm_matmul(h, norm_mix_w[l][None, :], w_main, w_dt, tm=1024, tn=tn_proj)
        y_ssd = _ssd(proj, dt_raw, conv_w[l], conv_b[l][None, :], _pad_lanes(dt_bias[l][None, :]),
                     _pad_lanes(a_log[l][None, :]), dskip_x, ssd_norm_w[l][None, :], e_mat,
                     bsz, seq, inner, bc, bc_col0 // bc)
        y_sb = _stickbreak(proj.reshape(bsz, seq, n_proj), bsz, seq, q_col0 // SB_HEAD_DIM)
        h1 = _merge(h, y_ssd, y_sb.reshape(t, sb_width), proj,
                    w_ssd_branch[l].astype(BF16), w_sb_branch[l].astype(BF16), w_mix_out[l].astype(BF16),
                    g_col0 // tn_merge, tm=512, tn=tn_merge)

        m_len = mem.shape[1]
        w_kv = jnp.concatenate([w_xk[l], w_xv[l]], axis=1).astype(BF16)
        kv, _ = _norm_matmul(mem.reshape(bsz * m_len, d), norm_mem_w[l][None, :], w_kv,
                             jnp.zeros((d, LANES), BF16), tm=bsz * m_len, tn=512)
        w_r = _pad_lanes(jnp.concatenate([w_router_group[l], w_router_expert[l]], axis=1))
        wr_hi = w_r.astype(BF16)
        wr_lo = (w_r - wr_hi.astype(F32)).astype(BF16)
        h2, um, route = _xattn(h1, kv, norm_xa_w[l][None, :], w_xq[l].astype(BF16), w_xo[l].astype(BF16),
                               norm_moe_w[l][None, :], jnp.concatenate([wr_hi, wr_lo], axis=1), seq, tm=512)

        slot, pad_row, pad_n, block_e, n_valid, n_slots = _slots(route, t)
        xs = _dispatch(slot, pad_row, pad_n, n_valid, um, n_slots, tm=256)
        yb = _experts(block_e, n_valid, xs, w_expert_gate[l], w_expert_up[l], w_expert_down[l])
        h = _combine(slot, h2, route, norm_final_w[None, :], yb, tm=256)
    return h.reshape(bsz, seq, d)
```

```python
import functools

import jax
import jax.numpy as jnp
from jax import lax
from jax.experimental import pallas as pl
from jax.experimental.pallas import tpu as pltpu

F32 = jnp.float32
BF16 = jnp.bfloat16
U32 = jnp.uint32
EPS = 1e-6

SSD_HEAD_DIM = 64
SSD_GROUPS = 4
SSD_STATE = 128
SSD_CONV = 4
SSD_CHUNK = 128
SB_HEADS = 4
SB_HEAD_DIM = 128
SB_BLOCK = 128
SB_SUB = 8
XA_HEADS = 4
XA_HEAD_DIM = 128
N_GROUPS = 4
EXPERTS_PER_GROUP = 8
N_EXPERTS = N_GROUPS * EXPERTS_PER_GROUP
TOP_K = 2
MOE_BLOCK = 256

LANES = 128
SUBLANES = 8
VMEM_LIMIT = 56 * 1024 * 1024

SB_EXIT_LOG = -88.0
NEG_BIG = -1e30


def _cparams(sem):
    return pltpu.CompilerParams(dimension_semantics=sem, vmem_limit_bytes=VMEM_LIMIT)


def _rms(x, w):
    var = jnp.mean(x * x, axis=-1, keepdims=True)
    return x * lax.rsqrt(var + EPS) * w


def _split2(v):
    hi = v.astype(BF16)
    lo = (v - hi.astype(F32)).astype(BF16)
    return hi, lo


def _dot(a, b):
    return jnp.dot(a, b, preferred_element_type=F32)


def _pack_halves(v):
    n = v.shape[1] // 2
    lo = lax.bitcast_convert_type(v[:, :n].astype(BF16).astype(F32), U32)
    hi = lax.bitcast_convert_type(v[:, n:].astype(BF16).astype(F32), U32)
    return (lo >> 16) | hi


def _unpack_halves(w):
    lo = lax.bitcast_convert_type(w << 16, F32)
    hi = lax.bitcast_convert_type(w & jnp.uint32(0xFFFF0000), F32)
    return lo, hi


def _silu(x):
    return x / (1.0 + jnp.exp(-x))


def _softplus(x):
    return jnp.maximum(x, 0.0) + jnp.log(1.0 + jnp.exp(-jnp.abs(x)))


def _norm_matmul_kernel(x_ref, nw_ref, w_ref, ws_ref, o_ref, os_ref, u_ref):
    @pl.when(pl.program_id(1) == 0)
    def _():
        u = _rms(x_ref[...], nw_ref[...]).astype(BF16)
        u_ref[...] = u
        os_ref[...] = _dot(u, ws_ref[...])

    o_ref[...] = _dot(u_ref[...], w_ref[...]).astype(o_ref.dtype)


def _norm_matmul(x, nw, w, w_side, tm, tn):
    m, k = x.shape
    n = w.shape[1]
    ns = w_side.shape[1]
    tm = min(tm, m)
    return pl.pallas_call(
        _norm_matmul_kernel,
        grid=(m // tm, n // tn),
        in_specs=[
            pl.BlockSpec((tm, k), lambda i, j: (i, 0)),
            pl.BlockSpec((1, k), lambda i, j: (0, 0)),
            pl.BlockSpec((k, tn), lambda i, j: (0, j)),
            pl.BlockSpec((k, ns), lambda i, j: (0, 0)),
        ],
        out_specs=[
            pl.BlockSpec((tm, tn), lambda i, j: (i, j)),
            pl.BlockSpec((tm, ns), lambda i, j: (i, 0)),
        ],
        out_shape=[jax.ShapeDtypeStruct((m, n), BF16), jax.ShapeDtypeStruct((m, ns), F32)],
        scratch_shapes=[pltpu.VMEM((tm, k), BF16)],
        compiler_params=_cparams(("parallel", "arbitrary")),
        name="norm_matmul",
    )(x, nw, w, w_side)


def _ssd_kernel(z_ref, xs_ref, b_ref, c_ref, dtr_ref, cw_ref, cb_ref, dtb_ref, alog_ref, dsk_ref,
                nw_ref, e_ref, o_ref,
                tail_ref, state_ref, xact_ref, bact_ref, cact_ref, dtx_ref, eax_ref, wsx_ref, y_ref):
    L = SSD_CHUNK
    inner = xs_ref.shape[1]
    bc = b_ref.shape[1]
    n_pairs = inner // LANES

    @pl.when(pl.program_id(1) == 0)
    def _():
        tail_ref[...] = jnp.zeros_like(tail_ref)
        state_ref[...] = jnp.zeros_like(state_ref)

    row8 = lax.broadcasted_iota(jnp.int32, (SUBLANES, 2 * LANES), 0)

    def conv_seg(src_ref, col0, width, dst_ref):
        for j in range(0, width, 2 * LANES):
            cols = slice(col0 + j, col0 + j + 2 * LANES)
            xin = src_ref[:, j:j + 2 * LANES].astype(F32)
            t8 = tail_ref[:, cols]
            acc = xin * cw_ref[SSD_CONV - 1:SSD_CONV, cols] + cb_ref[:, cols]
            for s in range(1, SSD_CONV):
                r = pltpu.roll(xin, s, 0)
                top = jnp.where(row8 < s, pltpu.roll(t8, s, 0), r[:SUBLANES])
                r = jnp.concatenate([top, r[SUBLANES:]], axis=0)
                acc = acc + r * cw_ref[SSD_CONV - 1 - s:SSD_CONV - s, cols]
            tail_ref[:, cols] = xin[L - SUBLANES:]
            dst_ref[:, j:j + 2 * LANES] = _silu(acc).astype(dst_ref.dtype)

    conv_seg(xs_ref, 0, inner, xact_ref)
    conv_seg(b_ref, inner, bc, bact_ref)
    conv_seg(c_ref, inner + bc, bc, cact_ref)

    dt = _softplus(dtr_ref[...] + dtb_ref[...])
    da = dt * (-jnp.exp(alog_ref[...]))
    rowl = lax.broadcasted_iota(jnp.int32, (L, L), 0)
    coll = lax.broadcasted_iota(jnp.int32, (L, L), 1)
    lower = rowl >= coll
    tri = jnp.where(lower, 1.0, 0.0).astype(BF16)
    d1 = da.astype(BF16)
    r1 = da - d1.astype(F32)
    d2 = r1.astype(BF16)
    d3 = (r1 - d2.astype(F32)).astype(BF16)
    a_cum = _dot(tri, d1) + _dot(tri, d2) + _dot(tri, d3)
    a_cum_t = a_cum.T
    a_last = a_cum[L - 1:L, :]
    e_mat = e_ref[...]

    def expand(v):
        hi, lo = _split2(v)
        return _dot(hi, e_mat) + _dot(lo, e_mat)

    dtx_ref[...] = expand(dt)
    eax_ref[...] = expand(jnp.exp(a_cum))
    wsx_ref[...] = expand(jnp.exp(a_last - a_cum))
    elx = expand(jnp.broadcast_to(jnp.exp(a_last), (SUBLANES, LANES)))[0:1]

    lane = lax.broadcasted_iota(jnp.int32, (L, LANES), 1)
    pairs_per_group = n_pairs // SSD_GROUPS
    for g in range(SSD_GROUPS):
        gcols = slice(g * SSD_STATE, (g + 1) * SSD_STATE)
        bg_t = bact_ref[:, gcols].T.astype(BF16)
        cg = cact_ref[:, gcols]
        cb = _dot(cg, bg_t)
        for pp in range(pairs_per_group):
            p = g * pairs_per_group + pp
            pc = slice(p * LANES, (p + 1) * LANES)
            ms = []
            for hh in (2 * p, 2 * p + 1):
                seg = a_cum[:, hh:hh + 1] - a_cum_t[hh:hh + 1, :]
                dec = jnp.exp(jnp.where(lower, seg, NEG_BIG))
                ms.append((cb * dec).astype(BF16))
            lhs = jnp.concatenate(ms, axis=1)
            xs_p = xact_ref[:, pc]
            xdt = xs_p * dtx_ref[:, pc]
            rhs = jnp.concatenate([jnp.where(lane < SSD_HEAD_DIM, xdt, 0.0).astype(BF16),
                                   jnp.where(lane >= SSD_HEAD_DIM, xdt, 0.0).astype(BF16)], axis=0)
            st = state_ref[p]
            y = _dot(lhs, rhs)
            y = y + _dot(cg, st.astype(BF16)) * eax_ref[:, pc]
            y = y + xs_p * dsk_ref[:, pc]
            y_ref[:, pc] = y
            xw = (xdt * wsx_ref[:, pc]).astype(BF16)
            state_ref[p] = st * elx[:, pc] + _dot(bg_t, xw)

    yg = y_ref[...] * _silu(z_ref[...].astype(F32))
    o_ref[...] = _rms(yg, nw_ref[...]).astype(o_ref.dtype)


def _ssd(proj, dt_raw, conv_w, conv_b, dt_bias_p, a_log_p, dskip_x, norm_w, e_mat, bsz, seq, inner, bc, b_blk):
    L = SSD_CHUNK
    nc = seq // L
    conv_dim = inner + 2 * bc
    n_pairs = inner // LANES

    def rows(b, c):
        return b * nc + c

    return pl.pallas_call(
        _ssd_kernel,
        grid=(bsz, nc),
        in_specs=[
            pl.BlockSpec((L, inner), lambda b, c: (rows(b, c), 0)),
            pl.BlockSpec((L, inner), lambda b, c: (rows(b, c), 1)),
            pl.BlockSpec((L, bc), lambda b, c: (rows(b, c), b_blk)),
            pl.BlockSpec((L, bc), lambda b, c: (rows(b, c), b_blk + 1)),
            pl.BlockSpec((L, LANES), lambda b, c: (rows(b, c), 0)),
            pl.BlockSpec((SSD_CONV, conv_dim), lambda b, c: (0, 0)),
            pl.BlockSpec((1, conv_dim), lambda b, c: (0, 0)),
            pl.BlockSpec((1, LANES), lambda b, c: (0, 0)),
            pl.BlockSpec((1, LANES), lambda b, c: (0, 0)),
            pl.BlockSpec((1, inner), lambda b, c: (0, 0)),
            pl.BlockSpec((1, inner), lambda b, c: (0, 0)),
            pl.BlockSpec((LANES, inner), lambda b, c: (0, 0)),
        ],
        out_specs=pl.BlockSpec((L, inner), lambda b, c: (rows(b, c), 0)),
        out_shape=jax.ShapeDtypeStruct((bsz * seq, inner), BF16),
        scratch_shapes=[
            pltpu.VMEM((SUBLANES, conv_dim), F32),
            pltpu.VMEM((n_pairs, SSD_STATE, LANES), F32),
            pltpu.VMEM((L, inner), F32),
            pltpu.VMEM((L, bc), F32),
            pltpu.VMEM((L, bc), BF16),
            pltpu.VMEM((L, inner), F32),
            pltpu.VMEM((L, inner), F32),
            pltpu.VMEM((L, inner), F32),
            pltpu.VMEM((L, inner), F32),
        ],
        compiler_params=_cparams(("parallel", "arbitrary")),
        name="ssd",
    )(proj, proj, proj, proj, dt_raw, conv_w, conv_b, dt_bias_p, a_log_p, dskip_x, norm_w, e_mat)


def _sb_kernel(q_ref, k_ref, v_ref, o_ref, acc_ref, carry_ref):
    BL = SB_BLOCK
    i = pl.program_id(2)
    scale = SB_HEAD_DIM ** -0.5
    row = lax.broadcasted_iota(jnp.int32, (BL, BL), 0)
    col = lax.broadcasted_iota(jnp.int32, (BL, BL), 1)
    causal = col < row
    r2 = lax.broadcasted_iota(jnp.int32, (BL, 2 * BL), 0)
    c2 = lax.broadcasted_iota(jnp.int32, (BL, 2 * BL), 1)
    tri = jnp.where((r2 > c2) | (c2 >= BL), 1.0, 0.0).astype(BF16)

    def sweep(s, first):
        subs = range(SB_SUB)
        kbs = [i * SB_SUB + j - s for j in subs]
        starts = [pl.multiple_of(jnp.maximum(kb, 0) * BL, BL) for kb in kbs]
        zls = [lax.dot_general(q_ref[j * BL:(j + 1) * BL, :], k_ref[pl.ds(starts[j], BL), :],
                               (((1,), (1,)), ((), ())), preferred_element_type=F32) * scale for j in subs]
        sps = [_softplus(zl) for zl in zls]
        log_betas = [zl - sp for zl, sp in zip(zls, sps)]
        log_keeps = [jnp.where(causal, -sp, 0.0) if first else -sp for sp in sps]
        splits = [_split2(lk) for lk in log_keeps]
        t2s = [_dot(hi, tri) + _dot(lo, tri) for hi, lo in splits]
        if first:
            atts = [jnp.where(causal, jnp.exp(lb + t2[:, :BL]), 0.0) for lb, t2 in zip(log_betas, t2s)]
            carries = [t2[:, BL:] for t2 in t2s]
        else:
            olds = [carry_ref[j] for j in subs]
            atts = [jnp.where(kb >= 0, jnp.exp(lb + t2[:, :BL] + old), 0.0)
                    for kb, lb, t2, old in zip(kbs, log_betas, t2s, olds)]
            carries = [old + jnp.where(kb >= 0, t2[:, BL:], 0.0) for kb, t2, old in zip(kbs, t2s, olds)]
        pvs = [_dot(atts[j].astype(BF16), v_ref[pl.ds(starts[j], BL), :]) for j in subs]
        worst = None
        for j in subs:
            if first:
                acc_ref[j] = pvs[j]
            else:
                acc_ref[j] += pvs[j]
            carry_ref[j] = carries[j]
            live = jnp.where(kbs[j] >= 1, carries[j], NEG_BIG)
            worst = live if worst is None else jnp.maximum(worst, live)
        return jnp.max(worst) > SB_EXIT_LOG

    sweep(0, True)
    go = sweep(1, False)

    def body(st):
        s, _ = st
        return s + 1, sweep(s, False)

    lax.while_loop(lambda st: st[1], body, (jnp.int32(2), go))
    for j in range(SB_SUB):
        o_ref[j * BL:(j + 1) * BL, :] = acc_ref[j].astype(o_ref.dtype)


def _stickbreak(proj3, bsz, seq, q_blk0):
    tq = SB_SUB * SB_BLOCK
    return pl.pallas_call(
        _sb_kernel,
        grid=(bsz, SB_HEADS, seq // tq),
        in_specs=[
            pl.BlockSpec((None, tq, SB_HEAD_DIM), lambda b, h, i: (b, i, q_blk0 + h)),
            pl.BlockSpec((None, seq, SB_HEAD_DIM), lambda b, h, i: (b, 0, q_blk0 + SB_HEADS + h)),
            pl.BlockSpec((None, seq, SB_HEAD_DIM), lambda b, h, i: (b, 0, q_blk0 + 2 * SB_HEADS + h)),
        ],
        out_specs=pl.BlockSpec((None, tq, SB_HEAD_DIM), lambda b, h, i: (b, i, h)),
        out_shape=jax.ShapeDtypeStruct((bsz, seq, SB_HEADS * SB_HEAD_DIM), BF16),
        scratch_shapes=[pltpu.VMEM((SB_SUB, SB_BLOCK, SB_HEAD_DIM), F32),
                        pltpu.VMEM((SB_SUB, SB_BLOCK, SB_BLOCK), F32)],
        compiler_params=_cparams(("parallel", "parallel", "arbitrary")),
        name="stickbreak",
    )(proj3, proj3, proj3)


def _merge_kernel(x_ref, ys_ref, yb_ref, g1_ref, g2_ref, w1_ref, w2_ref, wm_ref, o_ref):
    @pl.when(pl.program_id(1) == 0)
    def _():
        o_ref[...] = x_ref[...]

    a = _dot(ys_ref[...], w1_ref[...])
    b = _dot(yb_ref[...], w2_ref[...])
    s1 = jax.nn.sigmoid(g1_ref[...].astype(F32))
    s2 = jax.nn.sigmoid(g2_ref[...].astype(F32))
    m = (s1 * a + s2 * b).astype(BF16)
    o_ref[...] += _dot(m, wm_ref[...])


def _merge(x2, y_ssd, y_sb, proj, w1, w2, wm, g_blk0, tm, tn):
    t, d = x2.shape
    tm = min(tm, t)
    nj = d // tn
    return pl.pallas_call(
        _merge_kernel,
        grid=(t // tm, nj),
        in_specs=[
            pl.BlockSpec((tm, d), lambda i, j: (i, 0)),
            pl.BlockSpec((tm, y_ssd.shape[1]), lambda i, j: (i, 0)),
            pl.BlockSpec((tm, y_sb.shape[1]), lambda i, j: (i, 0)),
            pl.BlockSpec((tm, tn), lambda i, j: (i, g_blk0 + j)),
            pl.BlockSpec((tm, tn), lambda i, j: (i, g_blk0 + nj + j)),
            pl.BlockSpec((w1.shape[0], tn), lambda i, j: (0, j)),
            pl.BlockSpec((w2.shape[0], tn), lambda i, j: (0, j)),
            pl.BlockSpec((tn, d), lambda i, j: (j, 0)),
        ],
        out_specs=pl.BlockSpec((tm, d), lambda i, j: (i, 0)),
        out_shape=jax.ShapeDtypeStruct((t, d), F32),
        compiler_params=_cparams(("parallel", "arbitrary")),
        name="merge",
    )(x2, y_ssd, y_sb, proj, proj, w1, w2, wm)


def _xattn_kernel(h_ref, kv_ref, nxa_ref, wq_ref, wo_ref, nmoe_ref, wr_ref, h2_ref, um_ref, rt_ref, cnt_ref,
                  base_ref):
    h1 = h_ref[...]
    un = _rms(h1, nxa_ref[...]).astype(BF16)
    q = _dot(un, wq_ref[...]).astype(BF16)
    width = XA_HEADS * XA_HEAD_DIM
    outs = []
    for hd in range(XA_HEADS):
        cs = slice(hd * XA_HEAD_DIM, (hd + 1) * XA_HEAD_DIM)
        k = kv_ref[:, cs]
        v = kv_ref[:, width + hd * XA_HEAD_DIM: width + (hd + 1) * XA_HEAD_DIM]
        sc = lax.dot_general(q[:, cs], k, (((1,), (1,)), ((), ())),
                             preferred_element_type=F32) * (XA_HEAD_DIM ** -0.5)
        sc = sc - jnp.max(sc, axis=-1, keepdims=True)
        p = jnp.exp(sc)
        p = p / jnp.sum(p, axis=-1, keepdims=True)
        outs.append(_dot(p.astype(BF16), v).astype(BF16))
    o = jnp.concatenate(outs, axis=1)
    h2 = h1 + _dot(o, wo_ref[...])
    h2_ref[...] = h2
    um = _rms(h2, nmoe_ref[...])
    um_ref[...] = _pack_halves(um)
    hi, lo = _split2(um)
    nr = rt_ref.shape[1]
    both = _dot(hi, wr_ref[...])
    lg = both[:, :nr] + both[:, nr:] + _dot(lo, wr_ref[:, :nr])

    lane = lax.broadcasted_iota(jnp.int32, lg.shape, 1)
    lane_f = lane.astype(F32)
    is_g = lane < N_GROUPS
    gl = jnp.where(is_g, lg, NEG_BIG)
    gmax = jnp.max(gl, axis=-1, keepdims=True)
    g_sel = jnp.min(jnp.where(gl == gmax, lane_f, float(LANES)), axis=-1, keepdims=True)
    g_gate = 1.0 / jnp.sum(jnp.where(is_g, jnp.exp(gl - gmax), 0.0), axis=-1, keepdims=True)
    lo_lane = N_GROUPS + EXPERTS_PER_GROUP * g_sel
    el = jnp.where((lane_f >= lo_lane) & (lane_f < lo_lane + EXPERTS_PER_GROUP), lg, NEG_BIG)
    m1 = jnp.max(el, axis=-1, keepdims=True)
    i1 = jnp.min(jnp.where(el == m1, lane_f, float(LANES)), axis=-1, keepdims=True)
    el2 = jnp.where(lane_f == i1, NEG_BIG, el)
    m2 = jnp.max(el2, axis=-1, keepdims=True)
    i2 = jnp.min(jnp.where(el2 == m2, lane_f, float(LANES)), axis=-1, keepdims=True)
    ex = jnp.exp(m2 - m1)
    p1 = 1.0 / (1.0 + ex)
    e1 = i1 - N_GROUPS
    e2 = i2 - N_GROUPS

    @pl.when(pl.program_id(0) == 0)
    def _():
        base_ref[...] = jnp.zeros_like(base_ref)

    tm = lg.shape[0]
    hot1 = lane_f == e1
    hot2 = lane_f == e2
    onehot = jnp.where(hot1 | hot2, 1.0, 0.0)
    tr = lax.broadcasted_iota(jnp.int32, (tm, tm), 0)
    tc = lax.broadcasted_iota(jnp.int32, (tm, tm), 1)
    before = _dot(jnp.where(tc < tr, 1.0, 0.0).astype(BF16), onehot.astype(BF16)) + base_ref[...]
    r1 = jnp.sum(jnp.where(hot1, before, 0.0), axis=-1, keepdims=True)
    r2 = jnp.sum(jnp.where(hot2, before, 0.0), axis=-1, keepdims=True)
    base_ref[...] += jnp.sum(onehot, axis=0, keepdims=True)
    cnt_ref[...] = jnp.broadcast_to(base_ref[...], cnt_ref.shape)

    vals = (e1, e2, p1 * g_gate, ex * p1 * g_gate, r1, r2)
    out = jnp.zeros_like(lg)
    for j, v in enumerate(vals):
        out = jnp.where(lane == j, v, out)
    rt_ref[...] = out


def _xattn(h1, kv, nxa, wq, wo, nmoe, wr, seq, tm):
    t, d = h1.shape
    tm = min(tm, seq)
    per_b = seq // tm
    m_len = kv.shape[0] // (t // seq)
    nr = wr.shape[1] // 2
    return pl.pallas_call(
        _xattn_kernel,
        grid=(t // tm,),
        in_specs=[
            pl.BlockSpec((tm, d), lambda i: (i, 0)),
            pl.BlockSpec((m_len, kv.shape[1]), lambda i: (i // per_b, 0)),
            pl.BlockSpec((1, d), lambda i: (0, 0)),
            pl.BlockSpec(wq.shape, lambda i: (0, 0)),
            pl.BlockSpec(wo.shape, lambda i: (0, 0)),
            pl.BlockSpec((1, d), lambda i: (0, 0)),
            pl.BlockSpec((d, 2 * nr), lambda i: (0, 0)),
        ],
        out_specs=[
            pl.BlockSpec((tm, d), lambda i: (i, 0)),
            pl.BlockSpec((tm, d // 2), lambda i: (i, 0)),
            pl.BlockSpec((tm, nr), lambda i: (i, 0)),
            pl.BlockSpec((SUBLANES, nr), lambda i: (0, 0)),
        ],
        out_shape=[jax.ShapeDtypeStruct((t, d), F32), jax.ShapeDtypeStruct((t, d // 2), U32),
                   jax.ShapeDtypeStruct((t, nr), F32), jax.ShapeDtypeStruct((SUBLANES, nr), F32)],
        scratch_shapes=[pltpu.VMEM((1, nr), F32)],
        compiler_params=_cparams(("arbitrary",)),
        name="xattn",
    )(h1, kv, nxa, wq, wo, nmoe, wr)


def _dispatch_kernel(slot_ref, padrow_ref, padn_ref, nv_ref, um_ref, xs_hbm, sa_ref, sb_ref, z_ref, sem, zsem):
    i = pl.program_id(0)
    tm = um_ref.shape[0]
    last = pl.num_programs(0) - 1
    stage = (sa_ref, sb_ref)

    def zero_rows(start):
        for e in range(N_EXPERTS):
            def body(j, c, e=e):
                cp = pltpu.make_async_copy(z_ref.at[pl.ds(0, 1)], xs_hbm.at[pl.ds(padrow_ref[e] + j, 1)], zsem)
                if start:
                    cp.start()
                else:
                    cp.wait()
                return c

            lax.fori_loop(0, padn_ref[e], body, 0)

        def tail(b, c):
            cp = pltpu.make_async_copy(
                z_ref, xs_hbm.at[pl.ds(pl.multiple_of(b * MOE_BLOCK, MOE_BLOCK), MOE_BLOCK)], zsem)
            if start:
                cp.start()
            else:
                cp.wait()
            return c

        lax.fori_loop(nv_ref[0], xs_hbm.shape[0] // MOE_BLOCK, tail, 0)

    @pl.when(i == 0)
    def _():
        z_ref[...] = jnp.zeros_like(z_ref)
        zero_rows(True)

    def wait(par):
        for _ in range(TOP_K):
            pltpu.make_async_copy(stage[par], xs_hbm.at[pl.ds(0, tm)], sem.at[par]).wait()

    def step(par):
        pl.when(i >= 2)(functools.partial(wait, par))
        stage[par][...] = um_ref[...]
        base = i * tm * TOP_K
        for r in range(tm):
            for k in range(TOP_K):
                pltpu.make_async_copy(stage[par].at[pl.ds(r, 1)],
                                      xs_hbm.at[pl.ds(slot_ref[base + TOP_K * r + k], 1)],
                                      sem.at[par]).start(priority=k)

        @pl.when(i == last)
        def _():
            wait(par)
            pl.when(i >= 1)(functools.partial(wait, 1 - par))

    for par in range(2):
        pl.when(i % 2 == par)(functools.partial(step, par))

    @pl.when(i == last)
    def _():
        zero_rows(False)


def _dispatch(slot, pad_row, pad_n, n_valid, um, n_slots, tm):
    t, d = um.shape
    tm = min(tm, t)
    grid_spec = pltpu.PrefetchScalarGridSpec(
        num_scalar_prefetch=4,
        grid=(t // tm,),
        in_specs=[pl.BlockSpec((tm, d), lambda i, s, pr, pn, nv: (i, 0))],
        out_specs=pl.BlockSpec(memory_space=pl.ANY),
        scratch_shapes=[pltpu.VMEM((tm, d), um.dtype), pltpu.VMEM((tm, d), um.dtype),
                        pltpu.VMEM((MOE_BLOCK, d), um.dtype),
                        pltpu.SemaphoreType.DMA((2,)), pltpu.SemaphoreType.DMA(())],
    )
    return pl.pallas_call(
        _dispatch_kernel,
        grid_spec=grid_spec,
        out_shape=jax.ShapeDtypeStruct((n_slots, d), um.dtype),
        compiler_params=_cparams(("arbitrary",)),
        name="dispatch",
    )(slot, pad_row, pad_n, n_valid, um)


def _expert_kernel(be_ref, nv_ref, x_ref, wg_ref, wu_ref, wd_ref, o_ref, wgb_ref, wub_ref, wdb_ref):
    i = pl.program_id(0)

    @pl.when(jnp.logical_or(i == 0, be_ref[i] != be_ref[jnp.maximum(i - 1, 0)]))
    def _():
        wgb_ref[...] = wg_ref[...].astype(BF16)
        wub_ref[...] = wu_ref[...].astype(BF16)
        wdb_ref[...] = wd_ref[...].astype(BF16)

    @pl.when(i < nv_ref[0])
    def _():
        lo, hi = _unpack_halves(x_ref[...])
        xb = jnp.concatenate([lo.astype(BF16), hi.astype(BF16)], axis=1)
        hid = (_silu(_dot(xb, wgb_ref[...])) * _dot(xb, wub_ref[...])).astype(BF16)
        o_ref[...] = _pack_halves(_dot(hid, wdb_ref[...]))

    @pl.when(i >= nv_ref[0])
    def _():
        o_ref[...] = jnp.zeros_like(o_ref)


def _experts(block_e, n_valid, xs, wg, wu, wd):
    n_slots, dp = xs.shape
    d, ff = wg.shape[1:]

    def x_map(i, be, nv):
        return (jnp.minimum(i, nv[0] - 1), 0)

    grid_spec = pltpu.PrefetchScalarGridSpec(
        num_scalar_prefetch=2,
        grid=(n_slots // MOE_BLOCK,),
        in_specs=[
            pl.BlockSpec((MOE_BLOCK, dp), x_map),
            pl.BlockSpec((None, d, ff), lambda i, be, nv: (be[i], 0, 0)),
            pl.BlockSpec((None, d, ff), lambda i, be, nv: (be[i], 0, 0)),
            pl.BlockSpec((None, ff, d), lambda i, be, nv: (be[i], 0, 0)),
        ],
        out_specs=pl.BlockSpec((MOE_BLOCK, dp), lambda i, be, nv: (i, 0)),
        scratch_shapes=[pltpu.VMEM((d, ff), BF16), pltpu.VMEM((d, ff), BF16), pltpu.VMEM((ff, d), BF16)],
    )
    return pl.pallas_call(
        _expert_kernel,
        grid_spec=grid_spec,
        out_shape=jax.ShapeDtypeStruct((n_slots, dp), U32),
        compiler_params=_cparams(("arbitrary",)),
        name="experts",
    )(block_e, n_valid, xs, wg, wu, wd)


def _combine_kernel(slot_ref, h_ref, w_ref, nw_ref, yb_hbm, o_ref, ya0_ref, ya1_ref, yb0_ref, yb1_ref, sem):
    i = pl.program_id(0)
    tm = h_ref.shape[0]
    last = pl.num_programs(0) - 1
    bufs = ((ya0_ref, ya1_ref), (yb0_ref, yb1_ref))

    def gather(blk, par):
        base = blk * tm * TOP_K
        for r in range(tm):
            for k in range(TOP_K):
                pltpu.make_async_copy(yb_hbm.at[pl.ds(slot_ref[base + TOP_K * r + k], 1)],
                                      bufs[par][k].at[pl.ds(r, 1)], sem.at[par]).start(priority=k)

    def wait(par):
        for k in range(TOP_K):
            pltpu.make_async_copy(yb_hbm.at[pl.ds(0, tm)], bufs[par][k], sem.at[par]).wait()

    @pl.when(i == 0)
    def _():
        gather(i, 0)

    def step(par):
        wait(par)
        gather(jnp.minimum(i + 1, last), 1 - par)
        w0 = w_ref[:, TOP_K:TOP_K + 1]
        w1 = w_ref[:, TOP_K + 1:TOP_K + 2]
        half = h_ref.shape[1] // 2
        ya = _unpack_halves(bufs[par][0][...])
        yb = _unpack_halves(bufs[par][1][...])
        h3 = [h_ref[:, p * half:(p + 1) * half] + w0 * ya[p] + w1 * yb[p] for p in range(2)]
        var = sum(jnp.sum(v * v, axis=-1, keepdims=True) for v in h3) / h_ref.shape[1]
        inv = lax.rsqrt(var + EPS)
        for p in range(2):
            o_ref[:, p * half:(p + 1) * half] = h3[p] * inv * nw_ref[:, p * half:(p + 1) * half]

        @pl.when(i == last)
        def _():
            wait(1 - par)

    for par in range(2):
        pl.when(i % 2 == par)(functools.partial(step, par))


def _combine(slot, h2, route, nw, yb, tm):
    t, d = h2.shape
    tm = min(tm, t)
    grid_spec = pltpu.PrefetchScalarGridSpec(
        num_scalar_prefetch=1,
        grid=(t // tm,),
        in_specs=[
            pl.BlockSpec((tm, d), lambda i, s: (i, 0)),
            pl.BlockSpec((tm, route.shape[1]), lambda i, s: (i, 0)),
            pl.BlockSpec((1, d), lambda i, s: (0, 0)),
            pl.BlockSpec(memory_space=pl.ANY),
        ],
        out_specs=pl.BlockSpec((tm, d), lambda i, s: (i, 0)),
        scratch_shapes=[pltpu.VMEM((tm, yb.shape[1]), yb.dtype)] * (2 * TOP_K) + [pltpu.SemaphoreType.DMA((2,))],
    )
    return pl.pallas_call(
        _combine_kernel,
        grid_spec=grid_spec,
        out_shape=jax.ShapeDtypeStruct((t, d), F32),
        compiler_params=_cparams(("arbitrary",)),
        name="combine",
    )(slot, h2, route, nw, yb)


def _slots(route, cnt, t):
    flat_e = route[:, :TOP_K].astype(jnp.int32).reshape(-1)
    rank = route[:, 2 * TOP_K:3 * TOP_K].astype(jnp.int32).reshape(-1)
    onehot = flat_e[:, None] == jnp.arange(N_EXPERTS, dtype=jnp.int32)[None, :]
    counts = cnt[0, :N_EXPERTS].astype(jnp.int32)
    padded = (counts + MOE_BLOCK - 1) // MOE_BLOCK * MOE_BLOCK
    pad_end = jnp.cumsum(padded)
    pad_start = pad_end - padded
    slot = jnp.sum(jnp.where(onehot, pad_start[None, :], 0), axis=1).astype(jnp.int32) + rank
    n_slots = t * TOP_K + N_EXPERTS * MOE_BLOCK
    n_blocks = n_slots // MOE_BLOCK
    block_start = jnp.arange(n_blocks, dtype=jnp.int32) * MOE_BLOCK
    block_e = jnp.minimum(jnp.sum((pad_end[None, :] <= block_start[:, None]).astype(jnp.int32), axis=1),
                          N_EXPERTS - 1)
    n_valid = (pad_end[-1:] // MOE_BLOCK).astype(jnp.int32)
    return slot, pad_start + counts, padded - counts, block_e, n_valid, n_slots


def _pad_lanes(v, n=LANES):
    return jnp.pad(v, ((0, 0), (0, n - v.shape[1])))


def kernel(x, mem, norm_mix_w, w_in, conv_w, conv_b, dt_bias, a_log, d_skip, ssd_norm_w, w_ssd_branch, w_sb_branch, w_mix_out, norm_xa_w, norm_mem_w, w_xq, w_xk, w_xv, w_xo, norm_moe_w, w_router_group, w_router_expert, w_expert_gate, w_expert_up, w_expert_down, norm_final_w):
    bsz, seq, d = x.shape
    depth = w_in.shape[0]
    t = bsz * seq
    inner = ssd_norm_w.shape[1]
    heads = dt_bias.shape[1]
    bc = SSD_GROUPS * SSD_STATE
    conv_dim = inner + 2 * bc
    sb_width = SB_HEADS * SB_HEAD_DIM
    col_dt = inner + conv_dim
    col_qkv = col_dt + heads
    col_g = col_qkv + 3 * sb_width
    g_col0 = 2 * inner
    bc_col0 = g_col0 + 2 * d
    q_col0 = bc_col0 + 2 * bc
    n_main = q_col0 + 3 * sb_width
    tn_proj = 1280
    n_proj = -(-n_main // tn_proj) * tn_proj
    tn_merge = 1024

    assert depth == 1, "single-layer configuration: the final RMSNorm is fused into the last MoE combine"
    h = x.reshape(t, d)
    for l in range(depth):
        w_l = w_in[l]
        w_main = jnp.concatenate([w_l[:, :2 * inner], w_l[:, col_g:], w_l[:, 2 * inner:col_dt],
                                  w_l[:, col_qkv:col_g], jnp.zeros((d, n_proj - n_main), F32)],
                                 axis=1).astype(BF16)
        w_dt = _pad_lanes(w_l[:, col_dt:col_dt + heads]).astype(BF16)
        e_mat = (jnp.arange(inner, dtype=jnp.int32)[None, :] // SSD_HEAD_DIM
                 == jnp.arange(LANES, dtype=jnp.int32)[:, None]).astype(BF16)
        dskip_x = jnp.repeat(d_skip[l], SSD_HEAD_DIM)[None, :]

        proj, dt_raw = _norm_matmul(h, norm_mix_w[l][None, :], w_main, w_dt, tm=1024, tn=tn_proj)
        y_ssd = _ssd(proj, dt_raw, conv_w[l], conv_b[l][None, :], _pad_lanes(dt_bias[l][None, :]),
                     _pad_lanes(a_log[l][None, :]), dskip_x, ssd_norm_w[l][None, :], e_mat,
                     bsz, seq, inner, bc, bc_col0 // bc)
        y_sb = _stickbreak(proj.reshape(bsz, seq, n_proj), bsz, seq, q_col0 // SB_HEAD_DIM)
        h1 = _merge(h, y_ssd, y_sb.reshape(t, sb_width), proj,
                    w_ssd_branch[l].astype(BF16), w_sb_branch[l].astype(BF16), w_mix_out[l].astype(BF16),
                    g_col0 // tn_merge, tm=512, tn=tn_merge)

        m_len = mem.shape[1]
        w_kv = jnp.concatenate([w_xk[l], w_xv[l]], axis=1).astype(BF16)
        kv, _ = _norm_matmul(mem.reshape(bsz * m_len, d), norm_mem_w[l][None, :], w_kv,
                             jnp.zeros((d, LANES), BF16), tm=bsz * m_len, tn=512)
        w_r = _pad_lanes(jnp.concatenate([w_router_group[l], w_router_expert[l]], axis=1))
        wr_hi = w_r.astype(BF16)
        wr_lo = (w_r - wr_hi.astype(F32)).astype(BF16)
        h2, um, route, cnt = _xattn(h1, kv, norm_xa_w[l][None, :], w_xq[l].astype(BF16), w_xo[l].astype(BF16),
                               norm_moe_w[l][None, :], jnp.concatenate([wr_hi, wr_lo], axis=1), seq, tm=512)

        slot, pad_row, pad_n, block_e, n_valid, n_slots = _slots(route, cnt, t)
        xs = _dispatch(slot, pad_row, pad_n, n_valid, um, n_slots, tm=256)
        yb = _experts(block_e, n_valid, xs, w_expert_gate[l], w_expert_up[l], w_expert_down[l])
        h = _combine(slot, h2, route, norm_final_w[None, :], yb, tm=256)
    return h.reshape(bsz, seq, d)
```

```python
import functools

import jax
import jax.numpy as jnp
from jax import lax
from jax.experimental import pallas as pl
from jax.experimental.pallas import tpu as pltpu

F32 = jnp.float32
BF16 = jnp.bfloat16
U32 = jnp.uint32
EPS = 1e-6

SSD_HEAD_DIM = 64
SSD_GROUPS = 4
SSD_STATE = 128
SSD_CONV = 4
SSD_CHUNK = 128
SB_HEADS = 4
SB_HEAD_DIM = 128
SB_BLOCK = 128
SB_SUB = 8
XA_HEADS = 4
XA_HEAD_DIM = 128
N_GROUPS = 4
EXPERTS_PER_GROUP = 8
N_EXPERTS = N_GROUPS * EXPERTS_PER_GROUP
TOP_K = 2
MOE_BLOCK = 256

LANES = 128
SUBLANES = 8
VMEM_LIMIT = 56 * 1024 * 1024

SB_EXIT_LOG = -88.0
NEG_BIG = -1e30


def _cparams(sem):
    return pltpu.CompilerParams(dimension_semantics=sem, vmem_limit_bytes=VMEM_LIMIT)


def _rms(x, w):
    var = jnp.mean(x * x, axis=-1, keepdims=True)
    return x * lax.rsqrt(var + EPS) * w


def _split2(v):
    hi = v.astype(BF16)
    lo = (v - hi.astype(F32)).astype(BF16)
    return hi, lo


def _dot(a, b):
    return jnp.dot(a, b, preferred_element_type=F32)


def _pack_halves(v):
    n = v.shape[1] // 2
    lo = lax.bitcast_convert_type(v[:, :n].astype(BF16).astype(F32), U32)
    hi = lax.bitcast_convert_type(v[:, n:].astype(BF16).astype(F32), U32)
    return (lo >> 16) | hi


def _unpack_halves(w):
    lo = lax.bitcast_convert_type(w << 16, F32)
    hi = lax.bitcast_convert_type(w & jnp.uint32(0xFFFF0000), F32)
    return lo, hi


def _silu(x):
    return x / (1.0 + jnp.exp(-x))


def _softplus(x):
    return jnp.maximum(x, 0.0) + jnp.log(1.0 + jnp.exp(-jnp.abs(x)))


def _norm_matmul_kernel(x_ref, nw_ref, w_ref, ws_ref, o_ref, os_ref, u_ref):
    @pl.when(pl.program_id(1) == 0)
    def _():
        u = _rms(x_ref[...], nw_ref[...]).astype(BF16)
        u_ref[...] = u
        os_ref[...] = _dot(u, ws_ref[...])

    o_ref[...] = _dot(u_ref[...], w_ref[...]).astype(o_ref.dtype)


def _norm_matmul(x, nw, w, w_side, tm, tn):
    m, k = x.shape
    n = w.shape[1]
    ns = w_side.shape[1]
    tm = min(tm, m)
    return pl.pallas_call(
        _norm_matmul_kernel,
        grid=(m // tm, n // tn),
        in_specs=[
            pl.BlockSpec((tm, k), lambda i, j: (i, 0)),
            pl.BlockSpec((1, k), lambda i, j: (0, 0)),
            pl.BlockSpec((k, tn), lambda i, j: (0, j)),
            pl.BlockSpec((k, ns), lambda i, j: (0, 0)),
        ],
        out_specs=[
            pl.BlockSpec((tm, tn), lambda i, j: (i, j)),
            pl.BlockSpec((tm, ns), lambda i, j: (i, 0)),
        ],
        out_shape=[jax.ShapeDtypeStruct((m, n), BF16), jax.ShapeDtypeStruct((m, ns), F32)],
        scratch_shapes=[pltpu.VMEM((tm, k), BF16)],
        compiler_params=_cparams(("parallel", "arbitrary")),
        name="norm_matmul",
    )(x, nw, w, w_side)


def _ssd_kernel(z_ref, xs_ref, b_ref, c_ref, dtr_ref, cw_ref, cb_ref, dtb_ref, alog_ref, dsk_ref,
                nw_ref, e_ref, o_ref,
                tail_ref, win_ref, state_ref, xact_ref, bact_ref, cact_ref, dtx_ref, eax_ref, wsx_ref, y_ref):
    L = SSD_CHUNK
    inner = xs_ref.shape[1]
    bc = b_ref.shape[1]
    n_pairs = inner // LANES

    @pl.when(pl.program_id(1) == 0)
    def _():
        tail_ref[...] = jnp.zeros_like(tail_ref)
        state_ref[...] = jnp.zeros_like(state_ref)


    def conv_seg(src_ref, col0, width, dst_ref):
        for j in range(0, width, 2 * LANES):
            cols = slice(col0 + j, col0 + j + 2 * LANES)
            xin = src_ref[:, j:j + 2 * LANES].astype(F32)
            win_ref[:SUBLANES, :] = tail_ref[:, cols]
            win_ref[SUBLANES:, :] = xin
            acc = xin * cw_ref[SSD_CONV - 1:SSD_CONV, cols] + cb_ref[:, cols]
            for s in range(1, SSD_CONV):
                acc = acc + win_ref[SUBLANES - s:SUBLANES - s + L, :] * cw_ref[SSD_CONV - 1 - s:SSD_CONV - s, cols]
            tail_ref[:, cols] = xin[L - SUBLANES:]
            dst_ref[:, j:j + 2 * LANES] = _silu(acc).astype(dst_ref.dtype)

    conv_seg(xs_ref, 0, inner, xact_ref)
    conv_seg(b_ref, inner, bc, bact_ref)
    conv_seg(c_ref, inner + bc, bc, cact_ref)

    dt = _softplus(dtr_ref[...] + dtb_ref[...])
    da = dt * (-jnp.exp(alog_ref[...]))
    rowl = lax.broadcasted_iota(jnp.int32, (L, L), 0)
    coll = lax.broadcasted_iota(jnp.int32, (L, L), 1)
    lower = rowl >= coll
    tri = jnp.where(lower, 1.0, 0.0).astype(BF16)
    d1 = da.astype(BF16)
    r1 = da - d1.astype(F32)
    d2 = r1.astype(BF16)
    d3 = (r1 - d2.astype(F32)).astype(BF16)
    a_cum = _dot(tri, d1) + _dot(tri, d2) + _dot(tri, d3)
    a_cum_t = a_cum.T
    a_last = a_cum[L - 1:L, :]
    e_mat = e_ref[...]

    def expand(v):
        hi, lo = _split2(v)
        return _dot(hi, e_mat) + _dot(lo, e_mat)

    dtx_ref[...] = expand(dt)
    eax_ref[...] = expand(jnp.exp(a_cum))
    wsx_ref[...] = expand(jnp.exp(a_last - a_cum))
    elx = expand(jnp.broadcast_to(jnp.exp(a_last), (SUBLANES, LANES)))[0:1]

    lane = lax.broadcasted_iota(jnp.int32, (L, LANES), 1)
    pairs_per_group = n_pairs // SSD_GROUPS
    for g in range(SSD_GROUPS):
        gcols = slice(g * SSD_STATE, (g + 1) * SSD_STATE)
        bg_t = bact_ref[:, gcols].T.astype(BF16)
        cg = cact_ref[:, gcols]
        cb = _dot(cg, bg_t)
        for pp in range(pairs_per_group):
            p = g * pairs_per_group + pp
            pc = slice(p * LANES, (p + 1) * LANES)
            ms = []
            for hh in (2 * p, 2 * p + 1):
                seg = a_cum[:, hh:hh + 1] - a_cum_t[hh:hh + 1, :]
                dec = jnp.exp(jnp.where(lower, seg, NEG_BIG))
                ms.append((cb * dec).astype(BF16))
            lhs = jnp.concatenate(ms, axis=1)
            xs_p = xact_ref[:, pc]
            xdt = xs_p * dtx_ref[:, pc]
            rhs = jnp.concatenate([jnp.where(lane < SSD_HEAD_DIM, xdt, 0.0).astype(BF16),
                                   jnp.where(lane >= SSD_HEAD_DIM, xdt, 0.0).astype(BF16)], axis=0)
            st = state_ref[p]
            y = _dot(lhs, rhs)
            y = y + _dot(cg, st.astype(BF16)) * eax_ref[:, pc]
            y = y + xs_p * dsk_ref[:, pc]
            y_ref[:, pc] = y
            xw = (xdt * wsx_ref[:, pc]).astype(BF16)
            state_ref[p] = st * elx[:, pc] + _dot(bg_t, xw)

    yg = y_ref[...] * _silu(z_ref[...].astype(F32))
    o_ref[...] = _rms(yg, nw_ref[...]).astype(o_ref.dtype)


def _ssd(proj, dt_raw, conv_w, conv_b, dt_bias_p, a_log_p, dskip_x, norm_w, e_mat, bsz, seq, inner, bc, b_blk):
    L = SSD_CHUNK
    nc = seq // L
    conv_dim = inner + 2 * bc
    n_pairs = inner // LANES

    def rows(b, c):
        return b * nc + c

    return pl.pallas_call(
        _ssd_kernel,
        grid=(bsz, nc),
        in_specs=[
            pl.BlockSpec((L, inner), lambda b, c: (rows(b, c), 0)),
            pl.BlockSpec((L, inner), lambda b, c: (rows(b, c), 1)),
            pl.BlockSpec((L, bc), lambda b, c: (rows(b, c), b_blk)),
            pl.BlockSpec((L, bc), lambda b, c: (rows(b, c), b_blk + 1)),
            pl.BlockSpec((L, LANES), lambda b, c: (rows(b, c), 0)),
            pl.BlockSpec((SSD_CONV, conv_dim), lambda b, c: (0, 0)),
            pl.BlockSpec((1, conv_dim), lambda b, c: (0, 0)),
            pl.BlockSpec((1, LANES), lambda b, c: (0, 0)),
            pl.BlockSpec((1, LANES), lambda b, c: (0, 0)),
            pl.BlockSpec((1, inner), lambda b, c: (0, 0)),
            pl.BlockSpec((1, inner), lambda b, c: (0, 0)),
            pl.BlockSpec((LANES, inner), lambda b, c: (0, 0)),
        ],
        out_specs=pl.BlockSpec((L, inner), lambda b, c: (rows(b, c), 0)),
        out_shape=jax.ShapeDtypeStruct((bsz * seq, inner), BF16),
        scratch_shapes=[
            pltpu.VMEM((SUBLANES, conv_dim), F32),
            pltpu.VMEM((SUBLANES + L, 2 * LANES), F32),
            pltpu.VMEM((n_pairs, SSD_STATE, LANES), F32),
            pltpu.VMEM((L, inner), F32),
            pltpu.VMEM((L, bc), F32),
            pltpu.VMEM((L, bc), BF16),
            pltpu.VMEM((L, inner), F32),
            pltpu.VMEM((L, inner), F32),
            pltpu.VMEM((L, inner), F32),
            pltpu.VMEM((L, inner), F32),
        ],
        compiler_params=_cparams(("parallel", "arbitrary")),
        name="ssd",
    )(proj, proj, proj, proj, dt_raw, conv_w, conv_b, dt_bias_p, a_log_p, dskip_x, norm_w, e_mat)


def _sb_kernel(q_ref, k_ref, v_ref, o_ref, acc_ref, carry_ref):
    BL = SB_BLOCK
    i = pl.program_id(2)
    scale = SB_HEAD_DIM ** -0.5
    row = lax.broadcasted_iota(jnp.int32, (BL, BL), 0)
    col = lax.broadcasted_iota(jnp.int32, (BL, BL), 1)
    causal = col < row
    r2 = lax.broadcasted_iota(jnp.int32, (BL, 2 * BL), 0)
    c2 = lax.broadcasted_iota(jnp.int32, (BL, 2 * BL), 1)
    tri = jnp.where((r2 > c2) | (c2 >= BL), 1.0, 0.0).astype(BF16)

    def sweep(s, first):
        subs = range(SB_SUB)
        kbs = [i * SB_SUB + j - s for j in subs]
        starts = [pl.multiple_of(jnp.maximum(kb, 0) * BL, BL) for kb in kbs]
        zls = [lax.dot_general(q_ref[j * BL:(j + 1) * BL, :], k_ref[pl.ds(starts[j], BL), :],
                               (((1,), (1,)), ((), ())), preferred_element_type=F32) * scale for j in subs]
        sps = [_softplus(zl) for zl in zls]
        log_betas = [zl - sp for zl, sp in zip(zls, sps)]
        log_keeps = [jnp.where(causal, -sp, 0.0) if first else -sp for sp in sps]
        splits = [_split2(lk) for lk in log_keeps]
        t2s = [_dot(hi, tri) + _dot(lo, tri) for hi, lo in splits]
        if first:
            atts = [jnp.where(causal, jnp.exp(lb + t2[:, :BL]), 0.0) for lb, t2 in zip(log_betas, t2s)]
            carries = [t2[:, BL:] for t2 in t2s]
        else:
            olds = [carry_ref[j] for j in subs]
            atts = [jnp.where(kb >= 0, jnp.exp(lb + t2[:, :BL] + old), 0.0)
                    for kb, lb, t2, old in zip(kbs, log_betas, t2s, olds)]
            carries = [old + jnp.where(kb >= 0, t2[:, BL:], 0.0) for kb, t2, old in zip(kbs, t2s, olds)]
        pvs = [_dot(atts[j].astype(BF16), v_ref[pl.ds(starts[j], BL), :]) for j in subs]
        worst = None
        for j in subs:
            if first:
                acc_ref[j] = pvs[j]
            else:
                acc_ref[j] += pvs[j]
            carry_ref[j] = carries[j]
            live = jnp.where(kbs[j] >= 1, carries[j], NEG_BIG)
            worst = live if worst is None else jnp.maximum(worst, live)
        return jnp.max(worst) > SB_EXIT_LOG

    sweep(0, True)
    go = sweep(1, False)

    def body(st):
        s, _ = st
        return s + 1, sweep(s, False)

    lax.while_loop(lambda st: st[1], body, (jnp.int32(2), go))
    for j in range(SB_SUB):
        o_ref[j * BL:(j + 1) * BL, :] = acc_ref[j].astype(o_ref.dtype)


def _stickbreak(proj3, bsz, seq, q_blk0):
    tq = SB_SUB * SB_BLOCK
    return pl.pallas_call(
        _sb_kernel,
        grid=(bsz, SB_HEADS, seq // tq),
        in_specs=[
            pl.BlockSpec((None, tq, SB_HEAD_DIM), lambda b, h, i: (b, i, q_blk0 + h)),
            pl.BlockSpec((None, seq, SB_HEAD_DIM), lambda b, h, i: (b, 0, q_blk0 + SB_HEADS + h)),
            pl.BlockSpec((None, seq, SB_HEAD_DIM), lambda b, h, i: (b, 0, q_blk0 + 2 * SB_HEADS + h)),
        ],
        out_specs=pl.BlockSpec((None, tq, SB_HEAD_DIM), lambda b, h, i: (b, i, h)),
        out_shape=jax.ShapeDtypeStruct((bsz, seq, SB_HEADS * SB_HEAD_DIM), BF16),
        scratch_shapes=[pltpu.VMEM((SB_SUB, SB_BLOCK, SB_HEAD_DIM), F32),
                        pltpu.VMEM((SB_SUB, SB_BLOCK, SB_BLOCK), F32)],
        compiler_params=_cparams(("parallel", "parallel", "arbitrary")),
        name="stickbreak",
    )(proj3, proj3, proj3)


def _merge_kernel(x_ref, ys_ref, yb_ref, *rest):
    *g_refs, w1_ref, w2_ref, wm_ref, o_ref = rest
    nj = len(g_refs) // 2
    tn = g_refs[0].shape[1]
    ys = ys_ref[...]
    yb = yb_ref[...]
    ms = []
    for j in range(nj):
        cs = slice(j * tn, (j + 1) * tn)
        s1 = jax.nn.sigmoid(g_refs[j][...].astype(F32))
        s2 = jax.nn.sigmoid(g_refs[nj + j][...].astype(F32))
        ms.append((s1 * _dot(ys, w1_ref[:, cs]) + s2 * _dot(yb, w2_ref[:, cs])).astype(BF16))
    o_ref[...] = x_ref[...] + _dot(jnp.concatenate(ms, axis=1), wm_ref[...])


def _merge(x2, y_ssd, y_sb, proj, w1, w2, wm, g_blk0, tm, tn):
    t, d = x2.shape
    tm = min(tm, t)
    nj = d // tn
    resident = pl.Buffered(1)
    g_specs = [pl.BlockSpec((tm, tn), lambda i, b=g_blk0 + b: (i, b)) for b in range(2 * nj)]
    return pl.pallas_call(
        _merge_kernel,
        grid=(t // tm,),
        in_specs=[
            pl.BlockSpec((tm, d), lambda i: (i, 0)),
            pl.BlockSpec((tm, y_ssd.shape[1]), lambda i: (i, 0)),
            pl.BlockSpec((tm, y_sb.shape[1]), lambda i: (i, 0)),
            *g_specs,
            pl.BlockSpec(w1.shape, lambda i: (0, 0), pipeline_mode=resident),
            pl.BlockSpec(w2.shape, lambda i: (0, 0), pipeline_mode=resident),
            pl.BlockSpec(wm.shape, lambda i: (0, 0), pipeline_mode=resident),
        ],
        out_specs=pl.BlockSpec((tm, d), lambda i: (i, 0)),
        out_shape=jax.ShapeDtypeStruct((t, d), F32),
        compiler_params=_cparams(("parallel",)),
        name="merge",
    )(x2, y_ssd, y_sb, *([proj] * (2 * nj)), w1, w2, wm)


def _xattn_kernel(h_ref, kv_ref, nxa_ref, wq_ref, wo_ref, nmoe_ref, wr_ref, h2_ref, um_ref, rt_ref, cnt_ref,
                  base_ref):
    h1 = h_ref[...]
    un = _rms(h1, nxa_ref[...]).astype(BF16)
    q = _dot(un, wq_ref[...]).astype(BF16)
    width = XA_HEADS * XA_HEAD_DIM
    outs = []
    for hd in range(XA_HEADS):
        cs = slice(hd * XA_HEAD_DIM, (hd + 1) * XA_HEAD_DIM)
        k = kv_ref[:, cs]
        v = kv_ref[:, width + hd * XA_HEAD_DIM: width + (hd + 1) * XA_HEAD_DIM]
        sc = lax.dot_general(q[:, cs], k, (((1,), (1,)), ((), ())),
                             preferred_element_type=F32) * (XA_HEAD_DIM ** -0.5)
        sc = sc - jnp.max(sc, axis=-1, keepdims=True)
        p = jnp.exp(sc)
        p = p / jnp.sum(p, axis=-1, keepdims=True)
        outs.append(_dot(p.astype(BF16), v).astype(BF16))
    o = jnp.concatenate(outs, axis=1)
    h2 = h1 + _dot(o, wo_ref[...])
    h2_ref[...] = h2
    um = _rms(h2, nmoe_ref[...])
    um_ref[...] = _pack_halves(um)
    hi, lo = _split2(um)
    nr = rt_ref.shape[1]
    both = _dot(hi, wr_ref[...])
    lg = both[:, :nr] + both[:, nr:] + _dot(lo, wr_ref[:, :nr])

    lane = lax.broadcasted_iota(jnp.int32, lg.shape, 1)
    lane_f = lane.astype(F32)
    is_g = lane < N_GROUPS
    gl = jnp.where(is_g, lg, NEG_BIG)
    gmax = jnp.max(gl, axis=-1, keepdims=True)
    g_sel = jnp.min(jnp.where(gl == gmax, lane_f, float(LANES)), axis=-1, keepdims=True)
    g_gate = 1.0 / jnp.sum(jnp.where(is_g, jnp.exp(gl - gmax), 0.0), axis=-1, keepdims=True)
    lo_lane = N_GROUPS + EXPERTS_PER_GROUP * g_sel
    el = jnp.where((lane_f >= lo_lane) & (lane_f < lo_lane + EXPERTS_PER_GROUP), lg, NEG_BIG)
    m1 = jnp.max(el, axis=-1, keepdims=True)
    i1 = jnp.min(jnp.where(el == m1, lane_f, float(LANES)), axis=-1, keepdims=True)
    el2 = jnp.where(lane_f == i1, NEG_BIG, el)
    m2 = jnp.max(el2, axis=-1, keepdims=True)
    i2 = jnp.min(jnp.where(el2 == m2, lane_f, float(LANES)), axis=-1, keepdims=True)
    ex = jnp.exp(m2 - m1)
    p1 = 1.0 / (1.0 + ex)
    e1 = i1 - N_GROUPS
    e2 = i2 - N_GROUPS

    @pl.when(pl.program_id(0) == 0)
    def _():
        base_ref[...] = jnp.zeros_like(base_ref)

    tm = lg.shape[0]
    hot1 = lane_f == e1
    hot2 = lane_f == e2
    onehot = jnp.where(hot1 | hot2, 1.0, 0.0)
    tr = lax.broadcasted_iota(jnp.int32, (tm, tm), 0)
    tc = lax.broadcasted_iota(jnp.int32, (tm, tm), 1)
    before = _dot(jnp.where(tc < tr, 1.0, 0.0).astype(BF16), onehot.astype(BF16)) + base_ref[...]
    r1 = jnp.sum(jnp.where(hot1, before, 0.0), axis=-1, keepdims=True)
    r2 = jnp.sum(jnp.where(hot2, before, 0.0), axis=-1, keepdims=True)
    base_ref[...] += jnp.sum(onehot, axis=0, keepdims=True)
    cnt_ref[...] = jnp.broadcast_to(base_ref[...], cnt_ref.shape)

    vals = (e1, e2, p1 * g_gate, ex * p1 * g_gate, r1, r2)
    out = jnp.zeros_like(lg)
    for j, v in enumerate(vals):
        out = jnp.where(lane == j, v, out)
    rt_ref[...] = out


def _xattn(h1, kv, nxa, wq, wo, nmoe, wr, seq, tm):
    t, d = h1.shape
    tm = min(tm, seq)
    per_b = seq // tm
    m_len = kv.shape[0] // (t // seq)
    nr = wr.shape[1] // 2
    return pl.pallas_call(
        _xattn_kernel,
        grid=(t // tm,),
        in_specs=[
            pl.BlockSpec((tm, d), lambda i: (i, 0)),
            pl.BlockSpec((m_len, kv.shape[1]), lambda i: (i // per_b, 0)),
            pl.BlockSpec((1, d), lambda i: (0, 0)),
            pl.BlockSpec(wq.shape, lambda i: (0, 0)),
            pl.BlockSpec(wo.shape, lambda i: (0, 0)),
            pl.BlockSpec((1, d), lambda i: (0, 0)),
            pl.BlockSpec((d, 2 * nr), lambda i: (0, 0)),
        ],
        out_specs=[
            pl.BlockSpec((tm, d), lambda i: (i, 0)),
            pl.BlockSpec((tm, d // 2), lambda i: (i, 0)),
            pl.BlockSpec((tm, nr), lambda i: (i, 0)),
            pl.BlockSpec((SUBLANES, nr), lambda i: (0, 0)),
        ],
        out_shape=[jax.ShapeDtypeStruct((t, d), F32), jax.ShapeDtypeStruct((t, d // 2), U32),
                   jax.ShapeDtypeStruct((t, nr), F32), jax.ShapeDtypeStruct((SUBLANES, nr), F32)],
        scratch_shapes=[pltpu.VMEM((1, nr), F32)],
        compiler_params=_cparams(("arbitrary",)),
        name="xattn",
    )(h1, kv, nxa, wq, wo, nmoe, wr)


def _dispatch_kernel(slot_ref, padrow_ref, padn_ref, nv_ref, um_ref, xs_hbm, sa_ref, sb_ref, z_ref, sem, zsem):
    i = pl.program_id(0)
    tm = um_ref.shape[0]
    last = pl.num_programs(0) - 1
    stage = (sa_ref, sb_ref)

    def zero_rows(start):
        for e in range(N_EXPERTS):
            def body(j, c, e=e):
                cp = pltpu.make_async_copy(z_ref.at[pl.ds(0, 1)], xs_hbm.at[pl.ds(padrow_ref[e] + j, 1)], zsem)
                if start:
                    cp.start()
                else:
                    cp.wait()
                return c

            lax.fori_loop(0, padn_ref[e], body, 0)

        def tail(b, c):
            cp = pltpu.make_async_copy(
                z_ref, xs_hbm.at[pl.ds(pl.multiple_of(b * MOE_BLOCK, MOE_BLOCK), MOE_BLOCK)], zsem)
            if start:
                cp.start()
            else:
                cp.wait()
            return c

        lax.fori_loop(nv_ref[0], xs_hbm.shape[0] // MOE_BLOCK, tail, 0)

    @pl.when(i == 0)
    def _():
        z_ref[...] = jnp.zeros_like(z_ref)
        zero_rows(True)

    def wait(par):
        for _ in range(TOP_K):
            pltpu.make_async_copy(stage[par], xs_hbm.at[pl.ds(0, tm)], sem.at[par]).wait()

    def step(par):
        pl.when(i >= 2)(functools.partial(wait, par))
        stage[par][...] = um_ref[...]
        base = i * tm * TOP_K
        for r in range(tm):
            for k in range(TOP_K):
                pltpu.make_async_copy(stage[par].at[pl.ds(r, 1)],
                                      xs_hbm.at[pl.ds(slot_ref[base + TOP_K * r + k], 1)],
                                      sem.at[par]).start(priority=k)

        @pl.when(i == last)
        def _():
            wait(par)
            pl.when(i >= 1)(functools.partial(wait, 1 - par))

    for par in range(2):
        pl.when(i % 2 == par)(functools.partial(step, par))

    @pl.when(i == last)
    def _():
        zero_rows(False)


def _dispatch(slot, pad_row, pad_n, n_valid, um, n_slots, tm):
    t, d = um.shape
    tm = min(tm, t)
    grid_spec = pltpu.PrefetchScalarGridSpec(
        num_scalar_prefetch=4,
        grid=(t // tm,),
        in_specs=[pl.BlockSpec((tm, d), lambda i, s, pr, pn, nv: (i, 0))],
        out_specs=pl.BlockSpec(memory_space=pl.ANY),
        scratch_shapes=[pltpu.VMEM((tm, d), um.dtype), pltpu.VMEM((tm, d), um.dtype),
                        pltpu.VMEM((MOE_BLOCK, d), um.dtype),
                        pltpu.SemaphoreType.DMA((2,)), pltpu.SemaphoreType.DMA(())],
    )
    return pl.pallas_call(
        _dispatch_kernel,
        grid_spec=grid_spec,
        out_shape=jax.ShapeDtypeStruct((n_slots, d), um.dtype),
        compiler_params=_cparams(("arbitrary",)),
        name="dispatch",
    )(slot, pad_row, pad_n, n_valid, um)


def _expert_kernel(be_ref, nv_ref, x_ref, wg_ref, wu_ref, wd_ref, o_ref, wgb_ref, wub_ref, wdb_ref):
    i = pl.program_id(0)

    @pl.when(jnp.logical_or(i == 0, be_ref[i] != be_ref[jnp.maximum(i - 1, 0)]))
    def _():
        wgb_ref[...] = wg_ref[...].astype(BF16)
        wub_ref[...] = wu_ref[...].astype(BF16)
        wdb_ref[...] = wd_ref[...].astype(BF16)

    @pl.when(i < nv_ref[0])
    def _():
        lo, hi = _unpack_halves(x_ref[...])
        xb = jnp.concatenate([lo.astype(BF16), hi.astype(BF16)], axis=1)
        hid = (_silu(_dot(xb, wgb_ref[...])) * _dot(xb, wub_ref[...])).astype(BF16)
        o_ref[...] = _pack_halves(_dot(hid, wdb_ref[...]))

    @pl.when(i >= nv_ref[0])
    def _():
        o_ref[...] = jnp.zeros_like(o_ref)


def _experts(block_e, n_valid, xs, wg, wu, wd):
    n_slots, dp = xs.shape
    d, ff = wg.shape[1:]

    def x_map(i, be, nv):
        return (jnp.minimum(i, nv[0] - 1), 0)

    grid_spec = pltpu.PrefetchScalarGridSpec(
        num_scalar_prefetch=2,
        grid=(n_slots // MOE_BLOCK,),
        in_specs=[
            pl.BlockSpec((MOE_BLOCK, dp), x_map),
            pl.BlockSpec((None, d, ff), lambda i, be, nv: (be[i], 0, 0)),
            pl.BlockSpec((None, d, ff), lambda i, be, nv: (be[i], 0, 0)),
            pl.BlockSpec((None, ff, d), lambda i, be, nv: (be[i], 0, 0)),
        ],
        out_specs=pl.BlockSpec((MOE_BLOCK, dp), lambda i, be, nv: (i, 0)),
        scratch_shapes=[pltpu.VMEM((d, ff), BF16), pltpu.VMEM((d, ff), BF16), pltpu.VMEM((ff, d), BF16)],
    )
    return pl.pallas_call(
        _expert_kernel,
        grid_spec=grid_spec,
        out_shape=jax.ShapeDtypeStruct((n_slots, dp), U32),
        compiler_params=_cparams(("arbitrary",)),
        name="experts",
    )(block_e, n_valid, xs, wg, wu, wd)


def _combine_kernel(slot_ref, h_ref, w_ref, nw_ref, yb_hbm, o_ref, ya0_ref, ya1_ref, yb0_ref, yb1_ref, sem):
    i = pl.program_id(0)
    tm = h_ref.shape[0]
    last = pl.num_programs(0) - 1
    bufs = ((ya0_ref, ya1_ref), (yb0_ref, yb1_ref))

    def gather(blk, par):
        base = blk * tm * TOP_K
        for r in range(tm):
            for k in range(TOP_K):
                pltpu.make_async_copy(yb_hbm.at[pl.ds(slot_ref[base + TOP_K * r + k], 1)],
                                      bufs[par][k].at[pl.ds(r, 1)], sem.at[par]).start(priority=k)

    def wait(par):
        for k in range(TOP_K):
            pltpu.make_async_copy(yb_hbm.at[pl.ds(0, tm)], bufs[par][k], sem.at[par]).wait()

    @pl.when(i == 0)
    def _():
        gather(i, 0)

    def step(par):
        wait(par)
        gather(jnp.minimum(i + 1, last), 1 - par)
        w0 = w_ref[:, TOP_K:TOP_K + 1]
        w1 = w_ref[:, TOP_K + 1:TOP_K + 2]
        half = h_ref.shape[1] // 2
        ya = _unpack_halves(bufs[par][0][...])
        yb = _unpack_halves(bufs[par][1][...])
        h3 = [h_ref[:, p * half:(p + 1) * half] + w0 * ya[p] + w1 * yb[p] for p in range(2)]
        var = sum(jnp.sum(v * v, axis=-1, keepdims=True) for v in h3) / h_ref.shape[1]
        inv = lax.rsqrt(var + EPS)
        for p in range(2):
            o_ref[:, p * half:(p + 1) * half] = h3[p] * inv * nw_ref[:, p * half:(p + 1) * half]

        @pl.when(i == last)
        def _():
            wait(1 - par)

    for par in range(2):
        pl.when(i % 2 == par)(functools.partial(step, par))


def _combine(slot, h2, route, nw, yb, tm):
    t, d = h2.shape
    tm = min(tm, t)
    grid_spec = pltpu.PrefetchScalarGridSpec(
        num_scalar_prefetch=1,
        grid=(t // tm,),
        in_specs=[
            pl.BlockSpec((tm, d), lambda i, s: (i, 0)),
            pl.BlockSpec((tm, route.shape[1]), lambda i, s: (i, 0)),
            pl.BlockSpec((1, d), lambda i, s: (0, 0)),
            pl.BlockSpec(memory_space=pl.ANY),
        ],
        out_specs=pl.BlockSpec((tm, d), lambda i, s: (i, 0)),
        scratch_shapes=[pltpu.VMEM((tm, yb.shape[1]), yb.dtype)] * (2 * TOP_K) + [pltpu.SemaphoreType.DMA((2,))],
    )
    return pl.pallas_call(
        _combine_kernel,
        grid_spec=grid_spec,
        out_shape=jax.ShapeDtypeStruct((t, d), F32),
        compiler_params=_cparams(("arbitrary",)),
        name="combine",
    )(slot, h2, route, nw, yb)


def _slots(route, cnt, t):
    flat_e = route[:, :TOP_K].astype(jnp.int32).reshape(-1)
    rank = route[:, 2 * TOP_K:3 * TOP_K].astype(jnp.int32).reshape(-1)
    onehot = flat_e[:, None] == jnp.arange(N_EXPERTS, dtype=jnp.int32)[None, :]
    counts = cnt[0, :N_EXPERTS].astype(jnp.int32)
    padded = (counts + MOE_BLOCK - 1) // MOE_BLOCK * MOE_BLOCK
    pad_end = jnp.cumsum(padded)
    pad_start = pad_end - padded
    slot = jnp.sum(jnp.where(onehot, pad_start[None, :], 0), axis=1).astype(jnp.int32) + rank
    n_slots = t * TOP_K + N_EXPERTS * MOE_BLOCK
    n_blocks = n_slots // MOE_BLOCK
    block_start = jnp.arange(n_blocks, dtype=jnp.int32) * MOE_BLOCK
    block_e = jnp.minimum(jnp.sum((pad_end[None, :] <= block_start[:, None]).astype(jnp.int32), axis=1),
                          N_EXPERTS - 1)
    n_valid = (pad_end[-1:] // MOE_BLOCK).astype(jnp.int32)
    return slot, pad_start + counts, padded - counts, block_e, n_valid, n_slots


def _pad_lanes(v, n=LANES):
    return jnp.pad(v, ((0, 0), (0, n - v.shape[1])))


def kernel(x, mem, norm_mix_w, w_in, conv_w, conv_b, dt_bias, a_log, d_skip, ssd_norm_w, w_ssd_branch, w_sb_branch, w_mix_out, norm_xa_w, norm_mem_w, w_xq, w_xk, w_xv, w_xo, norm_moe_w, w_router_group, w_router_expert, w_expert_gate, w_expert_up, w_expert_down, norm_final_w):
    bsz, seq, d = x.shape
    depth = w_in.shape[0]
    t = bsz * seq
    inner = ssd_norm_w.shape[1]
    heads = dt_bias.shape[1]
    bc = SSD_GROUPS * SSD_STATE
    conv_dim = inner + 2 * bc
    sb_width = SB_HEADS * SB_HEAD_DIM
    col_dt = inner + conv_dim
    col_qkv = col_dt + heads
    col_g = col_qkv + 3 * sb_width
    g_col0 = 2 * inner
    bc_col0 = g_col0 + 2 * d
    q_col0 = bc_col0 + 2 * bc
    n_main = q_col0 + 3 * sb_width
    tn_proj = 1280
    n_proj = -(-n_main // tn_proj) * tn_proj
    tn_merge = 1024

    assert depth == 1, "single-layer configuration: the final RMSNorm is fused into the last MoE combine"
    h = x.reshape(t, d)
    for l in range(depth):
        w_l = w_in[l]
        w_main = jnp.concatenate([w_l[:, :2 * inner], w_l[:, col_g:], w_l[:, 2 * inner:col_dt],
                                  w_l[:, col_qkv:col_g], jnp.zeros((d, n_proj - n_main), F32)],
                                 axis=1).astype(BF16)
        w_dt = _pad_lanes(w_l[:, col_dt:col_dt + heads]).astype(BF16)
        e_mat = (jnp.arange(inner, dtype=jnp.int32)[None, :] // SSD_HEAD_DIM
                 == jnp.arange(LANES, dtype=jnp.int32)[:, None]).astype(BF16)
        dskip_x = jnp.repeat(d_skip[l], SSD_HEAD_DIM)[None, :]

        proj, dt_raw = _norm_matmul(h, norm_mix_w[l][None, :], w_main, w_dt, tm=1024, tn=tn_proj)
        y_ssd = _ssd(proj, dt_raw, conv_w[l], conv_b[l][None, :], _pad_lanes(dt_bias[l][None, :]),
                     _pad_lanes(a_log[l][None, :]), dskip_x, ssd_norm_w[l][None, :], e_mat,
                     bsz, seq, inner, bc, bc_col0 // bc)
        y_sb = _stickbreak(proj.reshape(bsz, seq, n_proj), bsz, seq, q_col0 // SB_HEAD_DIM)
        h1 = _merge(h, y_ssd, y_sb.reshape(t, sb_width), proj,
                    w_ssd_branch[l].astype(BF16), w_sb_branch[l].astype(BF16), w_mix_out[l].astype(BF16),
                    g_col0 // tn_merge, tm=512, tn=tn_merge)

        m_len = mem.shape[1]
        w_kv = jnp.concatenate([w_xk[l], w_xv[l]], axis=1).astype(BF16)
        kv, _ = _norm_matmul(mem.reshape(bsz * m_len, d), norm_mem_w[l][None, :], w_kv,
                             jnp.zeros((d, LANES), BF16), tm=bsz * m_len, tn=512)
        w_r = _pad_lanes(jnp.concatenate([w_router_group[l], w_router_expert[l]], axis=1))
        wr_hi = w_r.astype(BF16)
        wr_lo = (w_r - wr_hi.astype(F32)).astype(BF16)
        h2, um, route, cnt = _xattn(h1, kv, norm_xa_w[l][None, :], w_xq[l].astype(BF16), w_xo[l].astype(BF16),
                               norm_moe_w[l][None, :], jnp.concatenate([wr_hi, wr_lo], axis=1), seq, tm=512)

        slot, pad_row, pad_n, block_e, n_valid, n_slots = _slots(route, cnt, t)
        xs = _dispatch(slot, pad_row, pad_n, n_valid, um, n_slots, tm=256)
        yb = _experts(block_e, n_valid, xs, w_expert_gate[l], w_expert_up[l], w_expert_down[l])
        h = _combine(slot, h2, route, norm_final_w[None, :], yb, tm=256)
    return h.reshape(bsz, seq, d)
```

```python
import functools

import jax
import jax.numpy as jnp
from jax import lax
from jax.experimental import pallas as pl
from jax.experimental.pallas import tpu as pltpu

F32 = jnp.float32
BF16 = jnp.bfloat16
U32 = jnp.uint32
EPS = 1e-6

SSD_HEAD_DIM = 64
SSD_GROUPS = 4
SSD_STATE = 128
SSD_CONV = 4
SSD_CHUNK = 128
SB_HEADS = 4
SB_HEAD_DIM = 128
SB_BLOCK = 128
SB_SUB = 8
XA_HEADS = 4
XA_HEAD_DIM = 128
N_GROUPS = 4
EXPERTS_PER_GROUP = 8
N_EXPERTS = N_GROUPS * EXPERTS_PER_GROUP
TOP_K = 2
MOE_BLOCK = 256

LANES = 128
SUBLANES = 8
VMEM_LIMIT = 56 * 1024 * 1024

SB_EXIT_LOG = -88.0
NEG_BIG = -1e30


def _cparams(sem):
    return pltpu.CompilerParams(dimension_semantics=sem, vmem_limit_bytes=VMEM_LIMIT)


def _rms(x, w):
    var = jnp.mean(x * x, axis=-1, keepdims=True)
    return x * lax.rsqrt(var + EPS) * w


def _split2(v):
    hi = v.astype(BF16)
    lo = (v - hi.astype(F32)).astype(BF16)
    return hi, lo


def _dot(a, b):
    return jnp.dot(a, b, preferred_element_type=F32)


def _pack_halves(v):
    n = v.shape[1] // 2
    lo = lax.bitcast_convert_type(v[:, :n].astype(BF16).astype(F32), U32)
    hi = lax.bitcast_convert_type(v[:, n:].astype(BF16).astype(F32), U32)
    return (lo >> 16) | hi


def _unpack_halves(w):
    lo = lax.bitcast_convert_type(w << 16, F32)
    hi = lax.bitcast_convert_type(w & jnp.uint32(0xFFFF0000), F32)
    return lo, hi


def _silu(x):
    return x / (1.0 + jnp.exp(-x))


def _softplus(x):
    return jnp.maximum(x, 0.0) + jnp.log(1.0 + jnp.exp(-jnp.abs(x)))


def _norm_matmul_kernel(x_ref, nw_ref, w_ref, ws_ref, o_ref, os_ref, u_ref):
    @pl.when(pl.program_id(1) == 0)
    def _():
        u = _rms(x_ref[...], nw_ref[...]).astype(BF16)
        u_ref[...] = u
        os_ref[...] = _dot(u, ws_ref[...])

    o_ref[...] = _dot(u_ref[...], w_ref[...]).astype(o_ref.dtype)


def _norm_matmul(x, nw, w, w_side, tm, tn):
    m, k = x.shape
    n = w.shape[1]
    ns = w_side.shape[1]
    tm = min(tm, m)
    return pl.pallas_call(
        _norm_matmul_kernel,
        grid=(m // tm, n // tn),
        in_specs=[
            pl.BlockSpec((tm, k), lambda i, j: (i, 0)),
            pl.BlockSpec((1, k), lambda i, j: (0, 0)),
            pl.BlockSpec((k, tn), lambda i, j: (0, j)),
            pl.BlockSpec((k, ns), lambda i, j: (0, 0)),
        ],
        out_specs=[
            pl.BlockSpec((tm, tn), lambda i, j: (i, j)),
            pl.BlockSpec((tm, ns), lambda i, j: (i, 0)),
        ],
        out_shape=[jax.ShapeDtypeStruct((m, n), BF16), jax.ShapeDtypeStruct((m, ns), F32)],
        scratch_shapes=[pltpu.VMEM((tm, k), BF16)],
        compiler_params=_cparams(("parallel", "arbitrary")),
        name="norm_matmul",
    )(x, nw, w, w_side)


def _proj_stream_kernel(x_ref, nw_ref, ws_ref, w_hbm, os_ref, o_hbm, u_ref, wbuf_ref, obuf_ref, wsem, osem):
    tm = x_ref.shape[0]
    tn = wbuf_ref.shape[2]
    nj = w_hbm.shape[1] // tn
    row0 = pl.multiple_of(pl.program_id(0) * tm, tm)

    def w_copy(j):
        return pltpu.make_async_copy(w_hbm.at[:, pl.ds(j * tn, tn)], wbuf_ref.at[j % 2], wsem.at[j % 2])

    def o_copy(j):
        return pltpu.make_async_copy(obuf_ref.at[j % 2], o_hbm.at[pl.ds(row0, tm), pl.ds(j * tn, tn)],
                                     osem.at[j % 2])

    w_copy(0).start()
    u = _rms(x_ref[...], nw_ref[...]).astype(BF16)
    u_ref[...] = u
    os_ref[...] = _dot(u, ws_ref[...])
    for j in range(nj):
        if j + 1 < nj:
            w_copy(j + 1).start()
        w_copy(j).wait()
        if j >= 2:
            o_copy(j - 2).wait()
        obuf_ref[j % 2] = _dot(u_ref[...], wbuf_ref[j % 2]).astype(obuf_ref.dtype)
        o_copy(j).start()
    for j in range(max(nj - 2, 0), nj):
        o_copy(j).wait()


def _proj_stream(x, nw, w, w_side, tm, tn):
    m, k = x.shape
    n = w.shape[1]
    ns = w_side.shape[1]
    tm = min(tm, m)
    return pl.pallas_call(
        _proj_stream_kernel,
        grid=(m // tm,),
        in_specs=[
            pl.BlockSpec((tm, k), lambda i: (i, 0)),
            pl.BlockSpec((1, k), lambda i: (0, 0)),
            pl.BlockSpec((k, ns), lambda i: (0, 0)),
            pl.BlockSpec(memory_space=pl.ANY),
        ],
        out_specs=[
            pl.BlockSpec((tm, ns), lambda i: (i, 0)),
            pl.BlockSpec(memory_space=pl.ANY),
        ],
        out_shape=[jax.ShapeDtypeStruct((m, ns), F32), jax.ShapeDtypeStruct((m, n), BF16)],
        scratch_shapes=[pltpu.VMEM((tm, k), BF16), pltpu.VMEM((2, k, tn), BF16), pltpu.VMEM((2, tm, tn), BF16),
                        pltpu.SemaphoreType.DMA((2,)), pltpu.SemaphoreType.DMA((2,))],
        compiler_params=_cparams(("arbitrary",)),
        name="proj_stream",
    )(x, nw, w_side, w)


def _ssd_kernel(z_ref, xs_ref, b_ref, c_ref, dtr_ref, cw_ref, cb_ref, dtb_ref, alog_ref, dsk_ref,
                nw_ref, e_ref, o_ref,
                tail_ref, win_ref, state_ref, xact_ref, bact_ref, cact_ref, dtx_ref, eax_ref, wsx_ref, y_ref):
    L = SSD_CHUNK
    inner = xs_ref.shape[1]
    bc = b_ref.shape[1]
    n_pairs = inner // LANES

    @pl.when(pl.program_id(1) == 0)
    def _():
        tail_ref[...] = jnp.zeros_like(tail_ref)
        state_ref[...] = jnp.zeros_like(state_ref)


    def conv_seg(src_ref, col0, width, dst_ref):
        for j in range(0, width, 2 * LANES):
            cols = slice(col0 + j, col0 + j + 2 * LANES)
            xin = src_ref[:, j:j + 2 * LANES].astype(F32)
            win_ref[:SUBLANES, :] = tail_ref[:, cols]
            win_ref[SUBLANES:, :] = xin
            acc = xin * cw_ref[SSD_CONV - 1:SSD_CONV, cols] + cb_ref[:, cols]
            for s in range(1, SSD_CONV):
                acc = acc + win_ref[SUBLANES - s:SUBLANES - s + L, :] * cw_ref[SSD_CONV - 1 - s:SSD_CONV - s, cols]
            tail_ref[:, cols] = xin[L - SUBLANES:]
            dst_ref[:, j:j + 2 * LANES] = _silu(acc).astype(dst_ref.dtype)

    conv_seg(xs_ref, 0, inner, xact_ref)
    conv_seg(b_ref, inner, bc, bact_ref)
    conv_seg(c_ref, inner + bc, bc, cact_ref)

    dt = _softplus(dtr_ref[...] + dtb_ref[...])
    da = dt * (-jnp.exp(alog_ref[...]))
    rowl = lax.broadcasted_iota(jnp.int32, (L, L), 0)
    coll = lax.broadcasted_iota(jnp.int32, (L, L), 1)
    lower = rowl >= coll
    tri = jnp.where(lower, 1.0, 0.0).astype(BF16)
    d1 = da.astype(BF16)
    r1 = da - d1.astype(F32)
    d2 = r1.astype(BF16)
    d3 = (r1 - d2.astype(F32)).astype(BF16)
    a_cum = _dot(tri, d1) + _dot(tri, d2) + _dot(tri, d3)
    a_cum_t = a_cum.T
    a_last = a_cum[L - 1:L, :]
    e_mat = e_ref[...]

    def expand(v):
        hi, lo = _split2(v)
        return _dot(hi, e_mat) + _dot(lo, e_mat)

    dtx_ref[...] = expand(dt)
    eax_ref[...] = expand(jnp.exp(a_cum))
    wsx_ref[...] = expand(jnp.exp(a_last - a_cum))
    elx = expand(jnp.broadcast_to(jnp.exp(a_last), (SUBLANES, LANES)))[0:1]

    lane = lax.broadcasted_iota(jnp.int32, (L, LANES), 1)
    pairs_per_group = n_pairs // SSD_GROUPS
    for g in range(SSD_GROUPS):
        gcols = slice(g * SSD_STATE, (g + 1) * SSD_STATE)
        bg_t = bact_ref[:, gcols].T.astype(BF16)
        cg = cact_ref[:, gcols]
        cb = _dot(cg, bg_t)
        for pp in range(pairs_per_group):
            p = g * pairs_per_group + pp
            pc = slice(p * LANES, (p + 1) * LANES)
            ms = []
            for hh in (2 * p, 2 * p + 1):
                seg = a_cum[:, hh:hh + 1] - a_cum_t[hh:hh + 1, :]
                dec = jnp.exp(jnp.where(lower, seg, NEG_BIG))
                ms.append((cb * dec).astype(BF16))
            lhs = jnp.concatenate(ms, axis=1)
            xs_p = xact_ref[:, pc]
            xdt = xs_p * dtx_ref[:, pc]
            rhs = jnp.concatenate([jnp.where(lane < SSD_HEAD_DIM, xdt, 0.0).astype(BF16),
                                   jnp.where(lane >= SSD_HEAD_DIM, xdt, 0.0).astype(BF16)], axis=0)
            st = state_ref[p]
            y = _dot(lhs, rhs)
            y = y + _dot(cg, st.astype(BF16)) * eax_ref[:, pc]
            y = y + xs_p * dsk_ref[:, pc]
            y_ref[:, pc] = y
            xw = (xdt * wsx_ref[:, pc]).astype(BF16)
            state_ref[p] = st * elx[:, pc] + _dot(bg_t, xw)

    yg = y_ref[...] * _silu(z_ref[...].astype(F32))
    o_ref[...] = _rms(yg, nw_ref[...]).astype(o_ref.dtype)


def _ssd(proj, dt_raw, conv_w, conv_b, dt_bias_p, a_log_p, dskip_x, norm_w, e_mat, bsz, seq, inner, bc, b_blk):
    L = SSD_CHUNK
    nc = seq // L
    conv_dim = inner + 2 * bc
    n_pairs = inner // LANES

    def rows(b, c):
        return b * nc + c

    return pl.pallas_call(
        _ssd_kernel,
        grid=(bsz, nc),
        in_specs=[
            pl.BlockSpec((L, inner), lambda b, c: (rows(b, c), 0)),
            pl.BlockSpec((L, inner), lambda b, c: (rows(b, c), 1)),
            pl.BlockSpec((L, bc), lambda b, c: (rows(b, c), b_blk)),
            pl.BlockSpec((L, bc), lambda b, c: (rows(b, c), b_blk + 1)),
            pl.BlockSpec((L, LANES), lambda b, c: (rows(b, c), 0)),
            pl.BlockSpec((SSD_CONV, conv_dim), lambda b, c: (0, 0)),
            pl.BlockSpec((1, conv_dim), lambda b, c: (0, 0)),
            pl.BlockSpec((1, LANES), lambda b, c: (0, 0)),
            pl.BlockSpec((1, LANES), lambda b, c: (0, 0)),
            pl.BlockSpec((1, inner), lambda b, c: (0, 0)),
            pl.BlockSpec((1, inner), lambda b, c: (0, 0)),
            pl.BlockSpec((LANES, inner), lambda b, c: (0, 0)),
        ],
        out_specs=pl.BlockSpec((L, inner), lambda b, c: (rows(b, c), 0)),
        out_shape=jax.ShapeDtypeStruct((bsz * seq, inner), BF16),
        scratch_shapes=[
            pltpu.VMEM((SUBLANES, conv_dim), F32),
            pltpu.VMEM((SUBLANES + L, 2 * LANES), F32),
            pltpu.VMEM((n_pairs, SSD_STATE, LANES), F32),
            pltpu.VMEM((L, inner), F32),
            pltpu.VMEM((L, bc), F32),
            pltpu.VMEM((L, bc), BF16),
            pltpu.VMEM((L, inner), F32),
            pltpu.VMEM((L, inner), F32),
            pltpu.VMEM((L, inner), F32),
            pltpu.VMEM((L, inner), F32),
        ],
        compiler_params=_cparams(("parallel", "arbitrary")),
        name="ssd",
    )(proj, proj, proj, proj, dt_raw, conv_w, conv_b, dt_bias_p, a_log_p, dskip_x, norm_w, e_mat)


def _sb_kernel(q_ref, k_ref, v_ref, o_ref, acc_ref, carry_ref):
    BL = SB_BLOCK
    i = pl.program_id(2)
    scale = SB_HEAD_DIM ** -0.5
    row = lax.broadcasted_iota(jnp.int32, (BL, BL), 0)
    col = lax.broadcasted_iota(jnp.int32, (BL, BL), 1)
    causal = col < row
    r2 = lax.broadcasted_iota(jnp.int32, (BL, 2 * BL), 0)
    c2 = lax.broadcasted_iota(jnp.int32, (BL, 2 * BL), 1)
    tri = jnp.where((r2 > c2) | (c2 >= BL), 1.0, 0.0).astype(BF16)

    def sweep(s, first):
        subs = range(SB_SUB)
        kbs = [i * SB_SUB + j - s for j in subs]
        starts = [pl.multiple_of(jnp.maximum(kb, 0) * BL, BL) for kb in kbs]
        zls = [lax.dot_general(q_ref[j * BL:(j + 1) * BL, :], k_ref[pl.ds(starts[j], BL), :],
                               (((1,), (1,)), ((), ())), preferred_element_type=F32) * scale for j in subs]
        sps = [_softplus(zl) for zl in zls]
        log_betas = [zl - sp for zl, sp in zip(zls, sps)]
        log_keeps = [jnp.where(causal, -sp, 0.0) if first else -sp for sp in sps]
        splits = [_split2(lk) for lk in log_keeps]
        t2s = [_dot(hi, tri) + _dot(lo, tri) for hi, lo in splits]
        if first:
            atts = [jnp.where(causal, jnp.exp(lb + t2[:, :BL]), 0.0) for lb, t2 in zip(log_betas, t2s)]
            carries = [t2[:, BL:] for t2 in t2s]
        else:
            olds = [carry_ref[j] for j in subs]
            atts = [jnp.where(kb >= 0, jnp.exp(lb + t2[:, :BL] + old), 0.0)
                    for kb, lb, t2, old in zip(kbs, log_betas, t2s, olds)]
            carries = [old + jnp.where(kb >= 0, t2[:, BL:], 0.0) for kb, t2, old in zip(kbs, t2s, olds)]
        pvs = [_dot(atts[j].astype(BF16), v_ref[pl.ds(starts[j], BL), :]) for j in subs]
        worst = None
        for j in subs:
            if first:
                acc_ref[j] = pvs[j]
            else:
                acc_ref[j] += pvs[j]
            carry_ref[j] = carries[j]
            live = jnp.where(kbs[j] >= 1, carries[j], NEG_BIG)
            worst = live if worst is None else jnp.maximum(worst, live)
        return jnp.max(worst) > SB_EXIT_LOG

    sweep(0, True)
    go = sweep(1, False)

    def body(st):
        s, _ = st
        return s + 1, sweep(s, False)

    lax.while_loop(lambda st: st[1], body, (jnp.int32(2), go))
    for j in range(SB_SUB):
        o_ref[j * BL:(j + 1) * BL, :] = acc_ref[j].astype(o_ref.dtype)


def _stickbreak(proj3, bsz, seq, q_blk0):
    tq = SB_SUB * SB_BLOCK
    return pl.pallas_call(
        _sb_kernel,
        grid=(bsz, SB_HEADS, seq // tq),
        in_specs=[
            pl.BlockSpec((None, tq, SB_HEAD_DIM), lambda b, h, i: (b, i, q_blk0 + h)),
            pl.BlockSpec((None, seq, SB_HEAD_DIM), lambda b, h, i: (b, 0, q_blk0 + SB_HEADS + h)),
            pl.BlockSpec((None, seq, SB_HEAD_DIM), lambda b, h, i: (b, 0, q_blk0 + 2 * SB_HEADS + h)),
        ],
        out_specs=pl.BlockSpec((None, tq, SB_HEAD_DIM), lambda b, h, i: (b, i, h)),
        out_shape=jax.ShapeDtypeStruct((bsz, seq, SB_HEADS * SB_HEAD_DIM), BF16),
        scratch_shapes=[pltpu.VMEM((SB_SUB, SB_BLOCK, SB_HEAD_DIM), F32),
                        pltpu.VMEM((SB_SUB, SB_BLOCK, SB_BLOCK), F32)],
        compiler_params=_cparams(("parallel", "parallel", "arbitrary")),
        name="stickbreak",
    )(proj3, proj3, proj3)


def _merge_kernel(x_ref, ys_ref, yb_ref, *rest):
    *g_refs, w1_ref, w2_ref, wm_ref, o_ref = rest
    nj = len(g_refs) // 2
    tn = g_refs[0].shape[1]
    ys = ys_ref[...]
    yb = yb_ref[...]
    ms = []
    for j in range(nj):
        cs = slice(j * tn, (j + 1) * tn)
        s1 = jax.nn.sigmoid(g_refs[j][...].astype(F32))
        s2 = jax.nn.sigmoid(g_refs[nj + j][...].astype(F32))
        ms.append((s1 * _dot(ys, w1_ref[:, cs]) + s2 * _dot(yb, w2_ref[:, cs])).astype(BF16))
    o_ref[...] = x_ref[...] + _dot(jnp.concatenate(ms, axis=1), wm_ref[...])


def _merge(x2, y_ssd, y_sb, proj, w1, w2, wm, g_blk0, tm, tn):
    t, d = x2.shape
    tm = min(tm, t)
    nj = d // tn
    resident = pl.Buffered(1)
    g_specs = [pl.BlockSpec((tm, tn), lambda i, b=g_blk0 + b: (i, b)) for b in range(2 * nj)]
    return pl.pallas_call(
        _merge_kernel,
        grid=(t // tm,),
        in_specs=[
            pl.BlockSpec((tm, d), lambda i: (i, 0)),
            pl.BlockSpec((tm, y_ssd.shape[1]), lambda i: (i, 0)),
            pl.BlockSpec((tm, y_sb.shape[1]), lambda i: (i, 0)),
            *g_specs,
            pl.BlockSpec(w1.shape, lambda i: (0, 0), pipeline_mode=resident),
            pl.BlockSpec(w2.shape, lambda i: (0, 0), pipeline_mode=resident),
            pl.BlockSpec(wm.shape, lambda i: (0, 0), pipeline_mode=resident),
        ],
        out_specs=pl.BlockSpec((tm, d), lambda i: (i, 0)),
        out_shape=jax.ShapeDtypeStruct((t, d), F32),
        compiler_params=_cparams(("parallel",)),
        name="merge",
    )(x2, y_ssd, y_sb, *([proj] * (2 * nj)), w1, w2, wm)


def _xattn_kernel(h_ref, kv_ref, nxa_ref, wq_ref, wo_ref, nmoe_ref, wr_ref, h2_ref, um_ref, rt_ref, cnt_ref,
                  base_ref):
    h1 = h_ref[...]
    un = _rms(h1, nxa_ref[...]).astype(BF16)
    q = _dot(un, wq_ref[...]).astype(BF16)
    width = XA_HEADS * XA_HEAD_DIM
    outs = []
    for hd in range(XA_HEADS):
        cs = slice(hd * XA_HEAD_DIM, (hd + 1) * XA_HEAD_DIM)
        k = kv_ref[:, cs]
        v = kv_ref[:, width + hd * XA_HEAD_DIM: width + (hd + 1) * XA_HEAD_DIM]
        sc = lax.dot_general(q[:, cs], k, (((1,), (1,)), ((), ())),
                             preferred_element_type=F32) * (XA_HEAD_DIM ** -0.5)
        sc = sc - jnp.max(sc, axis=-1, keepdims=True)
        p = jnp.exp(sc)
        p = p / jnp.sum(p, axis=-1, keepdims=True)
        outs.append(_dot(p.astype(BF16), v).astype(BF16))
    o = jnp.concatenate(outs, axis=1)
    h2 = h1 + _dot(o, wo_ref[...])
    h2_ref[...] = h2
    um = _rms(h2, nmoe_ref[...])
    um_ref[...] = _pack_halves(um)
    hi, lo = _split2(um)
    nr = rt_ref.shape[1]
    both = _dot(hi, wr_ref[...])
    lg = both[:, :nr] + both[:, nr:] + _dot(lo, wr_ref[:, :nr])

    lane = lax.broadcasted_iota(jnp.int32, lg.shape, 1)
    lane_f = lane.astype(F32)
    is_g = lane < N_GROUPS
    gl = jnp.where(is_g, lg, NEG_BIG)
    gmax = jnp.max(gl, axis=-1, keepdims=True)
    g_sel = jnp.min(jnp.where(gl == gmax, lane_f, float(LANES)), axis=-1, keepdims=True)
    g_gate = 1.0 / jnp.sum(jnp.where(is_g, jnp.exp(gl - gmax), 0.0), axis=-1, keepdims=True)
    lo_lane = N_GROUPS + EXPERTS_PER_GROUP * g_sel
    el = jnp.where((lane_f >= lo_lane) & (lane_f < lo_lane + EXPERTS_PER_GROUP), lg, NEG_BIG)
    m1 = jnp.max(el, axis=-1, keepdims=True)
    i1 = jnp.min(jnp.where(el == m1, lane_f, float(LANES)), axis=-1, keepdims=True)
    el2 = jnp.where(lane_f == i1, NEG_BIG, el)
    m2 = jnp.max(el2, axis=-1, keepdims=True)
    i2 = jnp.min(jnp.where(el2 == m2, lane_f, float(LANES)), axis=-1, keepdims=True)
    ex = jnp.exp(m2 - m1)
    p1 = 1.0 / (1.0 + ex)
    e1 = i1 - N_GROUPS
    e2 = i2 - N_GROUPS

    @pl.when(pl.program_id(0) == 0)
    def _():
        base_ref[...] = jnp.zeros_like(base_ref)

    tm = lg.shape[0]
    hot1 = lane_f == e1
    hot2 = lane_f == e2
    onehot = jnp.where(hot1 | hot2, 1.0, 0.0)
    tr = lax.broadcasted_iota(jnp.int32, (tm, tm), 0)
    tc = lax.broadcasted_iota(jnp.int32, (tm, tm), 1)
    before = _dot(jnp.where(tc < tr, 1.0, 0.0).astype(BF16), onehot.astype(BF16)) + base_ref[...]
    r1 = jnp.sum(jnp.where(hot1, before, 0.0), axis=-1, keepdims=True)
    r2 = jnp.sum(jnp.where(hot2, before, 0.0), axis=-1, keepdims=True)
    base_ref[...] += jnp.sum(onehot, axis=0, keepdims=True)
    cnt_ref[...] = jnp.broadcast_to(base_ref[...], cnt_ref.shape)

    vals = (e1, e2, p1 * g_gate, ex * p1 * g_gate, r1, r2)
    out = jnp.zeros_like(lg)
    for j, v in enumerate(vals):
        out = jnp.where(lane == j, v, out)
    rt_ref[...] = out


def _xattn(h1, kv, nxa, wq, wo, nmoe, wr, seq, tm):
    t, d = h1.shape
    tm = min(tm, seq)
    per_b = seq // tm
    m_len = kv.shape[0] // (t // seq)
    nr = wr.shape[1] // 2
    return pl.pallas_call(
        _xattn_kernel,
        grid=(t // tm,),
        in_specs=[
            pl.BlockSpec((tm, d), lambda i: (i, 0)),
            pl.BlockSpec((m_len, kv.shape[1]), lambda i: (i // per_b, 0)),
            pl.BlockSpec((1, d), lambda i: (0, 0)),
            pl.BlockSpec(wq.shape, lambda i: (0, 0)),
            pl.BlockSpec(wo.shape, lambda i: (0, 0)),
            pl.BlockSpec((1, d), lambda i: (0, 0)),
            pl.BlockSpec((d, 2 * nr), lambda i: (0, 0)),
        ],
        out_specs=[
            pl.BlockSpec((tm, d), lambda i: (i, 0)),
            pl.BlockSpec((tm, d // 2), lambda i: (i, 0)),
            pl.BlockSpec((tm, nr), lambda i: (i, 0)),
            pl.BlockSpec((SUBLANES, nr), lambda i: (0, 0)),
        ],
        out_shape=[jax.ShapeDtypeStruct((t, d), F32), jax.ShapeDtypeStruct((t, d // 2), U32),
                   jax.ShapeDtypeStruct((t, nr), F32), jax.ShapeDtypeStruct((SUBLANES, nr), F32)],
        scratch_shapes=[pltpu.VMEM((1, nr), F32)],
        compiler_params=_cparams(("arbitrary",)),
        name="xattn",
    )(h1, kv, nxa, wq, wo, nmoe, wr)


def _dispatch_kernel(slot_ref, padrow_ref, padn_ref, nv_ref, um_ref, xs_hbm, sa_ref, sb_ref, z_ref, sem, zsem):
    i = pl.program_id(0)
    tm = um_ref.shape[0]
    last = pl.num_programs(0) - 1
    stage = (sa_ref, sb_ref)

    def zero_rows(start):
        for e in range(N_EXPERTS):
            def body(j, c, e=e):
                cp = pltpu.make_async_copy(z_ref.at[pl.ds(0, 1)], xs_hbm.at[pl.ds(padrow_ref[e] + j, 1)], zsem)
                if start:
                    cp.start()
                else:
                    cp.wait()
                return c

            lax.fori_loop(0, padn_ref[e], body, 0)

        def tail(b, c):
            cp = pltpu.make_async_copy(
                z_ref, xs_hbm.at[pl.ds(pl.multiple_of(b * MOE_BLOCK, MOE_BLOCK), MOE_BLOCK)], zsem)
            if start:
                cp.start()
            else:
                cp.wait()
            return c

        lax.fori_loop(nv_ref[0], xs_hbm.shape[0] // MOE_BLOCK, tail, 0)

    @pl.when(i == 0)
    def _():
        z_ref[...] = jnp.zeros_like(z_ref)
        zero_rows(True)

    def wait(par):
        for _ in range(TOP_K):
            pltpu.make_async_copy(stage[par], xs_hbm.at[pl.ds(0, tm)], sem.at[par]).wait()

    def step(par):
        pl.when(i >= 2)(functools.partial(wait, par))
        stage[par][...] = um_ref[...]
        tokens = tm * pl.num_programs(0)
        for r in range(tm):
            for k in range(TOP_K):
                pltpu.make_async_copy(stage[par].at[pl.ds(r, 1)],
                                      xs_hbm.at[pl.ds(slot_ref[k * tokens + i * tm + r], 1)],
                                      sem.at[par]).start(priority=k)

        @pl.when(i == last)
        def _():
            wait(par)
            pl.when(i >= 1)(functools.partial(wait, 1 - par))

    for par in range(2):
        pl.when(i % 2 == par)(functools.partial(step, par))

    @pl.when(i == last)
    def _():
        zero_rows(False)


def _dispatch(slot, pad_row, pad_n, n_valid, um, n_slots, tm):
    t, d = um.shape
    tm = min(tm, t)
    grid_spec = pltpu.PrefetchScalarGridSpec(
        num_scalar_prefetch=4,
        grid=(t // tm,),
        in_specs=[pl.BlockSpec((tm, d), lambda i, s, pr, pn, nv: (i, 0))],
        out_specs=pl.BlockSpec(memory_space=pl.ANY),
        scratch_shapes=[pltpu.VMEM((tm, d), um.dtype), pltpu.VMEM((tm, d), um.dtype),
                        pltpu.VMEM((MOE_BLOCK, d), um.dtype),
                        pltpu.SemaphoreType.DMA((2,)), pltpu.SemaphoreType.DMA(())],
    )
    return pl.pallas_call(
        _dispatch_kernel,
        grid_spec=grid_spec,
        out_shape=jax.ShapeDtypeStruct((n_slots, d), um.dtype),
        compiler_params=_cparams(("arbitrary",)),
        name="dispatch",
    )(slot, pad_row, pad_n, n_valid, um)


def _expert_kernel(be_ref, nv_ref, x_ref, wg_ref, wu_ref, wd_ref, o_ref, wgb_ref, wub_ref, wdb_ref):
    i = pl.program_id(0)

    @pl.when(jnp.logical_or(i == 0, be_ref[i] != be_ref[jnp.maximum(i - 1, 0)]))
    def _():
        wgb_ref[...] = wg_ref[...].astype(BF16)
        wub_ref[...] = wu_ref[...].astype(BF16)
        wdb_ref[...] = wd_ref[...].astype(BF16)

    @pl.when(i < nv_ref[0])
    def _():
        lo, hi = _unpack_halves(x_ref[...])
        xb = jnp.concatenate([lo.astype(BF16), hi.astype(BF16)], axis=1)
        hid = (_silu(_dot(xb, wgb_ref[...])) * _dot(xb, wub_ref[...])).astype(BF16)
        o_ref[...] = _pack_halves(_dot(hid, wdb_ref[...]))

    @pl.when(i >= nv_ref[0])
    def _():
        o_ref[...] = jnp.zeros_like(o_ref)


def _experts(block_e, n_valid, xs, wg, wu, wd):
    n_slots, dp = xs.shape
    d, ff = wg.shape[1:]

    def x_map(i, be, nv):
        return (jnp.minimum(i, nv[0] - 1), 0)

    grid_spec = pltpu.PrefetchScalarGridSpec(
        num_scalar_prefetch=2,
        grid=(n_slots // MOE_BLOCK,),
        in_specs=[
            pl.BlockSpec((MOE_BLOCK, dp), x_map),
            pl.BlockSpec((None, d, ff), lambda i, be, nv: (be[i], 0, 0)),
            pl.BlockSpec((None, d, ff), lambda i, be, nv: (be[i], 0, 0)),
            pl.BlockSpec((None, ff, d), lambda i, be, nv: (be[i], 0, 0)),
        ],
        out_specs=pl.BlockSpec((MOE_BLOCK, dp), lambda i, be, nv: (i, 0)),
        scratch_shapes=[pltpu.VMEM((d, ff), BF16), pltpu.VMEM((d, ff), BF16), pltpu.VMEM((ff, d), BF16)],
    )
    return pl.pallas_call(
        _expert_kernel,
        grid_spec=grid_spec,
        out_shape=jax.ShapeDtypeStruct((n_slots, dp), U32),
        compiler_params=_cparams(("arbitrary",)),
        name="experts",
    )(block_e, n_valid, xs, wg, wu, wd)


def _combine_kernel(slot_ref, h_ref, w_ref, nw_ref, yb_hbm, o_ref, ya0_ref, ya1_ref, yb0_ref, yb1_ref, sem):
    i = pl.program_id(0)
    tm = h_ref.shape[0]
    last = pl.num_programs(0) - 1
    bufs = ((ya0_ref, ya1_ref), (yb0_ref, yb1_ref))

    def gather(blk, par):
        tokens = tm * pl.num_programs(0)
        for r in range(tm):
            for k in range(TOP_K):
                pltpu.make_async_copy(yb_hbm.at[pl.ds(slot_ref[k * tokens + blk * tm + r], 1)],
                                      bufs[par][k].at[pl.ds(r, 1)], sem.at[par]).start(priority=k)

    def wait(par):
        for k in range(TOP_K):
            pltpu.make_async_copy(yb_hbm.at[pl.ds(0, tm)], bufs[par][k], sem.at[par]).wait()

    @pl.when(i == 0)
    def _():
        gather(i, 0)

    def step(par):
        wait(par)
        gather(jnp.minimum(i + 1, last), 1 - par)
        w0 = w_ref[:, TOP_K:TOP_K + 1]
        w1 = w_ref[:, TOP_K + 1:TOP_K + 2]
        half = h_ref.shape[1] // 2
        ya = _unpack_halves(bufs[par][0][...])
        yb = _unpack_halves(bufs[par][1][...])
        h3 = [h_ref[:, p * half:(p + 1) * half] + w0 * ya[p] + w1 * yb[p] for p in range(2)]
        var = sum(jnp.sum(v * v, axis=-1, keepdims=True) for v in h3) / h_ref.shape[1]
        inv = lax.rsqrt(var + EPS)
        for p in range(2):
            o_ref[:, p * half:(p + 1) * half] = h3[p] * inv * nw_ref[:, p * half:(p + 1) * half]

        @pl.when(i == last)
        def _():
            wait(1 - par)

    for par in range(2):
        pl.when(i % 2 == par)(functools.partial(step, par))


def _combine(slot, h2, route, nw, yb, tm):
    t, d = h2.shape
    tm = min(tm, t)
    grid_spec = pltpu.PrefetchScalarGridSpec(
        num_scalar_prefetch=1,
        grid=(t // tm,),
        in_specs=[
            pl.BlockSpec((tm, d), lambda i, s: (i, 0)),
            pl.BlockSpec((tm, route.shape[1]), lambda i, s: (i, 0)),
            pl.BlockSpec((1, d), lambda i, s: (0, 0)),
            pl.BlockSpec(memory_space=pl.ANY),
        ],
        out_specs=pl.BlockSpec((tm, d), lambda i, s: (i, 0)),
        scratch_shapes=[pltpu.VMEM((tm, yb.shape[1]), yb.dtype)] * (2 * TOP_K) + [pltpu.SemaphoreType.DMA((2,))],
    )
    return pl.pallas_call(
        _combine_kernel,
        grid_spec=grid_spec,
        out_shape=jax.ShapeDtypeStruct((t, d), F32),
        compiler_params=_cparams(("arbitrary",)),
        name="combine",
    )(slot, h2, route, nw, yb)


def _slots(route, cnt, t):
    rt = route[:, :SUBLANES].T.astype(jnp.int32)
    e_kt = rt[:TOP_K]
    rank = rt[2 * TOP_K:3 * TOP_K]
    onehot = e_kt[None] == jnp.arange(N_EXPERTS, dtype=jnp.int32)[:, None, None]
    counts = cnt[0, :N_EXPERTS].astype(jnp.int32)
    padded = (counts + MOE_BLOCK - 1) // MOE_BLOCK * MOE_BLOCK
    pad_end = jnp.cumsum(padded)
    pad_start = pad_end - padded
    slot = (jnp.sum(jnp.where(onehot, pad_start[:, None, None], 0), axis=0) + rank).reshape(-1)
    n_slots = t * TOP_K + N_EXPERTS * MOE_BLOCK
    n_blocks = n_slots // MOE_BLOCK
    block_start = jnp.arange(n_blocks, dtype=jnp.int32) * MOE_BLOCK
    block_e = jnp.minimum(jnp.sum((pad_end[None, :] <= block_start[:, None]).astype(jnp.int32), axis=1),
                          N_EXPERTS - 1)
    n_valid = (pad_end[-1:] // MOE_BLOCK).astype(jnp.int32)
    return slot, pad_start + counts, padded - counts, block_e, n_valid, n_slots


def _pad_lanes(v, n=LANES):
    return jnp.pad(v, ((0, 0), (0, n - v.shape[1])))


def kernel(x, mem, norm_mix_w, w_in, conv_w, conv_b, dt_bias, a_log, d_skip, ssd_norm_w, w_ssd_branch, w_sb_branch, w_mix_out, norm_xa_w, norm_mem_w, w_xq, w_xk, w_xv, w_xo, norm_moe_w, w_router_group, w_router_expert, w_expert_gate, w_expert_up, w_expert_down, norm_final_w):
    bsz, seq, d = x.shape
    depth = w_in.shape[0]
    t = bsz * seq
    inner = ssd_norm_w.shape[1]
    heads = dt_bias.shape[1]
    bc = SSD_GROUPS * SSD_STATE
    conv_dim = inner + 2 * bc
    sb_width = SB_HEADS * SB_HEAD_DIM
    col_dt = inner + conv_dim
    col_qkv = col_dt + heads
    col_g = col_qkv + 3 * sb_width
    g_col0 = 2 * inner
    bc_col0 = g_col0 + 2 * d
    q_col0 = bc_col0 + 2 * bc
    n_main = q_col0 + 3 * sb_width
    tn_proj = 512
    n_proj = -(-n_main // tn_proj) * tn_proj
    tn_merge = 1024

    assert depth == 1, "single-layer configuration: the final RMSNorm is fused into the last MoE combine"
    h = x.reshape(t, d)
    for l in range(depth):
        w_l = w_in[l]
        w_main = jnp.concatenate([w_l[:, :2 * inner], w_l[:, col_g:], w_l[:, 2 * inner:col_dt],
                                  w_l[:, col_qkv:col_g], jnp.zeros((d, n_proj - n_main), F32)],
                                 axis=1).astype(BF16)
        w_dt = _pad_lanes(w_l[:, col_dt:col_dt + heads]).astype(BF16)
        e_mat = (jnp.arange(inner, dtype=jnp.int32)[None, :] // SSD_HEAD_DIM
                 == jnp.arange(LANES, dtype=jnp.int32)[:, None]).astype(BF16)
        dskip_x = jnp.repeat(d_skip[l], SSD_HEAD_DIM)[None, :]

        dt_raw, proj = _proj_stream(h, norm_mix_w[l][None, :], w_main, w_dt, tm=1024, tn=tn_proj)
        y_ssd = _ssd(proj, dt_raw, conv_w[l], conv_b[l][None, :], _pad_lanes(dt_bias[l][None, :]),
                     _pad_lanes(a_log[l][None, :]), dskip_x, ssd_norm_w[l][None, :], e_mat,
                     bsz, seq, inner, bc, bc_col0 // bc)
        y_sb = _stickbreak(proj.reshape(bsz, seq, n_proj), bsz, seq, q_col0 // SB_HEAD_DIM)
        h1 = _merge(h, y_ssd, y_sb.reshape(t, sb_width), proj,
                    w_ssd_branch[l].astype(BF16), w_sb_branch[l].astype(BF16), w_mix_out[l].astype(BF16),
                    g_col0 // tn_merge, tm=512, tn=tn_merge)

        m_len = mem.shape[1]
        w_kv = jnp.concatenate([w_xk[l], w_xv[l]], axis=1).astype(BF16)
        kv, _ = _norm_matmul(mem.reshape(bsz * m_len, d), norm_mem_w[l][None, :], w_kv,
                             jnp.zeros((d, LANES), BF16), tm=bsz * m_len, tn=512)
        w_r = _pad_lanes(jnp.concatenate([w_router_group[l], w_router_expert[l]], axis=1))
        wr_hi = w_r.astype(BF16)
        wr_lo = (w_r - wr_hi.astype(F32)).astype(BF16)
        h2, um, route, cnt = _xattn(h1, kv, norm_xa_w[l][None, :], w_xq[l].astype(BF16), w_xo[l].astype(BF16),
                               norm_moe_w[l][None, :], jnp.concatenate([wr_hi, wr_lo], axis=1), seq, tm=512)

        slot, pad_row, pad_n, block_e, n_valid, n_slots = _slots(route, cnt, t)
        xs = _dispatch(slot, pad_row, pad_n, n_valid, um, n_slots, tm=512)
        yb = _experts(block_e, n_valid, xs, w_expert_gate[l], w_expert_up[l], w_expert_down[l])
        h = _combine(slot, h2, route, norm_final_w[None, :], yb, tm=512)
    return h.reshape(bsz, seq, d)
```

```python
import functools

import jax
import jax.numpy as jnp
from jax import lax
from jax.experimental import pallas as pl
from jax.experimental.pallas import tpu as pltpu

F32 = jnp.float32
BF16 = jnp.bfloat16
U32 = jnp.uint32
EPS = 1e-6

SSD_HEAD_DIM = 64
SSD_GROUPS = 4
SSD_STATE = 128
SSD_CONV = 4
SSD_CHUNK = 128
SB_HEADS = 4
SB_HEAD_DIM = 128
SB_BLOCK = 128
SB_SUB = 8
XA_HEADS = 4
XA_HEAD_DIM = 128
N_GROUPS = 4
EXPERTS_PER_GROUP = 8
N_EXPERTS = N_GROUPS * EXPERTS_PER_GROUP
TOP_K = 2
MOE_BLOCK = 256

LANES = 128
SUBLANES = 8
VMEM_LIMIT = 56 * 1024 * 1024

SB_EXIT_LOG = -88.0
NEG_BIG = -1e30


def _cparams(sem):
    return pltpu.CompilerParams(dimension_semantics=sem, vmem_limit_bytes=VMEM_LIMIT)


def _rms(x, w):
    var = jnp.mean(x * x, axis=-1, keepdims=True)
    return x * lax.rsqrt(var + EPS) * w


def _split2(v):
    hi = v.astype(BF16)
    lo = (v - hi.astype(F32)).astype(BF16)
    return hi, lo


def _dot(a, b):
    return jnp.dot(a, b, preferred_element_type=F32)


def _pack_halves(v):
    n = v.shape[1] // 2
    lo = lax.bitcast_convert_type(v[:, :n].astype(BF16).astype(F32), U32)
    hi = lax.bitcast_convert_type(v[:, n:].astype(BF16).astype(F32), U32)
    return (lo >> 16) | hi


def _unpack_halves(w):
    lo = lax.bitcast_convert_type(w << 16, F32)
    hi = lax.bitcast_convert_type(w & jnp.uint32(0xFFFF0000), F32)
    return lo, hi


def _silu(x):
    return x / (1.0 + jnp.exp(-x))


def _softplus(x):
    return jnp.maximum(x, 0.0) + jnp.log(1.0 + jnp.exp(-jnp.abs(x)))


def _norm_matmul_kernel(x_ref, nw_ref, w_ref, ws_ref, o_ref, os_ref, u_ref):
    @pl.when(pl.program_id(1) == 0)
    def _():
        u = _rms(x_ref[...], nw_ref[...]).astype(BF16)
        u_ref[...] = u
        os_ref[...] = _dot(u, ws_ref[...])

    o_ref[...] = _dot(u_ref[...], w_ref[...]).astype(o_ref.dtype)


def _norm_matmul(x, nw, w, w_side, tm, tn):
    m, k = x.shape
    n = w.shape[1]
    ns = w_side.shape[1]
    tm = min(tm, m)
    return pl.pallas_call(
        _norm_matmul_kernel,
        grid=(m // tm, n // tn),
        in_specs=[
            pl.BlockSpec((tm, k), lambda i, j: (i, 0)),
            pl.BlockSpec((1, k), lambda i, j: (0, 0)),
            pl.BlockSpec((k, tn), lambda i, j: (0, j)),
            pl.BlockSpec((k, ns), lambda i, j: (0, 0)),
        ],
        out_specs=[
            pl.BlockSpec((tm, tn), lambda i, j: (i, j)),
            pl.BlockSpec((tm, ns), lambda i, j: (i, 0)),
        ],
        out_shape=[jax.ShapeDtypeStruct((m, n), BF16), jax.ShapeDtypeStruct((m, ns), F32)],
        scratch_shapes=[pltpu.VMEM((tm, k), BF16)],
        compiler_params=_cparams(("parallel", "arbitrary")),
        name="norm_matmul",
    )(x, nw, w, w_side)


def _ssd_kernel(z_ref, xs_ref, b_ref, c_ref, dtr_ref, cw_ref, cb_ref, dtb_ref, alog_ref, dsk_ref,
                nw_ref, e_ref, o_ref,
                tail_ref, win_ref, state_ref, xact_ref, bact_ref, cact_ref, dtx_ref, eax_ref, wsx_ref, y_ref):
    L = SSD_CHUNK
    inner = xs_ref.shape[1]
    bc = b_ref.shape[1]
    n_pairs = inner // LANES

    @pl.when(pl.program_id(1) == 0)
    def _():
        tail_ref[...] = jnp.zeros_like(tail_ref)
        state_ref[...] = jnp.zeros_like(state_ref)


    def conv_seg(src_ref, col0, width, dst_ref):
        for j in range(0, width, 2 * LANES):
            cols = slice(col0 + j, col0 + j + 2 * LANES)
            xin = src_ref[:, j:j + 2 * LANES].astype(F32)
            win_ref[:SUBLANES, :] = tail_ref[:, cols]
            win_ref[SUBLANES:, :] = xin
            acc = xin * cw_ref[SSD_CONV - 1:SSD_CONV, cols] + cb_ref[:, cols]
            for s in range(1, SSD_CONV):
                acc = acc + win_ref[SUBLANES - s:SUBLANES - s + L, :] * cw_ref[SSD_CONV - 1 - s:SSD_CONV - s, cols]
            tail_ref[:, cols] = xin[L - SUBLANES:]
            dst_ref[:, j:j + 2 * LANES] = _silu(acc).astype(dst_ref.dtype)

    conv_seg(xs_ref, 0, inner, xact_ref)
    conv_seg(b_ref, inner, bc, bact_ref)
    conv_seg(c_ref, inner + bc, bc, cact_ref)

    dt = _softplus(dtr_ref[...] + dtb_ref[...])
    da = dt * (-jnp.exp(alog_ref[...]))
    rowl = lax.broadcasted_iota(jnp.int32, (L, L), 0)
    coll = lax.broadcasted_iota(jnp.int32, (L, L), 1)
    lower = rowl >= coll
    tri = jnp.where(lower, 1.0, 0.0).astype(BF16)
    d1 = da.astype(BF16)
    r1 = da - d1.astype(F32)
    d2 = r1.astype(BF16)
    d3 = (r1 - d2.astype(F32)).astype(BF16)
    a_cum = _dot(tri, d1) + _dot(tri, d2) + _dot(tri, d3)
    a_cum_t = a_cum.T
    a_last = a_cum[L - 1:L, :]
    e_mat = e_ref[...]

    def expand(v):
        hi, lo = _split2(v)
        return _dot(hi, e_mat) + _dot(lo, e_mat)

    dtx_ref[...] = expand(dt)
    eax_ref[...] = expand(jnp.exp(a_cum))
    wsx_ref[...] = expand(jnp.exp(a_last - a_cum))
    elx = expand(jnp.broadcast_to(jnp.exp(a_last), (SUBLANES, LANES)))[0:1]

    lane = lax.broadcasted_iota(jnp.int32, (L, LANES), 1)
    pairs_per_group = n_pairs // SSD_GROUPS
    for g in range(SSD_GROUPS):
        gcols = slice(g * SSD_STATE, (g + 1) * SSD_STATE)
        bg_t = bact_ref[:, gcols].T.astype(BF16)
        cg = cact_ref[:, gcols]
        cb = _dot(cg, bg_t)
        for pp in range(pairs_per_group):
            p = g * pairs_per_group + pp
            pc = slice(p * LANES, (p + 1) * LANES)
            ms = []
            for hh in (2 * p, 2 * p + 1):
                seg = a_cum[:, hh:hh + 1] - a_cum_t[hh:hh + 1, :]
                dec = jnp.exp(jnp.where(lower, seg, NEG_BIG))
                ms.append((cb * dec).astype(BF16))
            lhs = jnp.concatenate(ms, axis=1)
            xs_p = xact_ref[:, pc]
            xdt = xs_p * dtx_ref[:, pc]
            rhs = jnp.concatenate([jnp.where(lane < SSD_HEAD_DIM, xdt, 0.0).astype(BF16),
                                   jnp.where(lane >= SSD_HEAD_DIM, xdt, 0.0).astype(BF16)], axis=0)
            st = state_ref[p]
            y = _dot(lhs, rhs)
            y = y + _dot(cg, st.astype(BF16)) * eax_ref[:, pc]
            y = y + xs_p * dsk_ref[:, pc]
            y_ref[:, pc] = y
            xw = (xdt * wsx_ref[:, pc]).astype(BF16)
            state_ref[p] = st * elx[:, pc] + _dot(bg_t, xw)

    yg = y_ref[...] * _silu(z_ref[...].astype(F32))
    o_ref[...] = _rms(yg, nw_ref[...]).astype(o_ref.dtype)


def _ssd(proj, dt_raw, conv_w, conv_b, dt_bias_p, a_log_p, dskip_x, norm_w, e_mat, bsz, seq, inner, bc, b_blk):
    L = SSD_CHUNK
    nc = seq // L
    conv_dim = inner + 2 * bc
    n_pairs = inner // LANES

    def rows(b, c):
        return b * nc + c

    return pl.pallas_call(
        _ssd_kernel,
        grid=(bsz, nc),
        in_specs=[
            pl.BlockSpec((L, inner), lambda b, c: (rows(b, c), 0)),
            pl.BlockSpec((L, inner), lambda b, c: (rows(b, c), 1)),
            pl.BlockSpec((L, bc), lambda b, c: (rows(b, c), b_blk)),
            pl.BlockSpec((L, bc), lambda b, c: (rows(b, c), b_blk + 1)),
            pl.BlockSpec((L, LANES), lambda b, c: (rows(b, c), 0)),
            pl.BlockSpec((SSD_CONV, conv_dim), lambda b, c: (0, 0)),
            pl.BlockSpec((1, conv_dim), lambda b, c: (0, 0)),
            pl.BlockSpec((1, LANES), lambda b, c: (0, 0)),
            pl.BlockSpec((1, LANES), lambda b, c: (0, 0)),
            pl.BlockSpec((1, inner), lambda b, c: (0, 0)),
            pl.BlockSpec((1, inner), lambda b, c: (0, 0)),
            pl.BlockSpec((LANES, inner), lambda b, c: (0, 0)),
        ],
        out_specs=pl.BlockSpec((L, inner), lambda b, c: (rows(b, c), 0)),
        out_shape=jax.ShapeDtypeStruct((bsz * seq, inner), BF16),
        scratch_shapes=[
            pltpu.VMEM((SUBLANES, conv_dim), F32),
            pltpu.VMEM((SUBLANES + L, 2 * LANES), F32),
            pltpu.VMEM((n_pairs, SSD_STATE, LANES), F32),
            pltpu.VMEM((L, inner), F32),
            pltpu.VMEM((L, bc), F32),
            pltpu.VMEM((L, bc), BF16),
            pltpu.VMEM((L, inner), F32),
            pltpu.VMEM((L, inner), F32),
            pltpu.VMEM((L, inner), F32),
            pltpu.VMEM((L, inner), F32),
        ],
        compiler_params=_cparams(("parallel", "arbitrary")),
        name="ssd",
    )(proj, proj, proj, proj, dt_raw, conv_w, conv_b, dt_bias_p, a_log_p, dskip_x, norm_w, e_mat)


def _sb_kernel(q_ref, k_ref, v_ref, o_ref, acc_ref, carry_ref):
    BL = SB_BLOCK
    i = pl.program_id(2)
    scale = SB_HEAD_DIM ** -0.5
    row = lax.broadcasted_iota(jnp.int32, (BL, BL), 0)
    col = lax.broadcasted_iota(jnp.int32, (BL, BL), 1)
    causal = col < row
    r2 = lax.broadcasted_iota(jnp.int32, (BL, 2 * BL), 0)
    c2 = lax.broadcasted_iota(jnp.int32, (BL, 2 * BL), 1)
    tri = jnp.where((r2 > c2) | (c2 >= BL), 1.0, 0.0).astype(BF16)

    def sweep(s, first):
        subs = range(SB_SUB)
        kbs = [i * SB_SUB + j - s for j in subs]
        starts = [pl.multiple_of(jnp.maximum(kb, 0) * BL, BL) for kb in kbs]
        zls = [lax.dot_general(q_ref[j * BL:(j + 1) * BL, :], k_ref[pl.ds(starts[j], BL), :],
                               (((1,), (1,)), ((), ())), preferred_element_type=F32) * scale for j in subs]
        sps = [_softplus(zl) for zl in zls]
        log_betas = [zl - sp for zl, sp in zip(zls, sps)]
        log_keeps = [jnp.where(causal, -sp, 0.0) if first else -sp for sp in sps]
        splits = [_split2(lk) for lk in log_keeps]
        t2s = [_dot(hi, tri) + _dot(lo, tri) for hi, lo in splits]
        if first:
            atts = [jnp.where(causal, jnp.exp(lb + t2[:, :BL]), 0.0) for lb, t2 in zip(log_betas, t2s)]
            carries = [t2[:, BL:] for t2 in t2s]
        else:
            olds = [carry_ref[j] for j in subs]
            atts = [jnp.where(kb >= 0, jnp.exp(lb + t2[:, :BL] + old), 0.0)
                    for kb, lb, t2, old in zip(kbs, log_betas, t2s, olds)]
            carries = [old + jnp.where(kb >= 0, t2[:, BL:], 0.0) for kb, t2, old in zip(kbs, t2s, olds)]
        pvs = [_dot(atts[j].astype(BF16), v_ref[pl.ds(starts[j], BL), :]) for j in subs]
        worst = None
        for j in subs:
            if first:
                acc_ref[j] = pvs[j]
            else:
                acc_ref[j] += pvs[j]
            carry_ref[j] = carries[j]
            live = jnp.where(kbs[j] >= 1, carries[j], NEG_BIG)
            worst = live if worst is None else jnp.maximum(worst, live)
        return jnp.max(worst) > SB_EXIT_LOG

    sweep(0, True)
    go = sweep(1, False)

    def body(st):
        s, _ = st
        return s + 1, sweep(s, False)

    lax.while_loop(lambda st: st[1], body, (jnp.int32(2), go))
    for j in range(SB_SUB):
        o_ref[j * BL:(j + 1) * BL, :] = acc_ref[j].astype(o_ref.dtype)


def _stickbreak(proj3, bsz, seq, q_blk0):
    tq = SB_SUB * SB_BLOCK
    return pl.pallas_call(
        _sb_kernel,
        grid=(bsz, SB_HEADS, seq // tq),
        in_specs=[
            pl.BlockSpec((None, tq, SB_HEAD_DIM), lambda b, h, i: (b, i, q_blk0 + h)),
            pl.BlockSpec((None, seq, SB_HEAD_DIM), lambda b, h, i: (b, 0, q_blk0 + SB_HEADS + h)),
            pl.BlockSpec((None, seq, SB_HEAD_DIM), lambda b, h, i: (b, 0, q_blk0 + 2 * SB_HEADS + h)),
        ],
        out_specs=pl.BlockSpec((None, tq, SB_HEAD_DIM), lambda b, h, i: (b, i, h)),
        out_shape=jax.ShapeDtypeStruct((bsz, seq, SB_HEADS * SB_HEAD_DIM), BF16),
        scratch_shapes=[pltpu.VMEM((SB_SUB, SB_BLOCK, SB_HEAD_DIM), F32),
                        pltpu.VMEM((SB_SUB, SB_BLOCK, SB_BLOCK), F32)],
        compiler_params=_cparams(("parallel", "parallel", "arbitrary")),
        name="stickbreak",
    )(proj3, proj3, proj3)


def _merge_kernel(x_ref, ys_ref, yb_ref, *rest):
    *g_refs, w1_ref, w2_ref, wm_ref, o_ref = rest
    nj = len(g_refs) // 2
    tn = g_refs[0].shape[1]
    ys = ys_ref[...]
    yb = yb_ref[...]
    ms = []
    for j in range(nj):
        cs = slice(j * tn, (j + 1) * tn)
        s1 = jax.nn.sigmoid(g_refs[j][...].astype(F32))
        s2 = jax.nn.sigmoid(g_refs[nj + j][...].astype(F32))
        ms.append((s1 * _dot(ys, w1_ref[:, cs]) + s2 * _dot(yb, w2_ref[:, cs])).astype(BF16))
    o_ref[...] = x_ref[...] + _dot(jnp.concatenate(ms, axis=1), wm_ref[...])


def _merge(x2, y_ssd, y_sb, proj, w1, w2, wm, g_blk0, tm, tn):
    t, d = x2.shape
    tm = min(tm, t)
    nj = d // tn
    resident = pl.Buffered(1)
    g_specs = [pl.BlockSpec((tm, tn), lambda i, b=g_blk0 + b: (i, b)) for b in range(2 * nj)]
    return pl.pallas_call(
        _merge_kernel,
        grid=(t // tm,),
        in_specs=[
            pl.BlockSpec((tm, d), lambda i: (i, 0)),
            pl.BlockSpec((tm, y_ssd.shape[1]), lambda i: (i, 0)),
            pl.BlockSpec((tm, y_sb.shape[1]), lambda i: (i, 0)),
            *g_specs,
            pl.BlockSpec(w1.shape, lambda i: (0, 0), pipeline_mode=resident),
            pl.BlockSpec(w2.shape, lambda i: (0, 0), pipeline_mode=resident),
            pl.BlockSpec(wm.shape, lambda i: (0, 0), pipeline_mode=resident),
        ],
        out_specs=pl.BlockSpec((tm, d), lambda i: (i, 0)),
        out_shape=jax.ShapeDtypeStruct((t, d), F32),
        compiler_params=_cparams(("parallel",)),
        name="merge",
    )(x2, y_ssd, y_sb, *([proj] * (2 * nj)), w1, w2, wm)


def _xattn_kernel(h_ref, kv_ref, nxa_ref, wq_ref, wo_ref, nmoe_ref, wr_ref, h2_ref, um_ref, rt_ref, cnt_ref,
                  base_ref):
    h1 = h_ref[...]
    un = _rms(h1, nxa_ref[...]).astype(BF16)
    q = _dot(un, wq_ref[...]).astype(BF16)
    width = XA_HEADS * XA_HEAD_DIM
    outs = []
    for hd in range(XA_HEADS):
        cs = slice(hd * XA_HEAD_DIM, (hd + 1) * XA_HEAD_DIM)
        k = kv_ref[:, cs]
        v = kv_ref[:, width + hd * XA_HEAD_DIM: width + (hd + 1) * XA_HEAD_DIM]
        sc = lax.dot_general(q[:, cs], k, (((1,), (1,)), ((), ())),
                             preferred_element_type=F32) * (XA_HEAD_DIM ** -0.5)
        sc = sc - jnp.max(sc, axis=-1, keepdims=True)
        p = jnp.exp(sc)
        p = p / jnp.sum(p, axis=-1, keepdims=True)
        outs.append(_dot(p.astype(BF16), v).astype(BF16))
    o = jnp.concatenate(outs, axis=1)
    h2 = h1 + _dot(o, wo_ref[...])
    h2_ref[...] = h2
    um = _rms(h2, nmoe_ref[...])
    um_ref[...] = _pack_halves(um)
    hi, lo = _split2(um)
    nr = rt_ref.shape[1]
    both = _dot(hi, wr_ref[...])
    lg = both[:, :nr] + both[:, nr:] + _dot(lo, wr_ref[:, :nr])

    lane = lax.broadcasted_iota(jnp.int32, lg.shape, 1)
    lane_f = lane.astype(F32)
    is_g = lane < N_GROUPS
    gl = jnp.where(is_g, lg, NEG_BIG)
    gmax = jnp.max(gl, axis=-1, keepdims=True)
    g_sel = jnp.min(jnp.where(gl == gmax, lane_f, float(LANES)), axis=-1, keepdims=True)
    g_gate = 1.0 / jnp.sum(jnp.where(is_g, jnp.exp(gl - gmax), 0.0), axis=-1, keepdims=True)
    lo_lane = N_GROUPS + EXPERTS_PER_GROUP * g_sel
    el = jnp.where((lane_f >= lo_lane) & (lane_f < lo_lane + EXPERTS_PER_GROUP), lg, NEG_BIG)
    m1 = jnp.max(el, axis=-1, keepdims=True)
    i1 = jnp.min(jnp.where(el == m1, lane_f, float(LANES)), axis=-1, keepdims=True)
    el2 = jnp.where(lane_f == i1, NEG_BIG, el)
    m2 = jnp.max(el2, axis=-1, keepdims=True)
    i2 = jnp.min(jnp.where(el2 == m2, lane_f, float(LANES)), axis=-1, keepdims=True)
    ex = jnp.exp(m2 - m1)
    p1 = 1.0 / (1.0 + ex)
    e1 = i1 - N_GROUPS
    e2 = i2 - N_GROUPS

    @pl.when(pl.program_id(0) == 0)
    def _():
        base_ref[...] = jnp.zeros_like(base_ref)

    tm = lg.shape[0]
    hot1 = lane_f == e1
    hot2 = lane_f == e2
    onehot = jnp.where(hot1 | hot2, 1.0, 0.0)
    tr = lax.broadcasted_iota(jnp.int32, (tm, tm), 0)
    tc = lax.broadcasted_iota(jnp.int32, (tm, tm), 1)
    before = _dot(jnp.where(tc < tr, 1.0, 0.0).astype(BF16), onehot.astype(BF16)) + base_ref[...]
    r1 = jnp.sum(jnp.where(hot1, before, 0.0), axis=-1, keepdims=True)
    r2 = jnp.sum(jnp.where(hot2, before, 0.0), axis=-1, keepdims=True)
    base_ref[...] += jnp.sum(onehot, axis=0, keepdims=True)
    cnt_ref[...] = jnp.broadcast_to(base_ref[...], cnt_ref.shape)

    vals = (e1, e2, p1 * g_gate, ex * p1 * g_gate, r1, r2)
    out = jnp.zeros_like(lg)
    for j, v in enumerate(vals):
        out = jnp.where(lane == j, v, out)
    rt_ref[...] = out


def _xattn(h1, kv, nxa, wq, wo, nmoe, wr, seq, tm):
    t, d = h1.shape
    tm = min(tm, seq)
    per_b = seq // tm
    m_len = kv.shape[0] // (t // seq)
    nr = wr.shape[1] // 2
    return pl.pallas_call(
        _xattn_kernel,
        grid=(t // tm,),
        in_specs=[
            pl.BlockSpec((tm, d), lambda i: (i, 0)),
            pl.BlockSpec((m_len, kv.shape[1]), lambda i: (i // per_b, 0)),
            pl.BlockSpec((1, d), lambda i: (0, 0)),
            pl.BlockSpec(wq.shape, lambda i: (0, 0)),
            pl.BlockSpec(wo.shape, lambda i: (0, 0)),
            pl.BlockSpec((1, d), lambda i: (0, 0)),
            pl.BlockSpec((d, 2 * nr), lambda i: (0, 0)),
        ],
        out_specs=[
            pl.BlockSpec((tm, d), lambda i: (i, 0)),
            pl.BlockSpec((tm, d // 2), lambda i: (i, 0)),
            pl.BlockSpec((tm, nr), lambda i: (i, 0)),
            pl.BlockSpec((SUBLANES, nr), lambda i: (0, 0)),
        ],
        out_shape=[jax.ShapeDtypeStruct((t, d), F32), jax.ShapeDtypeStruct((t, d // 2), U32),
                   jax.ShapeDtypeStruct((t, nr), F32), jax.ShapeDtypeStruct((SUBLANES, nr), F32)],
        scratch_shapes=[pltpu.VMEM((1, nr), F32)],
        compiler_params=_cparams(("arbitrary",)),
        name="xattn",
    )(h1, kv, nxa, wq, wo, nmoe, wr)


def _dispatch_kernel(slot_ref, padrow_ref, padn_ref, nv_ref, um_ref, xs_hbm, sa_ref, sb_ref, z_ref, sem, zsem):
    i = pl.program_id(0)
    tm = um_ref.shape[0]
    last = pl.num_programs(0) - 1
    stage = (sa_ref, sb_ref)

    def zero_rows(start):
        for e in range(N_EXPERTS):
            def body(j, c, e=e):
                cp = pltpu.make_async_copy(z_ref.at[pl.ds(0, 1)], xs_hbm.at[pl.ds(padrow_ref[e] + j, 1)], zsem)
                if start:
                    cp.start()
                else:
                    cp.wait()
                return c

            lax.fori_loop(0, padn_ref[e], body, 0)

        def tail(b, c):
            cp = pltpu.make_async_copy(
                z_ref, xs_hbm.at[pl.ds(pl.multiple_of(b * MOE_BLOCK, MOE_BLOCK), MOE_BLOCK)], zsem)
            if start:
                cp.start()
            else:
                cp.wait()
            return c

        lax.fori_loop(nv_ref[0], xs_hbm.shape[0] // MOE_BLOCK, tail, 0)

    @pl.when(i == 0)
    def _():
        z_ref[...] = jnp.zeros_like(z_ref)
        zero_rows(True)

    def wait(par):
        for _ in range(TOP_K):
            pltpu.make_async_copy(stage[par], xs_hbm.at[pl.ds(0, tm)], sem.at[par]).wait()

    def step(par):
        pl.when(i >= 2)(functools.partial(wait, par))
        stage[par][...] = um_ref[...]
        tokens = tm * pl.num_programs(0)
        for r in range(tm):
            for k in range(TOP_K):
                pltpu.make_async_copy(stage[par].at[pl.ds(r, 1)],
                                      xs_hbm.at[pl.ds(slot_ref[k * tokens + i * tm + r], 1)],
                                      sem.at[par]).start(priority=k)

        @pl.when(i == last)
        def _():
            wait(par)
            pl.when(i >= 1)(functools.partial(wait, 1 - par))

    for par in range(2):
        pl.when(i % 2 == par)(functools.partial(step, par))

    @pl.when(i == last)
    def _():
        zero_rows(False)


def _dispatch(slot, pad_row, pad_n, n_valid, um, n_slots, tm):
    t, d = um.shape
    tm = min(tm, t)
    grid_spec = pltpu.PrefetchScalarGridSpec(
        num_scalar_prefetch=4,
        grid=(t // tm,),
        in_specs=[pl.BlockSpec((tm, d), lambda i, s, pr, pn, nv: (i, 0))],
        out_specs=pl.BlockSpec(memory_space=pl.ANY),
        scratch_shapes=[pltpu.VMEM((tm, d), um.dtype), pltpu.VMEM((tm, d), um.dtype),
                        pltpu.VMEM((MOE_BLOCK, d), um.dtype),
                        pltpu.SemaphoreType.DMA((2,)), pltpu.SemaphoreType.DMA(())],
    )
    return pl.pallas_call(
        _dispatch_kernel,
        grid_spec=grid_spec,
        out_shape=jax.ShapeDtypeStruct((n_slots, d), um.dtype),
        compiler_params=_cparams(("arbitrary",)),
        name="dispatch",
    )(slot, pad_row, pad_n, n_valid, um)


def _expert_kernel(be_ref, nv_ref, x_ref, wg_ref, wu_ref, wd_ref, o_ref, wgb_ref, wub_ref, wdb_ref):
    i = pl.program_id(0)

    @pl.when(jnp.logical_or(i == 0, be_ref[i] != be_ref[jnp.maximum(i - 1, 0)]))
    def _():
        wgb_ref[...] = wg_ref[...].astype(BF16)
        wub_ref[...] = wu_ref[...].astype(BF16)
        wdb_ref[...] = wd_ref[...].astype(BF16)

    @pl.when(i < nv_ref[0])
    def _():
        lo, hi = _unpack_halves(x_ref[...])
        xb = jnp.concatenate([lo.astype(BF16), hi.astype(BF16)], axis=1)
        hid = (_silu(_dot(xb, wgb_ref[...])) * _dot(xb, wub_ref[...])).astype(BF16)
        o_ref[...] = _pack_halves(_dot(hid, wdb_ref[...]))

    @pl.when(i >= nv_ref[0])
    def _():
        o_ref[...] = jnp.zeros_like(o_ref)


def _experts(block_e, n_valid, xs, wg, wu, wd):
    n_slots, dp = xs.shape
    d, ff = wg.shape[1:]

    def x_map(i, be, nv):
        return (jnp.minimum(i, nv[0] - 1), 0)

    grid_spec = pltpu.PrefetchScalarGridSpec(
        num_scalar_prefetch=2,
        grid=(n_slots // MOE_BLOCK,),
        in_specs=[
            pl.BlockSpec((MOE_BLOCK, dp), x_map),
            pl.BlockSpec((None, d, ff), lambda i, be, nv: (be[i], 0, 0)),
            pl.BlockSpec((None, d, ff), lambda i, be, nv: (be[i], 0, 0)),
            pl.BlockSpec((None, ff, d), lambda i, be, nv: (be[i], 0, 0)),
        ],
        out_specs=pl.BlockSpec((MOE_BLOCK, dp), lambda i, be, nv: (i, 0)),
        scratch_shapes=[pltpu.VMEM((d, ff), BF16), pltpu.VMEM((d, ff), BF16), pltpu.VMEM((ff, d), BF16)],
    )
    return pl.pallas_call(
        _expert_kernel,
        grid_spec=grid_spec,
        out_shape=jax.ShapeDtypeStruct((n_slots, dp), U32),
        compiler_params=_cparams(("arbitrary",)),
        name="experts",
    )(block_e, n_valid, xs, wg, wu, wd)


def _combine_kernel(slot_ref, h_ref, w_ref, nw_ref, yb_hbm, o_ref, ya0_ref, ya1_ref, yb0_ref, yb1_ref, sem):
    i = pl.program_id(0)
    tm = h_ref.shape[0]
    last = pl.num_programs(0) - 1
    bufs = ((ya0_ref, ya1_ref), (yb0_ref, yb1_ref))

    def gather(blk, par):
        tokens = tm * pl.num_programs(0)
        for r in range(tm):
            for k in range(TOP_K):
                pltpu.make_async_copy(yb_hbm.at[pl.ds(slot_ref[k * tokens + blk * tm + r], 1)],
                                      bufs[par][k].at[pl.ds(r, 1)], sem.at[par]).start(priority=k)

    def wait(par):
        for k in range(TOP_K):
            pltpu.make_async_copy(yb_hbm.at[pl.ds(0, tm)], bufs[par][k], sem.at[par]).wait()

    @pl.when(i == 0)
    def _():
        gather(i, 0)

    def step(par):
        wait(par)
        gather(jnp.minimum(i + 1, last), 1 - par)
        w0 = w_ref[:, TOP_K:TOP_K + 1]
        w1 = w_ref[:, TOP_K + 1:TOP_K + 2]
        half = h_ref.shape[1] // 2
        ya = _unpack_halves(bufs[par][0][...])
        yb = _unpack_halves(bufs[par][1][...])
        h3 = [h_ref[:, p * half:(p + 1) * half] + w0 * ya[p] + w1 * yb[p] for p in range(2)]
        var = sum(jnp.sum(v * v, axis=-1, keepdims=True) for v in h3) / h_ref.shape[1]
        inv = lax.rsqrt(var + EPS)
        for p in range(2):
            o_ref[:, p * half:(p + 1) * half] = h3[p] * inv * nw_ref[:, p * half:(p + 1) * half]

        @pl.when(i == last)
        def _():
            wait(1 - par)

    for par in range(2):
        pl.when(i % 2 == par)(functools.partial(step, par))


def _combine(slot, h2, route, nw, yb, tm):
    t, d = h2.shape
    tm = min(tm, t)
    grid_spec = pltpu.PrefetchScalarGridSpec(
        num_scalar_prefetch=1,
        grid=(t // tm,),
        in_specs=[
            pl.BlockSpec((tm, d), lambda i, s: (i, 0)),
            pl.BlockSpec((tm, route.shape[1]), lambda i, s: (i, 0)),
            pl.BlockSpec((1, d), lambda i, s: (0, 0)),
            pl.BlockSpec(memory_space=pl.ANY),
        ],
        out_specs=pl.BlockSpec((tm, d), lambda i, s: (i, 0)),
        scratch_shapes=[pltpu.VMEM((tm, yb.shape[1]), yb.dtype)] * (2 * TOP_K) + [pltpu.SemaphoreType.DMA((2,))],
    )
    return pl.pallas_call(
        _combine_kernel,
        grid_spec=grid_spec,
        out_shape=jax.ShapeDtypeStruct((t, d), F32),
        compiler_params=_cparams(("arbitrary",)),
        name="combine",
    )(slot, h2, route, nw, yb)


def _slots(route, cnt, t):
    rt = route[:, :SUBLANES].T.astype(jnp.int32)
    e_kt = rt[:TOP_K]
    rank = rt[2 * TOP_K:3 * TOP_K]
    onehot = e_kt[None] == jnp.arange(N_EXPERTS, dtype=jnp.int32)[:, None, None]
    counts = cnt[0, :N_EXPERTS].astype(jnp.int32)
    padded = (counts + MOE_BLOCK - 1) // MOE_BLOCK * MOE_BLOCK
    pad_end = jnp.cumsum(padded)
    pad_start = pad_end - padded
    slot = (jnp.sum(jnp.where(onehot, pad_start[:, None, None], 0), axis=0) + rank).reshape(-1)
    n_slots = t * TOP_K + N_EXPERTS * MOE_BLOCK
    n_blocks = n_slots // MOE_BLOCK
    block_start = jnp.arange(n_blocks, dtype=jnp.int32) * MOE_BLOCK
    block_e = jnp.minimum(jnp.sum((pad_end[None, :] <= block_start[:, None]).astype(jnp.int32), axis=1),
                          N_EXPERTS - 1)
    n_valid = (pad_end[-1:] // MOE_BLOCK).astype(jnp.int32)
    return slot, pad_start + counts, padded - counts, block_e, n_valid, n_slots


def _pack_w_kernel(w_ref, o_ref, *, pieces):
    filled = 0
    for src0, src1, dst0 in pieces:
        o_ref[:, dst0:dst0 + src1 - src0] = w_ref[:, src0:src1].astype(o_ref.dtype)
        filled = max(filled, dst0 + src1 - src0)
    if filled < o_ref.shape[1]:
        o_ref[:, filled:] = jnp.zeros((o_ref.shape[0], o_ref.shape[1] - filled), o_ref.dtype)


def _pack_w(w, pieces, n_out, tk):
    k, n_in = w.shape
    return pl.pallas_call(
        functools.partial(_pack_w_kernel, pieces=pieces),
        grid=(k // tk,),
        in_specs=[pl.BlockSpec((tk, n_in), lambda i: (i, 0))],
        out_specs=pl.BlockSpec((tk, n_out), lambda i: (i, 0)),
        out_shape=jax.ShapeDtypeStruct((k, n_out), BF16),
        compiler_params=_cparams(("parallel",)),
        name="pack_w",
    )(w)


def _pad_lanes(v, n=LANES):
    return jnp.pad(v, ((0, 0), (0, n - v.shape[1])))


def kernel(x, mem, norm_mix_w, w_in, conv_w, conv_b, dt_bias, a_log, d_skip, ssd_norm_w, w_ssd_branch, w_sb_branch, w_mix_out, norm_xa_w, norm_mem_w, w_xq, w_xk, w_xv, w_xo, norm_moe_w, w_router_group, w_router_expert, w_expert_gate, w_expert_up, w_expert_down, norm_final_w):
    bsz, seq, d = x.shape
    depth = w_in.shape[0]
    t = bsz * seq
    inner = ssd_norm_w.shape[1]
    heads = dt_bias.shape[1]
    bc = SSD_GROUPS * SSD_STATE
    conv_dim = inner + 2 * bc
    sb_width = SB_HEADS * SB_HEAD_DIM
    col_dt = inner + conv_dim
    col_qkv = col_dt + heads
    col_g = col_qkv + 3 * sb_width
    g_col0 = 2 * inner
    bc_col0 = g_col0 + 2 * d
    q_col0 = bc_col0 + 2 * bc
    n_main = q_col0 + 3 * sb_width
    tn_proj = 1280
    n_proj = -(-n_main // tn_proj) * tn_proj
    tn_merge = 1024

    assert depth == 1, "single-layer configuration: the final RMSNorm is fused into the last MoE combine"
    h = x.reshape(t, d)
    for l in range(depth):
        w_l = w_in[l]
        w_main = _pack_w(w_l, [(0, 2 * inner, 0), (col_g, col_g + 2 * d, g_col0),
                               (2 * inner, col_dt, bc_col0), (col_qkv, col_g, q_col0)], n_proj, tk=256)
        w_dt = _pad_lanes(w_l[:, col_dt:col_dt + heads]).astype(BF16)
        e_mat = (jnp.arange(inner, dtype=jnp.int32)[None, :] // SSD_HEAD_DIM
                 == jnp.arange(LANES, dtype=jnp.int32)[:, None]).astype(BF16)
        dskip_x = jnp.repeat(d_skip[l], SSD_HEAD_DIM)[None, :]

        proj, dt_raw = _norm_matmul(h, norm_mix_w[l][None, :], w_main, w_dt, tm=1024, tn=tn_proj)
        y_ssd = _ssd(proj, dt_raw, conv_w[l], conv_b[l][None, :], _pad_lanes(dt_bias[l][None, :]),
                     _pad_lanes(a_log[l][None, :]), dskip_x, ssd_norm_w[l][None, :], e_mat,
                     bsz, seq, inner, bc, bc_col0 // bc)
        y_sb = _stickbreak(proj.reshape(bsz, seq, n_proj), bsz, seq, q_col0 // SB_HEAD_DIM)
        h1 = _merge(h, y_ssd, y_sb.reshape(t, sb_width), proj,
                    w_ssd_branch[l].astype(BF16), w_sb_branch[l].astype(BF16), w_mix_out[l].astype(BF16),
                    g_col0 // tn_merge, tm=512, tn=tn_merge)

        m_len = mem.shape[1]
        w_kv = jnp.concatenate([w_xk[l], w_xv[l]], axis=1).astype(BF16)
        kv, _ = _norm_matmul(mem.reshape(bsz * m_len, d), norm_mem_w[l][None, :], w_kv,
                             jnp.zeros((d, LANES), BF16), tm=bsz * m_len, tn=512)
        w_r = _pad_lanes(jnp.concatenate([w_router_group[l], w_router_expert[l]], axis=1))
        wr_hi = w_r.astype(BF16)
        wr_lo = (w_r - wr_hi.astype(F32)).astype(BF16)
        h2, um, route, cnt = _xattn(h1, kv, norm_xa_w[l][None, :], w_xq[l].astype(BF16), w_xo[l].astype(BF16),
                               norm_moe_w[l][None, :], jnp.concatenate([wr_hi, wr_lo], axis=1), seq, tm=512)

        slot, pad_row, pad_n, block_e, n_valid, n_slots = _slots(route, cnt, t)
        xs = _dispatch(slot, pad_row, pad_n, n_valid, um, n_slots, tm=512)
        yb = _experts(block_e, n_valid, xs, w_expert_gate[l], w_expert_up[l], w_expert_down[l])
        h = _combine(slot, h2, route, norm_final_w[None, :], yb, tm=512)
    return h.reshape(bsz, seq, d)
```

```python
import functools

import jax
import jax.numpy as jnp
from jax import lax
from jax.experimental import pallas as pl
from jax.experimental.pallas import tpu as pltpu

F32 = jnp.float32
BF16 = jnp.bfloat16
U32 = jnp.uint32
EPS = 1e-6

SSD_HEAD_DIM = 64
SSD_GROUPS = 4
SSD_STATE = 128
SSD_CONV = 4
SSD_CHUNK = 128
SB_HEADS = 4
SB_HEAD_DIM = 128
SB_BLOCK = 128
SB_SUB = 8
XA_HEADS = 4
XA_HEAD_DIM = 128
N_GROUPS = 4
EXPERTS_PER_GROUP = 8
N_EXPERTS = N_GROUPS * EXPERTS_PER_GROUP
TOP_K = 2
MOE_BLOCK = 256

LANES = 128
SUBLANES = 8
VMEM_LIMIT = 56 * 1024 * 1024

SB_EXIT_LOG = -88.0
NEG_BIG = -1e30


def _cparams(sem):
    return pltpu.CompilerParams(dimension_semantics=sem, vmem_limit_bytes=VMEM_LIMIT)


def _rms(x, w):
    var = jnp.mean(x * x, axis=-1, keepdims=True)
    return x * lax.rsqrt(var + EPS) * w


def _split2(v):
    hi = v.astype(BF16)
    lo = (v - hi.astype(F32)).astype(BF16)
    return hi, lo


def _dot(a, b):
    return jnp.dot(a, b, preferred_element_type=F32)


def _dot_nt(a, bt):
    return lax.dot_general(a, bt, (((1,), (1,)), ((), ())), preferred_element_type=F32)


def _pack_halves(v):
    n = v.shape[1] // 2
    lo = lax.bitcast_convert_type(v[:, :n].astype(BF16).astype(F32), U32)
    hi = lax.bitcast_convert_type(v[:, n:].astype(BF16).astype(F32), U32)
    return (lo >> 16) | hi


def _unpack_halves(w):
    lo = lax.bitcast_convert_type(w << 16, F32)
    hi = lax.bitcast_convert_type(w & jnp.uint32(0xFFFF0000), F32)
    return lo, hi


def _silu(x):
    return x / (1.0 + jnp.exp(-x))


def _softplus(x):
    return jnp.maximum(x, 0.0) + jnp.log(1.0 + jnp.exp(-jnp.abs(x)))


def _norm_matmul_kernel(x_ref, nw_ref, w_ref, ws_ref, o_ref, os_ref, u_ref):
    @pl.when(pl.program_id(1) == 0)
    def _():
        u = _rms(x_ref[...], nw_ref[...]).astype(BF16)
        u_ref[...] = u
        os_ref[...] = _dot_nt(u, ws_ref[...])

    o_ref[...] = _dot_nt(u_ref[...], w_ref[...]).astype(o_ref.dtype)


def _norm_matmul(x, nw, wt, wt_side, tm, tn):
    m, k = x.shape
    n = wt.shape[0]
    ns = wt_side.shape[0]
    tm = min(tm, m)
    return pl.pallas_call(
        _norm_matmul_kernel,
        grid=(m // tm, n // tn),
        in_specs=[
            pl.BlockSpec((tm, k), lambda i, j: (i, 0)),
            pl.BlockSpec((1, k), lambda i, j: (0, 0)),
            pl.BlockSpec((tn, k), lambda i, j: (j, 0)),
            pl.BlockSpec((ns, k), lambda i, j: (0, 0)),
        ],
        out_specs=[
            pl.BlockSpec((tm, tn), lambda i, j: (i, j)),
            pl.BlockSpec((tm, ns), lambda i, j: (i, 0)),
        ],
        out_shape=[jax.ShapeDtypeStruct((m, n), BF16), jax.ShapeDtypeStruct((m, ns), F32)],
        scratch_shapes=[pltpu.VMEM((tm, k), BF16)],
        compiler_params=_cparams(("parallel", "arbitrary")),
        name="norm_matmul",
    )(x, nw, wt, wt_side)


def _ssd_kernel(z_ref, xs_ref, b_ref, c_ref, dtr_ref, cw_ref, cb_ref, dtb_ref, alog_ref, dsk_ref,
                nw_ref, e_ref, o_ref,
                tail_ref, win_ref, state_ref, xact_ref, bact_ref, cact_ref, dtx_ref, eax_ref, wsx_ref, y_ref):
    L = SSD_CHUNK
    inner = xs_ref.shape[1]
    bc = b_ref.shape[1]
    n_pairs = inner // LANES

    @pl.when(pl.program_id(1) == 0)
    def _():
        tail_ref[...] = jnp.zeros_like(tail_ref)
        state_ref[...] = jnp.zeros_like(state_ref)


    def conv_seg(src_ref, col0, width, dst_ref):
        for j in range(0, width, 2 * LANES):
            cols = slice(col0 + j, col0 + j + 2 * LANES)
            xin = src_ref[:, j:j + 2 * LANES].astype(F32)
            win_ref[:SUBLANES, :] = tail_ref[:, cols]
            win_ref[SUBLANES:, :] = xin
            acc = xin * cw_ref[SSD_CONV - 1:SSD_CONV, cols] + cb_ref[:, cols]
            for s in range(1, SSD_CONV):
                acc = acc + win_ref[SUBLANES - s:SUBLANES - s + L, :] * cw_ref[SSD_CONV - 1 - s:SSD_CONV - s, cols]
            tail_ref[:, cols] = xin[L - SUBLANES:]
            dst_ref[:, j:j + 2 * LANES] = _silu(acc).astype(dst_ref.dtype)

    conv_seg(xs_ref, 0, inner, xact_ref)
    conv_seg(b_ref, inner, bc, bact_ref)
    conv_seg(c_ref, inner + bc, bc, cact_ref)

    dt = _softplus(dtr_ref[...] + dtb_ref[...])
    da = dt * (-jnp.exp(alog_ref[...]))
    rowl = lax.broadcasted_iota(jnp.int32, (L, L), 0)
    coll = lax.broadcasted_iota(jnp.int32, (L, L), 1)
    lower = rowl >= coll
    tri = jnp.where(lower, 1.0, 0.0).astype(BF16)
    d1 = da.astype(BF16)
    r1 = da - d1.astype(F32)
    d2 = r1.astype(BF16)
    d3 = (r1 - d2.astype(F32)).astype(BF16)
    a_cum = _dot(tri, d1) + _dot(tri, d2) + _dot(tri, d3)
    a_cum_t = a_cum.T
    a_last = a_cum[L - 1:L, :]
    e_mat = e_ref[...]

    def expand(v):
        hi, lo = _split2(v)
        return _dot(hi, e_mat) + _dot(lo, e_mat)

    dtx_ref[...] = expand(dt)
    eax_ref[...] = expand(jnp.exp(a_cum))
    wsx_ref[...] = expand(jnp.exp(a_last - a_cum))
    elx = expand(jnp.broadcast_to(jnp.exp(a_last), (SUBLANES, LANES)))[0:1]

    lane = lax.broadcasted_iota(jnp.int32, (L, LANES), 1)
    pairs_per_group = n_pairs // SSD_GROUPS
    for g in range(SSD_GROUPS):
        gcols = slice(g * SSD_STATE, (g + 1) * SSD_STATE)
        bg_t = bact_ref[:, gcols].T.astype(BF16)
        cg = cact_ref[:, gcols]
        cb = _dot(cg, bg_t)
        for pp in range(pairs_per_group):
            p = g * pairs_per_group + pp
            pc = slice(p * LANES, (p + 1) * LANES)
            ms = []
            for hh in (2 * p, 2 * p + 1):
                seg = a_cum[:, hh:hh + 1] - a_cum_t[hh:hh + 1, :]
                dec = jnp.exp(jnp.where(lower, seg, NEG_BIG))
                ms.append((cb * dec).astype(BF16))
            lhs = jnp.concatenate(ms, axis=1)
            xs_p = xact_ref[:, pc]
            xdt = xs_p * dtx_ref[:, pc]
            rhs = jnp.concatenate([jnp.where(lane < SSD_HEAD_DIM, xdt, 0.0).astype(BF16),
                                   jnp.where(lane >= SSD_HEAD_DIM, xdt, 0.0).astype(BF16)], axis=0)
            st = state_ref[p]
            y = _dot(lhs, rhs)
            y = y + _dot(cg, st.astype(BF16)) * eax_ref[:, pc]
            y = y + xs_p * dsk_ref[:, pc]
            y_ref[:, pc] = y
            xw = (xdt * wsx_ref[:, pc]).astype(BF16)
            state_ref[p] = st * elx[:, pc] + _dot(bg_t, xw)

    yg = y_ref[...] * _silu(z_ref[...].astype(F32))
    o_ref[...] = _rms(yg, nw_ref[...]).astype(o_ref.dtype)


def _ssd(proj, dt_raw, conv_w, conv_b, dt_bias_p, a_log_p, dskip_x, norm_w, e_mat, bsz, seq, inner, bc, b_blk):
    L = SSD_CHUNK
    nc = seq // L
    conv_dim = inner + 2 * bc
    n_pairs = inner // LANES

    def rows(b, c):
        return b * nc + c

    return pl.pallas_call(
        _ssd_kernel,
        grid=(bsz, nc),
        in_specs=[
            pl.BlockSpec((L, inner), lambda b, c: (rows(b, c), 0)),
            pl.BlockSpec((L, inner), lambda b, c: (rows(b, c), 1)),
            pl.BlockSpec((L, bc), lambda b, c: (rows(b, c), b_blk)),
            pl.BlockSpec((L, bc), lambda b, c: (rows(b, c), b_blk + 1)),
            pl.BlockSpec((L, LANES), lambda b, c: (rows(b, c), 0)),
            pl.BlockSpec((SSD_CONV, conv_dim), lambda b, c: (0, 0)),
            pl.BlockSpec((1, conv_dim), lambda b, c: (0, 0)),
            pl.BlockSpec((1, LANES), lambda b, c: (0, 0)),
            pl.BlockSpec((1, LANES), lambda b, c: (0, 0)),
            pl.BlockSpec((1, inner), lambda b, c: (0, 0)),
            pl.BlockSpec((1, inner), lambda b, c: (0, 0)),
            pl.BlockSpec((LANES, inner), lambda b, c: (0, 0)),
        ],
        out_specs=pl.BlockSpec((L, inner), lambda b, c: (rows(b, c), 0)),
        out_shape=jax.ShapeDtypeStruct((bsz * seq, inner), BF16),
        scratch_shapes=[
            pltpu.VMEM((SUBLANES, conv_dim), F32),
            pltpu.VMEM((SUBLANES + L, 2 * LANES), F32),
            pltpu.VMEM((n_pairs, SSD_STATE, LANES), F32),
            pltpu.VMEM((L, inner), F32),
            pltpu.VMEM((L, bc), F32),
            pltpu.VMEM((L, bc), BF16),
            pltpu.VMEM((L, inner), F32),
            pltpu.VMEM((L, inner), F32),
            pltpu.VMEM((L, inner), F32),
            pltpu.VMEM((L, inner), F32),
        ],
        compiler_params=_cparams(("parallel", "arbitrary")),
        name="ssd",
    )(proj, proj, proj, proj, dt_raw, conv_w, conv_b, dt_bias_p, a_log_p, dskip_x, norm_w, e_mat)


def _sb_kernel(q_ref, k_ref, v_ref, o_ref, acc_ref, carry_ref):
    BL = SB_BLOCK
    i = pl.program_id(2)
    scale = SB_HEAD_DIM ** -0.5
    row = lax.broadcasted_iota(jnp.int32, (BL, BL), 0)
    col = lax.broadcasted_iota(jnp.int32, (BL, BL), 1)
    causal = col < row
    r2 = lax.broadcasted_iota(jnp.int32, (BL, 2 * BL), 0)
    c2 = lax.broadcasted_iota(jnp.int32, (BL, 2 * BL), 1)
    tri = jnp.where((r2 > c2) | (c2 >= BL), 1.0, 0.0).astype(BF16)

    def sweep(s, first):
        subs = range(SB_SUB)
        kbs = [i * SB_SUB + j - s for j in subs]
        starts = [pl.multiple_of(jnp.maximum(kb, 0) * BL, BL) for kb in kbs]
        zls = [lax.dot_general(q_ref[j * BL:(j + 1) * BL, :], k_ref[pl.ds(starts[j], BL), :],
                               (((1,), (1,)), ((), ())), preferred_element_type=F32) * scale for j in subs]
        sps = [_softplus(zl) for zl in zls]
        log_betas = [zl - sp for zl, sp in zip(zls, sps)]
        log_keeps = [jnp.where(causal, -sp, 0.0) if first else -sp for sp in sps]
        splits = [_split2(lk) for lk in log_keeps]
        t2s = [_dot(hi, tri) + _dot(lo, tri) for hi, lo in splits]
        if first:
            atts = [jnp.where(causal, jnp.exp(lb + t2[:, :BL]), 0.0) for lb, t2 in zip(log_betas, t2s)]
            carries = [t2[:, BL:] for t2 in t2s]
        else:
            olds = [carry_ref[j] for j in subs]
            atts = [jnp.where(kb >= 0, jnp.exp(lb + t2[:, :BL] + old), 0.0)
                    for kb, lb, t2, old in zip(kbs, log_betas, t2s, olds)]
            carries = [old + jnp.where(kb >= 0, t2[:, BL:], 0.0) for kb, t2, old in zip(kbs, t2s, olds)]
        pvs = [_dot(atts[j].astype(BF16), v_ref[pl.ds(starts[j], BL), :]) for j in subs]
        worst = None
        for j in subs:
            if first:
                acc_ref[j] = pvs[j]
            else:
                acc_ref[j] += pvs[j]
            carry_ref[j] = carries[j]
            live = jnp.where(kbs[j] >= 1, carries[j], NEG_BIG)
            worst = live if worst is None else jnp.maximum(worst, live)
        return jnp.max(worst) > SB_EXIT_LOG

    sweep(0, True)
    go = sweep(1, False)

    def body(st):
        s, _ = st
        return s + 1, sweep(s, False)

    lax.while_loop(lambda st: st[1], body, (jnp.int32(2), go))
    for j in range(SB_SUB):
        o_ref[j * BL:(j + 1) * BL, :] = acc_ref[j].astype(o_ref.dtype)


def _stickbreak(proj3, bsz, seq, q_blk0):
    tq = SB_SUB * SB_BLOCK
    return pl.pallas_call(
        _sb_kernel,
        grid=(bsz, SB_HEADS, seq // tq),
        in_specs=[
            pl.BlockSpec((None, tq, SB_HEAD_DIM), lambda b, h, i: (b, i, q_blk0 + h)),
            pl.BlockSpec((None, seq, SB_HEAD_DIM), lambda b, h, i: (b, 0, q_blk0 + SB_HEADS + h)),
            pl.BlockSpec((None, seq, SB_HEAD_DIM), lambda b, h, i: (b, 0, q_blk0 + 2 * SB_HEADS + h)),
        ],
        out_specs=pl.BlockSpec((None, tq, SB_HEAD_DIM), lambda b, h, i: (b, i, h)),
        out_shape=jax.ShapeDtypeStruct((bsz, seq, SB_HEADS * SB_HEAD_DIM), BF16),
        scratch_shapes=[pltpu.VMEM((SB_SUB, SB_BLOCK, SB_HEAD_DIM), F32),
                        pltpu.VMEM((SB_SUB, SB_BLOCK, SB_BLOCK), F32)],
        compiler_params=_cparams(("parallel", "parallel", "arbitrary")),
        name="stickbreak",
    )(proj3, proj3, proj3)


def _merge_kernel(x_ref, ys_ref, yb_ref, *rest):
    *g_refs, w1_ref, w2_ref, wm_ref, o_ref = rest
    nj = len(g_refs) // 2
    tn = g_refs[0].shape[1]
    ys = ys_ref[...]
    yb = yb_ref[...]
    ms = []
    for j in range(nj):
        cs = slice(j * tn, (j + 1) * tn)
        s1 = jax.nn.sigmoid(g_refs[j][...].astype(F32))
        s2 = jax.nn.sigmoid(g_refs[nj + j][...].astype(F32))
        ms.append((s1 * _dot(ys, w1_ref[:, cs]) + s2 * _dot(yb, w2_ref[:, cs])).astype(BF16))
    o_ref[...] = x_ref[...] + _dot(jnp.concatenate(ms, axis=1), wm_ref[...])


def _merge(x2, y_ssd, y_sb, proj, w1, w2, wm, g_blk0, tm, tn):
    t, d = x2.shape
    tm = min(tm, t)
    nj = d // tn
    resident = pl.Buffered(1)
    g_specs = [pl.BlockSpec((tm, tn), lambda i, b=g_blk0 + b: (i, b)) for b in range(2 * nj)]
    return pl.pallas_call(
        _merge_kernel,
        grid=(t // tm,),
        in_specs=[
            pl.BlockSpec((tm, d), lambda i: (i, 0)),
            pl.BlockSpec((tm, y_ssd.shape[1]), lambda i: (i, 0)),
            pl.BlockSpec((tm, y_sb.shape[1]), lambda i: (i, 0)),
            *g_specs,
            pl.BlockSpec(w1.shape, lambda i: (0, 0), pipeline_mode=resident),
            pl.BlockSpec(w2.shape, lambda i: (0, 0), pipeline_mode=resident),
            pl.BlockSpec(wm.shape, lambda i: (0, 0), pipeline_mode=resident),
        ],
        out_specs=pl.BlockSpec((tm, d), lambda i: (i, 0)),
        out_shape=jax.ShapeDtypeStruct((t, d), F32),
        compiler_params=_cparams(("parallel",)),
        name="merge",
    )(x2, y_ssd, y_sb, *([proj] * (2 * nj)), w1, w2, wm)


def _xattn_kernel(h_ref, kv_ref, nxa_ref, wq_ref, wo_ref, nmoe_ref, wr_ref, h2_ref, um_ref, rt_ref, cnt_ref,
                  base_ref):
    h1 = h_ref[...]
    un = _rms(h1, nxa_ref[...]).astype(BF16)
    q = _dot(un, wq_ref[...]).astype(BF16)
    width = XA_HEADS * XA_HEAD_DIM
    outs = []
    for hd in range(XA_HEADS):
        cs = slice(hd * XA_HEAD_DIM, (hd + 1) * XA_HEAD_DIM)
        k = kv_ref[:, cs]
        v = kv_ref[:, width + hd * XA_HEAD_DIM: width + (hd + 1) * XA_HEAD_DIM]
        sc = lax.dot_general(q[:, cs], k, (((1,), (1,)), ((), ())),
                             preferred_element_type=F32) * (XA_HEAD_DIM ** -0.5)
        sc = sc - jnp.max(sc, axis=-1, keepdims=True)
        p = jnp.exp(sc)
        p = p / jnp.sum(p, axis=-1, keepdims=True)
        outs.append(_dot(p.astype(BF16), v).astype(BF16))
    o = jnp.concatenate(outs, axis=1)
    h2 = h1 + _dot(o, wo_ref[...])
    h2_ref[...] = h2
    um = _rms(h2, nmoe_ref[...])
    um_ref[...] = _pack_halves(um)
    hi, lo = _split2(um)
    nr = rt_ref.shape[1]
    both = _dot(hi, wr_ref[...])
    lg = both[:, :nr] + both[:, nr:] + _dot(lo, wr_ref[:, :nr])

    lane = lax.broadcasted_iota(jnp.int32, lg.shape, 1)
    lane_f = lane.astype(F32)
    is_g = lane < N_GROUPS
    gl = jnp.where(is_g, lg, NEG_BIG)
    gmax = jnp.max(gl, axis=-1, keepdims=True)
    g_sel = jnp.min(jnp.where(gl == gmax, lane_f, float(LANES)), axis=-1, keepdims=True)
    g_gate = 1.0 / jnp.sum(jnp.where(is_g, jnp.exp(gl - gmax), 0.0), axis=-1, keepdims=True)
    lo_lane = N_GROUPS + EXPERTS_PER_GROUP * g_sel
    el = jnp.where((lane_f >= lo_lane) & (lane_f < lo_lane + EXPERTS_PER_GROUP), lg, NEG_BIG)
    m1 = jnp.max(el, axis=-1, keepdims=True)
    i1 = jnp.min(jnp.where(el == m1, lane_f, float(LANES)), axis=-1, keepdims=True)
    el2 = jnp.where(lane_f == i1, NEG_BIG, el)
    m2 = jnp.max(el2, axis=-1, keepdims=True)
    i2 = jnp.min(jnp.where(el2 == m2, lane_f, float(LANES)), axis=-1, keepdims=True)
    ex = jnp.exp(m2 - m1)
    p1 = 1.0 / (1.0 + ex)
    e1 = i1 - N_GROUPS
    e2 = i2 - N_GROUPS

    @pl.when(pl.program_id(0) == 0)
    def _():
        base_ref[...] = jnp.zeros_like(base_ref)

    tm = lg.shape[0]
    hot1 = lane_f == e1
    hot2 = lane_f == e2
    onehot = jnp.where(hot1 | hot2, 1.0, 0.0)
    tr = lax.broadcasted_iota(jnp.int32, (tm, tm), 0)
    tc = lax.broadcasted_iota(jnp.int32, (tm, tm), 1)
    before = _dot(jnp.where(tc < tr, 1.0, 0.0).astype(BF16), onehot.astype(BF16)) + base_ref[...]
    r1 = jnp.sum(jnp.where(hot1, before, 0.0), axis=-1, keepdims=True)
    r2 = jnp.sum(jnp.where(hot2, before, 0.0), axis=-1, keepdims=True)
    base_ref[...] += jnp.sum(onehot, axis=0, keepdims=True)
    cnt_ref[...] = jnp.broadcast_to(base_ref[...], cnt_ref.shape)

    vals = (e1, e2, p1 * g_gate, ex * p1 * g_gate, r1, r2)
    out = jnp.zeros_like(lg)
    for j, v in enumerate(vals):
        out = jnp.where(lane == j, v, out)
    rt_ref[...] = out


def _xattn(h1, kv, nxa, wq, wo, nmoe, wr, seq, tm):
    t, d = h1.shape
    tm = min(tm, seq)
    per_b = seq // tm
    m_len = kv.shape[0] // (t // seq)
    nr = wr.shape[1] // 2
    return pl.pallas_call(
        _xattn_kernel,
        grid=(t // tm,),
        in_specs=[
            pl.BlockSpec((tm, d), lambda i: (i, 0)),
            pl.BlockSpec((m_len, kv.shape[1]), lambda i: (i // per_b, 0)),
            pl.BlockSpec((1, d), lambda i: (0, 0)),
            pl.BlockSpec(wq.shape, lambda i: (0, 0)),
            pl.BlockSpec(wo.shape, lambda i: (0, 0)),
            pl.BlockSpec((1, d), lambda i: (0, 0)),
            pl.BlockSpec((d, 2 * nr), lambda i: (0, 0)),
        ],
        out_specs=[
            pl.BlockSpec((tm, d), lambda i: (i, 0)),
            pl.BlockSpec((tm, d // 2), lambda i: (i, 0)),
            pl.BlockSpec((tm, nr), lambda i: (i, 0)),
            pl.BlockSpec((SUBLANES, nr), lambda i: (0, 0)),
        ],
        out_shape=[jax.ShapeDtypeStruct((t, d), F32), jax.ShapeDtypeStruct((t, d // 2), U32),
                   jax.ShapeDtypeStruct((t, nr), F32), jax.ShapeDtypeStruct((SUBLANES, nr), F32)],
        scratch_shapes=[pltpu.VMEM((1, nr), F32)],
        compiler_params=_cparams(("arbitrary",)),
        name="xattn",
    )(h1, kv, nxa, wq, wo, nmoe, wr)


def _dispatch_kernel(slot_ref, padrow_ref, padn_ref, nv_ref, um_ref, xs_hbm, sa_ref, sb_ref, z_ref, sem, zsem):
    i = pl.program_id(0)
    tm = um_ref.shape[0]
    last = pl.num_programs(0) - 1
    stage = (sa_ref, sb_ref)

    def zero_rows(start):
        for e in range(N_EXPERTS):
            def body(j, c, e=e):
                cp = pltpu.make_async_copy(z_ref.at[pl.ds(0, 1)], xs_hbm.at[pl.ds(padrow_ref[e] + j, 1)], zsem)
                if start:
                    cp.start()
                else:
                    cp.wait()
                return c

            lax.fori_loop(0, padn_ref[e], body, 0)

        def tail(b, c):
            cp = pltpu.make_async_copy(
                z_ref, xs_hbm.at[pl.ds(pl.multiple_of(b * MOE_BLOCK, MOE_BLOCK), MOE_BLOCK)], zsem)
            if start:
                cp.start()
            else:
                cp.wait()
            return c

        lax.fori_loop(nv_ref[0], xs_hbm.shape[0] // MOE_BLOCK, tail, 0)

    @pl.when(i == 0)
    def _():
        z_ref[...] = jnp.zeros_like(z_ref)
        zero_rows(True)

    def wait(par):
        for _ in range(TOP_K):
            pltpu.make_async_copy(stage[par], xs_hbm.at[pl.ds(0, tm)], sem.at[par]).wait()

    def step(par):
        pl.when(i >= 2)(functools.partial(wait, par))
        stage[par][...] = um_ref[...]
        tokens = tm * pl.num_programs(0)
        for r in range(tm):
            for k in range(TOP_K):
                pltpu.make_async_copy(stage[par].at[pl.ds(r, 1)],
                                      xs_hbm.at[pl.ds(slot_ref[k * tokens + i * tm + r], 1)],
                                      sem.at[par]).start(priority=k)

        @pl.when(i == last)
        def _():
            wait(par)
            pl.when(i >= 1)(functools.partial(wait, 1 - par))

    for par in range(2):
        pl.when(i % 2 == par)(functools.partial(step, par))

    @pl.when(i == last)
    def _():
        zero_rows(False)


def _dispatch(slot, pad_row, pad_n, n_valid, um, n_slots, tm):
    t, d = um.shape
    tm = min(tm, t)
    grid_spec = pltpu.PrefetchScalarGridSpec(
        num_scalar_prefetch=4,
        grid=(t // tm,),
        in_specs=[pl.BlockSpec((tm, d), lambda i, s, pr, pn, nv: (i, 0))],
        out_specs=pl.BlockSpec(memory_space=pl.ANY),
        scratch_shapes=[pltpu.VMEM((tm, d), um.dtype), pltpu.VMEM((tm, d), um.dtype),
                        pltpu.VMEM((MOE_BLOCK, d), um.dtype),
                        pltpu.SemaphoreType.DMA((2,)), pltpu.SemaphoreType.DMA(())],
    )
    return pl.pallas_call(
        _dispatch_kernel,
        grid_spec=grid_spec,
        out_shape=jax.ShapeDtypeStruct((n_slots, d), um.dtype),
        compiler_params=_cparams(("arbitrary",)),
        name="dispatch",
    )(slot, pad_row, pad_n, n_valid, um)


def _expert_kernel(be_ref, nv_ref, x_ref, wg_ref, wu_ref, wd_ref, o_ref, wgb_ref, wub_ref, wdb_ref):
    i = pl.program_id(0)

    @pl.when(jnp.logical_or(i == 0, be_ref[i] != be_ref[jnp.maximum(i - 1, 0)]))
    def _():
        wgb_ref[...] = wg_ref[...].astype(BF16)
        wub_ref[...] = wu_ref[...].astype(BF16)
        wdb_ref[...] = wd_ref[...].astype(BF16)

    @pl.when(i < nv_ref[0])
    def _():
        lo, hi = _unpack_halves(x_ref[...])
        xb = jnp.concatenate([lo.astype(BF16), hi.astype(BF16)], axis=1)
        hid = (_silu(_dot(xb, wgb_ref[...])) * _dot(xb, wub_ref[...])).astype(BF16)
        o_ref[...] = _pack_halves(_dot(hid, wdb_ref[...]))

    @pl.when(i >= nv_ref[0])
    def _():
        o_ref[...] = jnp.zeros_like(o_ref)


def _experts(block_e, n_valid, xs, wg, wu, wd):
    n_slots, dp = xs.shape
    d, ff = wg.shape[1:]

    def x_map(i, be, nv):
        return (jnp.minimum(i, nv[0] - 1), 0)

    grid_spec = pltpu.PrefetchScalarGridSpec(
        num_scalar_prefetch=2,
        grid=(n_slots // MOE_BLOCK,),
        in_specs=[
            pl.BlockSpec((MOE_BLOCK, dp), x_map),
            pl.BlockSpec((None, d, ff), lambda i, be, nv: (be[i], 0, 0)),
            pl.BlockSpec((None, d, ff), lambda i, be, nv: (be[i], 0, 0)),
            pl.BlockSpec((None, ff, d), lambda i, be, nv: (be[i], 0, 0)),
        ],
        out_specs=pl.BlockSpec((MOE_BLOCK, dp), lambda i, be, nv: (i, 0)),
        scratch_shapes=[pltpu.VMEM((d, ff), BF16), pltpu.VMEM((d, ff), BF16), pltpu.VMEM((ff, d), BF16)],
    )
    return pl.pallas_call(
        _expert_kernel,
        grid_spec=grid_spec,
        out_shape=jax.ShapeDtypeStruct((n_slots, dp), U32),
        compiler_params=_cparams(("arbitrary",)),
        name="experts",
    )(block_e, n_valid, xs, wg, wu, wd)


def _combine_kernel(slot_ref, h_ref, w_ref, nw_ref, yb_hbm, o_ref, ya0_ref, ya1_ref, yb0_ref, yb1_ref, sem):
    i = pl.program_id(0)
    tm = h_ref.shape[0]
    last = pl.num_programs(0) - 1
    bufs = ((ya0_ref, ya1_ref), (yb0_ref, yb1_ref))

    def gather(blk, par):
        tokens = tm * pl.num_programs(0)
        for r in range(tm):
            for k in range(TOP_K):
                pltpu.make_async_copy(yb_hbm.at[pl.ds(slot_ref[k * tokens + blk * tm + r], 1)],
                                      bufs[par][k].at[pl.ds(r, 1)], sem.at[par]).start(priority=k)

    def wait(par):
        for k in range(TOP_K):
            pltpu.make_async_copy(yb_hbm.at[pl.ds(0, tm)], bufs[par][k], sem.at[par]).wait()

    @pl.when(i == 0)
    def _():
        gather(i, 0)

    def step(par):
        wait(par)
        gather(jnp.minimum(i + 1, last), 1 - par)
        w0 = w_ref[:, TOP_K:TOP_K + 1]
        w1 = w_ref[:, TOP_K + 1:TOP_K + 2]
        half = h_ref.shape[1] // 2
        ya = _unpack_halves(bufs[par][0][...])
        yb = _unpack_halves(bufs[par][1][...])
        h3 = [h_ref[:, p * half:(p + 1) * half] + w0 * ya[p] + w1 * yb[p] for p in range(2)]
        var = sum(jnp.sum(v * v, axis=-1, keepdims=True) for v in h3) / h_ref.shape[1]
        inv = lax.rsqrt(var + EPS)
        for p in range(2):
            o_ref[:, p * half:(p + 1) * half] = h3[p] * inv * nw_ref[:, p * half:(p + 1) * half]

        @pl.when(i == last)
        def _():
            wait(1 - par)

    for par in range(2):
        pl.when(i % 2 == par)(functools.partial(step, par))


def _combine(slot, h2, route, nw, yb, tm):
    t, d = h2.shape
    tm = min(tm, t)
    grid_spec = pltpu.PrefetchScalarGridSpec(
        num_scalar_prefetch=1,
        grid=(t // tm,),
        in_specs=[
            pl.BlockSpec((tm, d), lambda i, s: (i, 0)),
            pl.BlockSpec((tm, route.shape[1]), lambda i, s: (i, 0)),
            pl.BlockSpec((1, d), lambda i, s: (0, 0)),
            pl.BlockSpec(memory_space=pl.ANY),
        ],
        out_specs=pl.BlockSpec((tm, d), lambda i, s: (i, 0)),
        scratch_shapes=[pltpu.VMEM((tm, yb.shape[1]), yb.dtype)] * (2 * TOP_K) + [pltpu.SemaphoreType.DMA((2,))],
    )
    return pl.pallas_call(
        _combine_kernel,
        grid_spec=grid_spec,
        out_shape=jax.ShapeDtypeStruct((t, d), F32),
        compiler_params=_cparams(("arbitrary",)),
        name="combine",
    )(slot, h2, route, nw, yb)


def _slots(route, cnt, t):
    rt = route[:, :SUBLANES].T.astype(jnp.int32)
    e_kt = rt[:TOP_K]
    rank = rt[2 * TOP_K:3 * TOP_K]
    onehot = e_kt[None] == jnp.arange(N_EXPERTS, dtype=jnp.int32)[:, None, None]
    counts = cnt[0, :N_EXPERTS].astype(jnp.int32)
    padded = (counts + MOE_BLOCK - 1) // MOE_BLOCK * MOE_BLOCK
    pad_end = jnp.cumsum(padded)
    pad_start = pad_end - padded
    slot = (jnp.sum(jnp.where(onehot, pad_start[:, None, None], 0), axis=0) + rank).reshape(-1)
    n_slots = t * TOP_K + N_EXPERTS * MOE_BLOCK
    n_blocks = n_slots // MOE_BLOCK
    block_start = jnp.arange(n_blocks, dtype=jnp.int32) * MOE_BLOCK
    block_e = jnp.minimum(jnp.sum((pad_end[None, :] <= block_start[:, None]).astype(jnp.int32), axis=1),
                          N_EXPERTS - 1)
    n_valid = (pad_end[-1:] // MOE_BLOCK).astype(jnp.int32)
    return slot, pad_start + counts, padded - counts, block_e, n_valid, n_slots


def _pack_w_kernel(w_ref, o_ref, *, pieces):
    filled = 0
    for src0, src1, dst0 in pieces:
        o_ref[dst0:dst0 + src1 - src0, :] = w_ref[src0:src1, :].astype(o_ref.dtype)
        filled = max(filled, dst0 + src1 - src0)
    if filled < o_ref.shape[0]:
        o_ref[filled:, :] = jnp.zeros((o_ref.shape[0] - filled, o_ref.shape[1]), o_ref.dtype)


def _pack_w(wt, pieces, n_out, tk):
    n_in, k = wt.shape
    return pl.pallas_call(
        functools.partial(_pack_w_kernel, pieces=pieces),
        grid=(k // tk,),
        in_specs=[pl.BlockSpec((n_in, tk), lambda i: (0, i))],
        out_specs=pl.BlockSpec((n_out, tk), lambda i: (0, i)),
        out_shape=jax.ShapeDtypeStruct((n_out, k), BF16),
        compiler_params=_cparams(("parallel",)),
        name="pack_w",
    )(wt)


def _pad_lanes(v, n=LANES):
    return jnp.pad(v, ((0, 0), (0, n - v.shape[1])))


def kernel(x, mem, norm_mix_w, w_in, conv_w, conv_b, dt_bias, a_log, d_skip, ssd_norm_w, w_ssd_branch, w_sb_branch, w_mix_out, norm_xa_w, norm_mem_w, w_xq, w_xk, w_xv, w_xo, norm_moe_w, w_router_group, w_router_expert, w_expert_gate, w_expert_up, w_expert_down, norm_final_w):
    bsz, seq, d = x.shape
    depth = w_in.shape[0]
    t = bsz * seq
    inner = ssd_norm_w.shape[1]
    heads = dt_bias.shape[1]
    bc = SSD_GROUPS * SSD_STATE
    conv_dim = inner + 2 * bc
    sb_width = SB_HEADS * SB_HEAD_DIM
    col_dt = inner + conv_dim
    col_qkv = col_dt + heads
    col_g = col_qkv + 3 * sb_width
    g_col0 = 2 * inner
    bc_col0 = g_col0 + 2 * d
    q_col0 = bc_col0 + 2 * bc
    n_main = q_col0 + 3 * sb_width
    tn_proj = 1280
    n_proj = -(-n_main // tn_proj) * tn_proj
    tn_merge = 1024

    assert depth == 1, "single-layer configuration: the final RMSNorm is fused into the last MoE combine"
    h = x.reshape(t, d)
    for l in range(depth):
        w_t = jnp.swapaxes(w_in[l], 0, 1)
        w_main = _pack_w(w_t, [(0, 2 * inner, 0), (col_g, col_g + 2 * d, g_col0),
                               (2 * inner, col_dt, bc_col0), (col_qkv, col_g, q_col0)], n_proj, tk=256)
        w_dt = jnp.pad(w_t[col_dt:col_dt + heads], ((0, LANES - heads), (0, 0))).astype(BF16)
        e_mat = (jnp.arange(inner, dtype=jnp.int32)[None, :] // SSD_HEAD_DIM
                 == jnp.arange(LANES, dtype=jnp.int32)[:, None]).astype(BF16)
        dskip_x = jnp.repeat(d_skip[l], SSD_HEAD_DIM)[None, :]

        proj, dt_raw = _norm_matmul(h, norm_mix_w[l][None, :], w_main, w_dt, tm=1024, tn=tn_proj)
        y_ssd = _ssd(proj, dt_raw, conv_w[l], conv_b[l][None, :], _pad_lanes(dt_bias[l][None, :]),
                     _pad_lanes(a_log[l][None, :]), dskip_x, ssd_norm_w[l][None, :], e_mat,
                     bsz, seq, inner, bc, bc_col0 // bc)
        y_sb = _stickbreak(proj.reshape(bsz, seq, n_proj), bsz, seq, q_col0 // SB_HEAD_DIM)
        h1 = _merge(h, y_ssd, y_sb.reshape(t, sb_width), proj,
                    w_ssd_branch[l].astype(BF16), w_sb_branch[l].astype(BF16), w_mix_out[l].astype(BF16),
                    g_col0 // tn_merge, tm=512, tn=tn_merge)

        m_len = mem.shape[1]
        w_kv = jnp.concatenate([w_xk[l].T, w_xv[l].T], axis=0).astype(BF16)
        kv, _ = _norm_matmul(mem.reshape(bsz * m_len, d), norm_mem_w[l][None, :], w_kv,
                             jnp.zeros((LANES, d), BF16), tm=bsz * m_len, tn=512)
        w_r = _pad_lanes(jnp.concatenate([w_router_group[l], w_router_expert[l]], axis=1))
        wr_hi = w_r.astype(BF16)
        wr_lo = (w_r - wr_hi.astype(F32)).astype(BF16)
        h2, um, route, cnt = _xattn(h1, kv, norm_xa_w[l][None, :], w_xq[l].astype(BF16), w_xo[l].astype(BF16),
                               norm_moe_w[l][None, :], jnp.concatenate([wr_hi, wr_lo], axis=1), seq, tm=512)

        slot, pad_row, pad_n, block_e, n_valid, n_slots = _slots(route, cnt, t)
        xs = _dispatch(slot, pad_row, pad_n, n_valid, um, n_slots, tm=512)
        yb = _experts(block_e, n_valid, xs, w_expert_gate[l], w_expert_up[l], w_expert_down[l])
        h = _combine(slot, h2, route, norm_final_w[None, :], yb, tm=512)
    return h.reshape(bsz, seq, d)
```

```python
import functools

import jax
import jax.numpy as jnp
from jax import lax
from jax.experimental import pallas as pl
from jax.experimental.pallas import tpu as pltpu

F32 = jnp.float32
BF16 = jnp.bfloat16
U32 = jnp.uint32
EPS = 1e-6

SSD_HEAD_DIM = 64
SSD_GROUPS = 4
SSD_STATE = 128
SSD_CONV = 4
SSD_CHUNK = 128
SB_HEADS = 4
SB_HEAD_DIM = 128
SB_BLOCK = 128
SB_SUB = 8
XA_HEADS = 4
XA_HEAD_DIM = 128
N_GROUPS = 4
EXPERTS_PER_GROUP = 8
N_EXPERTS = N_GROUPS * EXPERTS_PER_GROUP
TOP_K = 2
MOE_BLOCK = 256

LANES = 128
SUBLANES = 8
ROW_TILE = SUBLANES
VMEM_LIMIT = 56 * 1024 * 1024

SB_EXIT_LOG = -88.0
NEG_BIG = -1e30


def _cparams(sem):
    return pltpu.CompilerParams(dimension_semantics=sem, vmem_limit_bytes=VMEM_LIMIT)


def _rms(x, w):
    var = jnp.mean(x * x, axis=-1, keepdims=True)
    return x * lax.rsqrt(var + EPS) * w


def _split2(v):
    hi = v.astype(BF16)
    lo = (v - hi.astype(F32)).astype(BF16)
    return hi, lo


def _dot(a, b):
    return jnp.dot(a, b, preferred_element_type=F32)


def _dot_nt(a, bt):
    return lax.dot_general(a, bt, (((1,), (1,)), ((), ())), preferred_element_type=F32)


def _pack_halves(v):
    n = v.shape[1] // 2
    lo = lax.bitcast_convert_type(v[:, :n].astype(BF16).astype(F32), U32)
    hi = lax.bitcast_convert_type(v[:, n:].astype(BF16).astype(F32), U32)
    return (lo >> 16) | hi


def _unpack_halves(w):
    lo = lax.bitcast_convert_type(w << 16, F32)
    hi = lax.bitcast_convert_type(w & jnp.uint32(0xFFFF0000), F32)
    return lo, hi


def _store_row_tiles(ref, packed):
    rows = packed.shape[0]
    for c in range(ROW_TILE):
        ref[pl.ds(c, rows, stride=ROW_TILE), :] = packed[:, c * LANES:(c + 1) * LANES]


def _row_tile(ref, row):
    start = row * ROW_TILE
    if not isinstance(row, int):
        start = pl.multiple_of(start, ROW_TILE)
    return ref.at[pl.ds(start, ROW_TILE)]


def _load_row_tile_chunk(ref, c, rows):
    return ref[pl.ds(c, rows, stride=ROW_TILE), :]


def _silu(x):
    return x / (1.0 + jnp.exp(-x))


def _softplus(x):
    return jnp.maximum(x, 0.0) + jnp.log(1.0 + jnp.exp(-jnp.abs(x)))


def _norm_matmul_kernel(x_ref, nw_ref, w_ref, ws_ref, o_ref, os_ref, u_ref):
    @pl.when(pl.program_id(1) == 0)
    def _():
        u = _rms(x_ref[...], nw_ref[...]).astype(BF16)
        u_ref[...] = u
        os_ref[...] = _dot_nt(u, ws_ref[...])

    o_ref[...] = _dot_nt(u_ref[...], w_ref[...]).astype(o_ref.dtype)


def _norm_matmul(x, nw, wt, wt_side, tm, tn):
    m, k = x.shape
    n = wt.shape[0]
    ns = wt_side.shape[0]
    tm = min(tm, m)
    return pl.pallas_call(
        _norm_matmul_kernel,
        grid=(m // tm, n // tn),
        in_specs=[
            pl.BlockSpec((tm, k), lambda i, j: (i, 0)),
            pl.BlockSpec((1, k), lambda i, j: (0, 0)),
            pl.BlockSpec((tn, k), lambda i, j: (j, 0)),
            pl.BlockSpec((ns, k), lambda i, j: (0, 0)),
        ],
        out_specs=[
            pl.BlockSpec((tm, tn), lambda i, j: (i, j)),
            pl.BlockSpec((tm, ns), lambda i, j: (i, 0)),
        ],
        out_shape=[jax.ShapeDtypeStruct((m, n), BF16), jax.ShapeDtypeStruct((m, ns), F32)],
        scratch_shapes=[pltpu.VMEM((tm, k), BF16)],
        compiler_params=_cparams(("parallel", "arbitrary")),
        name="norm_matmul",
    )(x, nw, wt, wt_side)


def _ssd_kernel(z_ref, xs_ref, b_ref, c_ref, dtr_ref, cw_ref, cb_ref, dtb_ref, alog_ref, dsk_ref,
                nw_ref, e_ref, o_ref,
                tail_ref, win_ref, state_ref, xact_ref, bact_ref, cact_ref, dtx_ref, eax_ref, wsx_ref, y_ref):
    L = SSD_CHUNK
    inner = xs_ref.shape[1]
    bc = b_ref.shape[1]
    n_pairs = inner // LANES

    @pl.when(pl.program_id(1) == 0)
    def _():
        tail_ref[...] = jnp.zeros_like(tail_ref)
        state_ref[...] = jnp.zeros_like(state_ref)


    def conv_seg(src_ref, col0, width, dst_ref):
        for j in range(0, width, 2 * LANES):
            cols = slice(col0 + j, col0 + j + 2 * LANES)
            xin = src_ref[:, j:j + 2 * LANES].astype(F32)
            win_ref[:SUBLANES, :] = tail_ref[:, cols]
            win_ref[SUBLANES:, :] = xin
            acc = xin * cw_ref[SSD_CONV - 1:SSD_CONV, cols] + cb_ref[:, cols]
            for s in range(1, SSD_CONV):
                acc = acc + win_ref[SUBLANES - s:SUBLANES - s + L, :] * cw_ref[SSD_CONV - 1 - s:SSD_CONV - s, cols]
            tail_ref[:, cols] = xin[L - SUBLANES:]
            dst_ref[:, j:j + 2 * LANES] = _silu(acc).astype(dst_ref.dtype)

    conv_seg(xs_ref, 0, inner, xact_ref)
    conv_seg(b_ref, inner, bc, bact_ref)
    conv_seg(c_ref, inner + bc, bc, cact_ref)

    dt = _softplus(dtr_ref[...] + dtb_ref[...])
    da = dt * (-jnp.exp(alog_ref[...]))
    rowl = lax.broadcasted_iota(jnp.int32, (L, L), 0)
    coll = lax.broadcasted_iota(jnp.int32, (L, L), 1)
    lower = rowl >= coll
    tri = jnp.where(lower, 1.0, 0.0).astype(BF16)
    d1 = da.astype(BF16)
    r1 = da - d1.astype(F32)
    d2 = r1.astype(BF16)
    d3 = (r1 - d2.astype(F32)).astype(BF16)
    a_cum = _dot(tri, d1) + _dot(tri, d2) + _dot(tri, d3)
    a_cum_t = a_cum.T
    a_last = a_cum[L - 1:L, :]
    e_mat = e_ref[...]

    def expand(v):
        hi, lo = _split2(v)
        return _dot(hi, e_mat) + _dot(lo, e_mat)

    dtx_ref[...] = expand(dt)
    eax_ref[...] = expand(jnp.exp(a_cum))
    wsx_ref[...] = expand(jnp.exp(a_last - a_cum))
    elx = expand(jnp.broadcast_to(jnp.exp(a_last), (SUBLANES, LANES)))[0:1]

    lane = lax.broadcasted_iota(jnp.int32, (L, LANES), 1)
    pairs_per_group = n_pairs // SSD_GROUPS
    for g in range(SSD_GROUPS):
        gcols = slice(g * SSD_STATE, (g + 1) * SSD_STATE)
        bg_t = bact_ref[:, gcols].T.astype(BF16)
        cg = cact_ref[:, gcols]
        cb = _dot(cg, bg_t)
        for pp in range(pairs_per_group):
            p = g * pairs_per_group + pp
            pc = slice(p * LANES, (p + 1) * LANES)
            ms = []
            for hh in (2 * p, 2 * p + 1):
                seg = a_cum[:, hh:hh + 1] - a_cum_t[hh:hh + 1, :]
                dec = jnp.exp(jnp.where(lower, seg, NEG_BIG))
                ms.append((cb * dec).astype(BF16))
            lhs = jnp.concatenate(ms, axis=1)
            xs_p = xact_ref[:, pc]
            xdt = xs_p * dtx_ref[:, pc]
            rhs = jnp.concatenate([jnp.where(lane < SSD_HEAD_DIM, xdt, 0.0).astype(BF16),
                                   jnp.where(lane >= SSD_HEAD_DIM, xdt, 0.0).astype(BF16)], axis=0)
            st = state_ref[p]
            y = _dot(lhs, rhs)
            y = y + _dot(cg, st.astype(BF16)) * eax_ref[:, pc]
            y = y + xs_p * dsk_ref[:, pc]
            y_ref[:, pc] = y
            xw = (xdt * wsx_ref[:, pc]).astype(BF16)
            state_ref[p] = st * elx[:, pc] + _dot(bg_t, xw)

    yg = y_ref[...] * _silu(z_ref[...].astype(F32))
    o_ref[...] = _rms(yg, nw_ref[...]).astype(o_ref.dtype)


def _ssd(proj, dt_raw, conv_w, conv_b, dt_bias_p, a_log_p, dskip_x, norm_w, e_mat, bsz, seq, inner, bc, b_blk):
    L = SSD_CHUNK
    nc = seq // L
    conv_dim = inner + 2 * bc
    n_pairs = inner // LANES

    def rows(b, c):
        return b * nc + c

    return pl.pallas_call(
        _ssd_kernel,
        grid=(bsz, nc),
        in_specs=[
            pl.BlockSpec((L, inner), lambda b, c: (rows(b, c), 0)),
            pl.BlockSpec((L, inner), lambda b, c: (rows(b, c), 1)),
            pl.BlockSpec((L, bc), lambda b, c: (rows(b, c), b_blk)),
            pl.BlockSpec((L, bc), lambda b, c: (rows(b, c), b_blk + 1)),
            pl.BlockSpec((L, LANES), lambda b, c: (rows(b, c), 0)),
            pl.BlockSpec((SSD_CONV, conv_dim), lambda b, c: (0, 0)),
            pl.BlockSpec((1, conv_dim), lambda b, c: (0, 0)),
            pl.BlockSpec((1, LANES), lambda b, c: (0, 0)),
            pl.BlockSpec((1, LANES), lambda b, c: (0, 0)),
            pl.BlockSpec((1, inner), lambda b, c: (0, 0)),
            pl.BlockSpec((1, inner), lambda b, c: (0, 0)),
            pl.BlockSpec((LANES, inner), lambda b, c: (0, 0)),
        ],
        out_specs=pl.BlockSpec((L, inner), lambda b, c: (rows(b, c), 0)),
        out_shape=jax.ShapeDtypeStruct((bsz * seq, inner), BF16),
        scratch_shapes=[
            pltpu.VMEM((SUBLANES, conv_dim), F32),
            pltpu.VMEM((SUBLANES + L, 2 * LANES), F32),
            pltpu.VMEM((n_pairs, SSD_STATE, LANES), F32),
            pltpu.VMEM((L, inner), F32),
            pltpu.VMEM((L, bc), F32),
            pltpu.VMEM((L, bc), BF16),
            pltpu.VMEM((L, inner), F32),
            pltpu.VMEM((L, inner), F32),
            pltpu.VMEM((L, inner), F32),
            pltpu.VMEM((L, inner), F32),
        ],
        compiler_params=_cparams(("parallel", "arbitrary")),
        name="ssd",
    )(proj, proj, proj, proj, dt_raw, conv_w, conv_b, dt_bias_p, a_log_p, dskip_x, norm_w, e_mat)


def _sb_kernel(q_ref, k_ref, v_ref, o_ref, acc_ref, carry_ref):
    BL = SB_BLOCK
    i = pl.program_id(2)
    scale = SB_HEAD_DIM ** -0.5
    row = lax.broadcasted_iota(jnp.int32, (BL, BL), 0)
    col = lax.broadcasted_iota(jnp.int32, (BL, BL), 1)
    causal = col < row
    r2 = lax.broadcasted_iota(jnp.int32, (BL, 2 * BL), 0)
    c2 = lax.broadcasted_iota(jnp.int32, (BL, 2 * BL), 1)
    tri = jnp.where((r2 > c2) | (c2 >= BL), 1.0, 0.0).astype(BF16)

    def sweep(s, first):
        subs = range(SB_SUB)
        kbs = [i * SB_SUB + j - s for j in subs]
        starts = [pl.multiple_of(jnp.maximum(kb, 0) * BL, BL) for kb in kbs]
        zls = [lax.dot_general(q_ref[j * BL:(j + 1) * BL, :], k_ref[pl.ds(starts[j], BL), :],
                               (((1,), (1,)), ((), ())), preferred_element_type=F32) * scale for j in subs]
        sps = [_softplus(zl) for zl in zls]
        log_betas = [zl - sp for zl, sp in zip(zls, sps)]
        log_keeps = [jnp.where(causal, -sp, 0.0) if first else -sp for sp in sps]
        splits = [_split2(lk) for lk in log_keeps]
        t2s = [_dot(hi, tri) + _dot(lo, tri) for hi, lo in splits]
        if first:
            atts = [jnp.where(causal, jnp.exp(lb + t2[:, :BL]), 0.0) for lb, t2 in zip(log_betas, t2s)]
            carries = [t2[:, BL:] for t2 in t2s]
        else:
            olds = [carry_ref[j] for j in subs]
            atts = [jnp.where(kb >= 0, jnp.exp(lb + t2[:, :BL] + old), 0.0)
                    for kb, lb, t2, old in zip(kbs, log_betas, t2s, olds)]
            carries = [old + jnp.where(kb >= 0, t2[:, BL:], 0.0) for kb, t2, old in zip(kbs, t2s, olds)]
        pvs = [_dot(atts[j].astype(BF16), v_ref[pl.ds(starts[j], BL), :]) for j in subs]
        worst = None
        for j in subs:
            if first:
                acc_ref[j] = pvs[j]
            else:
                acc_ref[j] += pvs[j]
            carry_ref[j] = carries[j]
            live = jnp.where(kbs[j] >= 1, carries[j], NEG_BIG)
            worst = live if worst is None else jnp.maximum(worst, live)
        return jnp.max(worst) > SB_EXIT_LOG

    sweep(0, True)
    go = sweep(1, False)

    def body(st):
        s, _ = st
        return s + 1, sweep(s, False)

    lax.while_loop(lambda st: st[1], body, (jnp.int32(2), go))
    for j in range(SB_SUB):
        o_ref[j * BL:(j + 1) * BL, :] = acc_ref[j].astype(o_ref.dtype)


def _stickbreak(proj3, bsz, seq, q_blk0):
    tq = SB_SUB * SB_BLOCK
    return pl.pallas_call(
        _sb_kernel,
        grid=(bsz, SB_HEADS, seq // tq),
        in_specs=[
            pl.BlockSpec((None, tq, SB_HEAD_DIM), lambda b, h, i: (b, i, q_blk0 + h)),
            pl.BlockSpec((None, seq, SB_HEAD_DIM), lambda b, h, i: (b, 0, q_blk0 + SB_HEADS + h)),
            pl.BlockSpec((None, seq, SB_HEAD_DIM), lambda b, h, i: (b, 0, q_blk0 + 2 * SB_HEADS + h)),
        ],
        out_specs=pl.BlockSpec((None, tq, SB_HEAD_DIM), lambda b, h, i: (b, i, h)),
        out_shape=jax.ShapeDtypeStruct((bsz, seq, SB_HEADS * SB_HEAD_DIM), BF16),
        scratch_shapes=[pltpu.VMEM((SB_SUB, SB_BLOCK, SB_HEAD_DIM), F32),
                        pltpu.VMEM((SB_SUB, SB_BLOCK, SB_BLOCK), F32)],
        compiler_params=_cparams(("parallel", "parallel", "arbitrary")),
        name="stickbreak",
    )(proj3, proj3, proj3)


def _merge_kernel(x_ref, ys_ref, yb_ref, *rest):
    *g_refs, w1_ref, w2_ref, wm_ref, o_ref = rest
    nj = len(g_refs) // 2
    tn = g_refs[0].shape[1]
    ys = ys_ref[...]
    yb = yb_ref[...]
    ms = []
    for j in range(nj):
        cs = slice(j * tn, (j + 1) * tn)
        s1 = jax.nn.sigmoid(g_refs[j][...].astype(F32))
        s2 = jax.nn.sigmoid(g_refs[nj + j][...].astype(F32))
        ms.append((s1 * _dot(ys, w1_ref[:, cs]) + s2 * _dot(yb, w2_ref[:, cs])).astype(BF16))
    o_ref[...] = x_ref[...] + _dot(jnp.concatenate(ms, axis=1), wm_ref[...])


def _merge(x2, y_ssd, y_sb, proj, w1, w2, wm, g_blk0, tm, tn):
    t, d = x2.shape
    tm = min(tm, t)
    nj = d // tn
    resident = pl.Buffered(1)
    g_specs = [pl.BlockSpec((tm, tn), lambda i, b=g_blk0 + b: (i, b)) for b in range(2 * nj)]
    return pl.pallas_call(
        _merge_kernel,
        grid=(t // tm,),
        in_specs=[
            pl.BlockSpec((tm, d), lambda i: (i, 0)),
            pl.BlockSpec((tm, y_ssd.shape[1]), lambda i: (i, 0)),
            pl.BlockSpec((tm, y_sb.shape[1]), lambda i: (i, 0)),
            *g_specs,
            pl.BlockSpec(w1.shape, lambda i: (0, 0), pipeline_mode=resident),
            pl.BlockSpec(w2.shape, lambda i: (0, 0), pipeline_mode=resident),
            pl.BlockSpec(wm.shape, lambda i: (0, 0), pipeline_mode=resident),
        ],
        out_specs=pl.BlockSpec((tm, d), lambda i: (i, 0)),
        out_shape=jax.ShapeDtypeStruct((t, d), F32),
        compiler_params=_cparams(("parallel",)),
        name="merge",
    )(x2, y_ssd, y_sb, *([proj] * (2 * nj)), w1, w2, wm)


def _xattn_kernel(h_ref, kv_ref, nxa_ref, wq_ref, wo_ref, nmoe_ref, wr_ref, h2_ref, um_ref, rt_ref, cnt_ref,
                  base_ref):
    h1 = h_ref[...]
    un = _rms(h1, nxa_ref[...]).astype(BF16)
    q = _dot(un, wq_ref[...]).astype(BF16)
    width = XA_HEADS * XA_HEAD_DIM
    outs = []
    for hd in range(XA_HEADS):
        cs = slice(hd * XA_HEAD_DIM, (hd + 1) * XA_HEAD_DIM)
        k = kv_ref[:, cs]
        v = kv_ref[:, width + hd * XA_HEAD_DIM: width + (hd + 1) * XA_HEAD_DIM]
        sc = lax.dot_general(q[:, cs], k, (((1,), (1,)), ((), ())),
                             preferred_element_type=F32) * (XA_HEAD_DIM ** -0.5)
        sc = sc - jnp.max(sc, axis=-1, keepdims=True)
        p = jnp.exp(sc)
        p = p / jnp.sum(p, axis=-1, keepdims=True)
        outs.append(_dot(p.astype(BF16), v).astype(BF16))
    o = jnp.concatenate(outs, axis=1)
    h2 = h1 + _dot(o, wo_ref[...])
    h2_ref[...] = h2
    um = _rms(h2, nmoe_ref[...])
    _store_row_tiles(um_ref, _pack_halves(um))
    hi, lo = _split2(um)
    nr = rt_ref.shape[1]
    both = _dot(hi, wr_ref[...])
    lg = both[:, :nr] + both[:, nr:] + _dot(lo, wr_ref[:, :nr])

    lane = lax.broadcasted_iota(jnp.int32, lg.shape, 1)
    lane_f = lane.astype(F32)
    is_g = lane < N_GROUPS
    gl = jnp.where(is_g, lg, NEG_BIG)
    gmax = jnp.max(gl, axis=-1, keepdims=True)
    g_sel = jnp.min(jnp.where(gl == gmax, lane_f, float(LANES)), axis=-1, keepdims=True)
    g_gate = 1.0 / jnp.sum(jnp.where(is_g, jnp.exp(gl - gmax), 0.0), axis=-1, keepdims=True)
    lo_lane = N_GROUPS + EXPERTS_PER_GROUP * g_sel
    el = jnp.where((lane_f >= lo_lane) & (lane_f < lo_lane + EXPERTS_PER_GROUP), lg, NEG_BIG)
    m1 = jnp.max(el, axis=-1, keepdims=True)
    i1 = jnp.min(jnp.where(el == m1, lane_f, float(LANES)), axis=-1, keepdims=True)
    el2 = jnp.where(lane_f == i1, NEG_BIG, el)
    m2 = jnp.max(el2, axis=-1, keepdims=True)
    i2 = jnp.min(jnp.where(el2 == m2, lane_f, float(LANES)), axis=-1, keepdims=True)
    ex = jnp.exp(m2 - m1)
    p1 = 1.0 / (1.0 + ex)
    e1 = i1 - N_GROUPS
    e2 = i2 - N_GROUPS

    @pl.when(pl.program_id(0) == 0)
    def _():
        base_ref[...] = jnp.zeros_like(base_ref)

    tm = lg.shape[0]
    hot1 = lane_f == e1
    hot2 = lane_f == e2
    onehot = jnp.where(hot1 | hot2, 1.0, 0.0)
    tr = lax.broadcasted_iota(jnp.int32, (tm, tm), 0)
    tc = lax.broadcasted_iota(jnp.int32, (tm, tm), 1)
    before = _dot(jnp.where(tc < tr, 1.0, 0.0).astype(BF16), onehot.astype(BF16)) + base_ref[...]
    r1 = jnp.sum(jnp.where(hot1, before, 0.0), axis=-1, keepdims=True)
    r2 = jnp.sum(jnp.where(hot2, before, 0.0), axis=-1, keepdims=True)
    base_ref[...] += jnp.sum(onehot, axis=0, keepdims=True)
    cnt_ref[...] = jnp.broadcast_to(base_ref[...], cnt_ref.shape)

    vals = (e1, e2, p1 * g_gate, ex * p1 * g_gate, r1, r2)
    out = jnp.zeros_like(lg)
    for j, v in enumerate(vals):
        out = jnp.where(lane == j, v, out)
    rt_ref[...] = out


def _xattn(h1, kv, nxa, wq, wo, nmoe, wr, seq, tm):
    t, d = h1.shape
    tm = min(tm, seq)
    per_b = seq // tm
    m_len = kv.shape[0] // (t // seq)
    nr = wr.shape[1] // 2
    return pl.pallas_call(
        _xattn_kernel,
        grid=(t // tm,),
        in_specs=[
            pl.BlockSpec((tm, d), lambda i: (i, 0)),
            pl.BlockSpec((m_len, kv.shape[1]), lambda i: (i // per_b, 0)),
            pl.BlockSpec((1, d), lambda i: (0, 0)),
            pl.BlockSpec(wq.shape, lambda i: (0, 0)),
            pl.BlockSpec(wo.shape, lambda i: (0, 0)),
            pl.BlockSpec((1, d), lambda i: (0, 0)),
            pl.BlockSpec((d, 2 * nr), lambda i: (0, 0)),
        ],
        out_specs=[
            pl.BlockSpec((tm, d), lambda i: (i, 0)),
            pl.BlockSpec((tm * ROW_TILE, LANES), lambda i: (i, 0)),
            pl.BlockSpec((tm, nr), lambda i: (i, 0)),
            pl.BlockSpec((SUBLANES, nr), lambda i: (0, 0)),
        ],
        out_shape=[jax.ShapeDtypeStruct((t, d), F32), jax.ShapeDtypeStruct((t * ROW_TILE, LANES), U32),
                   jax.ShapeDtypeStruct((t, nr), F32), jax.ShapeDtypeStruct((SUBLANES, nr), F32)],
        scratch_shapes=[pltpu.VMEM((1, nr), F32)],
        compiler_params=_cparams(("arbitrary",)),
        name="xattn",
    )(h1, kv, nxa, wq, wo, nmoe, wr)


def _dispatch_kernel(slot_ref, padrow_ref, padn_ref, nv_ref, um_ref, xs_hbm, sa_ref, sb_ref, z_ref, sem, zsem):
    i = pl.program_id(0)
    tm = um_ref.shape[0] // ROW_TILE
    last = pl.num_programs(0) - 1
    stage = (sa_ref, sb_ref)

    def zero_rows(start):
        for e in range(N_EXPERTS):
            def body(j, c, e=e):
                cp = pltpu.make_async_copy(_row_tile(z_ref, 0), _row_tile(xs_hbm, padrow_ref[e] + j), zsem)
                if start:
                    cp.start()
                else:
                    cp.wait()
                return c

            lax.fori_loop(0, padn_ref[e], body, 0)

        def tail(b, c):
            blk = MOE_BLOCK * ROW_TILE
            cp = pltpu.make_async_copy(z_ref, xs_hbm.at[pl.ds(pl.multiple_of(b * blk, blk), blk)], zsem)
            if start:
                cp.start()
            else:
                cp.wait()
            return c

        lax.fori_loop(nv_ref[0], xs_hbm.shape[0] // (MOE_BLOCK * ROW_TILE), tail, 0)

    @pl.when(i == 0)
    def _():
        z_ref[...] = jnp.zeros_like(z_ref)
        zero_rows(True)

    def wait(par):
        for _ in range(TOP_K):
            pltpu.make_async_copy(stage[par], xs_hbm.at[pl.ds(0, tm * ROW_TILE)], sem.at[par]).wait()

    def step(par):
        pl.when(i >= 2)(functools.partial(wait, par))
        stage[par][...] = um_ref[...]
        tokens = tm * pl.num_programs(0)
        for r in range(tm):
            for k in range(TOP_K):
                pltpu.make_async_copy(_row_tile(stage[par], r),
                                      _row_tile(xs_hbm, slot_ref[k * tokens + i * tm + r]),
                                      sem.at[par]).start(priority=k)

        @pl.when(i == last)
        def _():
            wait(par)
            pl.when(i >= 1)(functools.partial(wait, 1 - par))

    for par in range(2):
        pl.when(i % 2 == par)(functools.partial(step, par))

    @pl.when(i == last)
    def _():
        zero_rows(False)


def _dispatch(slot, pad_row, pad_n, n_valid, um, n_slots, tm):
    t = um.shape[0] // ROW_TILE
    tm = min(tm, t)
    rows = tm * ROW_TILE
    grid_spec = pltpu.PrefetchScalarGridSpec(
        num_scalar_prefetch=4,
        grid=(t // tm,),
        in_specs=[pl.BlockSpec((rows, LANES), lambda i, s, pr, pn, nv: (i, 0))],
        out_specs=pl.BlockSpec(memory_space=pl.ANY),
        scratch_shapes=[pltpu.VMEM((rows, LANES), um.dtype), pltpu.VMEM((rows, LANES), um.dtype),
                        pltpu.VMEM((MOE_BLOCK * ROW_TILE, LANES), um.dtype),
                        pltpu.SemaphoreType.DMA((2,)), pltpu.SemaphoreType.DMA(())],
    )
    return pl.pallas_call(
        _dispatch_kernel,
        grid_spec=grid_spec,
        out_shape=jax.ShapeDtypeStruct((n_slots * ROW_TILE, LANES), um.dtype),
        compiler_params=_cparams(("arbitrary",)),
        name="dispatch",
    )(slot, pad_row, pad_n, n_valid, um)


def _expert_kernel(be_ref, nv_ref, x_ref, wg_ref, wu_ref, wd_ref, o_ref, wgb_ref, wub_ref, wdb_ref, xb_ref):
    i = pl.program_id(0)

    @pl.when(jnp.logical_or(i == 0, be_ref[i] != be_ref[jnp.maximum(i - 1, 0)]))
    def _():
        wgb_ref[...] = wg_ref[...].astype(BF16)
        wub_ref[...] = wu_ref[...].astype(BF16)
        wdb_ref[...] = wd_ref[...].astype(BF16)

    @pl.when(i < nv_ref[0])
    def _():
        half = xb_ref.shape[1] // 2
        for c in range(ROW_TILE):
            lo, hi = _unpack_halves(_load_row_tile_chunk(x_ref, c, MOE_BLOCK))
            xb_ref[:, c * LANES:(c + 1) * LANES] = lo.astype(BF16)
            xb_ref[:, half + c * LANES:half + (c + 1) * LANES] = hi.astype(BF16)
        xb = xb_ref[...]
        hid = (_silu(_dot(xb, wgb_ref[...])) * _dot(xb, wub_ref[...])).astype(BF16)
        _store_row_tiles(o_ref, _pack_halves(_dot(hid, wdb_ref[...])))

    @pl.when(i >= nv_ref[0])
    def _():
        o_ref[...] = jnp.zeros_like(o_ref)


def _experts(block_e, n_valid, xs, wg, wu, wd):
    blk = MOE_BLOCK * ROW_TILE
    n_blocks = xs.shape[0] // blk
    d, ff = wg.shape[1:]

    def x_map(i, be, nv):
        return (jnp.minimum(i, nv[0] - 1), 0)

    grid_spec = pltpu.PrefetchScalarGridSpec(
        num_scalar_prefetch=2,
        grid=(n_blocks,),
        in_specs=[
            pl.BlockSpec((blk, LANES), x_map),
            pl.BlockSpec((None, d, ff), lambda i, be, nv: (be[i], 0, 0)),
            pl.BlockSpec((None, d, ff), lambda i, be, nv: (be[i], 0, 0)),
            pl.BlockSpec((None, ff, d), lambda i, be, nv: (be[i], 0, 0)),
        ],
        out_specs=pl.BlockSpec((blk, LANES), lambda i, be, nv: (i, 0)),
        scratch_shapes=[pltpu.VMEM((d, ff), BF16), pltpu.VMEM((d, ff), BF16), pltpu.VMEM((ff, d), BF16),
                        pltpu.VMEM((MOE_BLOCK, d), BF16)],
    )
    return pl.pallas_call(
        _expert_kernel,
        grid_spec=grid_spec,
        out_shape=jax.ShapeDtypeStruct(xs.shape, U32),
        compiler_params=_cparams(("arbitrary",)),
        name="experts",
    )(block_e, n_valid, xs, wg, wu, wd)


def _combine_kernel(slot_ref, h_ref, w_ref, nw_ref, yb_hbm, o_ref, ya0_ref, ya1_ref, yb0_ref, yb1_ref, sem):
    i = pl.program_id(0)
    tm = h_ref.shape[0]
    last = pl.num_programs(0) - 1
    bufs = ((ya0_ref, ya1_ref), (yb0_ref, yb1_ref))

    def gather(blk, par):
        tokens = tm * pl.num_programs(0)
        for r in range(tm):
            for k in range(TOP_K):
                pltpu.make_async_copy(_row_tile(yb_hbm, slot_ref[k * tokens + blk * tm + r]),
                                      _row_tile(bufs[par][k], r), sem.at[par]).start(priority=k)

    def wait(par):
        for k in range(TOP_K):
            pltpu.make_async_copy(yb_hbm.at[pl.ds(0, tm * ROW_TILE)], bufs[par][k], sem.at[par]).wait()

    @pl.when(i == 0)
    def _():
        gather(i, 0)

    def step(par):
        wait(par)
        gather(jnp.minimum(i + 1, last), 1 - par)
        w0 = w_ref[:, TOP_K:TOP_K + 1]
        w1 = w_ref[:, TOP_K + 1:TOP_K + 2]
        half = h_ref.shape[1] // 2
        sq = jnp.zeros((tm, LANES), F32)
        for c in range(ROW_TILE):
            ya = _unpack_halves(_load_row_tile_chunk(bufs[par][0], c, tm))
            yb = _unpack_halves(_load_row_tile_chunk(bufs[par][1], c, tm))
            for p in range(2):
                cs = slice(p * half + c * LANES, p * half + (c + 1) * LANES)
                h3 = h_ref[:, cs] + w0 * ya[p] + w1 * yb[p]
                sq = sq + h3 * h3
                o_ref[:, cs] = h3
        inv = lax.rsqrt(jnp.sum(sq, axis=-1, keepdims=True) / h_ref.shape[1] + EPS)
        o_ref[...] = o_ref[...] * inv * nw_ref[...]

        @pl.when(i == last)
        def _():
            wait(1 - par)

    for par in range(2):
        pl.when(i % 2 == par)(functools.partial(step, par))


def _combine(slot, h2, route, nw, yb, tm):
    t, d = h2.shape
    tm = min(tm, t)
    grid_spec = pltpu.PrefetchScalarGridSpec(
        num_scalar_prefetch=1,
        grid=(t // tm,),
        in_specs=[
            pl.BlockSpec((tm, d), lambda i, s: (i, 0)),
            pl.BlockSpec((tm, route.shape[1]), lambda i, s: (i, 0)),
            pl.BlockSpec((1, d), lambda i, s: (0, 0)),
            pl.BlockSpec(memory_space=pl.ANY),
        ],
        out_specs=pl.BlockSpec((tm, d), lambda i, s: (i, 0)),
        scratch_shapes=([pltpu.VMEM((tm * ROW_TILE, LANES), yb.dtype)] * (2 * TOP_K)
                        + [pltpu.SemaphoreType.DMA((2,))]),
    )
    return pl.pallas_call(
        _combine_kernel,
        grid_spec=grid_spec,
        out_shape=jax.ShapeDtypeStruct((t, d), F32),
        compiler_params=_cparams(("arbitrary",)),
        name="combine",
    )(slot, h2, route, nw, yb)


def _slots(route, cnt, t):
    rt = route[:, :SUBLANES].T.astype(jnp.int32)
    e_kt = rt[:TOP_K]
    rank = rt[2 * TOP_K:3 * TOP_K]
    onehot = e_kt[None] == jnp.arange(N_EXPERTS, dtype=jnp.int32)[:, None, None]
    counts = cnt[0, :N_EXPERTS].astype(jnp.int32)
    padded = (counts + MOE_BLOCK - 1) // MOE_BLOCK * MOE_BLOCK
    pad_end = jnp.cumsum(padded)
    pad_start = pad_end - padded
    slot = (jnp.sum(jnp.where(onehot, pad_start[:, None, None], 0), axis=0) + rank).reshape(-1)
    n_slots = t * TOP_K + N_EXPERTS * MOE_BLOCK
    n_blocks = n_slots // MOE_BLOCK
    block_start = jnp.arange(n_blocks, dtype=jnp.int32) * MOE_BLOCK
    block_e = jnp.minimum(jnp.sum((pad_end[None, :] <= block_start[:, None]).astype(jnp.int32), axis=1),
                          N_EXPERTS - 1)
    n_valid = (pad_end[-1:] // MOE_BLOCK).astype(jnp.int32)
    return slot, pad_start + counts, padded - counts, block_e, n_valid, n_slots


def _pack_w_kernel(w_ref, o_ref, *, pieces):
    filled = 0
    for src0, src1, dst0 in pieces:
        o_ref[dst0:dst0 + src1 - src0, :] = w_ref[src0:src1, :].astype(o_ref.dtype)
        filled = max(filled, dst0 + src1 - src0)
    if filled < o_ref.shape[0]:
        o_ref[filled:, :] = jnp.zeros((o_ref.shape[0] - filled, o_ref.shape[1]), o_ref.dtype)


def _pack_w(wt, pieces, n_out, tk):
    n_in, k = wt.shape
    return pl.pallas_call(
        functools.partial(_pack_w_kernel, pieces=pieces),
        grid=(k // tk,),
        in_specs=[pl.BlockSpec((n_in, tk), lambda i: (0, i))],
        out_specs=pl.BlockSpec((n_out, tk), lambda i: (0, i)),
        out_shape=jax.ShapeDtypeStruct((n_out, k), BF16),
        compiler_params=_cparams(("parallel",)),
        name="pack_w",
    )(wt)


def _pad_lanes(v, n=LANES):
    return jnp.pad(v, ((0, 0), (0, n - v.shape[1])))


def kernel(x, mem, norm_mix_w, w_in, conv_w, conv_b, dt_bias, a_log, d_skip, ssd_norm_w, w_ssd_branch, w_sb_branch, w_mix_out, norm_xa_w, norm_mem_w, w_xq, w_xk, w_xv, w_xo, norm_moe_w, w_router_group, w_router_expert, w_expert_gate, w_expert_up, w_expert_down, norm_final_w):
    bsz, seq, d = x.shape
    depth = w_in.shape[0]
    t = bsz * seq
    inner = ssd_norm_w.shape[1]
    heads = dt_bias.shape[1]
    bc = SSD_GROUPS * SSD_STATE
    conv_dim = inner + 2 * bc
    sb_width = SB_HEADS * SB_HEAD_DIM
    col_dt = inner + conv_dim
    col_qkv = col_dt + heads
    col_g = col_qkv + 3 * sb_width
    g_col0 = 2 * inner
    bc_col0 = g_col0 + 2 * d
    q_col0 = bc_col0 + 2 * bc
    n_main = q_col0 + 3 * sb_width
    tn_proj = 1280
    n_proj = -(-n_main // tn_proj) * tn_proj
    tn_merge = 1024

    assert depth == 1, "single-layer configuration: the final RMSNorm is fused into the last MoE combine"
    h = x.reshape(t, d)
    for l in range(depth):
        w_t = jnp.swapaxes(w_in[l], 0, 1)
        w_main = _pack_w(w_t, [(0, 2 * inner, 0), (col_g, col_g + 2 * d, g_col0),
                               (2 * inner, col_dt, bc_col0), (col_qkv, col_g, q_col0)], n_proj, tk=256)
        w_dt = jnp.pad(w_t[col_dt:col_dt + heads], ((0, LANES - heads), (0, 0))).astype(BF16)
        e_mat = (jnp.arange(inner, dtype=jnp.int32)[None, :] // SSD_HEAD_DIM
                 == jnp.arange(LANES, dtype=jnp.int32)[:, None]).astype(BF16)
        dskip_x = jnp.repeat(d_skip[l], SSD_HEAD_DIM)[None, :]

        proj, dt_raw = _norm_matmul(h, norm_mix_w[l][None, :], w_main, w_dt, tm=1024, tn=tn_proj)
        y_ssd = _ssd(proj, dt_raw, conv_w[l], conv_b[l][None, :], _pad_lanes(dt_bias[l][None, :]),
                     _pad_lanes(a_log[l][None, :]), dskip_x, ssd_norm_w[l][None, :], e_mat,
                     bsz, seq, inner, bc, bc_col0 // bc)
        y_sb = _stickbreak(proj.reshape(bsz, seq, n_proj), bsz, seq, q_col0 // SB_HEAD_DIM)
        h1 = _merge(h, y_ssd, y_sb.reshape(t, sb_width), proj,
                    w_ssd_branch[l].astype(BF16), w_sb_branch[l].astype(BF16), w_mix_out[l].astype(BF16),
                    g_col0 // tn_merge, tm=512, tn=tn_merge)

        m_len = mem.shape[1]
        w_kv = jnp.concatenate([w_xk[l].T, w_xv[l].T], axis=0).astype(BF16)
        kv, _ = _norm_matmul(mem.reshape(bsz * m_len, d), norm_mem_w[l][None, :], w_kv,
                             jnp.zeros((LANES, d), BF16), tm=bsz * m_len, tn=512)
        w_r = _pad_lanes(jnp.concatenate([w_router_group[l], w_router_expert[l]], axis=1))
        wr_hi = w_r.astype(BF16)
        wr_lo = (w_r - wr_hi.astype(F32)).astype(BF16)
        h2, um, route, cnt = _xattn(h1, kv, norm_xa_w[l][None, :], w_xq[l].astype(BF16), w_xo[l].astype(BF16),
                               norm_moe_w[l][None, :], jnp.concatenate([wr_hi, wr_lo], axis=1), seq, tm=512)

        slot, pad_row, pad_n, block_e, n_valid, n_slots = _slots(route, cnt, t)
        xs = _dispatch(slot, pad_row, pad_n, n_valid, um, n_slots, tm=512)
        yb = _experts(block_e, n_valid, xs, w_expert_gate[l], w_expert_up[l], w_expert_down[l])
        h = _combine(slot, h2, route, norm_final_w[None, :], yb, tm=512)
    return h.reshape(bsz, seq, d)
```

```python
import functools

import jax
import jax.numpy as jnp
from jax import lax
from jax.experimental import pallas as pl
from jax.experimental.pallas import tpu as pltpu

F32 = jnp.float32
BF16 = jnp.bfloat16
U32 = jnp.uint32
EPS = 1e-6

SSD_HEAD_DIM = 64
SSD_GROUPS = 4
SSD_STATE = 128
SSD_CONV = 4
SSD_CHUNK = 128
SB_HEADS = 4
SB_HEAD_DIM = 128
SB_BLOCK = 128
SB_SUB = 8
XA_HEADS = 4
XA_HEAD_DIM = 128
N_GROUPS = 4
EXPERTS_PER_GROUP = 8
N_EXPERTS = N_GROUPS * EXPERTS_PER_GROUP
TOP_K = 2
MOE_BLOCK = 256

LANES = 128
SUBLANES = 8
ROW_TILE = SUBLANES
VMEM_LIMIT = 56 * 1024 * 1024

SB_EXIT_LOG = -88.0
NEG_BIG = -1e30


def _cparams(sem):
    return pltpu.CompilerParams(dimension_semantics=sem, vmem_limit_bytes=VMEM_LIMIT)


def _rms(x, w):
    var = jnp.mean(x * x, axis=-1, keepdims=True)
    return x * lax.rsqrt(var + EPS) * w


def _split2(v):
    hi = v.astype(BF16)
    lo = (v - hi.astype(F32)).astype(BF16)
    return hi, lo


def _dot(a, b):
    return jnp.dot(a, b, preferred_element_type=F32)


def _dot_nt(a, bt):
    return lax.dot_general(a, bt, (((1,), (1,)), ((), ())), preferred_element_type=F32)


def _pack_halves(v):
    n = v.shape[1] // 2
    lo = lax.bitcast_convert_type(v[:, :n].astype(BF16).astype(F32), U32)
    hi = lax.bitcast_convert_type(v[:, n:].astype(BF16).astype(F32), U32)
    return (lo >> 16) | hi


def _unpack_halves(w):
    lo = lax.bitcast_convert_type(w << 16, F32)
    hi = lax.bitcast_convert_type(w & jnp.uint32(0xFFFF0000), F32)
    return lo, hi


def _store_row_tiles(ref, packed):
    rows = packed.shape[0]
    for c in range(ROW_TILE):
        ref[pl.ds(c, rows, stride=ROW_TILE), :] = packed[:, c * LANES:(c + 1) * LANES]


def _row_tile(ref, row):
    start = row * ROW_TILE
    if not isinstance(row, int):
        start = pl.multiple_of(start, ROW_TILE)
    return ref.at[pl.ds(start, ROW_TILE)]


def _load_row_tile_chunk(ref, c, rows):
    return ref[pl.ds(c, rows, stride=ROW_TILE), :]


def _silu(x):
    return x / (1.0 + jnp.exp(-x))


def _softplus(x):
    return jnp.maximum(x, 0.0) + jnp.log(1.0 + jnp.exp(-jnp.abs(x)))


def _norm_matmul_kernel(x_ref, nw_ref, w_ref, ws_ref, o_ref, os_ref, u_ref):
    @pl.when(pl.program_id(1) == 0)
    def _():
        u = _rms(x_ref[...], nw_ref[...]).astype(BF16)
        u_ref[...] = u
        os_ref[...] = _dot_nt(u, ws_ref[...])

    o_ref[...] = _dot_nt(u_ref[...], w_ref[...]).astype(o_ref.dtype)


def _norm_matmul(x, nw, wt, wt_side, tm, tn):
    m, k = x.shape
    n = wt.shape[0]
    ns = wt_side.shape[0]
    tm = min(tm, m)
    return pl.pallas_call(
        _norm_matmul_kernel,
        grid=(m // tm, n // tn),
        in_specs=[
            pl.BlockSpec((tm, k), lambda i, j: (i, 0)),
            pl.BlockSpec((1, k), lambda i, j: (0, 0)),
            pl.BlockSpec((tn, k), lambda i, j: (j, 0)),
            pl.BlockSpec((ns, k), lambda i, j: (0, 0)),
        ],
        out_specs=[
            pl.BlockSpec((tm, tn), lambda i, j: (i, j)),
            pl.BlockSpec((tm, ns), lambda i, j: (i, 0)),
        ],
        out_shape=[jax.ShapeDtypeStruct((m, n), BF16), jax.ShapeDtypeStruct((m, ns), F32)],
        scratch_shapes=[pltpu.VMEM((tm, k), BF16)],
        compiler_params=_cparams(("parallel", "arbitrary")),
        name="norm_matmul",
    )(x, nw, wt, wt_side)


def _ssd_kernel(z_ref, xs_ref, b_ref, c_ref, dtr_ref, cw_ref, cb_ref, dtb_ref, alog_ref, dsk_ref,
                nw_ref, e_ref, o_ref,
                tail_ref, win_ref, state_ref, xact_ref, bact_ref, cact_ref, dtx_ref, eax_ref, wsx_ref, y_ref):
    L = SSD_CHUNK
    inner = xs_ref.shape[1]
    bc = b_ref.shape[1]
    n_pairs = inner // LANES

    @pl.when(pl.program_id(1) == 0)
    def _():
        tail_ref[...] = jnp.zeros_like(tail_ref)
        state_ref[...] = jnp.zeros_like(state_ref)


    def conv_seg(src_ref, col0, width, dst_ref):
        for j in range(0, width, 2 * LANES):
            cols = slice(col0 + j, col0 + j + 2 * LANES)
            xin = src_ref[:, j:j + 2 * LANES].astype(F32)
            win_ref[:SUBLANES, :] = tail_ref[:, cols]
            win_ref[SUBLANES:, :] = xin
            acc = xin * cw_ref[SSD_CONV - 1:SSD_CONV, cols] + cb_ref[:, cols]
            for s in range(1, SSD_CONV):
                acc = acc + win_ref[SUBLANES - s:SUBLANES - s + L, :] * cw_ref[SSD_CONV - 1 - s:SSD_CONV - s, cols]
            tail_ref[:, cols] = xin[L - SUBLANES:]
            dst_ref[:, j:j + 2 * LANES] = _silu(acc).astype(dst_ref.dtype)

    conv_seg(xs_ref, 0, inner, xact_ref)
    conv_seg(b_ref, inner, bc, bact_ref)
    conv_seg(c_ref, inner + bc, bc, cact_ref)

    dt = _softplus(dtr_ref[...] + dtb_ref[...])
    da = dt * (-jnp.exp(alog_ref[...]))
    rowl = lax.broadcasted_iota(jnp.int32, (L, L), 0)
    coll = lax.broadcasted_iota(jnp.int32, (L, L), 1)
    lower = rowl >= coll
    tri = jnp.where(lower, 1.0, 0.0).astype(BF16)
    d1 = da.astype(BF16)
    r1 = da - d1.astype(F32)
    d2 = r1.astype(BF16)
    d3 = (r1 - d2.astype(F32)).astype(BF16)
    a_cum = _dot(tri, d1) + _dot(tri, d2) + _dot(tri, d3)
    a_cum_t = a_cum.T
    a_last = a_cum[L - 1:L, :]
    e_mat = e_ref[...]

    def expand(v):
        hi, lo = _split2(v)
        return _dot(hi, e_mat) + _dot(lo, e_mat)

    dtx_ref[...] = expand(dt)
    eax_ref[...] = expand(jnp.exp(a_cum))
    wsx_ref[...] = expand(jnp.exp(a_last - a_cum))
    elx = expand(jnp.broadcast_to(jnp.exp(a_last), (SUBLANES, LANES)))[0:1]

    lane = lax.broadcasted_iota(jnp.int32, (L, LANES), 1)
    pairs_per_group = n_pairs // SSD_GROUPS
    for g in range(SSD_GROUPS):
        gcols = slice(g * SSD_STATE, (g + 1) * SSD_STATE)
        bg_t = bact_ref[:, gcols].T.astype(BF16)
        cg = cact_ref[:, gcols]
        cb = _dot(cg, bg_t)
        for pp in range(pairs_per_group):
            p = g * pairs_per_group + pp
            pc = slice(p * LANES, (p + 1) * LANES)
            ms = []
            for hh in (2 * p, 2 * p + 1):
                seg = a_cum[:, hh:hh + 1] - a_cum_t[hh:hh + 1, :]
                dec = jnp.exp(jnp.where(lower, seg, NEG_BIG))
                ms.append((cb * dec).astype(BF16))
            lhs = jnp.concatenate(ms, axis=1)
            xs_p = xact_ref[:, pc]
            xdt = xs_p * dtx_ref[:, pc]
            rhs = jnp.concatenate([jnp.where(lane < SSD_HEAD_DIM, xdt, 0.0).astype(BF16),
                                   jnp.where(lane >= SSD_HEAD_DIM, xdt, 0.0).astype(BF16)], axis=0)
            st = state_ref[p]
            y = _dot(lhs, rhs)
            y = y + _dot(cg, st.astype(BF16)) * eax_ref[:, pc]
            y = y + xs_p * dsk_ref[:, pc]
            y_ref[:, pc] = y
            xw = (xdt * wsx_ref[:, pc]).astype(BF16)
            state_ref[p] = st * elx[:, pc] + _dot(bg_t, xw)

    yg = y_ref[...] * _silu(z_ref[...].astype(F32))
    o_ref[...] = _rms(yg, nw_ref[...]).astype(o_ref.dtype)


def _ssd(proj, dt_raw, conv_w, conv_b, dt_bias_p, a_log_p, dskip_x, norm_w, e_mat, bsz, seq, inner, bc, b_blk):
    L = SSD_CHUNK
    nc = seq // L
    conv_dim = inner + 2 * bc
    n_pairs = inner // LANES

    def rows(b, c):
        return b * nc + c

    return pl.pallas_call(
        _ssd_kernel,
        grid=(bsz, nc),
        in_specs=[
            pl.BlockSpec((L, inner), lambda b, c: (rows(b, c), 0)),
            pl.BlockSpec((L, inner), lambda b, c: (rows(b, c), 1)),
            pl.BlockSpec((L, bc), lambda b, c: (rows(b, c), b_blk)),
            pl.BlockSpec((L, bc), lambda b, c: (rows(b, c), b_blk + 1)),
            pl.BlockSpec((L, LANES), lambda b, c: (rows(b, c), 0)),
            pl.BlockSpec((SSD_CONV, conv_dim), lambda b, c: (0, 0)),
            pl.BlockSpec((1, conv_dim), lambda b, c: (0, 0)),
            pl.BlockSpec((1, LANES), lambda b, c: (0, 0)),
            pl.BlockSpec((1, LANES), lambda b, c: (0, 0)),
            pl.BlockSpec((1, inner), lambda b, c: (0, 0)),
            pl.BlockSpec((1, inner), lambda b, c: (0, 0)),
            pl.BlockSpec((LANES, inner), lambda b, c: (0, 0)),
        ],
        out_specs=pl.BlockSpec((L, inner), lambda b, c: (rows(b, c), 0)),
        out_shape=jax.ShapeDtypeStruct((bsz * seq, inner), BF16),
        scratch_shapes=[
            pltpu.VMEM((SUBLANES, conv_dim), F32),
            pltpu.VMEM((SUBLANES + L, 2 * LANES), F32),
            pltpu.VMEM((n_pairs, SSD_STATE, LANES), F32),
            pltpu.VMEM((L, inner), F32),
            pltpu.VMEM((L, bc), F32),
            pltpu.VMEM((L, bc), BF16),
            pltpu.VMEM((L, inner), F32),
            pltpu.VMEM((L, inner), F32),
            pltpu.VMEM((L, inner), F32),
            pltpu.VMEM((L, inner), F32),
        ],
        compiler_params=_cparams(("parallel", "arbitrary")),
        name="ssd",
    )(proj, proj, proj, proj, dt_raw, conv_w, conv_b, dt_bias_p, a_log_p, dskip_x, norm_w, e_mat)


def _sb_kernel(q_ref, k_ref, v_ref, o_ref, acc_ref, carry_ref):
    BL = SB_BLOCK
    i = pl.program_id(2)
    scale = SB_HEAD_DIM ** -0.5
    row = lax.broadcasted_iota(jnp.int32, (BL, BL), 0)
    col = lax.broadcasted_iota(jnp.int32, (BL, BL), 1)
    causal = col < row
    r2 = lax.broadcasted_iota(jnp.int32, (BL, 2 * BL), 0)
    c2 = lax.broadcasted_iota(jnp.int32, (BL, 2 * BL), 1)
    tri = jnp.where((r2 > c2) | (c2 >= BL), 1.0, 0.0).astype(BF16)

    def sweep(s, first):
        subs = range(SB_SUB)
        kbs = [i * SB_SUB + j - s for j in subs]
        starts = [pl.multiple_of(jnp.maximum(kb, 0) * BL, BL) for kb in kbs]
        zls = [lax.dot_general(q_ref[j * BL:(j + 1) * BL, :], k_ref[pl.ds(starts[j], BL), :],
                               (((1,), (1,)), ((), ())), preferred_element_type=F32) * scale for j in subs]
        sps = [_softplus(zl) for zl in zls]
        log_betas = [zl - sp for zl, sp in zip(zls, sps)]
        log_keeps = [jnp.where(causal, -sp, 0.0) if first else -sp for sp in sps]
        splits = [_split2(lk) for lk in log_keeps]
        t2s = [_dot(hi, tri) + _dot(lo, tri) for hi, lo in splits]
        if first:
            atts = [jnp.where(causal, jnp.exp(lb + t2[:, :BL]), 0.0) for lb, t2 in zip(log_betas, t2s)]
            carries = [t2[:, BL:] for t2 in t2s]
        else:
            olds = [carry_ref[j] for j in subs]
            atts = [jnp.where(kb >= 0, jnp.exp(lb + t2[:, :BL] + old), 0.0)
                    for kb, lb, t2, old in zip(kbs, log_betas, t2s, olds)]
            carries = [old + jnp.where(kb >= 0, t2[:, BL:], 0.0) for kb, t2, old in zip(kbs, t2s, olds)]
        pvs = [_dot(atts[j].astype(BF16), v_ref[pl.ds(starts[j], BL), :]) for j in subs]
        worst = None
        for j in subs:
            if first:
                acc_ref[j] = pvs[j]
            else:
                acc_ref[j] += pvs[j]
            carry_ref[j] = carries[j]
            live = jnp.where(kbs[j] >= 1, carries[j], NEG_BIG)
            worst = live if worst is None else jnp.maximum(worst, live)
        return jnp.max(worst) > SB_EXIT_LOG

    sweep(0, True)
    go = sweep(1, False)

    def body(st):
        s, _ = st
        return s + 1, sweep(s, False)

    lax.while_loop(lambda st: st[1], body, (jnp.int32(2), go))
    for j in range(SB_SUB):
        o_ref[j * BL:(j + 1) * BL, :] = acc_ref[j].astype(o_ref.dtype)


def _stickbreak(proj3, bsz, seq, q_blk0):
    tq = SB_SUB * SB_BLOCK
    return pl.pallas_call(
        _sb_kernel,
        grid=(bsz, SB_HEADS, seq // tq),
        in_specs=[
            pl.BlockSpec((None, tq, SB_HEAD_DIM), lambda b, h, i: (b, i, q_blk0 + h)),
            pl.BlockSpec((None, seq, SB_HEAD_DIM), lambda b, h, i: (b, 0, q_blk0 + SB_HEADS + h)),
            pl.BlockSpec((None, seq, SB_HEAD_DIM), lambda b, h, i: (b, 0, q_blk0 + 2 * SB_HEADS + h)),
        ],
        out_specs=pl.BlockSpec((None, tq, SB_HEAD_DIM), lambda b, h, i: (b, i, h)),
        out_shape=jax.ShapeDtypeStruct((bsz, seq, SB_HEADS * SB_HEAD_DIM), BF16),
        scratch_shapes=[pltpu.VMEM((SB_SUB, SB_BLOCK, SB_HEAD_DIM), F32),
                        pltpu.VMEM((SB_SUB, SB_BLOCK, SB_BLOCK), F32)],
        compiler_params=_cparams(("parallel", "parallel", "arbitrary")),
        name="stickbreak",
    )(proj3, proj3, proj3)


def _merge_kernel(x_ref, ys_ref, yb_ref, *rest):
    *g_refs, w1_ref, w2_ref, wm_ref, o_ref = rest
    nj = len(g_refs) // 2
    tn = g_refs[0].shape[1]
    ys = ys_ref[...]
    yb = yb_ref[...]
    ms = []
    for j in range(nj):
        cs = slice(j * tn, (j + 1) * tn)
        s1 = jax.nn.sigmoid(g_refs[j][...].astype(F32))
        s2 = jax.nn.sigmoid(g_refs[nj + j][...].astype(F32))
        ms.append((s1 * _dot(ys, w1_ref[:, cs]) + s2 * _dot(yb, w2_ref[:, cs])).astype(BF16))
    o_ref[...] = x_ref[...] + _dot(jnp.concatenate(ms, axis=1), wm_ref[...])


def _merge(x2, y_ssd, y_sb, proj, w1, w2, wm, g_blk0, tm, tn):
    t, d = x2.shape
    tm = min(tm, t)
    nj = d // tn
    resident = pl.Buffered(1)
    g_specs = [pl.BlockSpec((tm, tn), lambda i, b=g_blk0 + b: (i, b)) for b in range(2 * nj)]
    return pl.pallas_call(
        _merge_kernel,
        grid=(t // tm,),
        in_specs=[
            pl.BlockSpec((tm, d), lambda i: (i, 0)),
            pl.BlockSpec((tm, y_ssd.shape[1]), lambda i: (i, 0)),
            pl.BlockSpec((tm, y_sb.shape[1]), lambda i: (i, 0)),
            *g_specs,
            pl.BlockSpec(w1.shape, lambda i: (0, 0), pipeline_mode=resident),
            pl.BlockSpec(w2.shape, lambda i: (0, 0), pipeline_mode=resident),
            pl.BlockSpec(wm.shape, lambda i: (0, 0), pipeline_mode=resident),
        ],
        out_specs=pl.BlockSpec((tm, d), lambda i: (i, 0)),
        out_shape=jax.ShapeDtypeStruct((t, d), F32),
        compiler_params=_cparams(("parallel",)),
        name="merge",
    )(x2, y_ssd, y_sb, *([proj] * (2 * nj)), w1, w2, wm)


def _xattn_kernel(h_ref, kv_ref, nxa_ref, wq_ref, wo_ref, nmoe_ref, wr_ref, h2_ref, um_ref, rt_ref, cnt_ref,
                  base_ref):
    h1 = h_ref[...]
    un = _rms(h1, nxa_ref[...]).astype(BF16)
    q = _dot(un, wq_ref[...]).astype(BF16)
    width = XA_HEADS * XA_HEAD_DIM
    outs = []
    for hd in range(XA_HEADS):
        cs = slice(hd * XA_HEAD_DIM, (hd + 1) * XA_HEAD_DIM)
        k = kv_ref[:, cs]
        v = kv_ref[:, width + hd * XA_HEAD_DIM: width + (hd + 1) * XA_HEAD_DIM]
        sc = lax.dot_general(q[:, cs], k, (((1,), (1,)), ((), ())),
                             preferred_element_type=F32) * (XA_HEAD_DIM ** -0.5)
        sc = sc - jnp.max(sc, axis=-1, keepdims=True)
        p = jnp.exp(sc)
        p = p / jnp.sum(p, axis=-1, keepdims=True)
        outs.append(_dot(p.astype(BF16), v).astype(BF16))
    o = jnp.concatenate(outs, axis=1)
    h2 = h1 + _dot(o, wo_ref[...])
    h2_ref[...] = h2
    um = _rms(h2, nmoe_ref[...])
    _store_row_tiles(um_ref, _pack_halves(um))
    hi, lo = _split2(um)
    nr = rt_ref.shape[1]
    both = _dot(hi, wr_ref[...])
    lg = both[:, :nr] + both[:, nr:] + _dot(lo, wr_ref[:, :nr])

    lane = lax.broadcasted_iota(jnp.int32, lg.shape, 1)
    lane_f = lane.astype(F32)
    is_g = lane < N_GROUPS
    gl = jnp.where(is_g, lg, NEG_BIG)
    gmax = jnp.max(gl, axis=-1, keepdims=True)
    g_sel = jnp.min(jnp.where(gl == gmax, lane_f, float(LANES)), axis=-1, keepdims=True)
    g_gate = 1.0 / jnp.sum(jnp.where(is_g, jnp.exp(gl - gmax), 0.0), axis=-1, keepdims=True)
    lo_lane = N_GROUPS + EXPERTS_PER_GROUP * g_sel
    el = jnp.where((lane_f >= lo_lane) & (lane_f < lo_lane + EXPERTS_PER_GROUP), lg, NEG_BIG)
    m1 = jnp.max(el, axis=-1, keepdims=True)
    i1 = jnp.min(jnp.where(el == m1, lane_f, float(LANES)), axis=-1, keepdims=True)
    el2 = jnp.where(lane_f == i1, NEG_BIG, el)
    m2 = jnp.max(el2, axis=-1, keepdims=True)
    i2 = jnp.min(jnp.where(el2 == m2, lane_f, float(LANES)), axis=-1, keepdims=True)
    ex = jnp.exp(m2 - m1)
    p1 = 1.0 / (1.0 + ex)
    e1 = i1 - N_GROUPS
    e2 = i2 - N_GROUPS

    @pl.when(pl.program_id(0) == 0)
    def _():
        base_ref[...] = jnp.zeros_like(base_ref)

    tm = lg.shape[0]
    hot1 = lane_f == e1
    hot2 = lane_f == e2
    onehot = jnp.where(hot1 | hot2, 1.0, 0.0)
    tr = lax.broadcasted_iota(jnp.int32, (tm, tm), 0)
    tc = lax.broadcasted_iota(jnp.int32, (tm, tm), 1)
    before = _dot(jnp.where(tc < tr, 1.0, 0.0).astype(BF16), onehot.astype(BF16)) + base_ref[...]
    r1 = jnp.sum(jnp.where(hot1, before, 0.0), axis=-1, keepdims=True)
    r2 = jnp.sum(jnp.where(hot2, before, 0.0), axis=-1, keepdims=True)
    base_ref[...] += jnp.sum(onehot, axis=0, keepdims=True)
    cnt_ref[...] = jnp.broadcast_to(base_ref[...], cnt_ref.shape)

    vals = (e1, e2, p1 * g_gate, ex * p1 * g_gate, r1, r2)
    out = jnp.zeros_like(lg)
    for j, v in enumerate(vals):
        out = jnp.where(lane == j, v, out)
    rt_ref[...] = out


def _xattn(h1, kv, nxa, wq, wo, nmoe, wr, seq, tm):
    t, d = h1.shape
    tm = min(tm, seq)
    per_b = seq // tm
    m_len = kv.shape[0] // (t // seq)
    nr = wr.shape[1] // 2
    return pl.pallas_call(
        _xattn_kernel,
        grid=(t // tm,),
        in_specs=[
            pl.BlockSpec((tm, d), lambda i: (i, 0)),
            pl.BlockSpec((m_len, kv.shape[1]), lambda i: (i // per_b, 0)),
            pl.BlockSpec((1, d), lambda i: (0, 0)),
            pl.BlockSpec(wq.shape, lambda i: (0, 0)),
            pl.BlockSpec(wo.shape, lambda i: (0, 0)),
            pl.BlockSpec((1, d), lambda i: (0, 0)),
            pl.BlockSpec((d, 2 * nr), lambda i: (0, 0)),
        ],
        out_specs=[
            pl.BlockSpec((tm, d), lambda i: (i, 0)),
            pl.BlockSpec((tm * ROW_TILE, LANES), lambda i: (i, 0)),
            pl.BlockSpec((tm, nr), lambda i: (i, 0)),
            pl.BlockSpec((SUBLANES, nr), lambda i: (0, 0)),
        ],
        out_shape=[jax.ShapeDtypeStruct((t, d), F32), jax.ShapeDtypeStruct((t * ROW_TILE, LANES), U32),
                   jax.ShapeDtypeStruct((t, nr), F32), jax.ShapeDtypeStruct((SUBLANES, nr), F32)],
        scratch_shapes=[pltpu.VMEM((1, nr), F32)],
        compiler_params=_cparams(("arbitrary",)),
        name="xattn",
    )(h1, kv, nxa, wq, wo, nmoe, wr)


def _dispatch_kernel(slot_ref, padrow_ref, padn_ref, nv_ref, um_ref, xs_hbm, sa_ref, sb_ref, z_ref, sem, zsem):
    i = pl.program_id(0)
    tm = um_ref.shape[0] // ROW_TILE
    last = pl.num_programs(0) - 1
    stage = (sa_ref, sb_ref)

    def zero_rows(start):
        for e in range(N_EXPERTS):
            def body(j, c, e=e):
                cp = pltpu.make_async_copy(_row_tile(z_ref, 0), _row_tile(xs_hbm, padrow_ref[e] + j), zsem)
                if start:
                    cp.start()
                else:
                    cp.wait()
                return c

            lax.fori_loop(0, padn_ref[e], body, 0)

        def tail(b, c):
            blk = MOE_BLOCK * ROW_TILE
            cp = pltpu.make_async_copy(z_ref, xs_hbm.at[pl.ds(pl.multiple_of(b * blk, blk), blk)], zsem)
            if start:
                cp.start()
            else:
                cp.wait()
            return c

        lax.fori_loop(nv_ref[0], xs_hbm.shape[0] // (MOE_BLOCK * ROW_TILE), tail, 0)

    @pl.when(i == 0)
    def _():
        z_ref[...] = jnp.zeros_like(z_ref)
        zero_rows(True)

    def wait(par):
        for _ in range(TOP_K):
            pltpu.make_async_copy(stage[par], xs_hbm.at[pl.ds(0, tm * ROW_TILE)], sem.at[par]).wait()

    def step(par):
        pl.when(i >= 2)(functools.partial(wait, par))
        stage[par][...] = um_ref[...]
        tokens = tm * pl.num_programs(0)
        for r in range(tm):
            for k in range(TOP_K):
                pltpu.make_async_copy(_row_tile(stage[par], r),
                                      _row_tile(xs_hbm, slot_ref[k * tokens + i * tm + r]),
                                      sem.at[par]).start(priority=k)

        @pl.when(i == last)
        def _():
            wait(par)
            pl.when(i >= 1)(functools.partial(wait, 1 - par))

    for par in range(2):
        pl.when(i % 2 == par)(functools.partial(step, par))

    @pl.when(i == last)
    def _():
        zero_rows(False)


def _dispatch(slot, pad_row, pad_n, n_valid, um, n_slots, tm):
    t = um.shape[0] // ROW_TILE
    tm = min(tm, t)
    rows = tm * ROW_TILE
    grid_spec = pltpu.PrefetchScalarGridSpec(
        num_scalar_prefetch=4,
        grid=(t // tm,),
        in_specs=[pl.BlockSpec((rows, LANES), lambda i, s, pr, pn, nv: (i, 0))],
        out_specs=pl.BlockSpec(memory_space=pl.ANY),
        scratch_shapes=[pltpu.VMEM((rows, LANES), um.dtype), pltpu.VMEM((rows, LANES), um.dtype),
                        pltpu.VMEM((MOE_BLOCK * ROW_TILE, LANES), um.dtype),
                        pltpu.SemaphoreType.DMA((2,)), pltpu.SemaphoreType.DMA(())],
    )
    return pl.pallas_call(
        _dispatch_kernel,
        grid_spec=grid_spec,
        out_shape=jax.ShapeDtypeStruct((n_slots * ROW_TILE, LANES), um.dtype),
        compiler_params=_cparams(("arbitrary",)),
        name="dispatch",
    )(slot, pad_row, pad_n, n_valid, um)


def _expert_kernel(be_ref, nv_ref, x_ref, wg_ref, wu_ref, wd_ref, o_ref, wgb_ref, wub_ref, wdb_ref, xb_ref):
    i = pl.program_id(0)

    @pl.when(jnp.logical_or(i == 0, be_ref[i] != be_ref[jnp.maximum(i - 1, 0)]))
    def _():
        wgb_ref[...] = wg_ref[...].astype(BF16)
        wub_ref[...] = wu_ref[...].astype(BF16)
        wdb_ref[...] = wd_ref[...].astype(BF16)

    @pl.when(i < nv_ref[0])
    def _():
        half = xb_ref.shape[1] // 2
        for c in range(ROW_TILE):
            lo, hi = _unpack_halves(_load_row_tile_chunk(x_ref, c, MOE_BLOCK))
            xb_ref[:, c * LANES:(c + 1) * LANES] = lo.astype(BF16)
            xb_ref[:, half + c * LANES:half + (c + 1) * LANES] = hi.astype(BF16)
        xb = xb_ref[...]
        hid = (_silu(_dot(xb, wgb_ref[...])) * _dot(xb, wub_ref[...])).astype(BF16)
        _store_row_tiles(o_ref, _pack_halves(_dot(hid, wdb_ref[...])))

    @pl.when(i >= nv_ref[0])
    def _():
        o_ref[...] = jnp.zeros_like(o_ref)


def _experts(block_e, n_valid, xs, wg, wu, wd):
    blk = MOE_BLOCK * ROW_TILE
    n_blocks = xs.shape[0] // blk
    d, ff = wg.shape[1:]

    def x_map(i, be, nv):
        return (jnp.minimum(i, nv[0] - 1), 0)

    grid_spec = pltpu.PrefetchScalarGridSpec(
        num_scalar_prefetch=2,
        grid=(n_blocks,),
        in_specs=[
            pl.BlockSpec((blk, LANES), x_map),
            pl.BlockSpec((None, d, ff), lambda i, be, nv: (be[i], 0, 0)),
            pl.BlockSpec((None, d, ff), lambda i, be, nv: (be[i], 0, 0)),
            pl.BlockSpec((None, ff, d), lambda i, be, nv: (be[i], 0, 0)),
        ],
        out_specs=pl.BlockSpec((blk, LANES), lambda i, be, nv: (i, 0)),
        scratch_shapes=[pltpu.VMEM((d, ff), BF16), pltpu.VMEM((d, ff), BF16), pltpu.VMEM((ff, d), BF16),
                        pltpu.VMEM((MOE_BLOCK, d), BF16)],
    )
    return pl.pallas_call(
        _expert_kernel,
        grid_spec=grid_spec,
        out_shape=jax.ShapeDtypeStruct(xs.shape, U32),
        compiler_params=_cparams(("arbitrary",)),
        name="experts",
    )(block_e, n_valid, xs, wg, wu, wd)


def _combine_kernel(slot_ref, h_ref, w_ref, nw_ref, yb_hbm, o_ref, ya0_ref, ya1_ref, yb0_ref, yb1_ref, sem):
    i = pl.program_id(0)
    tm = h_ref.shape[0]
    last = pl.num_programs(0) - 1
    bufs = ((ya0_ref, ya1_ref), (yb0_ref, yb1_ref))

    def gather(blk, par):
        tokens = tm * pl.num_programs(0)
        for r in range(tm):
            for k in range(TOP_K):
                pltpu.make_async_copy(_row_tile(yb_hbm, slot_ref[k * tokens + blk * tm + r]),
                                      _row_tile(bufs[par][k], r), sem.at[par]).start(priority=k)

    def wait(par):
        for k in range(TOP_K):
            pltpu.make_async_copy(yb_hbm.at[pl.ds(0, tm * ROW_TILE)], bufs[par][k], sem.at[par]).wait()

    @pl.when(i == 0)
    def _():
        gather(i, 0)

    def step(par):
        wait(par)
        gather(jnp.minimum(i + 1, last), 1 - par)
        w0 = w_ref[:, TOP_K:TOP_K + 1]
        w1 = w_ref[:, TOP_K + 1:TOP_K + 2]
        half = h_ref.shape[1] // 2
        sq = jnp.zeros((tm, LANES), F32)
        for c in range(ROW_TILE):
            ya = _unpack_halves(_load_row_tile_chunk(bufs[par][0], c, tm))
            yb = _unpack_halves(_load_row_tile_chunk(bufs[par][1], c, tm))
            for p in range(2):
                cs = slice(p * half + c * LANES, p * half + (c + 1) * LANES)
                h3 = h_ref[:, cs] + w0 * ya[p] + w1 * yb[p]
                sq = sq + h3 * h3
                o_ref[:, cs] = h3
        inv = lax.rsqrt(jnp.sum(sq, axis=-1, keepdims=True) / h_ref.shape[1] + EPS)
        o_ref[...] = o_ref[...] * inv * nw_ref[...]

        @pl.when(i == last)
        def _():
            wait(1 - par)

    for par in range(2):
        pl.when(i % 2 == par)(functools.partial(step, par))


def _combine(slot, h2, route, nw, yb, tm):
    t, d = h2.shape
    tm = min(tm, t)
    grid_spec = pltpu.PrefetchScalarGridSpec(
        num_scalar_prefetch=1,
        grid=(t // tm,),
        in_specs=[
            pl.BlockSpec((tm, d), lambda i, s: (i, 0)),
            pl.BlockSpec((tm, route.shape[1]), lambda i, s: (i, 0)),
            pl.BlockSpec((1, d), lambda i, s: (0, 0)),
            pl.BlockSpec(memory_space=pl.ANY),
        ],
        out_specs=pl.BlockSpec((tm, d), lambda i, s: (i, 0)),
        scratch_shapes=([pltpu.VMEM((tm * ROW_TILE, LANES), yb.dtype)] * (2 * TOP_K)
                        + [pltpu.SemaphoreType.DMA((2,))]),
    )
    return pl.pallas_call(
        _combine_kernel,
        grid_spec=grid_spec,
        out_shape=jax.ShapeDtypeStruct((t, d), F32),
        compiler_params=_cparams(("arbitrary",)),
        name="combine",
    )(slot, h2, route, nw, yb)


def _slots(route, cnt, t):
    rt = route[:, :SUBLANES].T.astype(jnp.int32)
    e_kt = rt[:TOP_K]
    rank = rt[2 * TOP_K:3 * TOP_K]
    onehot = e_kt[None] == jnp.arange(N_EXPERTS, dtype=jnp.int32)[:, None, None]
    counts = cnt[0, :N_EXPERTS].astype(jnp.int32)
    padded = (counts + MOE_BLOCK - 1) // MOE_BLOCK * MOE_BLOCK
    pad_end = jnp.cumsum(padded)
    pad_start = pad_end - padded
    slot = (jnp.sum(jnp.where(onehot, pad_start[:, None, None], 0), axis=0) + rank).reshape(-1)
    n_slots = t * TOP_K + N_EXPERTS * MOE_BLOCK
    n_blocks = n_slots // MOE_BLOCK
    block_start = jnp.arange(n_blocks, dtype=jnp.int32) * MOE_BLOCK
    block_e = jnp.minimum(jnp.sum((pad_end[None, :] <= block_start[:, None]).astype(jnp.int32), axis=1),
                          N_EXPERTS - 1)
    n_valid = (pad_end[-1:] // MOE_BLOCK).astype(jnp.int32)
    return slot, pad_start + counts, padded - counts, block_e, n_valid, n_slots


def _pack_w_kernel(w_ref, o_ref, *, pieces):
    filled = 0
    for src0, src1, dst0 in pieces:
        o_ref[dst0:dst0 + src1 - src0, :] = w_ref[src0:src1, :].astype(o_ref.dtype)
        filled = max(filled, dst0 + src1 - src0)
    if filled < o_ref.shape[0]:
        o_ref[filled:, :] = jnp.zeros((o_ref.shape[0] - filled, o_ref.shape[1]), o_ref.dtype)


def _pack_w(wt, pieces, n_out, tk):
    n_in, k = wt.shape
    return pl.pallas_call(
        functools.partial(_pack_w_kernel, pieces=pieces),
        grid=(k // tk,),
        in_specs=[pl.BlockSpec((n_in, tk), lambda i: (0, i))],
        out_specs=pl.BlockSpec((n_out, tk), lambda i: (0, i)),
        out_shape=jax.ShapeDtypeStruct((n_out, k), BF16),
        compiler_params=_cparams(("parallel",)),
        name="pack_w",
    )(wt)


def _pad_lanes(v, n=LANES):
    return jnp.pad(v, ((0, 0), (0, n - v.shape[1])))


def kernel(x, mem, norm_mix_w, w_in, conv_w, conv_b, dt_bias, a_log, d_skip, ssd_norm_w, w_ssd_branch, w_sb_branch, w_mix_out, norm_xa_w, norm_mem_w, w_xq, w_xk, w_xv, w_xo, norm_moe_w, w_router_group, w_router_expert, w_expert_gate, w_expert_up, w_expert_down, norm_final_w):
    bsz, seq, d = x.shape
    depth = w_in.shape[0]
    t = bsz * seq
    inner = ssd_norm_w.shape[1]
    heads = dt_bias.shape[1]
    bc = SSD_GROUPS * SSD_STATE
    conv_dim = inner + 2 * bc
    sb_width = SB_HEADS * SB_HEAD_DIM
    col_dt = inner + conv_dim
    col_qkv = col_dt + heads
    col_g = col_qkv + 3 * sb_width
    g_col0 = 2 * inner
    bc_col0 = g_col0 + 2 * d
    q_col0 = bc_col0 + 2 * bc
    n_main = q_col0 + 3 * sb_width
    tn_proj = 2048
    n_proj = -(-n_main // tn_proj) * tn_proj
    tn_merge = 1024

    assert depth == 1, "single-layer configuration: the final RMSNorm is fused into the last MoE combine"
    h = x.reshape(t, d)
    for l in range(depth):
        w_t = jnp.swapaxes(w_in[l], 0, 1)
        w_main = _pack_w(w_t, [(0, 2 * inner, 0), (col_g, col_g + 2 * d, g_col0),
                               (2 * inner, col_dt, bc_col0), (col_qkv, col_g, q_col0)], n_proj, tk=256)
        w_dt = jnp.pad(w_t[col_dt:col_dt + heads], ((0, LANES - heads), (0, 0))).astype(BF16)
        e_mat = (jnp.arange(inner, dtype=jnp.int32)[None, :] // SSD_HEAD_DIM
                 == jnp.arange(LANES, dtype=jnp.int32)[:, None]).astype(BF16)
        dskip_x = jnp.repeat(d_skip[l], SSD_HEAD_DIM)[None, :]

        proj, dt_raw = _norm_matmul(h, norm_mix_w[l][None, :], w_main, w_dt, tm=1024, tn=tn_proj)
        y_ssd = _ssd(proj, dt_raw, conv_w[l], conv_b[l][None, :], _pad_lanes(dt_bias[l][None, :]),
                     _pad_lanes(a_log[l][None, :]), dskip_x, ssd_norm_w[l][None, :], e_mat,
                     bsz, seq, inner, bc, bc_col0 // bc)
        y_sb = _stickbreak(proj.reshape(bsz, seq, n_proj), bsz, seq, q_col0 // SB_HEAD_DIM)
        h1 = _merge(h, y_ssd, y_sb.reshape(t, sb_width), proj,
                    w_ssd_branch[l].astype(BF16), w_sb_branch[l].astype(BF16), w_mix_out[l].astype(BF16),
                    g_col0 // tn_merge, tm=512, tn=tn_merge)

        m_len = mem.shape[1]
        w_kv = jnp.concatenate([w_xk[l].T, w_xv[l].T], axis=0).astype(BF16)
        kv, _ = _norm_matmul(mem.reshape(bsz * m_len, d), norm_mem_w[l][None, :], w_kv,
                             jnp.zeros((LANES, d), BF16), tm=bsz * m_len, tn=512)
        w_r = _pad_lanes(jnp.concatenate([w_router_group[l], w_router_expert[l]], axis=1))
        wr_hi = w_r.astype(BF16)
        wr_lo = (w_r - wr_hi.astype(F32)).astype(BF16)
        h2, um, route, cnt = _xattn(h1, kv, norm_xa_w[l][None, :], w_xq[l].astype(BF16), w_xo[l].astype(BF16),
                               norm_moe_w[l][None, :], jnp.concatenate([wr_hi, wr_lo], axis=1), seq, tm=512)

        slot, pad_row, pad_n, block_e, n_valid, n_slots = _slots(route, cnt, t)
        xs = _dispatch(slot, pad_row, pad_n, n_valid, um, n_slots, tm=512)
        yb = _experts(block_e, n_valid, xs, w_expert_gate[l], w_expert_up[l], w_expert_down[l])
        h = _combine(slot, h2, route, norm_final_w[None, :], yb, tm=512)
    return h.reshape(bsz, seq, d)
```

```python
import functools

import jax
import jax.numpy as jnp
from jax import lax
from jax.experimental import pallas as pl
from jax.experimental.pallas import tpu as pltpu

F32 = jnp.float32
BF16 = jnp.bfloat16
U32 = jnp.uint32
EPS = 1e-6

SSD_HEAD_DIM = 64
SSD_GROUPS = 4
SSD_STATE = 128
SSD_CONV = 4
SSD_CHUNK = 128
SB_HEADS = 4
SB_HEAD_DIM = 128
SB_BLOCK = 128
SB_SUB = 16
XA_HEADS = 4
XA_HEAD_DIM = 128
N_GROUPS = 4
EXPERTS_PER_GROUP = 8
N_EXPERTS = N_GROUPS * EXPERTS_PER_GROUP
TOP_K = 2
MOE_BLOCK = 256

LANES = 128
SUBLANES = 8
ROW_TILE = SUBLANES
VMEM_LIMIT = 56 * 1024 * 1024

SB_EXIT_LOG = -88.0
NEG_BIG = -1e30


def _cparams(sem):
    return pltpu.CompilerParams(dimension_semantics=sem, vmem_limit_bytes=VMEM_LIMIT)


def _rms(x, w):
    var = jnp.mean(x * x, axis=-1, keepdims=True)
    return x * lax.rsqrt(var + EPS) * w


def _split2(v):
    hi = v.astype(BF16)
    lo = (v - hi.astype(F32)).astype(BF16)
    return hi, lo


def _dot(a, b):
    return jnp.dot(a, b, preferred_element_type=F32)


def _dot_nt(a, bt):
    return lax.dot_general(a, bt, (((1,), (1,)), ((), ())), preferred_element_type=F32)


def _pack_halves(v):
    n = v.shape[1] // 2
    lo = lax.bitcast_convert_type(v[:, :n].astype(BF16).astype(F32), U32)
    hi = lax.bitcast_convert_type(v[:, n:].astype(BF16).astype(F32), U32)
    return (lo >> 16) | hi


def _unpack_halves(w):
    lo = lax.bitcast_convert_type(w << 16, F32)
    hi = lax.bitcast_convert_type(w & jnp.uint32(0xFFFF0000), F32)
    return lo, hi


def _store_row_tiles(ref, packed):
    rows = packed.shape[0]
    for c in range(ROW_TILE):
        ref[pl.ds(c, rows, stride=ROW_TILE), :] = packed[:, c * LANES:(c + 1) * LANES]


def _row_tile(ref, row):
    start = row * ROW_TILE
    if not isinstance(row, int):
        start = pl.multiple_of(start, ROW_TILE)
    return ref.at[pl.ds(start, ROW_TILE)]


def _load_row_tile_chunk(ref, c, rows):
    return ref[pl.ds(c, rows, stride=ROW_TILE), :]


def _silu(x):
    return x / (1.0 + jnp.exp(-x))


def _softplus(x):
    return jnp.maximum(x, 0.0) + jnp.log(1.0 + jnp.exp(-jnp.abs(x)))


def _norm_matmul_kernel(x_ref, nw_ref, w_ref, ws_ref, o_ref, os_ref, u_ref):
    @pl.when(pl.program_id(1) == 0)
    def _():
        u = _rms(x_ref[...], nw_ref[...]).astype(BF16)
        u_ref[...] = u
        os_ref[...] = _dot_nt(u, ws_ref[...])

    o_ref[...] = _dot_nt(u_ref[...], w_ref[...]).astype(o_ref.dtype)


def _norm_matmul(x, nw, wt, wt_side, tm, tn):
    m, k = x.shape
    n = wt.shape[0]
    ns = wt_side.shape[0]
    tm = min(tm, m)
    return pl.pallas_call(
        _norm_matmul_kernel,
        grid=(m // tm, n // tn),
        in_specs=[
            pl.BlockSpec((tm, k), lambda i, j: (i, 0)),
            pl.BlockSpec((1, k), lambda i, j: (0, 0)),
            pl.BlockSpec((tn, k), lambda i, j: (j, 0)),
            pl.BlockSpec((ns, k), lambda i, j: (0, 0)),
        ],
        out_specs=[
            pl.BlockSpec((tm, tn), lambda i, j: (i, j)),
            pl.BlockSpec((tm, ns), lambda i, j: (i, 0)),
        ],
        out_shape=[jax.ShapeDtypeStruct((m, n), BF16), jax.ShapeDtypeStruct((m, ns), F32)],
        scratch_shapes=[pltpu.VMEM((tm, k), BF16)],
        compiler_params=_cparams(("parallel", "arbitrary")),
        name="norm_matmul",
    )(x, nw, wt, wt_side)


def _ssd_kernel(z_ref, xs_ref, b_ref, c_ref, dtr_ref, cw_ref, cb_ref, dtb_ref, alog_ref, dsk_ref,
                nw_ref, e_ref, o_ref,
                tail_ref, win_ref, state_ref, xact_ref, bact_ref, cact_ref, dtx_ref, eax_ref, wsx_ref, y_ref):
    L = SSD_CHUNK
    inner = xs_ref.shape[1]
    bc = b_ref.shape[1]
    n_pairs = inner // LANES

    @pl.when(pl.program_id(1) == 0)
    def _():
        tail_ref[...] = jnp.zeros_like(tail_ref)
        state_ref[...] = jnp.zeros_like(state_ref)


    def conv_seg(src_ref, col0, width, dst_ref):
        for j in range(0, width, 2 * LANES):
            cols = slice(col0 + j, col0 + j + 2 * LANES)
            xin = src_ref[:, j:j + 2 * LANES].astype(F32)
            win_ref[:SUBLANES, :] = tail_ref[:, cols]
            win_ref[SUBLANES:, :] = xin
            acc = xin * cw_ref[SSD_CONV - 1:SSD_CONV, cols] + cb_ref[:, cols]
            for s in range(1, SSD_CONV):
                acc = acc + win_ref[SUBLANES - s:SUBLANES - s + L, :] * cw_ref[SSD_CONV - 1 - s:SSD_CONV - s, cols]
            tail_ref[:, cols] = xin[L - SUBLANES:]
            dst_ref[:, j:j + 2 * LANES] = _silu(acc).astype(dst_ref.dtype)

    conv_seg(xs_ref, 0, inner, xact_ref)
    conv_seg(b_ref, inner, bc, bact_ref)
    conv_seg(c_ref, inner + bc, bc, cact_ref)

    dt = _softplus(dtr_ref[...] + dtb_ref[...])
    da = dt * (-jnp.exp(alog_ref[...]))
    rowl = lax.broadcasted_iota(jnp.int32, (L, L), 0)
    coll = lax.broadcasted_iota(jnp.int32, (L, L), 1)
    lower = rowl >= coll
    tri = jnp.where(lower, 1.0, 0.0).astype(BF16)
    d1 = da.astype(BF16)
    r1 = da - d1.astype(F32)
    d2 = r1.astype(BF16)
    d3 = (r1 - d2.astype(F32)).astype(BF16)
    a_cum = _dot(tri, d1) + _dot(tri, d2) + _dot(tri, d3)
    a_cum_t = a_cum.T
    a_last = a_cum[L - 1:L, :]
    e_mat = e_ref[...]

    def expand(v):
        hi, lo = _split2(v)
        return _dot(hi, e_mat) + _dot(lo, e_mat)

    dtx_ref[...] = expand(dt)
    eax_ref[...] = expand(jnp.exp(a_cum))
    wsx_ref[...] = expand(jnp.exp(a_last - a_cum))
    elx = expand(jnp.broadcast_to(jnp.exp(a_last), (SUBLANES, LANES)))[0:1]

    lane = lax.broadcasted_iota(jnp.int32, (L, LANES), 1)
    pairs_per_group = n_pairs // SSD_GROUPS
    for g in range(SSD_GROUPS):
        gcols = slice(g * SSD_STATE, (g + 1) * SSD_STATE)
        bg_t = bact_ref[:, gcols].T.astype(BF16)
        cg = cact_ref[:, gcols]
        cb = _dot(cg, bg_t)
        for pp in range(pairs_per_group):
            p = g * pairs_per_group + pp
            pc = slice(p * LANES, (p + 1) * LANES)
            ms = []
            for hh in (2 * p, 2 * p + 1):
                seg = a_cum[:, hh:hh + 1] - a_cum_t[hh:hh + 1, :]
                dec = jnp.exp(jnp.where(lower, seg, NEG_BIG))
                ms.append((cb * dec).astype(BF16))
            lhs = jnp.concatenate(ms, axis=1)
            xs_p = xact_ref[:, pc]
            xdt = xs_p * dtx_ref[:, pc]
            rhs = jnp.concatenate([jnp.where(lane < SSD_HEAD_DIM, xdt, 0.0).astype(BF16),
                                   jnp.where(lane >= SSD_HEAD_DIM, xdt, 0.0).astype(BF16)], axis=0)
            st = state_ref[p]
            y = _dot(lhs, rhs)
            y = y + _dot(cg, st.astype(BF16)) * eax_ref[:, pc]
            y = y + xs_p * dsk_ref[:, pc]
            y_ref[:, pc] = y
            xw = (xdt * wsx_ref[:, pc]).astype(BF16)
            state_ref[p] = st * elx[:, pc] + _dot(bg_t, xw)

    yg = y_ref[...] * _silu(z_ref[...].astype(F32))
    o_ref[...] = _rms(yg, nw_ref[...]).astype(o_ref.dtype)


def _ssd(proj, dt_raw, conv_w, conv_b, dt_bias_p, a_log_p, dskip_x, norm_w, e_mat, bsz, seq, inner, bc, b_blk):
    L = SSD_CHUNK
    nc = seq // L
    conv_dim = inner + 2 * bc
    n_pairs = inner // LANES

    def rows(b, c):
        return b * nc + c

    return pl.pallas_call(
        _ssd_kernel,
        grid=(bsz, nc),
        in_specs=[
            pl.BlockSpec((L, inner), lambda b, c: (rows(b, c), 0)),
            pl.BlockSpec((L, inner), lambda b, c: (rows(b, c), 1)),
            pl.BlockSpec((L, bc), lambda b, c: (rows(b, c), b_blk)),
            pl.BlockSpec((L, bc), lambda b, c: (rows(b, c), b_blk + 1)),
            pl.BlockSpec((L, LANES), lambda b, c: (rows(b, c), 0)),
            pl.BlockSpec((SSD_CONV, conv_dim), lambda b, c: (0, 0)),
            pl.BlockSpec((1, conv_dim), lambda b, c: (0, 0)),
            pl.BlockSpec((1, LANES), lambda b, c: (0, 0)),
            pl.BlockSpec((1, LANES), lambda b, c: (0, 0)),
            pl.BlockSpec((1, inner), lambda b, c: (0, 0)),
            pl.BlockSpec((1, inner), lambda b, c: (0, 0)),
            pl.BlockSpec((LANES, inner), lambda b, c: (0, 0)),
        ],
        out_specs=pl.BlockSpec((L, inner), lambda b, c: (rows(b, c), 0)),
        out_shape=jax.ShapeDtypeStruct((bsz * seq, inner), BF16),
        scratch_shapes=[
            pltpu.VMEM((SUBLANES, conv_dim), F32),
            pltpu.VMEM((SUBLANES + L, 2 * LANES), F32),
            pltpu.VMEM((n_pairs, SSD_STATE, LANES), F32),
            pltpu.VMEM((L, inner), F32),
            pltpu.VMEM((L, bc), F32),
            pltpu.VMEM((L, bc), BF16),
            pltpu.VMEM((L, inner), F32),
            pltpu.VMEM((L, inner), F32),
            pltpu.VMEM((L, inner), F32),
            pltpu.VMEM((L, inner), F32),
        ],
        compiler_params=_cparams(("parallel", "arbitrary")),
        name="ssd",
    )(proj, proj, proj, proj, dt_raw, conv_w, conv_b, dt_bias_p, a_log_p, dskip_x, norm_w, e_mat)


def _sb_kernel(q_ref, k_ref, v_ref, o_ref, acc_ref, carry_ref):
    BL = SB_BLOCK
    i = pl.program_id(2)
    scale = SB_HEAD_DIM ** -0.5
    row = lax.broadcasted_iota(jnp.int32, (BL, BL), 0)
    col = lax.broadcasted_iota(jnp.int32, (BL, BL), 1)
    causal = col < row
    r2 = lax.broadcasted_iota(jnp.int32, (BL, 2 * BL), 0)
    c2 = lax.broadcasted_iota(jnp.int32, (BL, 2 * BL), 1)
    tri = jnp.where((r2 > c2) | (c2 >= BL), 1.0, 0.0).astype(BF16)

    def sweep(s, first):
        subs = range(SB_SUB)
        kbs = [i * SB_SUB + j - s for j in subs]
        starts = [pl.multiple_of(jnp.maximum(kb, 0) * BL, BL) for kb in kbs]
        zls = [lax.dot_general(q_ref[j * BL:(j + 1) * BL, :], k_ref[pl.ds(starts[j], BL), :],
                               (((1,), (1,)), ((), ())), preferred_element_type=F32) * scale for j in subs]
        sps = [_softplus(zl) for zl in zls]
        log_betas = [zl - sp for zl, sp in zip(zls, sps)]
        log_keeps = [jnp.where(causal, -sp, 0.0) if first else -sp for sp in sps]
        splits = [_split2(lk) for lk in log_keeps]
        t2s = [_dot(hi, tri) + _dot(lo, tri) for hi, lo in splits]
        if first:
            atts = [jnp.where(causal, jnp.exp(lb + t2[:, :BL]), 0.0) for lb, t2 in zip(log_betas, t2s)]
            carries = [t2[:, BL:] for t2 in t2s]
        else:
            olds = [carry_ref[j] for j in subs]
            atts = [jnp.where(kb >= 0, jnp.exp(lb + t2[:, :BL] + old), 0.0)
                    for kb, lb, t2, old in zip(kbs, log_betas, t2s, olds)]
            carries = [old + jnp.where(kb >= 0, t2[:, BL:], 0.0) for kb, t2, old in zip(kbs, t2s, olds)]
        pvs = [_dot(atts[j].astype(BF16), v_ref[pl.ds(starts[j], BL), :]) for j in subs]
        worst = None
        for j in subs:
            if first:
                acc_ref[j] = pvs[j]
            else:
                acc_ref[j] += pvs[j]
            carry_ref[j] = carries[j]
            live = jnp.where(kbs[j] >= 1, carries[j], NEG_BIG)
            worst = live if worst is None else jnp.maximum(worst, live)
        return jnp.max(worst) > SB_EXIT_LOG

    sweep(0, True)
    go = sweep(1, False)

    def body(st):
        s, _ = st
        return s + 1, sweep(s, False)

    lax.while_loop(lambda st: st[1], body, (jnp.int32(2), go))
    for j in range(SB_SUB):
        o_ref[j * BL:(j + 1) * BL, :] = acc_ref[j].astype(o_ref.dtype)


def _stickbreak(proj3, bsz, seq, q_blk0):
    tq = SB_SUB * SB_BLOCK
    return pl.pallas_call(
        _sb_kernel,
        grid=(bsz, SB_HEADS, seq // tq),
        in_specs=[
            pl.BlockSpec((None, tq, SB_HEAD_DIM), lambda b, h, i: (b, i, q_blk0 + h)),
            pl.BlockSpec((None, seq, SB_HEAD_DIM), lambda b, h, i: (b, 0, q_blk0 + SB_HEADS + h)),
            pl.BlockSpec((None, seq, SB_HEAD_DIM), lambda b, h, i: (b, 0, q_blk0 + 2 * SB_HEADS + h)),
        ],
        out_specs=pl.BlockSpec((None, tq, SB_HEAD_DIM), lambda b, h, i: (b, i, h)),
        out_shape=jax.ShapeDtypeStruct((bsz, seq, SB_HEADS * SB_HEAD_DIM), BF16),
        scratch_shapes=[pltpu.VMEM((SB_SUB, SB_BLOCK, SB_HEAD_DIM), F32),
                        pltpu.VMEM((SB_SUB, SB_BLOCK, SB_BLOCK), F32)],
        compiler_params=_cparams(("parallel", "parallel", "arbitrary")),
        name="stickbreak",
    )(proj3, proj3, proj3)


def _merge_kernel(x_ref, ys_ref, yb_ref, *rest):
    *g_refs, w1_ref, w2_ref, wm_ref, o_ref = rest
    nj = len(g_refs) // 2
    tn = g_refs[0].shape[1]
    ys = ys_ref[...]
    yb = yb_ref[...]
    ms = []
    for j in range(nj):
        cs = slice(j * tn, (j + 1) * tn)
        s1 = jax.nn.sigmoid(g_refs[j][...].astype(F32))
        s2 = jax.nn.sigmoid(g_refs[nj + j][...].astype(F32))
        ms.append((s1 * _dot(ys, w1_ref[:, cs]) + s2 * _dot(yb, w2_ref[:, cs])).astype(BF16))
    o_ref[...] = x_ref[...] + _dot(jnp.concatenate(ms, axis=1), wm_ref[...])


def _merge(x2, y_ssd, y_sb, proj, w1, w2, wm, g_blk0, tm, tn):
    t, d = x2.shape
    tm = min(tm, t)
    nj = d // tn
    resident = pl.Buffered(1)
    g_specs = [pl.BlockSpec((tm, tn), lambda i, b=g_blk0 + b: (i, b)) for b in range(2 * nj)]
    return pl.pallas_call(
        _merge_kernel,
        grid=(t // tm,),
        in_specs=[
            pl.BlockSpec((tm, d), lambda i: (i, 0)),
            pl.BlockSpec((tm, y_ssd.shape[1]), lambda i: (i, 0)),
            pl.BlockSpec((tm, y_sb.shape[1]), lambda i: (i, 0)),
            *g_specs,
            pl.BlockSpec(w1.shape, lambda i: (0, 0), pipeline_mode=resident),
            pl.BlockSpec(w2.shape, lambda i: (0, 0), pipeline_mode=resident),
            pl.BlockSpec(wm.shape, lambda i: (0, 0), pipeline_mode=resident),
        ],
        out_specs=pl.BlockSpec((tm, d), lambda i: (i, 0)),
        out_shape=jax.ShapeDtypeStruct((t, d), F32),
        compiler_params=_cparams(("parallel",)),
        name="merge",
    )(x2, y_ssd, y_sb, *([proj] * (2 * nj)), w1, w2, wm)


def _xattn_kernel(h_ref, kv_ref, nxa_ref, wq_ref, wo_ref, nmoe_ref, wr_ref, h2_ref, um_ref, rt_ref, cnt_ref,
                  base_ref):
    h1 = h_ref[...]
    un = _rms(h1, nxa_ref[...]).astype(BF16)
    q = _dot(un, wq_ref[...]).astype(BF16)
    width = XA_HEADS * XA_HEAD_DIM
    outs = []
    for hd in range(XA_HEADS):
        cs = slice(hd * XA_HEAD_DIM, (hd + 1) * XA_HEAD_DIM)
        k = kv_ref[:, cs]
        v = kv_ref[:, width + hd * XA_HEAD_DIM: width + (hd + 1) * XA_HEAD_DIM]
        sc = lax.dot_general(q[:, cs], k, (((1,), (1,)), ((), ())),
                             preferred_element_type=F32) * (XA_HEAD_DIM ** -0.5)
        sc = sc - jnp.max(sc, axis=-1, keepdims=True)
        p = jnp.exp(sc)
        p = p / jnp.sum(p, axis=-1, keepdims=True)
        outs.append(_dot(p.astype(BF16), v).astype(BF16))
    o = jnp.concatenate(outs, axis=1)
    h2 = h1 + _dot(o, wo_ref[...])
    h2_ref[...] = h2
    um = _rms(h2, nmoe_ref[...])
    _store_row_tiles(um_ref, _pack_halves(um))
    hi, lo = _split2(um)
    nr = rt_ref.shape[1]
    both = _dot(hi, wr_ref[...])
    lg = both[:, :nr] + both[:, nr:] + _dot(lo, wr_ref[:, :nr])

    lane = lax.broadcasted_iota(jnp.int32, lg.shape, 1)
    lane_f = lane.astype(F32)
    is_g = lane < N_GROUPS
    gl = jnp.where(is_g, lg, NEG_BIG)
    gmax = jnp.max(gl, axis=-1, keepdims=True)
    g_sel = jnp.min(jnp.where(gl == gmax, lane_f, float(LANES)), axis=-1, keepdims=True)
    g_gate = 1.0 / jnp.sum(jnp.where(is_g, jnp.exp(gl - gmax), 0.0), axis=-1, keepdims=True)
    lo_lane = N_GROUPS + EXPERTS_PER_GROUP * g_sel
    el = jnp.where((lane_f >= lo_lane) & (lane_f < lo_lane + EXPERTS_PER_GROUP), lg, NEG_BIG)
    m1 = jnp.max(el, axis=-1, keepdims=True)
    i1 = jnp.min(jnp.where(el == m1, lane_f, float(LANES)), axis=-1, keepdims=True)
    el2 = jnp.where(lane_f == i1, NEG_BIG, el)
    m2 = jnp.max(el2, axis=-1, keepdims=True)
    i2 = jnp.min(jnp.where(el2 == m2, lane_f, float(LANES)), axis=-1, keepdims=True)
    ex = jnp.exp(m2 - m1)
    p1 = 1.0 / (1.0 + ex)
    e1 = i1 - N_GROUPS
    e2 = i2 - N_GROUPS

    @pl.when(pl.program_id(0) == 0)
    def _():
        base_ref[...] = jnp.zeros_like(base_ref)

    tm = lg.shape[0]
    hot1 = lane_f == e1
    hot2 = lane_f == e2
    onehot = jnp.where(hot1 | hot2, 1.0, 0.0)
    tr = lax.broadcasted_iota(jnp.int32, (tm, tm), 0)
    tc = lax.broadcasted_iota(jnp.int32, (tm, tm), 1)
    before = _dot(jnp.where(tc < tr, 1.0, 0.0).astype(BF16), onehot.astype(BF16)) + base_ref[...]
    r1 = jnp.sum(jnp.where(hot1, before, 0.0), axis=-1, keepdims=True)
    r2 = jnp.sum(jnp.where(hot2, before, 0.0), axis=-1, keepdims=True)
    base_ref[...] += jnp.sum(onehot, axis=0, keepdims=True)
    cnt_ref[...] = jnp.broadcast_to(base_ref[...], cnt_ref.shape)

    vals = (e1, e2, p1 * g_gate, ex * p1 * g_gate, r1, r2)
    out = jnp.zeros_like(lg)
    for j, v in enumerate(vals):
        out = jnp.where(lane == j, v, out)
    rt_ref[...] = out


def _xattn(h1, kv, nxa, wq, wo, nmoe, wr, seq, tm):
    t, d = h1.shape
    tm = min(tm, seq)
    per_b = seq // tm
    m_len = kv.shape[0] // (t // seq)
    nr = wr.shape[1] // 2
    return pl.pallas_call(
        _xattn_kernel,
        grid=(t // tm,),
        in_specs=[
            pl.BlockSpec((tm, d), lambda i: (i, 0)),
            pl.BlockSpec((m_len, kv.shape[1]), lambda i: (i // per_b, 0)),
            pl.BlockSpec((1, d), lambda i: (0, 0)),
            pl.BlockSpec(wq.shape, lambda i: (0, 0)),
            pl.BlockSpec(wo.shape, lambda i: (0, 0)),
            pl.BlockSpec((1, d), lambda i: (0, 0)),
            pl.BlockSpec((d, 2 * nr), lambda i: (0, 0)),
        ],
        out_specs=[
            pl.BlockSpec((tm, d), lambda i: (i, 0)),
            pl.BlockSpec((tm * ROW_TILE, LANES), lambda i: (i, 0)),
            pl.BlockSpec((tm, nr), lambda i: (i, 0)),
            pl.BlockSpec((SUBLANES, nr), lambda i: (0, 0)),
        ],
        out_shape=[jax.ShapeDtypeStruct((t, d), F32), jax.ShapeDtypeStruct((t * ROW_TILE, LANES), U32),
                   jax.ShapeDtypeStruct((t, nr), F32), jax.ShapeDtypeStruct((SUBLANES, nr), F32)],
        scratch_shapes=[pltpu.VMEM((1, nr), F32)],
        compiler_params=_cparams(("arbitrary",)),
        name="xattn",
    )(h1, kv, nxa, wq, wo, nmoe, wr)


def _dispatch_kernel(slot_ref, padrow_ref, padn_ref, nv_ref, um_ref, xs_hbm, sa_ref, sb_ref, z_ref, sem, zsem):
    i = pl.program_id(0)
    tm = um_ref.shape[0] // ROW_TILE
    last = pl.num_programs(0) - 1
    stage = (sa_ref, sb_ref)

    def zero_rows(start):
        for e in range(N_EXPERTS):
            def body(j, c, e=e):
                cp = pltpu.make_async_copy(_row_tile(z_ref, 0), _row_tile(xs_hbm, padrow_ref[e] + j), zsem)
                if start:
                    cp.start()
                else:
                    cp.wait()
                return c

            lax.fori_loop(0, padn_ref[e], body, 0)

        def tail(b, c):
            blk = MOE_BLOCK * ROW_TILE
            cp = pltpu.make_async_copy(z_ref, xs_hbm.at[pl.ds(pl.multiple_of(b * blk, blk), blk)], zsem)
            if start:
                cp.start()
            else:
                cp.wait()
            return c

        lax.fori_loop(nv_ref[0], xs_hbm.shape[0] // (MOE_BLOCK * ROW_TILE), tail, 0)

    @pl.when(i == 0)
    def _():
        z_ref[...] = jnp.zeros_like(z_ref)
        zero_rows(True)

    def wait(par):
        for _ in range(TOP_K):
            pltpu.make_async_copy(stage[par], xs_hbm.at[pl.ds(0, tm * ROW_TILE)], sem.at[par]).wait()

    def step(par):
        pl.when(i >= 2)(functools.partial(wait, par))
        stage[par][...] = um_ref[...]
        tokens = tm * pl.num_programs(0)
        for r in range(tm):
            for k in range(TOP_K):
                pltpu.make_async_copy(_row_tile(stage[par], r),
                                      _row_tile(xs_hbm, slot_ref[k * tokens + i * tm + r]),
                                      sem.at[par]).start(priority=k)

        @pl.when(i == last)
        def _():
            wait(par)
            pl.when(i >= 1)(functools.partial(wait, 1 - par))

    for par in range(2):
        pl.when(i % 2 == par)(functools.partial(step, par))

    @pl.when(i == last)
    def _():
        zero_rows(False)


def _dispatch(slot, pad_row, pad_n, n_valid, um, n_slots, tm):
    t = um.shape[0] // ROW_TILE
    tm = min(tm, t)
    rows = tm * ROW_TILE
    grid_spec = pltpu.PrefetchScalarGridSpec(
        num_scalar_prefetch=4,
        grid=(t // tm,),
        in_specs=[pl.BlockSpec((rows, LANES), lambda i, s, pr, pn, nv: (i, 0))],
        out_specs=pl.BlockSpec(memory_space=pl.ANY),
        scratch_shapes=[pltpu.VMEM((rows, LANES), um.dtype), pltpu.VMEM((rows, LANES), um.dtype),
                        pltpu.VMEM((MOE_BLOCK * ROW_TILE, LANES), um.dtype),
                        pltpu.SemaphoreType.DMA((2,)), pltpu.SemaphoreType.DMA(())],
    )
    return pl.pallas_call(
        _dispatch_kernel,
        grid_spec=grid_spec,
        out_shape=jax.ShapeDtypeStruct((n_slots * ROW_TILE, LANES), um.dtype),
        compiler_params=_cparams(("arbitrary",)),
        name="dispatch",
    )(slot, pad_row, pad_n, n_valid, um)


def _expert_kernel(be_ref, nv_ref, x_ref, wg_ref, wu_ref, wd_ref, o_ref, wgb_ref, wub_ref, wdb_ref, xb_ref):
    i = pl.program_id(0)

    @pl.when(jnp.logical_or(i == 0, be_ref[i] != be_ref[jnp.maximum(i - 1, 0)]))
    def _():
        wgb_ref[...] = wg_ref[...].astype(BF16)
        wub_ref[...] = wu_ref[...].astype(BF16)
        wdb_ref[...] = wd_ref[...].astype(BF16)

    @pl.when(i < nv_ref[0])
    def _():
        half = xb_ref.shape[1] // 2
        for c in range(ROW_TILE):
            lo, hi = _unpack_halves(_load_row_tile_chunk(x_ref, c, MOE_BLOCK))
            xb_ref[:, c * LANES:(c + 1) * LANES] = lo.astype(BF16)
            xb_ref[:, half + c * LANES:half + (c + 1) * LANES] = hi.astype(BF16)
        xb = xb_ref[...]
        hid = (_silu(_dot(xb, wgb_ref[...])) * _dot(xb, wub_ref[...])).astype(BF16)
        _store_row_tiles(o_ref, _pack_halves(_dot(hid, wdb_ref[...])))

    @pl.when(i >= nv_ref[0])
    def _():
        o_ref[...] = jnp.zeros_like(o_ref)


def _experts(block_e, n_valid, xs, wg, wu, wd):
    blk = MOE_BLOCK * ROW_TILE
    n_blocks = xs.shape[0] // blk
    d, ff = wg.shape[1:]

    def x_map(i, be, nv):
        return (jnp.minimum(i, nv[0] - 1), 0)

    grid_spec = pltpu.PrefetchScalarGridSpec(
        num_scalar_prefetch=2,
        grid=(n_blocks,),
        in_specs=[
            pl.BlockSpec((blk, LANES), x_map),
            pl.BlockSpec((None, d, ff), lambda i, be, nv: (be[i], 0, 0)),
            pl.BlockSpec((None, d, ff), lambda i, be, nv: (be[i], 0, 0)),
            pl.BlockSpec((None, ff, d), lambda i, be, nv: (be[i], 0, 0)),
        ],
        out_specs=pl.BlockSpec((blk, LANES), lambda i, be, nv: (i, 0)),
        scratch_shapes=[pltpu.VMEM((d, ff), BF16), pltpu.VMEM((d, ff), BF16), pltpu.VMEM((ff, d), BF16),
                        pltpu.VMEM((MOE_BLOCK, d), BF16)],
    )
    return pl.pallas_call(
        _expert_kernel,
        grid_spec=grid_spec,
        out_shape=jax.ShapeDtypeStruct(xs.shape, U32),
        compiler_params=_cparams(("arbitrary",)),
        name="experts",
    )(block_e, n_valid, xs, wg, wu, wd)


def _combine_kernel(slot_ref, h_ref, w_ref, nw_ref, yb_hbm, o_ref, ya0_ref, ya1_ref, yb0_ref, yb1_ref, sem):
    i = pl.program_id(0)
    tm = h_ref.shape[0]
    last = pl.num_programs(0) - 1
    bufs = ((ya0_ref, ya1_ref), (yb0_ref, yb1_ref))

    def gather(blk, par):
        tokens = tm * pl.num_programs(0)
        for r in range(tm):
            for k in range(TOP_K):
                pltpu.make_async_copy(_row_tile(yb_hbm, slot_ref[k * tokens + blk * tm + r]),
                                      _row_tile(bufs[par][k], r), sem.at[par]).start(priority=k)

    def wait(par):
        for k in range(TOP_K):
            pltpu.make_async_copy(yb_hbm.at[pl.ds(0, tm * ROW_TILE)], bufs[par][k], sem.at[par]).wait()

    @pl.when(i == 0)
    def _():
        gather(i, 0)

    def step(par):
        wait(par)
        gather(jnp.minimum(i + 1, last), 1 - par)
        w0 = w_ref[:, TOP_K:TOP_K + 1]
        w1 = w_ref[:, TOP_K + 1:TOP_K + 2]
        half = h_ref.shape[1] // 2
        sq = jnp.zeros((tm, LANES), F32)
        for c in range(ROW_TILE):
            ya = _unpack_halves(_load_row_tile_chunk(bufs[par][0], c, tm))
            yb = _unpack_halves(_load_row_tile_chunk(bufs[par][1], c, tm))
            for p in range(2):
                cs = slice(p * half + c * LANES, p * half + (c + 1) * LANES)
                h3 = h_ref[:, cs] + w0 * ya[p] + w1 * yb[p]
                sq = sq + h3 * h3
                o_ref[:, cs] = h3
        inv = lax.rsqrt(jnp.sum(sq, axis=-1, keepdims=True) / h_ref.shape[1] + EPS)
        o_ref[...] = o_ref[...] * inv * nw_ref[...]

        @pl.when(i == last)
        def _():
            wait(1 - par)

    for par in range(2):
        pl.when(i % 2 == par)(functools.partial(step, par))


def _combine(slot, h2, route, nw, yb, tm):
    t, d = h2.shape
    tm = min(tm, t)
    grid_spec = pltpu.PrefetchScalarGridSpec(
        num_scalar_prefetch=1,
        grid=(t // tm,),
        in_specs=[
            pl.BlockSpec((tm, d), lambda i, s: (i, 0)),
            pl.BlockSpec((tm, route.shape[1]), lambda i, s: (i, 0)),
            pl.BlockSpec((1, d), lambda i, s: (0, 0)),
            pl.BlockSpec(memory_space=pl.ANY),
        ],
        out_specs=pl.BlockSpec((tm, d), lambda i, s: (i, 0)),
        scratch_shapes=([pltpu.VMEM((tm * ROW_TILE, LANES), yb.dtype)] * (2 * TOP_K)
                        + [pltpu.SemaphoreType.DMA((2,))]),
    )
    return pl.pallas_call(
        _combine_kernel,
        grid_spec=grid_spec,
        out_shape=jax.ShapeDtypeStruct((t, d), F32),
        compiler_params=_cparams(("arbitrary",)),
        name="combine",
    )(slot, h2, route, nw, yb)


def _slots(route, cnt, t):
    rt = route[:, :SUBLANES].T.astype(jnp.int32)
    e_kt = rt[:TOP_K]
    rank = rt[2 * TOP_K:3 * TOP_K]
    onehot = e_kt[None] == jnp.arange(N_EXPERTS, dtype=jnp.int32)[:, None, None]
    counts = cnt[0, :N_EXPERTS].astype(jnp.int32)
    padded = (counts + MOE_BLOCK - 1) // MOE_BLOCK * MOE_BLOCK
    pad_end = jnp.cumsum(padded)
    pad_start = pad_end - padded
    slot = (jnp.sum(jnp.where(onehot, pad_start[:, None, None], 0), axis=0) + rank).reshape(-1)
    n_slots = t * TOP_K + N_EXPERTS * MOE_BLOCK
    n_blocks = n_slots // MOE_BLOCK
    block_start = jnp.arange(n_blocks, dtype=jnp.int32) * MOE_BLOCK
    block_e = jnp.minimum(jnp.sum((pad_end[None, :] <= block_start[:, None]).astype(jnp.int32), axis=1),
                          N_EXPERTS - 1)
    n_valid = (pad_end[-1:] // MOE_BLOCK).astype(jnp.int32)
    return slot, pad_start + counts, padded - counts, block_e, n_valid, n_slots


def _pack_w_kernel(w_ref, o_ref, *, pieces):
    filled = 0
    for src0, src1, dst0 in pieces:
        o_ref[dst0:dst0 + src1 - src0, :] = w_ref[src0:src1, :].astype(o_ref.dtype)
        filled = max(filled, dst0 + src1 - src0)
    if filled < o_ref.shape[0]:
        o_ref[filled:, :] = jnp.zeros((o_ref.shape[0] - filled, o_ref.shape[1]), o_ref.dtype)


def _pack_w(wt, pieces, n_out, tk):
    n_in, k = wt.shape
    return pl.pallas_call(
        functools.partial(_pack_w_kernel, pieces=pieces),
        grid=(k // tk,),
        in_specs=[pl.BlockSpec((n_in, tk), lambda i: (0, i))],
        out_specs=pl.BlockSpec((n_out, tk), lambda i: (0, i)),
        out_shape=jax.ShapeDtypeStruct((n_out, k), BF16),
        compiler_params=_cparams(("parallel",)),
        name="pack_w",
    )(wt)


def _pad_lanes(v, n=LANES):
    return jnp.pad(v, ((0, 0), (0, n - v.shape[1])))


def kernel(x, mem, norm_mix_w, w_in, conv_w, conv_b, dt_bias, a_log, d_skip, ssd_norm_w, w_ssd_branch, w_sb_branch, w_mix_out, norm_xa_w, norm_mem_w, w_xq, w_xk, w_xv, w_xo, norm_moe_w, w_router_group, w_router_expert, w_expert_gate, w_expert_up, w_expert_down, norm_final_w):
    bsz, seq, d = x.shape
    depth = w_in.shape[0]
    t = bsz * seq
    inner = ssd_norm_w.shape[1]
    heads = dt_bias.shape[1]
    bc = SSD_GROUPS * SSD_STATE
    conv_dim = inner + 2 * bc
    sb_width = SB_HEADS * SB_HEAD_DIM
    col_dt = inner + conv_dim
    col_qkv = col_dt + heads
    col_g = col_qkv + 3 * sb_width
    g_col0 = 2 * inner
    bc_col0 = g_col0 + 2 * d
    q_col0 = bc_col0 + 2 * bc
    n_main = q_col0 + 3 * sb_width
    tn_proj = 2048
    n_proj = -(-n_main // tn_proj) * tn_proj
    tn_merge = 1024

    assert depth == 1, "single-layer configuration: the final RMSNorm is fused into the last MoE combine"
    h = x.reshape(t, d)
    for l in range(depth):
        w_t = jnp.swapaxes(w_in[l], 0, 1)
        w_main = _pack_w(w_t, [(0, 2 * inner, 0), (col_g, col_g + 2 * d, g_col0),
                               (2 * inner, col_dt, bc_col0), (col_qkv, col_g, q_col0)], n_proj, tk=256)
        w_dt = jnp.pad(w_t[col_dt:col_dt + heads], ((0, LANES - heads), (0, 0))).astype(BF16)
        e_mat = (jnp.arange(inner, dtype=jnp.int32)[None, :] // SSD_HEAD_DIM
                 == jnp.arange(LANES, dtype=jnp.int32)[:, None]).astype(BF16)
        dskip_x = jnp.repeat(d_skip[l], SSD_HEAD_DIM)[None, :]

        proj, dt_raw = _norm_matmul(h, norm_mix_w[l][None, :], w_main, w_dt, tm=1024, tn=tn_proj)
        y_ssd = _ssd(proj, dt_raw, conv_w[l], conv_b[l][None, :], _pad_lanes(dt_bias[l][None, :]),
                     _pad_lanes(a_log[l][None, :]), dskip_x, ssd_norm_w[l][None, :], e_mat,
                     bsz, seq, inner, bc, bc_col0 // bc)
        y_sb = _stickbreak(proj.reshape(bsz, seq, n_proj), bsz, seq, q_col0 // SB_HEAD_DIM)
        h1 = _merge(h, y_ssd, y_sb.reshape(t, sb_width), proj,
                    w_ssd_branch[l].astype(BF16), w_sb_branch[l].astype(BF16), w_mix_out[l].astype(BF16),
                    g_col0 // tn_merge, tm=512, tn=tn_merge)

        m_len = mem.shape[1]
        w_kv = jnp.concatenate([w_xk[l].T, w_xv[l].T], axis=0).astype(BF16)
        kv, _ = _norm_matmul(mem.reshape(bsz * m_len, d), norm_mem_w[l][None, :], w_kv,
                             jnp.zeros((LANES, d), BF16), tm=bsz * m_len, tn=512)
        w_r = _pad_lanes(jnp.concatenate([w_router_group[l], w_router_expert[l]], axis=1))
        wr_hi = w_r.astype(BF16)
        wr_lo = (w_r - wr_hi.astype(F32)).astype(BF16)
        h2, um, route, cnt = _xattn(h1, kv, norm_xa_w[l][None, :], w_xq[l].astype(BF16), w_xo[l].astype(BF16),
                               norm_moe_w[l][None, :], jnp.concatenate([wr_hi, wr_lo], axis=1), seq, tm=512)

        slot, pad_row, pad_n, block_e, n_valid, n_slots = _slots(route, cnt, t)
        xs = _dispatch(slot, pad_row, pad_n, n_valid, um, n_slots, tm=512)
        yb = _experts(block_e, n_valid, xs, w_expert_gate[l], w_expert_up[l], w_expert_down[l])
        h = _combine(slot, h2, route, norm_final_w[None, :], yb, tm=512)
    return h.reshape(bsz, seq, d)
```
